```python
import jax
import jax.numpy as jnp
from jax import lax
import numpy as np

D_MODEL = 1024
BATCH = 4
SEQ = 4096
DEPTH = 2

NORM_EPS = 1e-6
LN_EPS = 1e-5
N_DIR = 2

CONV_CH = 512
CONV_WIDTH = 31
DN_HEADS = 4
DN_HEAD_DIM = 128
DN_WIDTH = DN_HEADS * DN_HEAD_DIM
DN_CONV_WIDTH = 5
DN_CHUNK = 64
SG_GROUPS = 4
SG_GROUP_DIM = 128
SG_WIDTH = SG_GROUPS * SG_GROUP_DIM
SG_CHUNK = 128
ML_HEADS = 4
ML_QK_DIM = 64
ML_V_DIM = 128
ML_QK_WIDTH = ML_HEADS * ML_QK_DIM
ML_V_WIDTH = ML_HEADS * ML_V_DIM
ML_CHUNK = 64
D_FF = ((8 * D_MODEL + 3 * 256 - 1) // (3 * 256)) * 256

EVEN_IN = 2 * CONV_CH + 4 * DN_WIDTH + 2 * N_DIR * DN_HEADS
EVEN_MIX = CONV_CH + DN_WIDTH
ODD_IN = 2 * SG_WIDTH + 2 * ML_QK_WIDTH + 2 * ML_V_WIDTH + 2 * N_DIR * ML_HEADS
ODD_MIX = SG_WIDTH + ML_V_WIDTH
N_EVEN = (DEPTH + 1) // 2
N_ODD = DEPTH // 2

kernel_name = "bidir_hybrid_conv_deltanet_gmlp_mlstm"


def rms_norm(x, g):
    xf = x.astype(jnp.float32)
    y = xf * lax.rsqrt(jnp.mean(xf * xf, axis=-1, keepdims=True) + NORM_EPS)
    return (y * g.astype(jnp.float32)).astype(x.dtype)


def layer_norm(x, g, b):
    xf = x.astype(jnp.float32)
    mu = jnp.mean(xf, axis=-1, keepdims=True)
    xc = xf - mu
    var = jnp.mean(xc * xc, axis=-1, keepdims=True)
    y = xc * lax.rsqrt(var + LN_EPS) * g.astype(jnp.float32) + b.astype(jnp.float32)
    return y.astype(x.dtype)


def l2_normalize(x):
    xf = x.astype(jnp.float32)
    return (xf * lax.rsqrt(jnp.sum(xf * xf, axis=-1, keepdims=True) + NORM_EPS)).astype(x.dtype)


def depthwise_conv(x, w):
    pad = w.shape[0] // 2
    return lax.conv_general_dilated(
        x, w[:, None, :].astype(x.dtype), window_strides=(1,), padding=[(pad, pad)],
        dimension_numbers=("NWC", "WIO", "NWC"), feature_group_count=x.shape[-1])


def _split(z, sizes):
    idx = [int(i) for i in np.cumsum(sizes)[:-1]]
    return jnp.split(z, idx, axis=-1)


def _chunk(t, size):
    return t.reshape(t.shape[0], t.shape[1], t.shape[2] // size, size, *t.shape[3:])


def _to_scan(t):
    return jnp.moveaxis(t, 2, 0)


def _flip(t):
    return jnp.flip(t, axis=2)


def gated_delta_rule(q, k, v, log_alpha, beta):
    out_dtype = v.dtype
    q, k, v, log_alpha, beta = (t.astype(jnp.float32) for t in (q, k, v, log_alpha, beta))
    bsz, nh, seq, dk = q.shape
    dv = v.shape[-1]
    L = DN_CHUNK
    q = q * (dk ** -0.5)
    q, k, v, log_alpha, beta = (_chunk(t, L) for t in (q, k, v, log_alpha, beta))
    g = jnp.cumsum(log_alpha, axis=-1)
    tri_incl = jnp.tril(jnp.ones((L, L), dtype=bool))
    tri_strict = jnp.tril(jnp.ones((L, L), dtype=bool), -1)
    decay = jnp.exp(jnp.where(tri_incl, g[..., :, None] - g[..., None, :], -jnp.inf))
    k_beta = k * beta[..., None]
    lower = jnp.where(tri_strict, jnp.einsum("bhnid,bhnjd->bhnij", k_beta, k) * decay, 0.0)
    unit_lower = jnp.eye(L, dtype=jnp.float32) + lower
    rhs = jnp.concatenate([v * beta[..., None], k_beta * jnp.exp(g)[..., None]], axis=-1)
    sol = lax.linalg.triangular_solve(unit_lower, rhs, left_side=True, lower=True, unit_diagonal=True)
    u, w = sol[..., :dv], sol[..., dv:]
    attn = jnp.einsum("bhnid,bhnjd->bhnij", q, k) * decay

    def step(state, inp):
        q_c, k_c, u_c, w_c, g_c, a_c = inp
        v_new = u_c - jnp.einsum("bhld,bhdv->bhlv", w_c, state)
        o_c = (jnp.einsum("bhld,bhdv->bhlv", q_c * jnp.exp(g_c)[..., None], state)
               + jnp.einsum("bhij,bhjv->bhiv", a_c, v_new))
        g_last = g_c[..., -1:]
        state = (state * jnp.exp(g_last)[..., None]
                 + jnp.einsum("bhld,bhlv->bhdv", k_c * jnp.exp(g_last - g_c)[..., None], v_new))
        return state, o_c

    state0 = jnp.zeros((bsz, nh, dk, dv), jnp.float32)
    xs = tuple(_to_scan(t) for t in (q, k, u, w, g, attn))
    _, o = lax.scan(step, state0, xs)
    return jnp.moveaxis(o, 0, 2).reshape(bsz, nh, seq, dv).astype(out_dtype)


def mlstm_chunkwise(q, k, v, log_i, log_f):
    out_dtype = v.dtype
    q, k, v, log_i, log_f = (t.astype(jnp.float32) for t in (q, k, v, log_i, log_f))
    bsz, nh, seq, dqk = q.shape
    dv = v.shape[-1]
    L = ML_CHUNK
    q = q * (dqk ** -0.5)
    q, k, v, log_i, log_f = (_chunk(t, L) for t in (q, k, v, log_i, log_f))
    b = jnp.cumsum(log_f, axis=-1)
    tri = jnp.tril(jnp.ones((L, L), dtype=bool))
    d_mat = jnp.where(tri, b[..., :, None] - b[..., None, :] + log_i[..., None, :], -jnp.inf)
    d_max = jnp.max(d_mat, axis=-1)
    w_end = b[..., -1:] - b + log_i
    scores = jnp.einsum("bhntd,bhnsd->bhnts", q, k)

    def step(carry, inp):
        c_st, n_st, m_st = carry
        q_c, k_c, v_c, b_c, dm_c, dmax_c, we_c, sc_c = inp
        inter_log = b_c + m_st[..., None]
        m_t = jnp.maximum(inter_log, dmax_c)
        inter_w = jnp.exp(inter_log - m_t)
        intra_w = jnp.exp(dm_c - m_t[..., None]) * sc_c
        num = (inter_w[..., None] * jnp.einsum("bhld,bhdv->bhlv", q_c, c_st)
               + jnp.einsum("bhts,bhsv->bhtv", intra_w, v_c))
        den = inter_w * jnp.einsum("bhld,bhd->bhl", q_c, n_st) + jnp.sum(intra_w, axis=-1)
        h_c = num / jnp.maximum(jnp.abs(den), jnp.exp(-m_t))[..., None]
        decay_log = b_c[..., -1] + m_st
        m_new = jnp.maximum(decay_log, jnp.max(we_c, axis=-1))
        carry_w = jnp.exp(decay_log - m_new)
        s_w = jnp.exp(we_c - m_new[..., None])
        c_new = carry_w[..., None, None] * c_st + jnp.einsum("bhl,bhld,bhlv->bhdv", s_w, k_c, v_c)
        n_new = carry_w[..., None] * n_st + jnp.einsum("bhl,bhld->bhd", s_w, k_c)
        return (c_new, n_new, m_new), h_c

    carry0 = (jnp.zeros((bsz, nh, dqk, dv), jnp.float32),
              jnp.zeros((bsz, nh, dqk), jnp.float32),
              jnp.zeros((bsz, nh), jnp.float32))
    xs = tuple(_to_scan(t) for t in (q, k, v, b, d_mat, d_max, w_end, scores))
    _, h = lax.scan(step, carry0, xs)
    return jnp.moveaxis(h, 0, 2).reshape(bsz, nh, seq, dv).astype(out_dtype)


def even_mixer(h, w_in, conv_w, conv_b, conv_ln_g, conv_ln_b, dn_conv_w, dn_a_log, dn_dt_bias, dn_norm_g, w_out):
    bsz, seq, _ = h.shape
    z = h @ w_in
    a_val, a_gate, qkv, gate, beta_pre, alpha_pre = _split(
        z, [CONV_CH, CONV_CH, 3 * DN_WIDTH, DN_WIDTH, N_DIR * DN_HEADS, N_DIR * DN_HEADS])
    ya = depthwise_conv(a_val * jax.nn.sigmoid(a_gate), conv_w) + conv_b
    ya = jax.nn.silu(layer_norm(ya, conv_ln_g, conv_ln_b))
    qkv = jax.nn.silu(depthwise_conv(qkv, dn_conv_w))
    heads = lambda t: t.reshape(bsz, seq, DN_HEADS, DN_HEAD_DIM).transpose(0, 2, 1, 3)
    q, k, v = (heads(t) for t in jnp.split(qkv, 3, axis=-1))
    q, k = l2_normalize(q), l2_normalize(k)
    per_dir = lambda t: t.reshape(bsz, seq, N_DIR, DN_HEADS).transpose(2, 0, 3, 1)
    beta = per_dir(jax.nn.sigmoid(beta_pre.astype(jnp.float32)))
    log_alpha = per_dir(-jnp.exp(dn_a_log.reshape(-1).astype(jnp.float32))
                        * jax.nn.softplus((alpha_pre + dn_dt_bias.reshape(-1)).astype(jnp.float32)))
    o_fwd = gated_delta_rule(q, k, v, log_alpha[0], beta[0])
    o_bwd = _flip(gated_delta_rule(_flip(q), _flip(k), _flip(v), _flip(log_alpha[1]), _flip(beta[1])))
    o = (o_fwd + o_bwd).transpose(0, 2, 1, 3)
    o = rms_norm(o, dn_norm_g) * jax.nn.silu(gate.reshape(bsz, seq, DN_HEADS, DN_HEAD_DIM))
    y = jnp.concatenate([ya, o.reshape(bsz, seq, DN_WIDTH)], axis=-1)
    return y @ w_out


def odd_mixer(h, w_in, sg_ln_g, sg_ln_b, sg_w, sg_b, ml_i_bias, ml_f_bias, ml_norm_g, w_out):
    bsz, seq, _ = h.shape
    n_chunks = seq // SG_CHUNK
    z = h @ w_in
    u_pre, v_pre, q, k, v, o_pre, i_pre, f_pre = _split(
        z, [SG_WIDTH, SG_WIDTH, ML_QK_WIDTH, ML_QK_WIDTH, ML_V_WIDTH, ML_V_WIDTH,
            N_DIR * ML_HEADS, N_DIR * ML_HEADS])
    u = jax.nn.gelu(u_pre)
    vv = layer_norm(jax.nn.gelu(v_pre), sg_ln_g, sg_ln_b)
    vv = vv.reshape(bsz, n_chunks, SG_CHUNK, SG_GROUPS, SG_GROUP_DIM)
    mixed = jnp.einsum("gts,bnsgc->bntgc", sg_w, vv) + sg_b.T[None, None, :, :, None]
    yc = u * mixed.reshape(bsz, seq, SG_WIDTH)
    heads = lambda t, d: t.reshape(bsz, seq, ML_HEADS, d).transpose(0, 2, 1, 3)
    q, k, v = heads(q, ML_QK_DIM), heads(k, ML_QK_DIM), heads(v, ML_V_DIM)
    per_dir = lambda t: t.reshape(bsz, seq, N_DIR, ML_HEADS).transpose(2, 0, 3, 1)
    log_i = per_dir((i_pre + ml_i_bias.reshape(-1)).astype(jnp.float32))
    log_f = per_dir(jax.nn.log_sigmoid((f_pre + ml_f_bias.reshape(-1)).astype(jnp.float32)))
    h_fwd = mlstm_chunkwise(q, k, v, log_i[0], log_f[0])
    h_bwd = _flip(mlstm_chunkwise(_flip(q), _flip(k), _flip(v), _flip(log_i[1]), _flip(log_f[1])))
    hd = (h_fwd + h_bwd).transpose(0, 2, 1, 3)
    hd = rms_norm(hd, ml_norm_g) * jax.nn.sigmoid(o_pre.reshape(bsz, seq, ML_HEADS, ML_V_DIM))
    y = jnp.concatenate([yc, hd.reshape(bsz, seq, ML_V_WIDTH)], axis=-1)
    return y @ w_out


def swiglu(h, w1, w3, w2):
    return (jax.nn.silu(h @ w1) * (h @ w3)) @ w2


def setup_inputs(seed: int = 0) -> dict:
    key = jax.random.key(seed)
    ks = iter(jax.random.split(key, 32))
    nrm = lambda shape, scale: jax.random.normal(next(ks), shape, jnp.float32) * scale
    gain = lambda shape: 1.0 + nrm(shape, 0.02)
    x = nrm((BATCH, SEQ, D_MODEL), 1.0)
    mix_norm_g = gain((DEPTH, D_MODEL))
    ev_w_in = nrm((N_EVEN, D_MODEL, EVEN_IN), D_MODEL ** -0.5)
    ev_conv_w = nrm((N_EVEN, CONV_WIDTH, CONV_CH), CONV_WIDTH ** -0.5)
    ev_conv_b = nrm((N_EVEN, CONV_CH), 0.02)
    ev_conv_ln_g = gain((N_EVEN, CONV_CH))
    ev_conv_ln_b = nrm((N_EVEN, CONV_CH), 0.02)
    ev_dn_conv_w = nrm((N_EVEN, DN_CONV_WIDTH, 3 * DN_WIDTH), DN_CONV_WIDTH ** -0.5)
    ev_dn_a_log = jnp.log(jax.random.uniform(next(ks), (N_EVEN, N_DIR, DN_HEADS), jnp.float32, 1.0, 16.0))
    dt = jnp.exp(jax.random.uniform(next(ks), (N_EVEN, N_DIR, DN_HEADS), jnp.float32,
                                    float(np.log(1e-3)), float(np.log(1e-1))))
    ev_dn_dt_bias = dt + jnp.log(-jnp.expm1(-dt))
    ev_dn_norm_g = gain((N_EVEN, DN_HEAD_DIM))
    ev_w_out = nrm((N_EVEN, EVEN_MIX, D_MODEL), EVEN_MIX ** -0.5)
    od_w_in = nrm((N_ODD, D_MODEL, ODD_IN), D_MODEL ** -0.5)
    od_sg_ln_g = gain((N_ODD, SG_WIDTH))
    od_sg_ln_b = nrm((N_ODD, SG_WIDTH), 0.02)
    od_sg_w = nrm((N_ODD, SG_GROUPS, SG_CHUNK, SG_CHUNK), SG_CHUNK ** -0.5)
    od_sg_b = 1.0 + nrm((N_ODD, SG_GROUPS, SG_CHUNK), 0.02)
    od_ml_i_bias = nrm((N_ODD, N_DIR, ML_HEADS), 0.1)
    od_ml_f_bias = 3.0 + 3.0 * jax.random.uniform(next(ks), (N_ODD, N_DIR, ML_HEADS), jnp.float32)
    od_ml_norm_g = gain((N_ODD, ML_V_DIM))
    od_w_out = nrm((N_ODD, ODD_MIX, D_MODEL), ODD_MIX ** -0.5)
    ffn_norm_g = gain((DEPTH, D_MODEL))
    ffn_w1 = nrm((DEPTH, D_MODEL, D_FF), D_MODEL ** -0.5)
    ffn_w3 = nrm((DEPTH, D_MODEL, D_FF), D_MODEL ** -0.5)
    ffn_w2 = nrm((DEPTH, D_FF, D_MODEL), D_FF ** -0.5)
    final_norm_g = gain((D_MODEL,))
    return {
        "x": x, "mix_norm_g": mix_norm_g,
        "ev_w_in": ev_w_in, "ev_conv_w": ev_conv_w, "ev_conv_b": ev_conv_b,
        "ev_conv_ln_g": ev_conv_ln_g, "ev_conv_ln_b": ev_conv_ln_b, "ev_dn_conv_w": ev_dn_conv_w,
        "ev_dn_a_log": ev_dn_a_log, "ev_dn_dt_bias": ev_dn_dt_bias, "ev_dn_norm_g": ev_dn_norm_g,
        "ev_w_out": ev_w_out,
        "od_w_in": od_w_in, "od_sg_ln_g": od_sg_ln_g, "od_sg_ln_b": od_sg_ln_b, "od_sg_w": od_sg_w,
        "od_sg_b": od_sg_b, "od_ml_i_bias": od_ml_i_bias, "od_ml_f_bias": od_ml_f_bias,
        "od_ml_norm_g": od_ml_norm_g, "od_w_out": od_w_out,
        "ffn_norm_g": ffn_norm_g, "ffn_w1": ffn_w1, "ffn_w3": ffn_w3, "ffn_w2": ffn_w2,
        "final_norm_g": final_norm_g,
    }


def reference(x, mix_norm_g, ev_w_in, ev_conv_w, ev_conv_b, ev_conv_ln_g, ev_conv_ln_b, ev_dn_conv_w,
              ev_dn_a_log, ev_dn_dt_bias, ev_dn_norm_g, ev_w_out, od_w_in, od_sg_ln_g, od_sg_ln_b, od_sg_w,
              od_sg_b, od_ml_i_bias, od_ml_f_bias, od_ml_norm_g, od_w_out, ffn_norm_g, ffn_w1, ffn_w3, ffn_w2,
              final_norm_g):
    h = x
    for layer in range(DEPTH):
        j = layer // 2
        hn = rms_norm(h, mix_norm_g[layer])
        if layer % 2 == 0:
            h = h + even_mixer(hn, ev_w_in[j], ev_conv_w[j], ev_conv_b[j], ev_conv_ln_g[j], ev_conv_ln_b[j],
                               ev_dn_conv_w[j], ev_dn_a_log[j], ev_dn_dt_bias[j], ev_dn_norm_g[j], ev_w_out[j])
        else:
            h = h + odd_mixer(hn, od_w_in[j], od_sg_ln_g[j], od_sg_ln_b[j], od_sg_w[j], od_sg_b[j],
                              od_ml_i_bias[j], od_ml_f_bias[j], od_ml_norm_g[j], od_w_out[j])
        h = h + swiglu(rms_norm(h, ffn_norm_g[layer]), ffn_w1[layer], ffn_w3[layer], ffn_w2[layer])
    return rms_norm(h, final_norm_g)
```

```python
import functools

import jax
import jax.numpy as jnp
from jax import lax
from jax.experimental import pallas as pl
from jax.experimental.pallas import tpu as pltpu

NORM_EPS = 1e-6
LN_EPS = 1e-5
N_DIR = 2

CONV_CH = 512
DN_HEADS = 4
DN_HEAD_DIM = 128
DN_WIDTH = DN_HEADS * DN_HEAD_DIM
SG_GROUPS = 4
SG_GROUP_DIM = 128
SG_WIDTH = SG_GROUPS * SG_GROUP_DIM
SG_CHUNK = 128
ML_HEADS = 4
ML_QK_DIM = 64
ML_V_DIM = 128
ML_QK_WIDTH = ML_HEADS * ML_QK_DIM
ML_V_WIDTH = ML_HEADS * ML_V_DIM

LANES = 128
SUBLANES = 8
VMEM_LIMIT_BYTES = 56 * 1024 * 1024

SCAN_CHUNK = 64
ROW_TILE = 512
CONV_TILE = 128
CONV_PAD = 16
NEG_BIG = -1e30

BF16 = jnp.bfloat16
F32 = jnp.float32


def _params(*sem):
    return pltpu.CompilerParams(dimension_semantics=sem, vmem_limit_bytes=VMEM_LIMIT_BYTES)


def _resident(shape):
    return pl.BlockSpec(shape, lambda *_: (0,) * len(shape), pipeline_mode=pl.Buffered(1))


def _dot(a, b):
    return jnp.dot(a, b, preferred_element_type=F32)


def _dot_nt(a, b):
    return lax.dot_general(a, b, (((1,), (1,)), ((), ())), preferred_element_type=F32)


def _dot_tn(a, b):
    return lax.dot_general(a, b, (((0,), (0,)), ((), ())), preferred_element_type=F32)


def _exact_dot01(t01, x):
    x1 = x.astype(BF16)
    r1 = x - x1.astype(F32)
    x2 = r1.astype(BF16)
    x3 = (r1 - x2.astype(F32)).astype(BF16)
    return _dot(t01, x1) + _dot(t01, x2) + _dot(t01, x3)


def _exact_dot01_r(x, t01):
    x1 = x.astype(BF16)
    r1 = x - x1.astype(F32)
    x2 = r1.astype(BF16)
    x3 = (r1 - x2.astype(F32)).astype(BF16)
    return _dot(x1, t01) + _dot(x2, t01) + _dot(x3, t01)


def _in_proj_kernel(h_ref, g_ref, w_ref, wg_ref, gp_ref, zm_ref, zg_ref, *, even):
    x = h_ref[...]
    ms = jnp.mean(x * x, axis=-1, keepdims=True)
    hn = (x * lax.rsqrt(ms + NORM_EPS) * g_ref[...]).astype(BF16)
    zm_ref[...] = _dot(hn, w_ref[...])
    zg = _dot(hn, wg_ref[...])
    p0 = gp_ref[0:1, :]
    p1 = gp_ref[1:2, :]
    lane = lax.broadcasted_iota(jnp.int32, zg.shape, 1)
    if even:
        first = jax.nn.sigmoid(zg)
        second = -jnp.exp(p0) * jax.nn.softplus(zg + p1)
    else:
        first = zg + p0
        second = jax.nn.log_sigmoid(zg + p1)
    zg_ref[...] = jnp.where(lane < 8, first, second)


def _in_proj(h, g, w_main, w_gate, gate_params, *, even):
    n, d = h.shape
    c = w_main.shape[1]
    tm = min(ROW_TILE, n)
    return pl.pallas_call(
        functools.partial(_in_proj_kernel, even=even),
        grid=(n // tm,),
        in_specs=[
            pl.BlockSpec((tm, d), lambda i: (i, 0)),
            _resident((1, d)),
            _resident((d, c)),
            _resident((d, LANES)),
            _resident((SUBLANES, LANES)),
        ],
        out_specs=[
            pl.BlockSpec((tm, c), lambda i: (i, 0)),
            pl.BlockSpec((tm, LANES), lambda i: (i, 0)),
        ],
        out_shape=[jax.ShapeDtypeStruct((n, c), F32), jax.ShapeDtypeStruct((n, LANES), F32)],
        compiler_params=_params("parallel"),
        name="in_proj_even" if even else "in_proj_odd",
    )(h, g.reshape(1, d), w_main, w_gate, gate_params)


def _conv_taps(pad_ref, w_ref, o_ref, bias, *, seq, width, post):
    half = width // 2

    def body(i, carry):
        t0 = pl.multiple_of(i * CONV_TILE, CONV_TILE)
        acc = jnp.zeros((CONV_TILE, LANES), F32)
        for j in range(width):
            acc = acc + pad_ref[pl.ds(t0 + (CONV_PAD - half + j), CONV_TILE), :] * w_ref[j:j + 1, :]
        if bias is not None:
            acc = acc + bias
        o_ref[pl.ds(t0, CONV_TILE), :] = post(acc)
        return carry

    lax.fori_loop(0, seq // CONV_TILE, body, 0)


def _fill_padded(pad_ref, x, seq):
    zeros = jnp.zeros((CONV_PAD, LANES), F32)
    pad_ref[0:CONV_PAD, :] = zeros
    pad_ref[CONV_PAD + seq:CONV_PAD + seq + CONV_PAD, :] = zeros
    pad_ref[CONV_PAD:CONV_PAD + seq, :] = x


def _conv_glu_kernel(av_ref, ag_ref, w_ref, b_ref, o_ref, pad_ref, *, seq, width):
    _fill_padded(pad_ref, av_ref[...] * jax.nn.sigmoid(ag_ref[...]), seq)
    _conv_taps(pad_ref, w_ref, o_ref, b_ref[...], seq=seq, width=width, post=lambda a: a)


def _conv_glu(z, conv_w, conv_b, *, batch, seq):
    width = conv_w.shape[0]
    nblk = CONV_CH // LANES
    return pl.pallas_call(
        functools.partial(_conv_glu_kernel, seq=seq, width=width),
        grid=(batch, nblk),
        in_specs=[
            pl.BlockSpec((seq, LANES), lambda b, c: (b, c)),
            pl.BlockSpec((seq, LANES), lambda b, c: (b, nblk + c)),
            pl.BlockSpec((width, LANES), lambda b, c: (0, c)),
            pl.BlockSpec((1, LANES), lambda b, c: (0, c)),
        ],
        out_specs=pl.BlockSpec((seq, LANES), lambda b, c: (b, c)),
        out_shape=jax.ShapeDtypeStruct((batch * seq, CONV_CH), F32),
        scratch_shapes=[pltpu.VMEM((seq + 2 * CONV_PAD, LANES), F32)],
        compiler_params=_params("parallel", "parallel"),
        name="conv_glu",
    )(z, z, conv_w, conv_b.reshape(1, CONV_CH))


def _conv_qkv_kernel(x_ref, w_ref, o_ref, pad_ref, *, seq, width, n_norm_blocks):
    _fill_padded(pad_ref, x_ref[...], seq)
    normalise = pl.program_id(1) < n_norm_blocks

    def post(a):
        y = a * jax.nn.sigmoid(a)
        yn = y * lax.rsqrt(jnp.sum(y * y, axis=-1, keepdims=True) + NORM_EPS)
        return jnp.where(normalise, yn, y)

    _conv_taps(pad_ref, w_ref, o_ref, None, seq=seq, width=width, post=post)


def _conv_qkv(z, dn_conv_w, *, batch, seq, col0):
    width = dn_conv_w.shape[0]
    nblk = 3 * DN_WIDTH // LANES
    blk0 = col0 // LANES
    return pl.pallas_call(
        functools.partial(_conv_qkv_kernel, seq=seq, width=width, n_norm_blocks=2 * DN_HEADS),
        grid=(batch, nblk),
        in_specs=[
            pl.BlockSpec((seq, LANES), lambda b, c: (b, blk0 + c)),
            pl.BlockSpec((width, LANES), lambda b, c: (0, c)),
        ],
        out_specs=pl.BlockSpec((seq, LANES), lambda b, c: (b, c)),
        out_shape=jax.ShapeDtypeStruct((batch * seq, 3 * DN_WIDTH), F32),
        scratch_shapes=[pltpu.VMEM((seq + 2 * CONV_PAD, LANES), F32)],
        compiler_params=_params("parallel", "parallel"),
        name="conv_qkv",
    )(z, dn_conv_w)


def _tri_masks(n, backward):
    row = lax.broadcasted_iota(jnp.int32, (n, n), 0)
    col = lax.broadcasted_iota(jnp.int32, (n, n), 1)
    d = jnp.where(backward, col - row, row - col)
    return row, col, d >= 0, d > 0, d <= 0


def _unit_tri_inverse(a, row, col):
    n = a.shape[0]
    eye = (row == col).astype(F32)
    same = (row >> 3) == (col >> 3)
    l8 = jnp.where(same, a, 0.0)
    l2 = _dot(l8, l8)
    l4 = _dot(l2, l2)
    x = eye - l8
    x = x + _dot(x, l2)
    x = x + _dot(x, l4)
    shift = 3
    while (1 << shift) < n:
        same_next = (row >> (shift + 1)) == (col >> (shift + 1))
        c = jnp.where(jnp.logical_and(same_next, jnp.logical_not(same)), a, 0.0)
        x = x - _dot(_dot(x, c), x)
        same = same_next
        shift += 1
    return x


def _delta_kernel(q_ref, k_ref, v_ref, gc_ref, gr_ref, o_ref, s_ref, *, batch):
    L = SCAN_CHUNK
    backward = pl.program_id(0) >= batch

    @pl.when(pl.program_id(1) == 0)
    def _():
        s_ref[...] = jnp.zeros_like(s_ref)

    row, col, incl, strict, incl_t = _tri_masks(L, backward)
    ones_ll = jnp.ones((L, L), BF16)
    gcols = gc_ref[...]
    grows = gr_ref[...]
    g_cols = _exact_dot01(jnp.where(incl, 1.0, 0.0).astype(BF16), gcols)
    g_rows = _exact_dot01_r(grows, jnp.where(incl_t, 1.0, 0.0).astype(BF16))
    gt_cols = _exact_dot01(ones_ll, gcols)
    scale = DN_HEAD_DIM ** -0.5

    for h in range(DN_HEADS):
        sl = slice(h * DN_HEAD_DIM, (h + 1) * DN_HEAD_DIM)
        q = q_ref[:, sl]
        k = k_ref[:, sl]
        v = v_ref[:, sl]
        beta = gcols[:, h:h + 1]
        g_c = g_cols[:, DN_HEADS + h:DN_HEADS + h + 1]
        g_r = g_rows[DN_HEADS + h:DN_HEADS + h + 1, :]
        g_t = gt_cols[:, DN_HEADS + h:DN_HEADS + h + 1]
        decay = jnp.exp(jnp.where(incl, g_c - g_r, NEG_BIG))
        kb = k * beta
        k16 = k.astype(BF16)
        a = jnp.where(strict, _dot_nt(kb.astype(BF16), k16) * decay, 0.0)
        x = _unit_tri_inverse(a, row, col)
        rhs = jnp.concatenate([v * beta, kb * jnp.exp(g_c)], axis=-1)
        uw = _dot(x, rhs)
        u = uw[:, :DN_HEAD_DIM]
        w = uw[:, DN_HEAD_DIM:]
        attn = jnp.where(incl, _dot_nt((q * scale).astype(BF16), k16) * decay, 0.0)
        state = s_ref[h]
        s16 = state.astype(BF16)
        v_new = u - _dot(w.astype(BF16), s16)
        o = _dot((q * (scale * jnp.exp(g_c))).astype(BF16), s16) + _dot(attn.astype(BF16), v_new.astype(BF16))
        o_ref[:, sl] = o
        kg = k * jnp.exp(g_t - g_c)
        s_ref[h] = state * jnp.exp(g_t[0:1, :]) + _dot_tn(kg.astype(BF16), v_new.astype(BF16))


def _scan_chunk_index(i, n, *, batch, n_chunks):
    b = i % batch
    backward = i // batch
    return b * n_chunks + n + backward * (n_chunks - 1 - 2 * n)


def _gate_layouts(zg, *, batch, seq):
    n_chunks = seq // SCAN_CHUNK
    g = zg[:, :16].reshape(batch * seq, 2, N_DIR, 4)
    cols = g.transpose(2, 0, 1, 3).reshape(N_DIR, batch * seq, 8)
    rows = cols.reshape(N_DIR, batch * n_chunks, SCAN_CHUNK, 8).transpose(0, 1, 3, 2)
    return cols, rows


def _delta_scan(qkv, zg, *, batch, seq):
    n_chunks = seq // SCAN_CHUNK
    L = SCAN_CHUNK
    cols, rows = _gate_layouts(zg, batch=batch, seq=seq)
    cidx = functools.partial(_scan_chunk_index, batch=batch, n_chunks=n_chunks)
    return pl.pallas_call(
        functools.partial(_delta_kernel, batch=batch),
        grid=(N_DIR * batch, n_chunks),
        in_specs=[
            pl.BlockSpec((L, DN_WIDTH), lambda i, n: (cidx(i, n), 0)),
            pl.BlockSpec((L, DN_WIDTH), lambda i, n: (cidx(i, n), 1)),
            pl.BlockSpec((L, DN_WIDTH), lambda i, n: (cidx(i, n), 2)),
            pl.BlockSpec((None, L, 8), lambda i, n: (i // batch, cidx(i, n), 0)),
            pl.BlockSpec((None, None, 8, L), lambda i, n: (i // batch, cidx(i, n), 0, 0)),
        ],
        out_specs=pl.BlockSpec((None, L, DN_WIDTH), lambda i, n: (i // batch, cidx(i, n), 0)),
        out_shape=jax.ShapeDtypeStruct((N_DIR, batch * seq, DN_WIDTH), F32),
        scratch_shapes=[pltpu.VMEM((DN_HEADS, DN_HEAD_DIM, DN_HEAD_DIM), F32)],
        compiler_params=_params("parallel", "arbitrary"),
        name="delta_scan",
    )(qkv, qkv, qkv, cols, rows)


def _mlstm_kernel(q_ref, k_ref, v_ref, gc_ref, gr_ref, o_ref, c_ref, m_ref, *, batch):
    L = SCAN_CHUNK
    backward = pl.program_id(0) >= batch

    @pl.when(pl.program_id(1) == 0)
    def _():
        c_ref[...] = jnp.zeros_like(c_ref)
        m_ref[...] = jnp.zeros_like(m_ref)

    _, _, incl, _, incl_t = _tri_masks(L, backward)
    ones_ll = jnp.ones((L, L), BF16)
    gcols = gc_ref[...]
    grows = gr_ref[...]
    b_cols = _exact_dot01(jnp.where(incl, 1.0, 0.0).astype(BF16), gcols)
    b_rows = _exact_dot01_r(grows, jnp.where(incl_t, 1.0, 0.0).astype(BF16))
    bt_cols = _exact_dot01(ones_ll, gcols)
    scale = ML_QK_DIM ** -0.5
    ones_v = jnp.ones((L, ML_V_DIM), BF16)

    for h in range(ML_HEADS):
        qs = slice(h * ML_QK_DIM, (h + 1) * ML_QK_DIM)
        vs = slice(h * ML_V_DIM, (h + 1) * ML_V_DIM)
        q = (q_ref[:, qs] * scale).astype(BF16)
        k = k_ref[:, qs]
        v_aug = jnp.concatenate([v_ref[:, vs].astype(BF16), ones_v], axis=-1)
        li_c = gcols[:, h:h + 1]
        li_r = grows[h:h + 1, :]
        b_c = b_cols[:, ML_HEADS + h:ML_HEADS + h + 1]
        b_r = b_rows[ML_HEADS + h:ML_HEADS + h + 1, :]
        b_t = bt_cols[:, ML_HEADS + h:ML_HEADS + h + 1]
        m_st = m_ref[h][0:1, 0:1]
        d_mat = jnp.where(incl, b_c - b_r + li_r, NEG_BIG)
        d_max = jnp.max(d_mat, axis=-1, keepdims=True)
        scores = _dot_nt(q, k.astype(BF16))
        inter_log = b_c + m_st
        m_t = jnp.maximum(inter_log, d_max)
        inter_w = jnp.exp(inter_log - m_t)
        intra_w = jnp.exp(d_mat - m_t) * scores
        c_aug = c_ref[h]
        numden = inter_w * _dot(q, c_aug.astype(BF16)) + _dot(intra_w.astype(BF16), v_aug)
        num = numden[:, :ML_V_DIM]
        den = numden[:, ML_V_DIM:]
        o_ref[:, vs] = num / jnp.maximum(jnp.abs(den), jnp.exp(-m_t))
        w_end = b_t - b_c + li_c
        decay_log = b_t[0:1, :] + m_st
        m_new = jnp.maximum(decay_log, jnp.max(w_end, axis=0, keepdims=True))
        carry_w = jnp.exp(decay_log - m_new)
        s_w = jnp.exp(w_end - m_new)
        c_ref[h] = carry_w * c_aug + _dot_tn((k * s_w).astype(BF16), v_aug)
        m_ref[h] = jnp.broadcast_to(m_new, (SUBLANES, LANES))


def _mlstm_scan(z, zg, *, batch, seq, q_col, k_col, v_col):
    n_chunks = seq // SCAN_CHUNK
    L = SCAN_CHUNK
    cols, rows = _gate_layouts(zg, batch=batch, seq=seq)
    cidx = functools.partial(_scan_chunk_index, batch=batch, n_chunks=n_chunks)
    qb, kb, vb = q_col // ML_QK_WIDTH, k_col // ML_QK_WIDTH, v_col // ML_V_WIDTH
    return pl.pallas_call(
        functools.partial(_mlstm_kernel, batch=batch),
        grid=(N_DIR * batch, n_chunks),
        in_specs=[
            pl.BlockSpec((L, ML_QK_WIDTH), lambda i, n: (cidx(i, n), qb)),
            pl.BlockSpec((L, ML_QK_WIDTH), lambda i, n: (cidx(i, n), kb)),
            pl.BlockSpec((L, ML_V_WIDTH), lambda i, n: (cidx(i, n), vb)),
            pl.BlockSpec((None, L, 8), lambda i, n: (i // batch, cidx(i, n), 0)),
            pl.BlockSpec((None, None, 8, L), lambda i, n: (i // batch, cidx(i, n), 0, 0)),
        ],
        out_specs=pl.BlockSpec((None, L, ML_V_WIDTH), lambda i, n: (i // batch, cidx(i, n), 0)),
        out_shape=jax.ShapeDtypeStruct((N_DIR, batch * seq, ML_V_WIDTH), F32),
        scratch_shapes=[
            pltpu.VMEM((ML_HEADS, ML_QK_DIM, 2 * ML_V_DIM), F32),
            pltpu.VMEM((ML_HEADS, SUBLANES, LANES), F32),
        ],
        compiler_params=_params("parallel", "arbitrary"),
        name="mlstm_scan",
    )(z, z, z, cols, rows)


def _head_rms_norm(x, g, n_heads, head_dim):
    parts = []
    for h in range(n_heads):
        xh = x[:, h * head_dim:(h + 1) * head_dim]
        ms = jnp.mean(xh * xh, axis=-1, keepdims=True)
        parts.append(xh * lax.rsqrt(ms + NORM_EPS) * g)
    return jnp.concatenate(parts, axis=-1)


def _layer_norm(x, g, b):
    mu = jnp.mean(x, axis=-1, keepdims=True)
    xc = x - mu
    var = jnp.mean(xc * xc, axis=-1, keepdims=True)
    return xc * lax.rsqrt(var + LN_EPS) * g + b


def _even_out_kernel(h_ref, c_ref, of_ref, ob_ref, gate_ref, lng_ref, lnb_ref, ng_ref, wa_ref, wb_ref, o_ref):
    ya = _layer_norm(c_ref[...], lng_ref[...], lnb_ref[...])
    ya = ya * jax.nn.sigmoid(ya)
    o = _head_rms_norm(of_ref[...] + ob_ref[...], ng_ref[...], DN_HEADS, DN_HEAD_DIM)
    gate = gate_ref[...]
    o = o * (gate * jax.nn.sigmoid(gate))
    o_ref[...] = h_ref[...] + _dot(ya.astype(BF16), wa_ref[...]) + _dot(o.astype(BF16), wb_ref[...])


def _even_out(h, conv, o_dirs, z, ln_g, ln_b, norm_g, w_out, *, gate_col):
    n, d = h.shape
    tm = min(ROW_TILE, n)
    gb = gate_col // DN_WIDTH
    row = lambda i: (i, 0)
    fixed = lambda i: (0, 0)
    return pl.pallas_call(
        _even_out_kernel,
        grid=(n // tm,),
        in_specs=[
            pl.BlockSpec((tm, d), row),
            pl.BlockSpec((tm, CONV_CH), row),
            pl.BlockSpec((None, tm, DN_WIDTH), lambda i: (0, i, 0)),
            pl.BlockSpec((None, tm, DN_WIDTH), lambda i: (1, i, 0)),
            pl.BlockSpec((tm, DN_WIDTH), lambda i: (i, gb)),
            _resident((1, CONV_CH)),
            _resident((1, CONV_CH)),
            _resident((1, DN_HEAD_DIM)),
            _resident((CONV_CH, d)),
            _resident((DN_WIDTH, d)),
        ],
        out_specs=pl.BlockSpec((tm, d), row),
        out_shape=jax.ShapeDtypeStruct((n, d), F32),
        compiler_params=_params("parallel"),
        name="even_out",
    )(h, conv, o_dirs, o_dirs, z, ln_g.reshape(1, -1), ln_b.reshape(1, -1), norm_g.reshape(1, -1),
      w_out[:CONV_CH].astype(BF16), w_out[CONV_CH:].astype(BF16))


def _odd_out_kernel(h_ref, u_ref, vp_ref, hf_ref, hb_ref, op_ref, lng_ref, lnb_ref, sgw_ref, sgb_ref, ng_ref,
                    wa_ref, wb_ref, o_ref):
    tm = h_ref.shape[0]
    u = jax.nn.gelu(u_ref[...])
    vv = _layer_norm(jax.nn.gelu(vp_ref[...]), lng_ref[...], lnb_ref[...]).astype(BF16)
    sgb = sgb_ref[...]
    rows = []
    for c in range(tm // SG_CHUNK):
        parts = []
        for g in range(SG_GROUPS):
            blk = vv[c * SG_CHUNK:(c + 1) * SG_CHUNK, g * SG_GROUP_DIM:(g + 1) * SG_GROUP_DIM]
            parts.append(_dot(sgw_ref[g], blk) + sgb[:, g:g + 1])
        rows.append(jnp.concatenate(parts, axis=-1))
    yc = u * jnp.concatenate(rows, axis=0)
    hd = _head_rms_norm(hf_ref[...] + hb_ref[...], ng_ref[...], ML_HEADS, ML_V_DIM)
    hd = hd * jax.nn.sigmoid(op_ref[...])
    o_ref[...] = h_ref[...] + _dot(yc.astype(BF16), wa_ref[...]) + _dot(hd.astype(BF16), wb_ref[...])


def _odd_out(h, z, h_dirs, ln_g, ln_b, sg_w, sg_b, norm_g, w_out, *, u_col, v_col, o_col):
    n, d = h.shape
    tm = min(ROW_TILE, n)
    row = lambda i: (i, 0)
    fixed = lambda i: (0, 0)
    return pl.pallas_call(
        _odd_out_kernel,
        grid=(n // tm,),
        in_specs=[
            pl.BlockSpec((tm, d), row),
            pl.BlockSpec((tm, SG_WIDTH), lambda i: (i, u_col // SG_WIDTH)),
            pl.BlockSpec((tm, SG_WIDTH), lambda i: (i, v_col // SG_WIDTH)),
            pl.BlockSpec((None, tm, ML_V_WIDTH), lambda i: (0, i, 0)),
            pl.BlockSpec((None, tm, ML_V_WIDTH), lambda i: (1, i, 0)),
            pl.BlockSpec((tm, ML_V_WIDTH), lambda i: (i, o_col // ML_V_WIDTH)),
            _resident((1, SG_WIDTH)),
            _resident((1, SG_WIDTH)),
            _resident((SG_GROUPS, SG_CHUNK, SG_CHUNK)),
            _resident((SG_CHUNK, SG_GROUPS)),
            _resident((1, ML_V_DIM)),
            _resident((SG_WIDTH, d)),
            _resident((ML_V_WIDTH, d)),
        ],
        out_specs=pl.BlockSpec((tm, d), row),
        out_shape=jax.ShapeDtypeStruct((n, d), F32),
        compiler_params=_params("parallel"),
        name="odd_out",
    )(h, z, z, h_dirs, h_dirs, z, ln_g.reshape(1, -1), ln_b.reshape(1, -1), sg_w.astype(BF16), sg_b.T,
      norm_g.reshape(1, -1), w_out[:SG_WIDTH].astype(BF16), w_out[SG_WIDTH:].astype(BF16))


def _ffn_kernel(h_ref, g_ref, w1_ref, w3_ref, w2_ref, fg_ref, o_ref, *, final_norm):
    x = h_ref[...]
    ms = jnp.mean(x * x, axis=-1, keepdims=True)
    hn = (x * lax.rsqrt(ms + NORM_EPS) * g_ref[...]).astype(BF16)
    a = _dot(hn, w1_ref[...])
    b = _dot(hn, w3_ref[...])
    y = x + _dot((a * jax.nn.sigmoid(a) * b).astype(BF16), w2_ref[...])
    if final_norm:
        ms = jnp.mean(y * y, axis=-1, keepdims=True)
        y = y * lax.rsqrt(ms + NORM_EPS) * fg_ref[...]
    o_ref[...] = y


def _ffn(h, g, w1, w3, w2, final_g, *, final_norm):
    n, d = h.shape
    f = w1.shape[1]
    tm = min(ROW_TILE, n)
    fixed = lambda i: (0, 0)
    return pl.pallas_call(
        functools.partial(_ffn_kernel, final_norm=final_norm),
        grid=(n // tm,),
        in_specs=[
            pl.BlockSpec((tm, d), lambda i: (i, 0)),
            _resident((1, d)),
            _resident((d, f)),
            _resident((d, f)),
            _resident((f, d)),
            _resident((1, d)),
        ],
        out_specs=pl.BlockSpec((tm, d), lambda i: (i, 0)),
        out_shape=jax.ShapeDtypeStruct((n, d), F32),
        compiler_params=_params("parallel"),
        name="ffn_final" if final_norm else "ffn",
    )(h, g.reshape(1, d), w1.astype(BF16), w3.astype(BF16), w2.astype(BF16), final_g.reshape(1, d))


def _gate_weight(w_gate_cols):
    d, c = w_gate_cols.shape
    return jnp.zeros((d, LANES), F32).at[:, :c].set(w_gate_cols).astype(BF16)


def _gate_param_rows(first, second):
    t = jnp.zeros((SUBLANES, LANES), F32)
    if first is not None:
        t = t.at[0, first[0]:first[0] + 8].set(first[1].reshape(-1))
    if second is not None:
        t = t.at[1, second[0]:second[0] + 8].set(second[1].reshape(-1))
    return t


def _even_layer(h, j, p, *, batch, seq):
    main = 2 * CONV_CH + 4 * DN_WIDTH
    w_in = p["ev_w_in"][j]
    gate_params = jnp.zeros((SUBLANES, LANES), F32)
    gate_params = gate_params.at[0, 8:16].set(p["ev_dn_a_log"][j].reshape(-1))
    gate_params = gate_params.at[1, 8:16].set(p["ev_dn_dt_bias"][j].reshape(-1))
    z, zg = _in_proj(h, p["mix_norm_g"][2 * j], w_in[:, :main].astype(BF16), _gate_weight(w_in[:, main:]),
                     gate_params, even=True)
    conv = _conv_glu(z, p["ev_conv_w"][j], p["ev_conv_b"][j], batch=batch, seq=seq)
    qkv = _conv_qkv(z, p["ev_dn_conv_w"][j], batch=batch, seq=seq, col0=2 * CONV_CH)
    o_dirs = _delta_scan(qkv, zg, batch=batch, seq=seq)
    return _even_out(h, conv, o_dirs, z, p["ev_conv_ln_g"][j], p["ev_conv_ln_b"][j], p["ev_dn_norm_g"][j],
                     p["ev_w_out"][j], gate_col=2 * CONV_CH + 3 * DN_WIDTH)


def _odd_layer(h, j, p, *, batch, seq):
    main = 2 * SG_WIDTH + 2 * ML_QK_WIDTH + 2 * ML_V_WIDTH
    w_in = p["od_w_in"][j]
    gate_params = jnp.zeros((SUBLANES, LANES), F32)
    gate_params = gate_params.at[0, 0:8].set(p["od_ml_i_bias"][j].reshape(-1))
    gate_params = gate_params.at[1, 8:16].set(p["od_ml_f_bias"][j].reshape(-1))
    z, zg = _in_proj(h, p["mix_norm_g"][2 * j + 1], w_in[:, :main].astype(BF16), _gate_weight(w_in[:, main:]),
                     gate_params, even=False)
    q_col = 2 * SG_WIDTH
    k_col = q_col + ML_QK_WIDTH
    v_col = k_col + ML_QK_WIDTH
    o_col = v_col + ML_V_WIDTH
    h_dirs = _mlstm_scan(z, zg, batch=batch, seq=seq, q_col=q_col, k_col=k_col, v_col=v_col)
    return _odd_out(h, z, h_dirs, p["od_sg_ln_g"][j], p["od_sg_ln_b"][j], p["od_sg_w"][j], p["od_sg_b"][j],
                    p["od_ml_norm_g"][j], p["od_w_out"][j], u_col=0, v_col=SG_WIDTH, o_col=o_col)


def kernel(x, mix_norm_g, ev_w_in, ev_conv_w, ev_conv_b, ev_conv_ln_g, ev_conv_ln_b, ev_dn_conv_w, ev_dn_a_log, ev_dn_dt_bias, ev_dn_norm_g, ev_w_out, od_w_in, od_sg_ln_g, od_sg_ln_b, od_sg_w, od_sg_b, od_ml_i_bias, od_ml_f_bias, od_ml_norm_g, od_w_out, ffn_norm_g, ffn_w1, ffn_w3, ffn_w2, final_norm_g):
    p = dict(mix_norm_g=mix_norm_g, ev_w_in=ev_w_in, ev_conv_w=ev_conv_w, ev_conv_b=ev_conv_b,
             ev_conv_ln_g=ev_conv_ln_g, ev_conv_ln_b=ev_conv_ln_b, ev_dn_conv_w=ev_dn_conv_w,
             ev_dn_a_log=ev_dn_a_log, ev_dn_dt_bias=ev_dn_dt_bias, ev_dn_norm_g=ev_dn_norm_g, ev_w_out=ev_w_out,
             od_w_in=od_w_in, od_sg_ln_g=od_sg_ln_g, od_sg_ln_b=od_sg_ln_b, od_sg_w=od_sg_w, od_sg_b=od_sg_b,
             od_ml_i_bias=od_ml_i_bias, od_ml_f_bias=od_ml_f_bias, od_ml_norm_g=od_ml_norm_g, od_w_out=od_w_out)
    batch, seq, d = x.shape
    depth = mix_norm_g.shape[0]
    h = x.reshape(batch * seq, d)
    for layer in range(depth):
        j = layer // 2
        if layer % 2 == 0:
            h = _even_layer(h, j, p, batch=batch, seq=seq)
        else:
            h = _odd_layer(h, j, p, batch=batch, seq=seq)
        h = _ffn(h, ffn_norm_g[layer], ffn_w1[layer], ffn_w3[layer], ffn_w2[layer], final_norm_g,
                 final_norm=layer == depth - 1)
    return h.reshape(batch, seq, d)
```

```python
import functools

import jax
import jax.numpy as jnp
from jax import lax
from jax.experimental import pallas as pl
from jax.experimental.pallas import tpu as pltpu

NORM_EPS = 1e-6
LN_EPS = 1e-5
N_DIR = 2

CONV_CH = 512
DN_HEADS = 4
DN_HEAD_DIM = 128
DN_WIDTH = DN_HEADS * DN_HEAD_DIM
SG_GROUPS = 4
SG_GROUP_DIM = 128
SG_WIDTH = SG_GROUPS * SG_GROUP_DIM
SG_CHUNK = 128
ML_HEADS = 4
ML_QK_DIM = 64
ML_V_DIM = 128
ML_QK_WIDTH = ML_HEADS * ML_QK_DIM
ML_V_WIDTH = ML_HEADS * ML_V_DIM

LANES = 128
SUBLANES = 8
VMEM_LIMIT_BYTES = 56 * 1024 * 1024

SCAN_CHUNK = 64
SCAN_STEP_CHUNKS = 2
PREP_GROUP = 128
PREP_ROWS = 512
ROW_TILE = 512
CONV_TILE = 128
CONV_PAD = 16
NEG_BIG = -1e30

BF16 = jnp.bfloat16
F32 = jnp.float32


def _params(*sem):
    return pltpu.CompilerParams(dimension_semantics=sem, vmem_limit_bytes=VMEM_LIMIT_BYTES)


def _resident(shape):
    return pl.BlockSpec(shape, lambda *_: (0,) * len(shape), pipeline_mode=pl.Buffered(1))


def _dot(a, b):
    return jnp.dot(a, b, preferred_element_type=F32)


def _dot_nt(a, b):
    return lax.dot_general(a, b, (((1,), (1,)), ((), ())), preferred_element_type=F32)


def _dot_tn(a, b):
    return lax.dot_general(a, b, (((0,), (0,)), ((), ())), preferred_element_type=F32)


def _exact_dot01(t01, x):
    x1 = x.astype(BF16)
    r1 = x - x1.astype(F32)
    x2 = r1.astype(BF16)
    x3 = (r1 - x2.astype(F32)).astype(BF16)
    return _dot(t01, x1) + _dot(t01, x2) + _dot(t01, x3)


def _exact_dot01_r(x, t01):
    x1 = x.astype(BF16)
    r1 = x - x1.astype(F32)
    x2 = r1.astype(BF16)
    x3 = (r1 - x2.astype(F32)).astype(BF16)
    return _dot(x1, t01) + _dot(x2, t01) + _dot(x3, t01)


def _in_proj_kernel(h_ref, g_ref, w_ref, wg_ref, gp_ref, zm_ref, zg_ref, *, even):
    x = h_ref[...]
    ms = jnp.mean(x * x, axis=-1, keepdims=True)
    hn = (x * lax.rsqrt(ms + NORM_EPS) * g_ref[...]).astype(BF16)
    zm_ref[...] = _dot(hn, w_ref[...])
    zg = _dot(hn, wg_ref[...])
    p0 = gp_ref[0:1, :]
    p1 = gp_ref[1:2, :]
    lane = lax.broadcasted_iota(jnp.int32, zg.shape, 1)
    if even:
        first = jax.nn.sigmoid(zg)
        second = -jnp.exp(p0) * jax.nn.softplus(zg + p1)
    else:
        first = zg + p0
        second = jax.nn.log_sigmoid(zg + p1)
    zg_ref[...] = jnp.where(lane < 8, first, second)


def _in_proj(h, g, w_main, w_gate, gate_params, *, even):
    n, d = h.shape
    c = w_main.shape[1]
    tm = min(ROW_TILE, n)
    return pl.pallas_call(
        functools.partial(_in_proj_kernel, even=even),
        grid=(n // tm,),
        in_specs=[
            pl.BlockSpec((tm, d), lambda i: (i, 0)),
            _resident((1, d)),
            _resident((d, c)),
            _resident((d, LANES)),
            _resident((SUBLANES, LANES)),
        ],
        out_specs=[
            pl.BlockSpec((tm, c), lambda i: (i, 0)),
            pl.BlockSpec((tm, LANES), lambda i: (i, 0)),
        ],
        out_shape=[jax.ShapeDtypeStruct((n, c), F32), jax.ShapeDtypeStruct((n, LANES), F32)],
        compiler_params=_params("parallel"),
        name="in_proj_even" if even else "in_proj_odd",
    )(h, g.reshape(1, d), w_main, w_gate, gate_params)


def _conv_taps(pad_ref, w_ref, o_ref, bias, *, seq, width, post):
    half = width // 2

    def body(i, carry):
        t0 = pl.multiple_of(i * CONV_TILE, CONV_TILE)
        acc = jnp.zeros((CONV_TILE, LANES), F32)
        for j in range(width):
            acc = acc + pad_ref[pl.ds(t0 + (CONV_PAD - half + j), CONV_TILE), :] * w_ref[j:j + 1, :]
        if bias is not None:
            acc = acc + bias
        o_ref[pl.ds(t0, CONV_TILE), :] = post(acc)
        return carry

    lax.fori_loop(0, seq // CONV_TILE, body, 0)


def _fill_padded(pad_ref, x, seq):
    zeros = jnp.zeros((CONV_PAD, LANES), F32)
    pad_ref[0:CONV_PAD, :] = zeros
    pad_ref[CONV_PAD + seq:CONV_PAD + seq + CONV_PAD, :] = zeros
    pad_ref[CONV_PAD:CONV_PAD + seq, :] = x


def _conv_glu_kernel(av_ref, ag_ref, w_ref, b_ref, o_ref, pad_ref, *, seq, width):
    _fill_padded(pad_ref, av_ref[...] * jax.nn.sigmoid(ag_ref[...]), seq)
    _conv_taps(pad_ref, w_ref, o_ref, b_ref[...], seq=seq, width=width, post=lambda a: a)


def _conv_glu(z, conv_w, conv_b, *, batch, seq):
    width = conv_w.shape[0]
    nblk = CONV_CH // LANES
    return pl.pallas_call(
        functools.partial(_conv_glu_kernel, seq=seq, width=width),
        grid=(batch, nblk),
        in_specs=[
            pl.BlockSpec((seq, LANES), lambda b, c: (b, c)),
            pl.BlockSpec((seq, LANES), lambda b, c: (b, nblk + c)),
            pl.BlockSpec((width, LANES), lambda b, c: (0, c)),
            pl.BlockSpec((1, LANES), lambda b, c: (0, c)),
        ],
        out_specs=pl.BlockSpec((seq, LANES), lambda b, c: (b, c)),
        out_shape=jax.ShapeDtypeStruct((batch * seq, CONV_CH), F32),
        scratch_shapes=[pltpu.VMEM((seq + 2 * CONV_PAD, LANES), F32)],
        compiler_params=_params("parallel", "parallel"),
        name="conv_glu",
    )(z, z, conv_w, conv_b.reshape(1, CONV_CH))


def _conv_qkv_kernel(x_ref, w_ref, o_ref, pad_ref, *, seq, width, n_norm_blocks):
    _fill_padded(pad_ref, x_ref[...], seq)
    normalise = pl.program_id(1) < n_norm_blocks

    def post(a):
        y = a * jax.nn.sigmoid(a)
        yn = y * lax.rsqrt(jnp.sum(y * y, axis=-1, keepdims=True) + NORM_EPS)
        return jnp.where(normalise, yn, y)

    _conv_taps(pad_ref, w_ref, o_ref, None, seq=seq, width=width, post=post)


def _conv_qkv(z, dn_conv_w, *, batch, seq, col0):
    width = dn_conv_w.shape[0]
    nblk = 3 * DN_WIDTH // LANES
    blk0 = col0 // LANES
    return pl.pallas_call(
        functools.partial(_conv_qkv_kernel, seq=seq, width=width, n_norm_blocks=2 * DN_HEADS),
        grid=(batch, nblk),
        in_specs=[
            pl.BlockSpec((seq, LANES), lambda b, c: (b, blk0 + c)),
            pl.BlockSpec((width, LANES), lambda b, c: (0, c)),
        ],
        out_specs=pl.BlockSpec((seq, LANES), lambda b, c: (b, c)),
        out_shape=jax.ShapeDtypeStruct((batch * seq, 3 * DN_WIDTH), F32),
        scratch_shapes=[pltpu.VMEM((seq + 2 * CONV_PAD, LANES), F32)],
        compiler_params=_params("parallel", "parallel"),
        name="conv_qkv",
    )(z, dn_conv_w)


def _tri_masks(n, backward):
    row = lax.broadcasted_iota(jnp.int32, (n, n), 0)
    col = lax.broadcasted_iota(jnp.int32, (n, n), 1)
    d = jnp.where(backward, col - row, row - col)
    return row, col, d >= 0, d > 0, d <= 0


def _unit_tri_inverse(a, row, col, n=None):
    return _unit_tri_inverses([a], row, col, a.shape[0] if n is None else n)[0]


def _bf16_all(mats):
    return [m.astype(BF16) for m in mats]


def _unit_tri_inverses(mats, row, col, n):
    eye = (row == col).astype(F32)
    same = (row >> 3) == (col >> 3)
    l8 = [jnp.where(same, a, 0.0) for a in mats]
    l8h = _bf16_all(l8)
    l2h = _bf16_all([_dot(p, p) for p in l8h])
    l4h = _bf16_all([_dot(p, p) for p in l2h])
    x = [eye - p for p in l8]
    x = [xi + _dot(xi.astype(BF16), p) for xi, p in zip(x, l2h)]
    x = [xi + _dot(xi.astype(BF16), p) for xi, p in zip(x, l4h)]
    shift = 3
    while (1 << shift) < n:
        same_next = (row >> (shift + 1)) == (col >> (shift + 1))
        off = jnp.logical_and(same_next, jnp.logical_not(same))
        ch = _bf16_all([jnp.where(off, a, 0.0) for a in mats])
        xh = _bf16_all(x)
        xch = _bf16_all([_dot(p, c) for p, c in zip(xh, ch)])
        x = [xi - _dot(p, q) for xi, p, q in zip(x, xch, xh)]
        same = same_next
        shift += 1
    return x


def _group_masks(backward):
    n = PREP_GROUP
    row = lax.broadcasted_iota(jnp.int32, (n, n), 0)
    col = lax.broadcasted_iota(jnp.int32, (n, n), 1)
    shift = SCAN_CHUNK.bit_length() - 1
    same = (row >> shift) == (col >> shift)
    d = (col - row) if backward else (row - col)
    land = jnp.logical_and
    return land(same, d >= 0), land(same, d > 0), land(same, d <= 0)


def _as01(mask):
    return jnp.where(mask, 1.0, 0.0).astype(BF16)


def _delta_prep_kernel(q_ref, k_ref, v_ref, gc_ref, gr_ref, u_ref, w_ref, qg_ref, kg_ref, at_ref):
    G, L = PREP_GROUP, SCAN_CHUNK
    row = lax.broadcasted_iota(jnp.int32, (G, G), 0)
    col = lax.broadcasted_iota(jnp.int32, (G, G), 1)
    shift = L.bit_length() - 1
    same01 = _as01((row >> shift) == (col >> shift))
    masks = [_group_masks(False), _group_masks(True)]
    cum_c = [_as01(m[0]) for m in masks]
    cum_r = [_as01(m[2]) for m in masks]
    fold = _as01((lax.broadcasted_iota(jnp.int32, (G, L), 0) & (L - 1)) == lax.broadcasted_iota(jnp.int32, (G, L), 1))
    pad = jnp.zeros((G, DN_HEAD_DIM - L), BF16)
    scale = DN_HEAD_DIM ** -0.5

    def group(i, carry):
        rows = pl.ds(pl.multiple_of(i * G, G), G)
        gc = gc_ref[rows, :]
        gr = gr_ref[i]
        g_tot = _exact_dot01(same01, gc)
        g_cum_c = [_exact_dot01(cum_c[d], gc) for d in range(N_DIR)]
        g_cum_r = [_exact_dot01_r(gr, cum_r[d]) for d in range(N_DIR)]
        heads = range(DN_HEADS)
        lanes = [slice(h * DN_HEAD_DIM, (h + 1) * DN_HEAD_DIM) for h in heads]
        chains = [(h, d) for h in heads for d in range(N_DIR)]
        q = [q_ref[rows, sl] for sl in lanes]
        k = [k_ref[rows, sl] for sl in lanes]
        k16 = _bf16_all(k)
        kk = [_dot_nt(p, p) for p in k16]
        qk = [_dot_nt((q[h] * scale).astype(BF16), k16[h]) for h in heads]
        beta, g_c, g_t, decay = {}, {}, {}, {}
        for h, d in chains:
            cb = d * DN_HEADS + h
            cl = N_DIR * DN_HEADS + cb
            beta[h, d] = gc[:, cb:cb + 1]
            g_c[h, d] = g_cum_c[d][:, cl:cl + 1]
            g_t[h, d] = g_tot[:, cl:cl + 1]
            g_r = g_cum_r[d][cl:cl + 1, :]
            decay[h, d] = jnp.exp(jnp.where(masks[d][0], g_c[h, d] - g_r, NEG_BIG))
        a = [jnp.where(masks[d][1], kk[h] * decay[h, d], 0.0) * beta[h, d] for h, d in chains]
        x = _unit_tri_inverses(a, row, col, L)
        eg = {c: jnp.exp(g_c[c]) for c in chains}
        rhs = [jnp.concatenate([v_ref[rows, lanes[h]] * beta[h, d], k[h] * (beta[h, d] * eg[h, d])],
                               axis=-1).astype(BF16) for h, d in chains]
        uw = [_dot(xi.astype(BF16), r) for xi, r in zip(x, rhs)]
        attn = [jnp.where(masks[d][0], qk[h] * decay[h, d], 0.0).astype(BF16) for h, d in chains]
        at = [_dot(p, fold) for p in attn]
        for (h, d), uwi, ati in zip(chains, uw, at):
            sl = lanes[h]
            u_ref[d, rows, sl] = uwi[:, :DN_HEAD_DIM].astype(BF16)
            w_ref[d, rows, sl] = uwi[:, DN_HEAD_DIM:].astype(BF16)
            qg_ref[d, rows, sl] = (q[h] * (scale * eg[h, d])).astype(BF16)
            kg_ref[d, rows, sl] = (k[h] * jnp.exp(g_t[h, d] - g_c[h, d])).astype(BF16)
            at_ref[d, rows, sl] = jnp.concatenate([ati.astype(BF16), pad], axis=-1)
        return carry

    lax.fori_loop(0, q_ref.shape[0] // G, group, 0)


def _delta_prep(qkv, zg, *, batch, seq):
    n = batch * seq
    R, G = min(PREP_ROWS, seq), PREP_GROUP
    steps = seq // R
    gr = zg[:, :16].reshape(n // G, G, 16).transpose(0, 2, 1)
    rowblk = lambda c: (lambda b, r: (b * steps + r, c))
    out = jax.ShapeDtypeStruct((N_DIR, n, DN_WIDTH), BF16)
    out_spec = pl.BlockSpec((N_DIR, R, DN_WIDTH), lambda b, r: (0, b * steps + r, 0))
    return pl.pallas_call(
        _delta_prep_kernel,
        grid=(batch, steps),
        in_specs=[
            pl.BlockSpec((R, DN_WIDTH), rowblk(0)),
            pl.BlockSpec((R, DN_WIDTH), rowblk(1)),
            pl.BlockSpec((R, DN_WIDTH), rowblk(2)),
            pl.BlockSpec((R, LANES), rowblk(0)),
            pl.BlockSpec((R // G, 16, G), lambda b, r: (b * steps + r, 0, 0)),
        ],
        out_specs=[out_spec] * 5,
        out_shape=[out] * 5,
        compiler_params=_params("parallel", "parallel"),
        name="delta_prep",
    )(qkv, qkv, qkv, zg, gr)


def _delta_scan_kernel(uf, wf, qgf, kgf, atf, zgf, ub, wb, qgb, kgb, atb, zgb, of_ref, ob_ref, s_ref, *, batch):
    L, C = SCAN_CHUNK, SCAN_STEP_CHUNKS

    @pl.when(pl.program_id(0) == 0)
    def _():
        s_ref[...] = jnp.zeros_like(s_ref)

    dirs = ((uf, wf, qgf, kgf, atf, zgf, of_ref), (ub, wb, qgb, kgb, atb, zgb, ob_ref))

    chains = [(d, b, h) for d in range(N_DIR) for b in range(batch) for h in range(DN_HEADS)]
    state = {c: s_ref[c] for c in chains}
    for step in range(C):
        chunk = {0: step, 1: C - 1 - step}
        rows = {d: slice(chunk[d] * L, (chunk[d] + 1) * L) for d in range(N_DIR)}
        egt = {(d, b): jnp.exp(jnp.sum(dirs[d][5][b, rows[d], :], axis=0, keepdims=True))
               for d in range(N_DIR) for b in range(batch)}
        ws, v_new, av = {}, {}, {}
        for c in chains:
            d, b, h = c
            sl = slice(h * DN_HEAD_DIM, (h + 1) * DN_HEAD_DIM)
            wq = jnp.concatenate([dirs[d][1][b, rows[d], sl], dirs[d][2][b, rows[d], sl]], axis=0)
            ws[c] = _dot(wq, state[c].astype(BF16))
        for c in chains:
            d, b, h = c
            sl = slice(h * DN_HEAD_DIM, (h + 1) * DN_HEAD_DIM)
            v_new[c] = (dirs[d][0][b, rows[d], sl].astype(F32) - ws[c][:L]).astype(BF16)
        for c in chains:
            d, b, h = c
            at = dirs[d][4][b, rows[d], h * DN_HEAD_DIM:h * DN_HEAD_DIM + L]
            av[c] = _dot(at, v_new[c])
        for c in chains:
            d, b, h = c
            sl = slice(h * DN_HEAD_DIM, (h + 1) * DN_HEAD_DIM)
            cl = (N_DIR + d) * DN_HEADS + h
            dirs[d][6][b, rows[d], sl] = ws[c][L:] + av[c]
            state[c] = state[c] * egt[d, b][:, cl:cl + 1] + _dot_tn(dirs[d][3][b, rows[d], sl], v_new[c])
    for c in chains:
        s_ref[c] = state[c]


def _scan_chunk_index(i, n, *, batch, n_chunks):
    b = i % batch
    backward = i // batch
    return b * n_chunks + n + backward * (n_chunks - 1 - 2 * n)


def _gate_layouts(zg, *, batch, seq):
    n_chunks = seq // SCAN_CHUNK
    g = zg[:, :16].reshape(batch * seq, 2, N_DIR, 4)
    cols = g.transpose(2, 0, 1, 3).reshape(N_DIR, batch * seq, 8)
    rows = cols.reshape(N_DIR, batch * n_chunks, SCAN_CHUNK, 8).transpose(0, 1, 3, 2)
    return cols, rows


def _delta_scan(qkv, zg, *, batch, seq):
    n = batch * seq
    T = SCAN_CHUNK * SCAN_STEP_CHUNKS
    steps = seq // T
    prepped = [t.reshape(N_DIR, batch, seq, DN_WIDTH) for t in _delta_prep(qkv, zg, batch=batch, seq=seq)]
    zg3 = zg.reshape(batch, seq, LANES)
    fwd = lambda s: s
    bwd = lambda s: steps - 1 - s
    specs = []
    for d, pos in ((0, fwd), (1, bwd)):
        specs += [pl.BlockSpec((None, batch, T, DN_WIDTH), lambda s, d=d, pos=pos: (d, 0, pos(s), 0))] * 5
        specs += [pl.BlockSpec((batch, T, LANES), lambda s, pos=pos: (0, pos(s), 0))]
    out = jax.ShapeDtypeStruct((batch, seq, DN_WIDTH), F32)
    o_f, o_b = pl.pallas_call(
        functools.partial(_delta_scan_kernel, batch=batch),
        grid=(steps,),
        in_specs=specs,
        out_specs=[pl.BlockSpec((batch, T, DN_WIDTH), lambda s: (0, fwd(s), 0)),
                   pl.BlockSpec((batch, T, DN_WIDTH), lambda s: (0, bwd(s), 0))],
        out_shape=[out, out],
        scratch_shapes=[pltpu.VMEM((N_DIR, batch, DN_HEADS, DN_HEAD_DIM, DN_HEAD_DIM), F32)],
        compiler_params=_params("arbitrary"),
        name="delta_scan",
    )(*prepped, zg3, *prepped, zg3)
    return o_f.reshape(n, DN_WIDTH), o_b.reshape(n, DN_WIDTH)


def _mlstm_kernel(q_ref, k_ref, v_ref, gc_ref, gr_ref, o_ref, c_ref, m_ref, *, batch):
    L = SCAN_CHUNK
    backward = pl.program_id(0) >= batch

    @pl.when(pl.program_id(1) == 0)
    def _():
        c_ref[...] = jnp.zeros_like(c_ref)
        m_ref[...] = jnp.zeros_like(m_ref)

    _, _, incl, _, incl_t = _tri_masks(L, backward)
    ones_ll = jnp.ones((L, L), BF16)
    gcols = gc_ref[...]
    grows = gr_ref[...]
    b_cols = _exact_dot01(jnp.where(incl, 1.0, 0.0).astype(BF16), gcols)
    b_rows = _exact_dot01_r(grows, jnp.where(incl_t, 1.0, 0.0).astype(BF16))
    bt_cols = _exact_dot01(ones_ll, gcols)
    scale = ML_QK_DIM ** -0.5
    ones_v = jnp.ones((L, ML_V_DIM), BF16)

    for h in range(ML_HEADS):
        qs = slice(h * ML_QK_DIM, (h + 1) * ML_QK_DIM)
        vs = slice(h * ML_V_DIM, (h + 1) * ML_V_DIM)
        q = (q_ref[:, qs] * scale).astype(BF16)
        k = k_ref[:, qs]
        v_aug = jnp.concatenate([v_ref[:, vs].astype(BF16), ones_v], axis=-1)
        li_c = gcols[:, h:h + 1]
        li_r = grows[h:h + 1, :]
        b_c = b_cols[:, ML_HEADS + h:ML_HEADS + h + 1]
        b_r = b_rows[ML_HEADS + h:ML_HEADS + h + 1, :]
        b_t = bt_cols[:, ML_HEADS + h:ML_HEADS + h + 1]
        m_st = m_ref[h][0:1, 0:1]
        d_mat = jnp.where(incl, b_c - b_r + li_r, NEG_BIG)
        d_max = jnp.max(d_mat, axis=-1, keepdims=True)
        scores = _dot_nt(q, k.astype(BF16))
        inter_log = b_c + m_st
        m_t = jnp.maximum(inter_log, d_max)
        inter_w = jnp.exp(inter_log - m_t)
        intra_w = jnp.exp(d_mat - m_t) * scores
        c_aug = c_ref[h]
        numden = inter_w * _dot(q, c_aug.astype(BF16)) + _dot(intra_w.astype(BF16), v_aug)
        num = numden[:, :ML_V_DIM]
        den = numden[:, ML_V_DIM:]
        o_ref[:, vs] = num / jnp.maximum(jnp.abs(den), jnp.exp(-m_t))
        w_end = b_t - b_c + li_c
        decay_log = b_t[0:1, :] + m_st
        m_new = jnp.maximum(decay_log, jnp.max(w_end, axis=0, keepdims=True))
        carry_w = jnp.exp(decay_log - m_new)
        s_w = jnp.exp(w_end - m_new)
        c_ref[h] = carry_w * c_aug + _dot_tn((k * s_w).astype(BF16), v_aug)
        m_ref[h] = jnp.broadcast_to(m_new, (SUBLANES, LANES))


def _mlstm_scan(z, zg, *, batch, seq, q_col, k_col, v_col):
    n_chunks = seq // SCAN_CHUNK
    L = SCAN_CHUNK
    cols, rows = _gate_layouts(zg, batch=batch, seq=seq)
    cidx = functools.partial(_scan_chunk_index, batch=batch, n_chunks=n_chunks)
    qb, kb, vb = q_col // ML_QK_WIDTH, k_col // ML_QK_WIDTH, v_col // ML_V_WIDTH
    return pl.pallas_call(
        functools.partial(_mlstm_kernel, batch=batch),
        grid=(N_DIR * batch, n_chunks),
        in_specs=[
            pl.BlockSpec((L, ML_QK_WIDTH), lambda i, n: (cidx(i, n), qb)),
            pl.BlockSpec((L, ML_QK_WIDTH), lambda i, n: (cidx(i, n), kb)),
            pl.BlockSpec((L, ML_V_WIDTH), lambda i, n: (cidx(i, n), vb)),
            pl.BlockSpec((None, L, 8), lambda i, n: (i // batch, cidx(i, n), 0)),
            pl.BlockSpec((None, None, 8, L), lambda i, n: (i // batch, cidx(i, n), 0, 0)),
        ],
        out_specs=pl.BlockSpec((None, L, ML_V_WIDTH), lambda i, n: (i // batch, cidx(i, n), 0)),
        out_shape=jax.ShapeDtypeStruct((N_DIR, batch * seq, ML_V_WIDTH), F32),
        scratch_shapes=[
            pltpu.VMEM((ML_HEADS, ML_QK_DIM, 2 * ML_V_DIM), F32),
            pltpu.VMEM((ML_HEADS, SUBLANES, LANES), F32),
        ],
        compiler_params=_params("parallel", "arbitrary"),
        name="mlstm_scan",
    )(z, z, z, cols, rows)


def _head_rms_norm(x, g, n_heads, head_dim):
    parts = []
    for h in range(n_heads):
        xh = x[:, h * head_dim:(h + 1) * head_dim]
        ms = jnp.mean(xh * xh, axis=-1, keepdims=True)
        parts.append(xh * lax.rsqrt(ms + NORM_EPS) * g)
    return jnp.concatenate(parts, axis=-1)


def _layer_norm(x, g, b):
    mu = jnp.mean(x, axis=-1, keepdims=True)
    xc = x - mu
    var = jnp.mean(xc * xc, axis=-1, keepdims=True)
    return xc * lax.rsqrt(var + LN_EPS) * g + b


def _even_out_kernel(h_ref, c_ref, of_ref, ob_ref, gate_ref, lng_ref, lnb_ref, ng_ref, wa_ref, wb_ref, o_ref):
    ya = _layer_norm(c_ref[...], lng_ref[...], lnb_ref[...])
    ya = ya * jax.nn.sigmoid(ya)
    o = _head_rms_norm(of_ref[...] + ob_ref[...], ng_ref[...], DN_HEADS, DN_HEAD_DIM)
    gate = gate_ref[...]
    o = o * (gate * jax.nn.sigmoid(gate))
    o_ref[...] = h_ref[...] + _dot(ya.astype(BF16), wa_ref[...]) + _dot(o.astype(BF16), wb_ref[...])


def _even_out(h, conv, o_fwd, o_bwd, z, ln_g, ln_b, norm_g, w_out, *, gate_col):
    n, d = h.shape
    tm = min(ROW_TILE, n)
    gb = gate_col // DN_WIDTH
    row = lambda i: (i, 0)
    return pl.pallas_call(
        _even_out_kernel,
        grid=(n // tm,),
        in_specs=[
            pl.BlockSpec((tm, d), row),
            pl.BlockSpec((tm, CONV_CH), row),
            pl.BlockSpec((tm, DN_WIDTH), row),
            pl.BlockSpec((tm, DN_WIDTH), row),
            pl.BlockSpec((tm, DN_WIDTH), lambda i: (i, gb)),
            _resident((1, CONV_CH)),
            _resident((1, CONV_CH)),
            _resident((1, DN_HEAD_DIM)),
            _resident((CONV_CH, d)),
            _resident((DN_WIDTH, d)),
        ],
        out_specs=pl.BlockSpec((tm, d), row),
        out_shape=jax.ShapeDtypeStruct((n, d), F32),
        compiler_params=_params("parallel"),
        name="even_out",
    )(h, conv, o_fwd, o_bwd, z, ln_g.reshape(1, -1), ln_b.reshape(1, -1), norm_g.reshape(1, -1),
      w_out[:CONV_CH].astype(BF16), w_out[CONV_CH:].astype(BF16))


def _odd_out_kernel(h_ref, u_ref, vp_ref, hf_ref, hb_ref, op_ref, lng_ref, lnb_ref, sgw_ref, sgb_ref, ng_ref,
                    wa_ref, wb_ref, o_ref):
    tm = h_ref.shape[0]
    u = jax.nn.gelu(u_ref[...])
    vv = _layer_norm(jax.nn.gelu(vp_ref[...]), lng_ref[...], lnb_ref[...]).astype(BF16)
    sgb = sgb_ref[...]
    rows = []
    for c in range(tm // SG_CHUNK):
        parts = []
        for g in range(SG_GROUPS):
            blk = vv[c * SG_CHUNK:(c + 1) * SG_CHUNK, g * SG_GROUP_DIM:(g + 1) * SG_GROUP_DIM]
            parts.append(_dot(sgw_ref[g], blk) + sgb[:, g:g + 1])
        rows.append(jnp.concatenate(parts, axis=-1))
    yc = u * jnp.concatenate(rows, axis=0)
    hd = _head_rms_norm(hf_ref[...] + hb_ref[...], ng_ref[...], ML_HEADS, ML_V_DIM)
    hd = hd * jax.nn.sigmoid(op_ref[...])
    o_ref[...] = h_ref[...] + _dot(yc.astype(BF16), wa_ref[...]) + _dot(hd.astype(BF16), wb_ref[...])


def _odd_out(h, z, h_dirs, ln_g, ln_b, sg_w, sg_b, norm_g, w_out, *, u_col, v_col, o_col):
    n, d = h.shape
    tm = min(ROW_TILE, n)
    row = lambda i: (i, 0)
    fixed = lambda i: (0, 0)
    return pl.pallas_call(
        _odd_out_kernel,
        grid=(n // tm,),
        in_specs=[
            pl.BlockSpec((tm, d), row),
            pl.BlockSpec((tm, SG_WIDTH), lambda i: (i, u_col // SG_WIDTH)),
            pl.BlockSpec((tm, SG_WIDTH), lambda i: (i, v_col // SG_WIDTH)),
            pl.BlockSpec((None, tm, ML_V_WIDTH), lambda i: (0, i, 0)),
            pl.BlockSpec((None, tm, ML_V_WIDTH), lambda i: (1, i, 0)),
            pl.BlockSpec((tm, ML_V_WIDTH), lambda i: (i, o_col // ML_V_WIDTH)),
            _resident((1, SG_WIDTH)),
            _resident((1, SG_WIDTH)),
            _resident((SG_GROUPS, SG_CHUNK, SG_CHUNK)),
            _resident((SG_CHUNK, SG_GROUPS)),
            _resident((1, ML_V_DIM)),
            _resident((SG_WIDTH, d)),
            _resident((ML_V_WIDTH, d)),
        ],
        out_specs=pl.BlockSpec((tm, d), row),
        out_shape=jax.ShapeDtypeStruct((n, d), F32),
        compiler_params=_params("parallel"),
        name="odd_out",
    )(h, z, z, h_dirs, h_dirs, z, ln_g.reshape(1, -1), ln_b.reshape(1, -1), sg_w.astype(BF16), sg_b.T,
      norm_g.reshape(1, -1), w_out[:SG_WIDTH].astype(BF16), w_out[SG_WIDTH:].astype(BF16))


def _ffn_kernel(h_ref, g_ref, w1_ref, w3_ref, w2_ref, fg_ref, o_ref, *, final_norm):
    x = h_ref[...]
    ms = jnp.mean(x * x, axis=-1, keepdims=True)
    hn = (x * lax.rsqrt(ms + NORM_EPS) * g_ref[...]).astype(BF16)
    a = _dot(hn, w1_ref[...])
    b = _dot(hn, w3_ref[...])
    y = x + _dot((a * jax.nn.sigmoid(a) * b).astype(BF16), w2_ref[...])
    if final_norm:
        ms = jnp.mean(y * y, axis=-1, keepdims=True)
        y = y * lax.rsqrt(ms + NORM_EPS) * fg_ref[...]
    o_ref[...] = y


def _ffn(h, g, w1, w3, w2, final_g, *, final_norm):
    n, d = h.shape
    f = w1.shape[1]
    tm = min(ROW_TILE, n)
    fixed = lambda i: (0, 0)
    return pl.pallas_call(
        functools.partial(_ffn_kernel, final_norm=final_norm),
        grid=(n // tm,),
        in_specs=[
            pl.BlockSpec((tm, d), lambda i: (i, 0)),
            _resident((1, d)),
            _resident((d, f)),
            _resident((d, f)),
            _resident((f, d)),
            _resident((1, d)),
        ],
        out_specs=pl.BlockSpec((tm, d), lambda i: (i, 0)),
        out_shape=jax.ShapeDtypeStruct((n, d), F32),
        compiler_params=_params("parallel"),
        name="ffn_final" if final_norm else "ffn",
    )(h, g.reshape(1, d), w1.astype(BF16), w3.astype(BF16), w2.astype(BF16), final_g.reshape(1, d))


def _gate_weight(w_gate_cols):
    d, c = w_gate_cols.shape
    return jnp.zeros((d, LANES), F32).at[:, :c].set(w_gate_cols).astype(BF16)


def _gate_param_rows(first, second):
    t = jnp.zeros((SUBLANES, LANES), F32)
    if first is not None:
        t = t.at[0, first[0]:first[0] + 8].set(first[1].reshape(-1))
    if second is not None:
        t = t.at[1, second[0]:second[0] + 8].set(second[1].reshape(-1))
    return t


def _even_layer(h, j, p, *, batch, seq):
    main = 2 * CONV_CH + 4 * DN_WIDTH
    w_in = p["ev_w_in"][j]
    gate_params = jnp.zeros((SUBLANES, LANES), F32)
    gate_params = gate_params.at[0, 8:16].set(p["ev_dn_a_log"][j].reshape(-1))
    gate_params = gate_params.at[1, 8:16].set(p["ev_dn_dt_bias"][j].reshape(-1))
    z, zg = _in_proj(h, p["mix_norm_g"][2 * j], w_in[:, :main].astype(BF16), _gate_weight(w_in[:, main:]),
                     gate_params, even=True)
    conv = _conv_glu(z, p["ev_conv_w"][j], p["ev_conv_b"][j], batch=batch, seq=seq)
    qkv = _conv_qkv(z, p["ev_dn_conv_w"][j], batch=batch, seq=seq, col0=2 * CONV_CH)
    o_fwd, o_bwd = _delta_scan(qkv, zg, batch=batch, seq=seq)
    return _even_out(h, conv, o_fwd, o_bwd, z, p["ev_conv_ln_g"][j], p["ev_conv_ln_b"][j], p["ev_dn_norm_g"][j],
                     p["ev_w_out"][j], gate_col=2 * CONV_CH + 3 * DN_WIDTH)


def _odd_layer(h, j, p, *, batch, seq):
    main = 2 * SG_WIDTH + 2 * ML_QK_WIDTH + 2 * ML_V_WIDTH
    w_in = p["od_w_in"][j]
    gate_params = jnp.zeros((SUBLANES, LANES), F32)
    gate_params = gate_params.at[0, 0:8].set(p["od_ml_i_bias"][j].reshape(-1))
    gate_params = gate_params.at[1, 8:16].set(p["od_ml_f_bias"][j].reshape(-1))
    z, zg = _in_proj(h, p["mix_norm_g"][2 * j + 1], w_in[:, :main].astype(BF16), _gate_weight(w_in[:, main:]),
                     gate_params, even=False)
    q_col = 2 * SG_WIDTH
    k_col = q_col + ML_QK_WIDTH
    v_col = k_col + ML_QK_WIDTH
    o_col = v_col + ML_V_WIDTH
    h_dirs = _mlstm_scan(z, zg, batch=batch, seq=seq, q_col=q_col, k_col=k_col, v_col=v_col)
    return _odd_out(h, z, h_dirs, p["od_sg_ln_g"][j], p["od_sg_ln_b"][j], p["od_sg_w"][j], p["od_sg_b"][j],
                    p["od_ml_norm_g"][j], p["od_w_out"][j], u_col=0, v_col=SG_WIDTH, o_col=o_col)


def kernel(x, mix_norm_g, ev_w_in, ev_conv_w, ev_conv_b, ev_conv_ln_g, ev_conv_ln_b, ev_dn_conv_w, ev_dn_a_log, ev_dn_dt_bias, ev_dn_norm_g, ev_w_out, od_w_in, od_sg_ln_g, od_sg_ln_b, od_sg_w, od_sg_b, od_ml_i_bias, od_ml_f_bias, od_ml_norm_g, od_w_out, ffn_norm_g, ffn_w1, ffn_w3, ffn_w2, final_norm_g):
    p = dict(mix_norm_g=mix_norm_g, ev_w_in=ev_w_in, ev_conv_w=ev_conv_w, ev_conv_b=ev_conv_b,
             ev_conv_ln_g=ev_conv_ln_g, ev_conv_ln_b=ev_conv_ln_b, ev_dn_conv_w=ev_dn_conv_w,
             ev_dn_a_log=ev_dn_a_log, ev_dn_dt_bias=ev_dn_dt_bias, ev_dn_norm_g=ev_dn_norm_g, ev_w_out=ev_w_out,
             od_w_in=od_w_in, od_sg_ln_g=od_sg_ln_g, od_sg_ln_b=od_sg_ln_b, od_sg_w=od_sg_w, od_sg_b=od_sg_b,
             od_ml_i_bias=od_ml_i_bias, od_ml_f_bias=od_ml_f_bias, od_ml_norm_g=od_ml_norm_g, od_w_out=od_w_out)
    batch, seq, d = x.shape
    depth = mix_norm_g.shape[0]
    h = x.reshape(batch * seq, d)
    for layer in range(depth):
        j = layer // 2
        if layer % 2 == 0:
            h = _even_layer(h, j, p, batch=batch, seq=seq)
        else:
            h = _odd_layer(h, j, p, batch=batch, seq=seq)
        h = _ffn(h, ffn_norm_g[layer], ffn_w1[layer], ffn_w3[layer], ffn_w2[layer], final_norm_g,
                 final_norm=layer == depth - 1)
    return h.reshape(batch, seq, d)
```

```python
import functools

import jax
import jax.numpy as jnp
from jax import lax
from jax.experimental import pallas as pl
from jax.experimental.pallas import tpu as pltpu

NORM_EPS = 1e-6
LN_EPS = 1e-5
N_DIR = 2

CONV_CH = 512
DN_HEADS = 4
DN_HEAD_DIM = 128
DN_WIDTH = DN_HEADS * DN_HEAD_DIM
SG_GROUPS = 4
SG_GROUP_DIM = 128
SG_WIDTH = SG_GROUPS * SG_GROUP_DIM
SG_CHUNK = 128
ML_HEADS = 4
ML_QK_DIM = 64
ML_V_DIM = 128
ML_QK_WIDTH = ML_HEADS * ML_QK_DIM
ML_V_WIDTH = ML_HEADS * ML_V_DIM

LANES = 128
SUBLANES = 8
VMEM_LIMIT_BYTES = 56 * 1024 * 1024

SCAN_CHUNK = 64
SCAN_STEP_CHUNKS = 2
PREP_GROUP = 128
PREP_ROWS = 512
ROW_TILE = 512
CONV_TILE = 128
CONV_PAD = 16
NEG_BIG = -1e30

BF16 = jnp.bfloat16
F32 = jnp.float32


def _params(*sem):
    return pltpu.CompilerParams(dimension_semantics=sem, vmem_limit_bytes=VMEM_LIMIT_BYTES)


def _resident(shape):
    return pl.BlockSpec(shape, lambda *_: (0,) * len(shape), pipeline_mode=pl.Buffered(1))


def _dot(a, b):
    return jnp.dot(a, b, preferred_element_type=F32)


def _dot_nt(a, b):
    return lax.dot_general(a, b, (((1,), (1,)), ((), ())), preferred_element_type=F32)


def _dot_tn(a, b):
    return lax.dot_general(a, b, (((0,), (0,)), ((), ())), preferred_element_type=F32)


def _exact_dot01(t01, x):
    x1 = x.astype(BF16)
    r1 = x - x1.astype(F32)
    x2 = r1.astype(BF16)
    x3 = (r1 - x2.astype(F32)).astype(BF16)
    return _dot(t01, x1) + _dot(t01, x2) + _dot(t01, x3)


def _exact_dot01_r(x, t01):
    x1 = x.astype(BF16)
    r1 = x - x1.astype(F32)
    x2 = r1.astype(BF16)
    x3 = (r1 - x2.astype(F32)).astype(BF16)
    return _dot(x1, t01) + _dot(x2, t01) + _dot(x3, t01)


def _in_proj_kernel(h_ref, g_ref, w_ref, wg_ref, gp_ref, zm_ref, zg_ref, *, even):
    x = h_ref[...]
    ms = jnp.mean(x * x, axis=-1, keepdims=True)
    hn = (x * lax.rsqrt(ms + NORM_EPS) * g_ref[...]).astype(BF16)
    zm_ref[...] = _dot(hn, w_ref[...])
    zg = _dot(hn, wg_ref[...])
    p0 = gp_ref[0:1, :]
    p1 = gp_ref[1:2, :]
    lane = lax.broadcasted_iota(jnp.int32, zg.shape, 1)
    if even:
        first = jax.nn.sigmoid(zg)
        second = -jnp.exp(p0) * jax.nn.softplus(zg + p1)
    else:
        first = zg + p0
        second = jax.nn.log_sigmoid(zg + p1)
    zg_ref[...] = jnp.where(lane < 8, first, second)


def _in_proj(h, g, w_main, w_gate, gate_params, *, even):
    n, d = h.shape
    c = w_main.shape[1]
    tm = min(ROW_TILE, n)
    return pl.pallas_call(
        functools.partial(_in_proj_kernel, even=even),
        grid=(n // tm,),
        in_specs=[
            pl.BlockSpec((tm, d), lambda i: (i, 0)),
            _resident((1, d)),
            _resident((d, c)),
            _resident((d, LANES)),
            _resident((SUBLANES, LANES)),
        ],
        out_specs=[
            pl.BlockSpec((tm, c), lambda i: (i, 0)),
            pl.BlockSpec((tm, LANES), lambda i: (i, 0)),
        ],
        out_shape=[jax.ShapeDtypeStruct((n, c), F32), jax.ShapeDtypeStruct((n, LANES), F32)],
        compiler_params=_params("parallel"),
        name="in_proj_even" if even else "in_proj_odd",
    )(h, g.reshape(1, d), w_main, w_gate, gate_params)


def _conv_taps(pad_ref, w_ref, o_ref, bias, *, seq, width, post):
    half = width // 2

    def body(i, carry):
        t0 = pl.multiple_of(i * CONV_TILE, CONV_TILE)
        acc = jnp.zeros((CONV_TILE, LANES), F32)
        for j in range(width):
            acc = acc + pad_ref[pl.ds(t0 + (CONV_PAD - half + j), CONV_TILE), :] * w_ref[j:j + 1, :]
        if bias is not None:
            acc = acc + bias
        o_ref[pl.ds(t0, CONV_TILE), :] = post(acc)
        return carry

    lax.fori_loop(0, seq // CONV_TILE, body, 0)


def _fill_padded(pad_ref, x, seq):
    zeros = jnp.zeros((CONV_PAD, LANES), F32)
    pad_ref[0:CONV_PAD, :] = zeros
    pad_ref[CONV_PAD + seq:CONV_PAD + seq + CONV_PAD, :] = zeros
    pad_ref[CONV_PAD:CONV_PAD + seq, :] = x


def _conv_glu_kernel(av_ref, ag_ref, w_ref, b_ref, o_ref, pad_ref, *, seq, width):
    _fill_padded(pad_ref, av_ref[...] * jax.nn.sigmoid(ag_ref[...]), seq)
    _conv_taps(pad_ref, w_ref, o_ref, b_ref[...], seq=seq, width=width, post=lambda a: a)


def _conv_glu(z, conv_w, conv_b, *, batch, seq):
    width = conv_w.shape[0]
    nblk = CONV_CH // LANES
    return pl.pallas_call(
        functools.partial(_conv_glu_kernel, seq=seq, width=width),
        grid=(batch, nblk),
        in_specs=[
            pl.BlockSpec((seq, LANES), lambda b, c: (b, c)),
            pl.BlockSpec((seq, LANES), lambda b, c: (b, nblk + c)),
            pl.BlockSpec((width, LANES), lambda b, c: (0, c)),
            pl.BlockSpec((1, LANES), lambda b, c: (0, c)),
        ],
        out_specs=pl.BlockSpec((seq, LANES), lambda b, c: (b, c)),
        out_shape=jax.ShapeDtypeStruct((batch * seq, CONV_CH), F32),
        scratch_shapes=[pltpu.VMEM((seq + 2 * CONV_PAD, LANES), F32)],
        compiler_params=_params("parallel", "parallel"),
        name="conv_glu",
    )(z, z, conv_w, conv_b.reshape(1, CONV_CH))


def _conv_qkv_kernel(x_ref, w_ref, o_ref, pad_ref, *, seq, width, n_norm_blocks):
    _fill_padded(pad_ref, x_ref[...], seq)
    normalise = pl.program_id(1) < n_norm_blocks

    def post(a):
        y = a * jax.nn.sigmoid(a)
        yn = y * lax.rsqrt(jnp.sum(y * y, axis=-1, keepdims=True) + NORM_EPS)
        return jnp.where(normalise, yn, y)

    _conv_taps(pad_ref, w_ref, o_ref, None, seq=seq, width=width, post=post)


def _conv_qkv(z, dn_conv_w, *, batch, seq, col0):
    width = dn_conv_w.shape[0]
    nblk = 3 * DN_WIDTH // LANES
    blk0 = col0 // LANES
    return pl.pallas_call(
        functools.partial(_conv_qkv_kernel, seq=seq, width=width, n_norm_blocks=2 * DN_HEADS),
        grid=(batch, nblk),
        in_specs=[
            pl.BlockSpec((seq, LANES), lambda b, c: (b, blk0 + c)),
            pl.BlockSpec((width, LANES), lambda b, c: (0, c)),
        ],
        out_specs=pl.BlockSpec((seq, LANES), lambda b, c: (b, c)),
        out_shape=jax.ShapeDtypeStruct((batch * seq, 3 * DN_WIDTH), F32),
        scratch_shapes=[pltpu.VMEM((seq + 2 * CONV_PAD, LANES), F32)],
        compiler_params=_params("parallel", "parallel"),
        name="conv_qkv",
    )(z, dn_conv_w)


def _tri_masks(n, backward):
    row = lax.broadcasted_iota(jnp.int32, (n, n), 0)
    col = lax.broadcasted_iota(jnp.int32, (n, n), 1)
    d = jnp.where(backward, col - row, row - col)
    return row, col, d >= 0, d > 0, d <= 0


def _unit_tri_inverse(a, row, col, n=None):
    return _unit_tri_inverses([a], row, col, a.shape[0] if n is None else n)[0]


def _split_hi_lo(x):
    hi = x.astype(BF16)
    return jnp.concatenate([hi, (x - hi.astype(F32)).astype(BF16)], axis=-1)


def _bf16_all(mats):
    return [m.astype(BF16) for m in mats]


def _unit_tri_inverses(mats, row, col, n):
    eye = (row == col).astype(F32)
    same = (row >> 3) == (col >> 3)
    l8 = [jnp.where(same, a, 0.0) for a in mats]
    l8h = _bf16_all(l8)
    l2h = _bf16_all([_dot(p, p) for p in l8h])
    l4h = _bf16_all([_dot(p, p) for p in l2h])
    x = [eye - p for p in l8]
    x = [xi + _dot(xi.astype(BF16), p) for xi, p in zip(x, l2h)]
    x = [xi + _dot(xi.astype(BF16), p) for xi, p in zip(x, l4h)]
    shift = 3
    while (1 << shift) < n:
        same_next = (row >> (shift + 1)) == (col >> (shift + 1))
        off = jnp.logical_and(same_next, jnp.logical_not(same))
        ch = _bf16_all([jnp.where(off, a, 0.0) for a in mats])
        xh = _bf16_all(x)
        xch = _bf16_all([_dot(p, c) for p, c in zip(xh, ch)])
        x = [xi - _dot(p, q) for xi, p, q in zip(x, xch, xh)]
        same = same_next
        shift += 1
    return x


def _group_masks(backward):
    n = PREP_GROUP
    row = lax.broadcasted_iota(jnp.int32, (n, n), 0)
    col = lax.broadcasted_iota(jnp.int32, (n, n), 1)
    shift = SCAN_CHUNK.bit_length() - 1
    same = (row >> shift) == (col >> shift)
    d = (col - row) if backward else (row - col)
    land = jnp.logical_and
    return land(same, d >= 0), land(same, d > 0), land(same, d <= 0)


def _as01(mask):
    return jnp.where(mask, 1.0, 0.0).astype(BF16)


def _delta_prep_kernel(q_ref, k_ref, v_ref, gc_ref, gr_ref, u_ref, w_ref, qg_ref, kg_ref, at_ref):
    G, L = PREP_GROUP, SCAN_CHUNK
    row = lax.broadcasted_iota(jnp.int32, (G, G), 0)
    col = lax.broadcasted_iota(jnp.int32, (G, G), 1)
    shift = L.bit_length() - 1
    same01 = _as01((row >> shift) == (col >> shift))
    masks = [_group_masks(False), _group_masks(True)]
    cum_c = [_as01(m[0]) for m in masks]
    cum_r = [_as01(m[2]) for m in masks]
    fold = _as01((lax.broadcasted_iota(jnp.int32, (G, L), 0) & (L - 1)) == lax.broadcasted_iota(jnp.int32, (G, L), 1))
    pad = jnp.zeros((G, DN_HEAD_DIM - L), BF16)
    scale = DN_HEAD_DIM ** -0.5

    def group(i, carry):
        rows = pl.ds(pl.multiple_of(i * G, G), G)
        gc = gc_ref[rows, :]
        gr = gr_ref[i]
        g_tot = _exact_dot01(same01, gc)
        g_cum_c = [_exact_dot01(cum_c[d], gc) for d in range(N_DIR)]
        g_cum_r = [_exact_dot01_r(gr, cum_r[d]) for d in range(N_DIR)]
        heads = range(DN_HEADS)
        lanes = [slice(h * DN_HEAD_DIM, (h + 1) * DN_HEAD_DIM) for h in heads]
        chains = [(h, d) for h in heads for d in range(N_DIR)]
        q = [q_ref[rows, sl] for sl in lanes]
        k = [k_ref[rows, sl] for sl in lanes]
        k16 = _bf16_all(k)
        kk = [_dot_nt(p, p) for p in k16]
        qk = [_dot_nt((q[h] * scale).astype(BF16), k16[h]) for h in heads]
        beta, g_c, g_t, decay = {}, {}, {}, {}
        for h, d in chains:
            cb = d * DN_HEADS + h
            cl = N_DIR * DN_HEADS + cb
            beta[h, d] = gc[:, cb:cb + 1]
            g_c[h, d] = g_cum_c[d][:, cl:cl + 1]
            g_t[h, d] = g_tot[:, cl:cl + 1]
            g_r = g_cum_r[d][cl:cl + 1, :]
            decay[h, d] = jnp.exp(jnp.where(masks[d][0], g_c[h, d] - g_r, NEG_BIG))
        a = [jnp.where(masks[d][1], kk[h] * decay[h, d], 0.0) * beta[h, d] for h, d in chains]
        x = _unit_tri_inverses(a, row, col, L)
        eg = {c: jnp.exp(g_c[c]) for c in chains}
        rhs = [jnp.concatenate([v_ref[rows, lanes[h]] * beta[h, d], k[h] * (beta[h, d] * eg[h, d])],
                               axis=-1).astype(BF16) for h, d in chains]
        uw = [_dot(xi.astype(BF16), r) for xi, r in zip(x, rhs)]
        attn = [jnp.where(masks[d][0], qk[h] * decay[h, d], 0.0).astype(BF16) for h, d in chains]
        at = [_dot(p, fold) for p in attn]
        for (h, d), uwi, ati in zip(chains, uw, at):
            sl = lanes[h]
            u_ref[d, rows, sl] = uwi[:, :DN_HEAD_DIM].astype(BF16)
            w_ref[d, rows, sl] = uwi[:, DN_HEAD_DIM:].astype(BF16)
            qg_ref[d, rows, sl] = (q[h] * (scale * eg[h, d])).astype(BF16)
            kg_ref[d, rows, sl] = (k[h] * jnp.exp(g_t[h, d] - g_c[h, d])).astype(BF16)
            at_ref[d, rows, sl] = jnp.concatenate([ati.astype(BF16), pad], axis=-1)
        return carry

    lax.fori_loop(0, q_ref.shape[0] // G, group, 0)


def _delta_prep(qkv, zg, *, batch, seq):
    n = batch * seq
    R, G = min(PREP_ROWS, seq), PREP_GROUP
    steps = seq // R
    gr = zg[:, :16].reshape(n // G, G, 16).transpose(0, 2, 1)
    rowblk = lambda c: (lambda b, r: (b * steps + r, c))
    out = jax.ShapeDtypeStruct((N_DIR, n, DN_WIDTH), BF16)
    out_spec = pl.BlockSpec((N_DIR, R, DN_WIDTH), lambda b, r: (0, b * steps + r, 0))
    return pl.pallas_call(
        _delta_prep_kernel,
        grid=(batch, steps),
        in_specs=[
            pl.BlockSpec((R, DN_WIDTH), rowblk(0)),
            pl.BlockSpec((R, DN_WIDTH), rowblk(1)),
            pl.BlockSpec((R, DN_WIDTH), rowblk(2)),
            pl.BlockSpec((R, LANES), rowblk(0)),
            pl.BlockSpec((R // G, 16, G), lambda b, r: (b * steps + r, 0, 0)),
        ],
        out_specs=[out_spec] * 5,
        out_shape=[out] * 5,
        compiler_params=_params("parallel", "parallel"),
        name="delta_prep",
    )(qkv, qkv, qkv, zg, gr)


def _delta_scan_kernel(uf, wf, qgf, kgf, atf, zgf, ub, wb, qgb, kgb, atb, zgb, of_ref, ob_ref, s_ref, *, batch):
    L, C = SCAN_CHUNK, SCAN_STEP_CHUNKS

    @pl.when(pl.program_id(0) == 0)
    def _():
        s_ref[...] = jnp.zeros_like(s_ref)

    dirs = ((uf, wf, qgf, kgf, atf, zgf, of_ref), (ub, wb, qgb, kgb, atb, zgb, ob_ref))

    chains = [(d, b, h) for d in range(N_DIR) for b in range(batch) for h in range(DN_HEADS)]
    state = {c: s_ref[c] for c in chains}
    for step in range(C):
        chunk = {0: step, 1: C - 1 - step}
        rows = {d: slice(chunk[d] * L, (chunk[d] + 1) * L) for d in range(N_DIR)}
        egt = {(d, b): jnp.exp(jnp.sum(dirs[d][5][b, rows[d], :], axis=0, keepdims=True))
               for d in range(N_DIR) for b in range(batch)}
        ws, v_new, av = {}, {}, {}
        for c in chains:
            d, b, h = c
            sl = slice(h * DN_HEAD_DIM, (h + 1) * DN_HEAD_DIM)
            wq = jnp.concatenate([dirs[d][1][b, rows[d], sl], dirs[d][2][b, rows[d], sl]], axis=0)
            ws[c] = _dot(wq, state[c].astype(BF16))
        for c in chains:
            d, b, h = c
            sl = slice(h * DN_HEAD_DIM, (h + 1) * DN_HEAD_DIM)
            v_new[c] = (dirs[d][0][b, rows[d], sl].astype(F32) - ws[c][:L]).astype(BF16)
        for c in chains:
            d, b, h = c
            at = dirs[d][4][b, rows[d], h * DN_HEAD_DIM:h * DN_HEAD_DIM + L]
            av[c] = _dot(at, v_new[c])
        for c in chains:
            d, b, h = c
            sl = slice(h * DN_HEAD_DIM, (h + 1) * DN_HEAD_DIM)
            cl = (N_DIR + d) * DN_HEADS + h
            dirs[d][6][b, rows[d], sl] = ws[c][L:] + av[c]
            state[c] = state[c] * egt[d, b][:, cl:cl + 1] + _dot_tn(dirs[d][3][b, rows[d], sl], v_new[c])
    for c in chains:
        s_ref[c] = state[c]


def _scan_chunk_index(i, n, *, batch, n_chunks):
    b = i % batch
    backward = i // batch
    return b * n_chunks + n + backward * (n_chunks - 1 - 2 * n)


def _gate_layouts(zg, *, batch, seq):
    n_chunks = seq // SCAN_CHUNK
    g = zg[:, :16].reshape(batch * seq, 2, N_DIR, 4)
    cols = g.transpose(2, 0, 1, 3).reshape(N_DIR, batch * seq, 8)
    rows = cols.reshape(N_DIR, batch * n_chunks, SCAN_CHUNK, 8).transpose(0, 1, 3, 2)
    return cols, rows


def _delta_scan(qkv, zg, *, batch, seq):
    n = batch * seq
    T = SCAN_CHUNK * SCAN_STEP_CHUNKS
    steps = seq // T
    prepped = [t.reshape(N_DIR, batch, seq, DN_WIDTH) for t in _delta_prep(qkv, zg, batch=batch, seq=seq)]
    zg3 = zg.reshape(batch, seq, LANES)
    fwd = lambda s: s
    bwd = lambda s: steps - 1 - s
    specs = []
    for d, pos in ((0, fwd), (1, bwd)):
        specs += [pl.BlockSpec((None, batch, T, DN_WIDTH), lambda s, d=d, pos=pos: (d, 0, pos(s), 0))] * 5
        specs += [pl.BlockSpec((batch, T, LANES), lambda s, pos=pos: (0, pos(s), 0))]
    out = jax.ShapeDtypeStruct((batch, seq, DN_WIDTH), F32)
    o_f, o_b = pl.pallas_call(
        functools.partial(_delta_scan_kernel, batch=batch),
        grid=(steps,),
        in_specs=specs,
        out_specs=[pl.BlockSpec((batch, T, DN_WIDTH), lambda s: (0, fwd(s), 0)),
                   pl.BlockSpec((batch, T, DN_WIDTH), lambda s: (0, bwd(s), 0))],
        out_shape=[out, out],
        scratch_shapes=[pltpu.VMEM((N_DIR, batch, DN_HEADS, DN_HEAD_DIM, DN_HEAD_DIM), F32)],
        compiler_params=_params("arbitrary"),
        name="delta_scan",
    )(*prepped, zg3, *prepped, zg3)
    return o_f.reshape(n, DN_WIDTH), o_b.reshape(n, DN_WIDTH)


ML_PAIRS = ML_HEADS // 2
ML_AUG = 2 * ML_V_DIM
ML_KV_ROWS = ML_PAIRS * 2 * ML_QK_DIM


def _lane_cols(cols, width):
    rows = cols[0].shape[0]
    lane = lax.broadcasted_iota(jnp.int32, (rows, width), 1)
    out = jnp.zeros((rows, width), F32)
    for j, c in enumerate(cols):
        out = jnp.where(lane == j, c, out)
    return out


def _mlstm_prep_kernel(q_ref, k_ref, v_ref, gc_ref, gr_ref, qs_ref, iv_ref, kv_ref, aux_ref):
    G, L = PREP_GROUP, SCAN_CHUNK
    row = lax.broadcasted_iota(jnp.int32, (G, G), 0)
    col = lax.broadcasted_iota(jnp.int32, (G, G), 1)
    shift = L.bit_length() - 1
    same01 = _as01((row >> shift) == (col >> shift))
    masks = [_group_masks(False), _group_masks(True)]
    cum_c = [_as01(m[0]) for m in masks]
    cum_r = [_as01(m[2]) for m in masks]
    lane = lax.broadcasted_iota(jnp.int32, (G, LANES), 1)
    first_head = lane < ML_QK_DIM
    lane_r = lax.broadcasted_iota(jnp.int32, (1, G), 1)
    row_c = lax.broadcasted_iota(jnp.int32, (G, 1), 0)
    ones_v = jnp.ones((G, ML_V_DIM), BF16)
    scale = ML_QK_DIM ** -0.5
    heads = range(ML_HEADS)
    chains = [(h, d) for h in heads for d in range(N_DIR)]

    def group(i, carry):
        rows = pl.ds(pl.multiple_of(i * G, G), G)
        gc = gc_ref[rows, :]
        gr = gr_ref[i]
        bt_c = _exact_dot01(same01, gc)
        bt_r = _exact_dot01_r(gr, same01)
        bc_c = [_exact_dot01(cum_c[d], gc) for d in range(N_DIR)]
        bc_r = [_exact_dot01_r(gr, cum_r[d]) for d in range(N_DIR)]
        qp = [q_ref[rows, p * LANES:(p + 1) * LANES] * scale for p in range(ML_PAIRS)]
        kp = [k_ref[rows, p * LANES:(p + 1) * LANES] for p in range(ML_PAIRS)]
        kp16 = _bf16_all(kp)
        v16 = [v_ref[rows, h * ML_V_DIM:(h + 1) * ML_V_DIM].astype(BF16) for h in heads]
        for p in range(ML_PAIRS):
            qs_ref[rows, p * LANES:(p + 1) * LANES] = qp[p].astype(BF16)
        own = [first_head, jnp.logical_not(first_head)]
        scores = [_dot_nt(jnp.where(own[h % 2], qp[h // 2], 0.0).astype(BF16), kp16[h // 2]) for h in heads]
        b_c, d_max, b_t, w_max, sw0, pmat = {}, {}, {}, {}, {}, {}
        for h, d in chains:
            ci = d * ML_HEADS + h
            cf = N_DIR * ML_HEADS + ci
            li_c, li_r = gc[:, ci:ci + 1], gr[ci:ci + 1, :]
            b_c[h, d] = bc_c[d][:, cf:cf + 1]
            b_t[h, d] = bt_c[:, cf:cf + 1]
            b_r = bc_r[d][cf:cf + 1, :]
            d_mat = jnp.where(masks[d][0], b_c[h, d] - b_r + li_r, NEG_BIG)
            d_max[h, d] = jnp.max(d_mat, axis=-1, keepdims=True)
            pmat[h, d] = (jnp.exp(d_mat - d_max[h, d]) * scores[h]).astype(BF16)
            w_end_c = b_t[h, d] - b_c[h, d] + li_c
            w_end_r = bt_r[cf:cf + 1, :] - b_r + li_r
            wm = jnp.full((G, 1), NEG_BIG, F32)
            for c in range(G // L):
                in_c = jnp.logical_and(lane_r >= c * L, lane_r < (c + 1) * L)
                wm_c = jnp.max(jnp.where(in_c, w_end_r, NEG_BIG), axis=-1, keepdims=True)
                wm = jnp.where(jnp.logical_and(row_c >= c * L, row_c < (c + 1) * L), wm_c, wm)
            w_max[h, d] = wm
            sw0[h, d] = jnp.exp(w_end_c - wm)
        iv = [_dot(pmat[h, d], jnp.concatenate([v16[h], ones_v], axis=-1)) for h, d in chains]
        for (h, d), ivi in zip(chains, iv):
            iv_ref[d, rows, h * ML_AUG:(h + 1) * ML_AUG] = ivi.astype(BF16)
        for d in range(N_DIR):
            aux_ref[d, rows, :LANES] = _lane_cols([b_c[h, d] for h in heads] + [b_t[h, d] for h in heads], LANES)
            aux_ref[d, rows, LANES:] = _lane_cols([d_max[h, d] for h in heads] + [w_max[h, d] for h in heads], LANES)
        pair_chunks = [(p, d, c) for p in range(ML_PAIRS) for d in range(N_DIR) for c in range(G // L)]
        ks = {(p, d): (kp[p] * jnp.where(first_head, sw0[2 * p, d], sw0[2 * p + 1, d])).astype(BF16)
              for p in range(ML_PAIRS) for d in range(N_DIR)}
        vcat = [jnp.concatenate([v16[2 * p], v16[2 * p + 1], ones_v], axis=-1) for p in range(ML_PAIRS)]
        kv = [_dot_tn(ks[p, d][c * L:(c + 1) * L], vcat[p][c * L:(c + 1) * L]) for p, d, c in pair_chunks]
        for (p, d, c), t in zip(pair_chunks, kv):
            top = jnp.concatenate([t[:ML_QK_DIM, :ML_V_DIM], t[:ML_QK_DIM, 2 * ML_V_DIM:]], axis=-1)
            bot = jnp.concatenate([t[ML_QK_DIM:, ML_V_DIM:2 * ML_V_DIM], t[ML_QK_DIM:, 2 * ML_V_DIM:]], axis=-1)
            r0 = pl.multiple_of(i * (G // L * ML_KV_ROWS) + (c * ML_PAIRS + p) * LANES, LANES)
            kv_ref[d, pl.ds(r0, LANES), :] = jnp.concatenate([top, bot], axis=0).astype(BF16)
        return carry

    lax.fori_loop(0, q_ref.shape[0] // G, group, 0)


def _mlstm_prep(z, zg, *, batch, seq, q_col, k_col, v_col):
    n = batch * seq
    R, G, L = min(PREP_ROWS, seq), PREP_GROUP, SCAN_CHUNK
    steps = seq // R
    gr = zg[:, :16].reshape(n // G, G, 16).transpose(0, 2, 1)
    rowblk = lambda c: (lambda b, r: (b * steps + r, c))
    dirblk = lambda b, r: (0, b * steps + r, 0)
    kv_rows = ML_KV_ROWS // L
    return pl.pallas_call(
        _mlstm_prep_kernel,
        grid=(batch, steps),
        in_specs=[
            pl.BlockSpec((R, ML_QK_WIDTH), rowblk(q_col // ML_QK_WIDTH)),
            pl.BlockSpec((R, ML_QK_WIDTH), rowblk(k_col // ML_QK_WIDTH)),
            pl.BlockSpec((R, ML_V_WIDTH), rowblk(v_col // ML_V_WIDTH)),
            pl.BlockSpec((R, LANES), rowblk(0)),
            pl.BlockSpec((R // G, 16, G), lambda b, r: (b * steps + r, 0, 0)),
        ],
        out_specs=[
            pl.BlockSpec((R, ML_QK_WIDTH), rowblk(0)),
            pl.BlockSpec((N_DIR, R, ML_HEADS * ML_AUG), dirblk),
            pl.BlockSpec((N_DIR, R * kv_rows, ML_AUG), dirblk),
            pl.BlockSpec((N_DIR, R, 2 * LANES), dirblk),
        ],
        out_shape=[
            jax.ShapeDtypeStruct((n, ML_QK_WIDTH), BF16),
            jax.ShapeDtypeStruct((N_DIR, n, ML_HEADS * ML_AUG), BF16),
            jax.ShapeDtypeStruct((N_DIR, n * kv_rows, ML_AUG), BF16),
            jax.ShapeDtypeStruct((N_DIR, n, 2 * LANES), F32),
        ],
        compiler_params=_params("parallel", "parallel"),
        name="mlstm_prep",
    )(z, z, z, zg, gr)


def _mlstm_scan_kernel(qf, ivf, kvf, auxf, qb, ivb, kvb, auxb, of_ref, ob_ref, c_ref, m_ref, *, batch):
    L, C = SCAN_CHUNK, SCAN_STEP_CHUNKS

    @pl.when(pl.program_id(0) == 0)
    def _():
        c_ref[...] = jnp.zeros_like(c_ref)
        m_ref[...] = jnp.zeros_like(m_ref)

    dirs = ((qf, ivf, kvf, auxf, of_ref), (qb, ivb, kvb, auxb, ob_ref))
    lane = lax.broadcasted_iota(jnp.int32, (L, LANES), 1)
    keep = [jnp.where(lane < ML_QK_DIM, 1.0, 0.0).astype(BF16), jnp.where(lane < ML_QK_DIM, 0.0, 1.0).astype(BF16)]
    first_rows = lax.broadcasted_iota(jnp.int32, (2 * ML_QK_DIM, 1), 0) < ML_QK_DIM
    pair_chains = [(d, b, p) for d in range(N_DIR) for b in range(batch) for p in range(ML_PAIRS)]
    chains = [(d, b, h) for d in range(N_DIR) for b in range(batch) for h in range(ML_HEADS)]
    groups = [(d, b) for d in range(N_DIR) for b in range(batch)]
    H = ML_HEADS
    lane8 = lax.broadcasted_iota(jnp.int32, (SUBLANES, LANES), 1)
    sel_row = lax.broadcasted_iota(jnp.int32, (2 * LANES, LANES), 0) & (LANES - 1)
    lane_select = [_as01(sel_row == j) for j in range(2 * H)]
    state = {c: c_ref[c] for c in pair_chains}
    m_st = {g: m_ref[g] for g in groups}
    for step in range(C):
        chunk = {0: step, 1: C - 1 - step}
        rows = {d: slice(chunk[d] * L, (chunk[d] + 1) * L) for d in range(N_DIR)}
        qc = {}
        for c in pair_chains:
            d, b, p = c
            qp = dirs[d][0][b, rows[d], p * LANES:(p + 1) * LANES]
            qc[c] = _dot(jnp.concatenate([qp * keep[0], qp * keep[1]], axis=0), state[c].astype(BF16))
        w_prev, w_cur, floor = {}, {}, {}
        for g in groups:
            d, b = g
            aux = dirs[d][3][b, rows[d], :]
            x = aux[:, :LANES] + m_st[g][0:1, :]
            y = jnp.maximum(x, aux[:, LANES:])
            w_prev[g] = jnp.exp(x - y)
            w_cur[g] = jnp.exp(aux[:, LANES:] - y)
            floor[g] = jnp.exp(-y)
            y0 = jnp.broadcast_to(y[0:1, :], (SUBLANES, LANES))
            m_st[g] = jnp.where(lane8 < H, pltpu.roll(y0, LANES - H, 1), jnp.where(lane8 < 2 * H, y0, 0.0))
        tiles = [t[g] for g in groups for t in (w_prev, w_cur, floor)]
        per_row = _split_hi_lo(jnp.concatenate(tiles, axis=0))
        per_chunk = _split_hi_lo(jnp.concatenate([t[0:1, :] for t in tiles] + [tiles[0][0:SUBLANES, :]], axis=0))
        bc_row, bc_chunk = [], []
        for p in range(ML_PAIRS):
            both = _dot(per_row, jnp.concatenate([lane_select[2 * p], lane_select[2 * p + 1]], axis=-1))
            bc_row += [both[:, :LANES], both[:, LANES:]]
            both = _dot(per_chunk, jnp.concatenate([lane_select[H + 2 * p], lane_select[H + 2 * p + 1]], axis=-1))
            bc_chunk += [both[:, :LANES], both[:, LANES:]]
        numden = {}
        for c in chains:
            d, b, h = c
            r0 = (h % 2) * L
            t0 = 3 * groups.index((d, b)) * L
            wp = bc_row[h][t0:t0 + L, :]
            wc = bc_row[h][t0 + L:t0 + 2 * L, :]
            iv = dirs[d][1][b, rows[d], h * ML_AUG:(h + 1) * ML_AUG].astype(F32)
            qch = qc[d, b, h // 2][r0:r0 + L, :]
            numden[c] = jnp.concatenate([wp * qch[:, :ML_V_DIM] + wc * iv[:, :ML_V_DIM],
                                         wp * qch[:, ML_V_DIM:] + wc * iv[:, ML_V_DIM:]], axis=-1)
        for c in chains:
            d, b, h = c
            t0 = 3 * groups.index((d, b)) * L
            den = jnp.maximum(jnp.abs(numden[c][:, ML_V_DIM:]), bc_row[h][t0 + 2 * L:t0 + 3 * L, :])
            dirs[d][4][b, rows[d], h * ML_V_DIM:(h + 1) * ML_V_DIM] = numden[c][:, :ML_V_DIM] / den
        for c in pair_chains:
            d, b, p = c
            r0 = (chunk[d] * ML_PAIRS + p) * LANES
            kv = dirs[d][2][b, r0:r0 + LANES, :].astype(F32)
            t0 = 3 * groups.index((d, b))
            cw = jnp.where(first_rows, bc_chunk[2 * p][t0:t0 + 1, :], bc_chunk[2 * p + 1][t0:t0 + 1, :])
            iw = jnp.where(first_rows, bc_chunk[2 * p][t0 + 1:t0 + 2, :], bc_chunk[2 * p + 1][t0 + 1:t0 + 2, :])
            state[c] = jnp.concatenate([cw * state[c][:, :ML_V_DIM] + iw * kv[:, :ML_V_DIM],
                                        cw * state[c][:, ML_V_DIM:] + iw * kv[:, ML_V_DIM:]], axis=-1)
    for c in pair_chains:
        c_ref[c] = state[c]
    for g in groups:
        m_ref[g] = m_st[g]


def _mlstm_scan(z, zg, *, batch, seq, q_col, k_col, v_col):
    n = batch * seq
    T = SCAN_CHUNK * SCAN_STEP_CHUNKS
    steps = seq // T
    kv_rows = ML_KV_ROWS // SCAN_CHUNK
    qs, iv, kv, aux = _mlstm_prep(z, zg, batch=batch, seq=seq, q_col=q_col, k_col=k_col, v_col=v_col)
    qs = qs.reshape(batch, seq, ML_QK_WIDTH)
    iv = iv.reshape(N_DIR, batch, seq, ML_HEADS * ML_AUG)
    kv = kv.reshape(N_DIR, batch, seq * kv_rows, ML_AUG)
    aux = aux.reshape(N_DIR, batch, seq, 2 * LANES)
    fwd = lambda s: s
    bwd = lambda s: steps - 1 - s
    specs = []
    for d, pos in ((0, fwd), (1, bwd)):
        specs += [
            pl.BlockSpec((batch, T, ML_QK_WIDTH), lambda s, pos=pos: (0, pos(s), 0)),
            pl.BlockSpec((None, batch, T, ML_HEADS * ML_AUG), lambda s, d=d, pos=pos: (d, 0, pos(s), 0)),
            pl.BlockSpec((None, batch, T * kv_rows, ML_AUG), lambda s, d=d, pos=pos: (d, 0, pos(s), 0)),
            pl.BlockSpec((None, batch, T, 2 * LANES), lambda s, d=d, pos=pos: (d, 0, pos(s), 0)),
        ]
    out = jax.ShapeDtypeStruct((batch, seq, ML_V_WIDTH), F32)
    h_f, h_b = pl.pallas_call(
        functools.partial(_mlstm_scan_kernel, batch=batch),
        grid=(steps,),
        in_specs=specs,
        out_specs=[pl.BlockSpec((batch, T, ML_V_WIDTH), lambda s: (0, fwd(s), 0)),
                   pl.BlockSpec((batch, T, ML_V_WIDTH), lambda s: (0, bwd(s), 0))],
        out_shape=[out, out],
        scratch_shapes=[
            pltpu.VMEM((N_DIR, batch, ML_PAIRS, 2 * ML_QK_DIM, ML_AUG), F32),
            pltpu.VMEM((N_DIR, batch, SUBLANES, LANES), F32),
        ],
        compiler_params=_params("arbitrary"),
        name="mlstm_scan",
    )(qs, iv, kv, aux, qs, iv, kv, aux)
    return h_f.reshape(n, ML_V_WIDTH), h_b.reshape(n, ML_V_WIDTH)


def _head_rms_norm(x, g, n_heads, head_dim):
    parts = []
    for h in range(n_heads):
        xh = x[:, h * head_dim:(h + 1) * head_dim]
        ms = jnp.mean(xh * xh, axis=-1, keepdims=True)
        parts.append(xh * lax.rsqrt(ms + NORM_EPS) * g)
    return jnp.concatenate(parts, axis=-1)


def _layer_norm(x, g, b):
    mu = jnp.mean(x, axis=-1, keepdims=True)
    xc = x - mu
    var = jnp.mean(xc * xc, axis=-1, keepdims=True)
    return xc * lax.rsqrt(var + LN_EPS) * g + b


def _even_out_kernel(h_ref, c_ref, of_ref, ob_ref, gate_ref, lng_ref, lnb_ref, ng_ref, wa_ref, wb_ref, o_ref):
    ya = _layer_norm(c_ref[...], lng_ref[...], lnb_ref[...])
    ya = ya * jax.nn.sigmoid(ya)
    o = _head_rms_norm(of_ref[...] + ob_ref[...], ng_ref[...], DN_HEADS, DN_HEAD_DIM)
    gate = gate_ref[...]
    o = o * (gate * jax.nn.sigmoid(gate))
    o_ref[...] = h_ref[...] + _dot(ya.astype(BF16), wa_ref[...]) + _dot(o.astype(BF16), wb_ref[...])


def _even_out(h, conv, o_fwd, o_bwd, z, ln_g, ln_b, norm_g, w_out, *, gate_col):
    n, d = h.shape
    tm = min(ROW_TILE, n)
    gb = gate_col // DN_WIDTH
    row = lambda i: (i, 0)
    return pl.pallas_call(
        _even_out_kernel,
        grid=(n // tm,),
        in_specs=[
            pl.BlockSpec((tm, d), row),
            pl.BlockSpec((tm, CONV_CH), row),
            pl.BlockSpec((tm, DN_WIDTH), row),
            pl.BlockSpec((tm, DN_WIDTH), row),
            pl.BlockSpec((tm, DN_WIDTH), lambda i: (i, gb)),
            _resident((1, CONV_CH)),
            _resident((1, CONV_CH)),
            _resident((1, DN_HEAD_DIM)),
            _resident((CONV_CH, d)),
            _resident((DN_WIDTH, d)),
        ],
        out_specs=pl.BlockSpec((tm, d), row),
        out_shape=jax.ShapeDtypeStruct((n, d), F32),
        compiler_params=_params("parallel"),
        name="even_out",
    )(h, conv, o_fwd, o_bwd, z, ln_g.reshape(1, -1), ln_b.reshape(1, -1), norm_g.reshape(1, -1),
      w_out[:CONV_CH].astype(BF16), w_out[CONV_CH:].astype(BF16))


def _odd_out_kernel(h_ref, u_ref, vp_ref, hf_ref, hb_ref, op_ref, lng_ref, lnb_ref, sgw_ref, sgb_ref, ng_ref,
                    wa_ref, wb_ref, o_ref):
    tm = h_ref.shape[0]
    u = jax.nn.gelu(u_ref[...])
    vv = _layer_norm(jax.nn.gelu(vp_ref[...]), lng_ref[...], lnb_ref[...]).astype(BF16)
    sgb = sgb_ref[...]
    rows = []
    for c in range(tm // SG_CHUNK):
        parts = []
        for g in range(SG_GROUPS):
            blk = vv[c * SG_CHUNK:(c + 1) * SG_CHUNK, g * SG_GROUP_DIM:(g + 1) * SG_GROUP_DIM]
            parts.append(_dot(sgw_ref[g], blk) + sgb[:, g:g + 1])
        rows.append(jnp.concatenate(parts, axis=-1))
    yc = u * jnp.concatenate(rows, axis=0)
    hd = _head_rms_norm(hf_ref[...] + hb_ref[...], ng_ref[...], ML_HEADS, ML_V_DIM)
    hd = hd * jax.nn.sigmoid(op_ref[...])
    o_ref[...] = h_ref[...] + _dot(yc.astype(BF16), wa_ref[...]) + _dot(hd.astype(BF16), wb_ref[...])


def _odd_out(h, z, h_fwd, h_bwd, ln_g, ln_b, sg_w, sg_b, norm_g, w_out, *, u_col, v_col, o_col):
    n, d = h.shape
    tm = min(ROW_TILE, n)
    row = lambda i: (i, 0)
    return pl.pallas_call(
        _odd_out_kernel,
        grid=(n // tm,),
        in_specs=[
            pl.BlockSpec((tm, d), row),
            pl.BlockSpec((tm, SG_WIDTH), lambda i: (i, u_col // SG_WIDTH)),
            pl.BlockSpec((tm, SG_WIDTH), lambda i: (i, v_col // SG_WIDTH)),
            pl.BlockSpec((tm, ML_V_WIDTH), row),
            pl.BlockSpec((tm, ML_V_WIDTH), row),
            pl.BlockSpec((tm, ML_V_WIDTH), lambda i: (i, o_col // ML_V_WIDTH)),
            _resident((1, SG_WIDTH)),
            _resident((1, SG_WIDTH)),
            _resident((SG_GROUPS, SG_CHUNK, SG_CHUNK)),
            _resident((SG_CHUNK, SG_GROUPS)),
            _resident((1, ML_V_DIM)),
            _resident((SG_WIDTH, d)),
            _resident((ML_V_WIDTH, d)),
        ],
        out_specs=pl.BlockSpec((tm, d), row),
        out_shape=jax.ShapeDtypeStruct((n, d), F32),
        compiler_params=_params("parallel"),
        name="odd_out",
    )(h, z, z, h_fwd, h_bwd, z, ln_g.reshape(1, -1), ln_b.reshape(1, -1), sg_w.astype(BF16), sg_b.T,
      norm_g.reshape(1, -1), w_out[:SG_WIDTH].astype(BF16), w_out[SG_WIDTH:].astype(BF16))


def _ffn_kernel(h_ref, g_ref, w1_ref, w3_ref, w2_ref, fg_ref, o_ref, *, final_norm):
    x = h_ref[...]
    ms = jnp.mean(x * x, axis=-1, keepdims=True)
    hn = (x * lax.rsqrt(ms + NORM_EPS) * g_ref[...]).astype(BF16)
    a = _dot(hn, w1_ref[...])
    b = _dot(hn, w3_ref[...])
    y = x + _dot((a * jax.nn.sigmoid(a) * b).astype(BF16), w2_ref[...])
    if final_norm:
        ms = jnp.mean(y * y, axis=-1, keepdims=True)
        y = y * lax.rsqrt(ms + NORM_EPS) * fg_ref[...]
    o_ref[...] = y


def _ffn(h, g, w1, w3, w2, final_g, *, final_norm):
    n, d = h.shape
    f = w1.shape[1]
    tm = min(ROW_TILE, n)
    return pl.pallas_call(
        functools.partial(_ffn_kernel, final_norm=final_norm),
        grid=(n // tm,),
        in_specs=[
            pl.BlockSpec((tm, d), lambda i: (i, 0)),
            _resident((1, d)),
            _resident((d, f)),
            _resident((d, f)),
            _resident((f, d)),
            _resident((1, d)),
        ],
        out_specs=pl.BlockSpec((tm, d), lambda i: (i, 0)),
        out_shape=jax.ShapeDtypeStruct((n, d), F32),
        compiler_params=_params("parallel"),
        name="ffn_final" if final_norm else "ffn",
    )(h, g.reshape(1, d), w1.astype(BF16), w3.astype(BF16), w2.astype(BF16), final_g.reshape(1, d))


def _gate_weight(w_gate_cols):
    d, c = w_gate_cols.shape
    return jnp.zeros((d, LANES), F32).at[:, :c].set(w_gate_cols).astype(BF16)


def _gate_param_rows(first, second):
    t = jnp.zeros((SUBLANES, LANES), F32)
    if first is not None:
        t = t.at[0, first[0]:first[0] + 8].set(first[1].reshape(-1))
    if second is not None:
        t = t.at[1, second[0]:second[0] + 8].set(second[1].reshape(-1))
    return t


def _even_layer(h, j, p, *, batch, seq):
    main = 2 * CONV_CH + 4 * DN_WIDTH
    w_in = p["ev_w_in"][j]
    gate_params = jnp.zeros((SUBLANES, LANES), F32)
    gate_params = gate_params.at[0, 8:16].set(p["ev_dn_a_log"][j].reshape(-1))
    gate_params = gate_params.at[1, 8:16].set(p["ev_dn_dt_bias"][j].reshape(-1))
    z, zg = _in_proj(h, p["mix_norm_g"][2 * j], w_in[:, :main].astype(BF16), _gate_weight(w_in[:, main:]),
                     gate_params, even=True)
    conv = _conv_glu(z, p["ev_conv_w"][j], p["ev_conv_b"][j], batch=batch, seq=seq)
    qkv = _conv_qkv(z, p["ev_dn_conv_w"][j], batch=batch, seq=seq, col0=2 * CONV_CH)
    o_fwd, o_bwd = _delta_scan(qkv, zg, batch=batch, seq=seq)
    return _even_out(h, conv, o_fwd, o_bwd, z, p["ev_conv_ln_g"][j], p["ev_conv_ln_b"][j], p["ev_dn_norm_g"][j],
                     p["ev_w_out"][j], gate_col=2 * CONV_CH + 3 * DN_WIDTH)


def _odd_layer(h, j, p, *, batch, seq):
    main = 2 * SG_WIDTH + 2 * ML_QK_WIDTH + 2 * ML_V_WIDTH
    w_in = p["od_w_in"][j]
    gate_params = jnp.zeros((SUBLANES, LANES), F32)
    gate_params = gate_params.at[0, 0:8].set(p["od_ml_i_bias"][j].reshape(-1))
    gate_params = gate_params.at[1, 8:16].set(p["od_ml_f_bias"][j].reshape(-1))
    z, zg = _in_proj(h, p["mix_norm_g"][2 * j + 1], w_in[:, :main].astype(BF16), _gate_weight(w_in[:, main:]),
                     gate_params, even=False)
    q_col = 2 * SG_WIDTH
    k_col = q_col + ML_QK_WIDTH
    v_col = k_col + ML_QK_WIDTH
    o_col = v_col + ML_V_WIDTH
    h_fwd, h_bwd = _mlstm_scan(z, zg, batch=batch, seq=seq, q_col=q_col, k_col=k_col, v_col=v_col)
    return _odd_out(h, z, h_fwd, h_bwd, p["od_sg_ln_g"][j], p["od_sg_ln_b"][j], p["od_sg_w"][j], p["od_sg_b"][j],
                    p["od_ml_norm_g"][j], p["od_w_out"][j], u_col=0, v_col=SG_WIDTH, o_col=o_col)


def kernel(x, mix_norm_g, ev_w_in, ev_conv_w, ev_conv_b, ev_conv_ln_g, ev_conv_ln_b, ev_dn_conv_w, ev_dn_a_log, ev_dn_dt_bias, ev_dn_norm_g, ev_w_out, od_w_in, od_sg_ln_g, od_sg_ln_b, od_sg_w, od_sg_b, od_ml_i_bias, od_ml_f_bias, od_ml_norm_g, od_w_out, ffn_norm_g, ffn_w1, ffn_w3, ffn_w2, final_norm_g):
    p = dict(mix_norm_g=mix_norm_g, ev_w_in=ev_w_in, ev_conv_w=ev_conv_w, ev_conv_b=ev_conv_b,
             ev_conv_ln_g=ev_conv_ln_g, ev_conv_ln_b=ev_conv_ln_b, ev_dn_conv_w=ev_dn_conv_w,
             ev_dn_a_log=ev_dn_a_log, ev_dn_dt_bias=ev_dn_dt_bias, ev_dn_norm_g=ev_dn_norm_g, ev_w_out=ev_w_out,
             od_w_in=od_w_in, od_sg_ln_g=od_sg_ln_g, od_sg_ln_b=od_sg_ln_b, od_sg_w=od_sg_w, od_sg_b=od_sg_b,
             od_ml_i_bias=od_ml_i_bias, od_ml_f_bias=od_ml_f_bias, od_ml_norm_g=od_ml_norm_g, od_w_out=od_w_out)
    batch, seq, d = x.shape
    depth = mix_norm_g.shape[0]
    h = x.reshape(batch * seq, d)
    for layer in range(depth):
        j = layer // 2
        if layer % 2 == 0:
            h = _even_layer(h, j, p, batch=batch, seq=seq)
        else:
            h = _odd_layer(h, j, p, batch=batch, seq=seq)
        h = _ffn(h, ffn_norm_g[layer], ffn_w1[layer], ffn_w3[layer], ffn_w2[layer], final_norm_g,
                 final_norm=layer == depth - 1)
    return h.reshape(batch, seq, d)
```

```python
import functools

import jax
import jax.numpy as jnp
from jax import lax
from jax.experimental import pallas as pl
from jax.experimental.pallas import tpu as pltpu

NORM_EPS = 1e-6
LN_EPS = 1e-5
N_DIR = 2

CONV_CH = 512
DN_HEADS = 4
DN_HEAD_DIM = 128
DN_WIDTH = DN_HEADS * DN_HEAD_DIM
SG_GROUPS = 4
SG_GROUP_DIM = 128
SG_WIDTH = SG_GROUPS * SG_GROUP_DIM
SG_CHUNK = 128
ML_HEADS = 4
ML_QK_DIM = 64
ML_V_DIM = 128
ML_QK_WIDTH = ML_HEADS * ML_QK_DIM
ML_V_WIDTH = ML_HEADS * ML_V_DIM

LANES = 128
SUBLANES = 8
VMEM_LIMIT_BYTES = 56 * 1024 * 1024

SCAN_CHUNK = 64
SCAN_STEP_CHUNKS = 2
PREP_GROUP = 128
PREP_ROWS = 512
PREP_STEP_GROUPS = 2
ROW_TILE = 512
CONV_TILE = 128
QKV_CONV_TILES = 4
CONV_PAD = 16
NEG_BIG = -1e30

BF16 = jnp.bfloat16
F32 = jnp.float32


def _params(*sem):
    return pltpu.CompilerParams(dimension_semantics=sem, vmem_limit_bytes=VMEM_LIMIT_BYTES)


def _resident(shape):
    return pl.BlockSpec(shape, lambda *_: (0,) * len(shape), pipeline_mode=pl.Buffered(1))


def _dot(a, b):
    return jnp.dot(a, b, preferred_element_type=F32)


def _dot_nt(a, b):
    return lax.dot_general(a, b, (((1,), (1,)), ((), ())), preferred_element_type=F32)


def _dot_tn(a, b):
    return lax.dot_general(a, b, (((0,), (0,)), ((), ())), preferred_element_type=F32)


def _exact_dot01(t01, x):
    x1 = x.astype(BF16)
    r1 = x - x1.astype(F32)
    x2 = r1.astype(BF16)
    x3 = (r1 - x2.astype(F32)).astype(BF16)
    return _dot(t01, x1) + _dot(t01, x2) + _dot(t01, x3)


def _exact_dot01_r(x, t01):
    x1 = x.astype(BF16)
    r1 = x - x1.astype(F32)
    x2 = r1.astype(BF16)
    x3 = (r1 - x2.astype(F32)).astype(BF16)
    return _dot(x1, t01) + _dot(x2, t01) + _dot(x3, t01)


def _in_proj_kernel(h_ref, g_ref, w_ref, wg_ref, gp_ref, zm_ref, zg_ref, *, even):
    x = h_ref[...]
    ms = jnp.mean(x * x, axis=-1, keepdims=True)
    hn = (x * lax.rsqrt(ms + NORM_EPS) * g_ref[...]).astype(BF16)
    zm_ref[...] = _dot(hn, w_ref[...])
    zg = _dot(hn, wg_ref[...])
    p0 = gp_ref[0:1, :]
    p1 = gp_ref[1:2, :]
    lane = lax.broadcasted_iota(jnp.int32, zg.shape, 1)
    if even:
        first = jax.nn.sigmoid(zg)
        second = -jnp.exp(p0) * jax.nn.softplus(zg + p1)
    else:
        first = zg + p0
        second = jax.nn.log_sigmoid(zg + p1)
    zg_ref[...] = jnp.where(lane < 8, first, second)


def _in_proj(h, g, w_main, w_gate, gate_params, *, even):
    n, d = h.shape
    c = w_main.shape[1]
    tm = min(ROW_TILE, n)
    return pl.pallas_call(
        functools.partial(_in_proj_kernel, even=even),
        grid=(n // tm,),
        in_specs=[
            pl.BlockSpec((tm, d), lambda i: (i, 0)),
            _resident((1, d)),
            _resident((d, c)),
            _resident((d, LANES)),
            _resident((SUBLANES, LANES)),
        ],
        out_specs=[
            pl.BlockSpec((tm, c), lambda i: (i, 0)),
            pl.BlockSpec((tm, LANES), lambda i: (i, 0)),
        ],
        out_shape=[jax.ShapeDtypeStruct((n, c), F32), jax.ShapeDtypeStruct((n, LANES), F32)],
        compiler_params=_params("parallel"),
        name="in_proj_even" if even else "in_proj_odd",
    )(h, g.reshape(1, d), w_main, w_gate, gate_params)


def _conv_taps(pad_ref, w_ref, o_ref, bias, *, seq, width, post, tiles_per_step=1):
    half = width // 2
    tiles_per_step = min(tiles_per_step, seq // CONV_TILE)
    step_rows = CONV_TILE * tiles_per_step

    def body(i, carry):
        t0 = pl.multiple_of(i * step_rows, step_rows)
        accs = []
        for s in range(tiles_per_step):
            acc = jnp.zeros((CONV_TILE, LANES), F32)
            for j in range(width):
                acc = acc + pad_ref[pl.ds(t0 + (s * CONV_TILE + CONV_PAD - half + j), CONV_TILE), :] * w_ref[j:j + 1, :]
            accs.append(acc if bias is None else acc + bias)
        for s, out in enumerate(post(accs)):
            o_ref[pl.ds(t0 + s * CONV_TILE, CONV_TILE), :] = out.astype(o_ref.dtype)
        return carry

    lax.fori_loop(0, seq // step_rows, body, 0)


def _fill_padded(pad_ref, x, seq):
    zeros = jnp.zeros((CONV_PAD, LANES), F32)
    pad_ref[0:CONV_PAD, :] = zeros
    pad_ref[CONV_PAD + seq:CONV_PAD + seq + CONV_PAD, :] = zeros
    pad_ref[CONV_PAD:CONV_PAD + seq, :] = x


def _conv_glu_kernel(av_ref, ag_ref, w_ref, b_ref, o_ref, pad_ref, *, seq, width):
    _fill_padded(pad_ref, av_ref[...] * jax.nn.sigmoid(ag_ref[...]), seq)
    _conv_taps(pad_ref, w_ref, o_ref, b_ref[...], seq=seq, width=width, post=lambda tiles: tiles)


def _conv_glu(z, conv_w, conv_b, *, batch, seq):
    width = conv_w.shape[0]
    nblk = CONV_CH // LANES
    return pl.pallas_call(
        functools.partial(_conv_glu_kernel, seq=seq, width=width),
        grid=(batch, nblk),
        in_specs=[
            pl.BlockSpec((seq, LANES), lambda b, c: (b, c)),
            pl.BlockSpec((seq, LANES), lambda b, c: (b, nblk + c)),
            pl.BlockSpec((width, LANES), lambda b, c: (0, c)),
            pl.BlockSpec((1, LANES), lambda b, c: (0, c)),
        ],
        out_specs=pl.BlockSpec((seq, LANES), lambda b, c: (b, c)),
        out_shape=jax.ShapeDtypeStruct((batch * seq, CONV_CH), F32),
        scratch_shapes=[pltpu.VMEM((seq + 2 * CONV_PAD, LANES), F32)],
        compiler_params=_params("parallel", "parallel"),
        name="conv_glu",
    )(z, z, conv_w, conv_b.reshape(1, CONV_CH))


def _conv_qkv_kernel(x_ref, w_ref, o_ref, pad_ref, *, seq, width, n_norm_blocks):
    _fill_padded(pad_ref, x_ref[...], seq)
    normalise = pl.program_id(1) < n_norm_blocks

    def post(tiles):
        ys = [a * jax.nn.sigmoid(a) for a in tiles]
        sq = [jnp.sum(y * y, axis=-1, keepdims=True) for y in ys]
        inv = [lax.rsqrt(s + NORM_EPS) for s in sq]
        return [jnp.where(normalise, y * r, y) for y, r in zip(ys, inv)]

    _conv_taps(pad_ref, w_ref, o_ref, None, seq=seq, width=width, post=post, tiles_per_step=QKV_CONV_TILES)


def _conv_qkv(z, dn_conv_w, *, batch, seq, col0):
    width = dn_conv_w.shape[0]
    nblk = 3 * DN_WIDTH // LANES
    blk0 = col0 // LANES
    return pl.pallas_call(
        functools.partial(_conv_qkv_kernel, seq=seq, width=width, n_norm_blocks=2 * DN_HEADS),
        grid=(batch, nblk),
        in_specs=[
            pl.BlockSpec((seq, LANES), lambda b, c: (b, blk0 + c)),
            pl.BlockSpec((width, LANES), lambda b, c: (0, c)),
        ],
        out_specs=pl.BlockSpec((seq, LANES), lambda b, c: (b, c)),
        out_shape=jax.ShapeDtypeStruct((batch * seq, 3 * DN_WIDTH), BF16),
        scratch_shapes=[pltpu.VMEM((seq + 2 * CONV_PAD, LANES), F32)],
        compiler_params=_params("parallel", "parallel"),
        name="conv_qkv",
    )(z, dn_conv_w)


def _tri_masks(n, backward):
    row = lax.broadcasted_iota(jnp.int32, (n, n), 0)
    col = lax.broadcasted_iota(jnp.int32, (n, n), 1)
    d = jnp.where(backward, col - row, row - col)
    return row, col, d >= 0, d > 0, d <= 0


def _unit_tri_inverse(a, row, col, n=None):
    return _unit_tri_inverses([a], row, col, a.shape[0] if n is None else n)[0]


def _split_hi_lo(x):
    hi = x.astype(BF16)
    return jnp.concatenate([hi, (x - hi.astype(F32)).astype(BF16)], axis=-1)


def _bf16_all(mats):
    return [m.astype(BF16) for m in mats]


def _unit_tri_inverses(mats, row, col, n):
    eye = (row == col).astype(F32)
    same = (row >> 3) == (col >> 3)
    l8 = [jnp.where(same, a, 0.0) for a in mats]
    l8h = _bf16_all(l8)
    l2h = _bf16_all([_dot(p, p) for p in l8h])
    l4h = _bf16_all([_dot(p, p) for p in l2h])
    x = [eye - p for p in l8]
    x = [xi + _dot(xi.astype(BF16), p) for xi, p in zip(x, l2h)]
    x = [xi + _dot(xi.astype(BF16), p) for xi, p in zip(x, l4h)]
    shift = 3
    while (1 << shift) < n:
        same_next = (row >> (shift + 1)) == (col >> (shift + 1))
        off = jnp.logical_and(same_next, jnp.logical_not(same))
        ch = _bf16_all([jnp.where(off, a, 0.0) for a in mats])
        xh = _bf16_all(x)
        xch = _bf16_all([_dot(p, c) for p, c in zip(xh, ch)])
        x = [xi - _dot(p, q) for xi, p, q in zip(x, xch, xh)]
        same = same_next
        shift += 1
    return x


def _group_masks(backward):
    n = PREP_GROUP
    row = lax.broadcasted_iota(jnp.int32, (n, n), 0)
    col = lax.broadcasted_iota(jnp.int32, (n, n), 1)
    shift = SCAN_CHUNK.bit_length() - 1
    same = (row >> shift) == (col >> shift)
    d = (col - row) if backward else (row - col)
    land = jnp.logical_and
    return land(same, d >= 0), land(same, d > 0), land(same, d <= 0)


def _as01(mask):
    return jnp.where(mask, 1.0, 0.0).astype(BF16)


def _delta_prep_kernel(q_ref, k_ref, v_ref, gc_ref, gr_ref, u_ref, w_ref, qg_ref, kg_ref, at_ref):
    G, L = PREP_GROUP, SCAN_CHUNK
    row = lax.broadcasted_iota(jnp.int32, (G, G), 0)
    col = lax.broadcasted_iota(jnp.int32, (G, G), 1)
    shift = L.bit_length() - 1
    same01 = _as01((row >> shift) == (col >> shift))
    masks = [_group_masks(False), _group_masks(True)]
    cum_c = [_as01(m[0]) for m in masks]
    cum_r = [_as01(m[2]) for m in masks]
    fold = _as01((lax.broadcasted_iota(jnp.int32, (G, L), 0) & (L - 1)) == lax.broadcasted_iota(jnp.int32, (G, L), 1))
    pad = jnp.zeros((G, DN_HEAD_DIM - L), BF16)
    scale = DN_HEAD_DIM ** -0.5

    def step(i, carry):
        groups = range(PREP_STEP_GROUPS)
        rows = [pl.ds(pl.multiple_of((i * PREP_STEP_GROUPS + g) * G, G), G) for g in groups]
        gc = [gc_ref[r, :] for r in rows]
        gr = [gr_ref[i * PREP_STEP_GROUPS + g] for g in groups]
        g_tot = [_exact_dot01(same01, x) for x in gc]
        g_cum_c = [[_exact_dot01(cum_c[d], x) for d in range(N_DIR)] for x in gc]
        g_cum_r = [[_exact_dot01_r(x, cum_r[d]) for d in range(N_DIR)] for x in gr]
        heads = range(DN_HEADS)
        lanes = [slice(h * DN_HEAD_DIM, (h + 1) * DN_HEAD_DIM) for h in heads]
        gh = [(g, h) for g in groups for h in heads]
        chains = [(g, h, d) for g, h in gh for d in range(N_DIR)]
        k16 = {(g, h): k_ref[rows[g], lanes[h]] for g, h in gh}
        q = {(g, h): q_ref[rows[g], lanes[h]].astype(F32) for g, h in gh}
        k = {c: k16[c].astype(F32) for c in gh}
        kk = {c: _dot_nt(k16[c], k16[c]) for c in gh}
        qk = {c: _dot_nt((q[c] * scale).astype(BF16), k16[c]) for c in gh}
        beta, g_c, g_t, decay = {}, {}, {}, {}
        for c in chains:
            g, h, d = c
            cb = d * DN_HEADS + h
            cl = N_DIR * DN_HEADS + cb
            beta[c] = gc[g][:, cb:cb + 1]
            g_c[c] = g_cum_c[g][d][:, cl:cl + 1]
            g_t[c] = g_tot[g][:, cl:cl + 1]
            g_r = g_cum_r[g][d][cl:cl + 1, :]
            decay[c] = jnp.exp(jnp.where(masks[d][0], g_c[c] - g_r, NEG_BIG))
        a = [jnp.where(masks[c[2]][1], kk[c[:2]] * decay[c], 0.0) * beta[c] for c in chains]
        x = _unit_tri_inverses(a, row, col, L)
        eg = {c: jnp.exp(g_c[c]) for c in chains}
        rhs = [jnp.concatenate([v_ref[rows[c[0]], lanes[c[1]]].astype(F32) * beta[c], k[c[:2]] * (beta[c] * eg[c])],
                               axis=-1).astype(BF16) for c in chains]
        uw = [_dot(xi.astype(BF16), r) for xi, r in zip(x, rhs)]
        attn = [jnp.where(masks[c[2]][0], qk[c[:2]] * decay[c], 0.0).astype(BF16) for c in chains]
        at = [_dot(p, fold) for p in attn]
        for c, uwi, ati in zip(chains, uw, at):
            g, h, d = c
            r, sl = rows[g], lanes[h]
            u_ref[d, r, sl] = uwi[:, :DN_HEAD_DIM].astype(BF16)
            w_ref[d, r, sl] = uwi[:, DN_HEAD_DIM:].astype(BF16)
            qg_ref[d, r, sl] = (q[g, h] * (scale * eg[c])).astype(BF16)
            kg_ref[d, r, sl] = (k[g, h] * jnp.exp(g_t[c] - g_c[c])).astype(BF16)
            at_ref[d, r, sl] = jnp.concatenate([ati.astype(BF16), pad], axis=-1)
        return carry

    lax.fori_loop(0, q_ref.shape[0] // (G * PREP_STEP_GROUPS), step, 0)


def _delta_prep(qkv, zg, *, batch, seq):
    n = batch * seq
    R, G = min(PREP_ROWS, seq), PREP_GROUP
    steps = seq // R
    gr = zg[:, :16].reshape(n // G, G, 16).transpose(0, 2, 1)
    rowblk = lambda c: (lambda b, r: (b * steps + r, c))
    out = jax.ShapeDtypeStruct((N_DIR, n, DN_WIDTH), BF16)
    out_spec = pl.BlockSpec((N_DIR, R, DN_WIDTH), lambda b, r: (0, b * steps + r, 0))
    return pl.pallas_call(
        _delta_prep_kernel,
        grid=(batch, steps),
        in_specs=[
            pl.BlockSpec((R, DN_WIDTH), rowblk(0)),
            pl.BlockSpec((R, DN_WIDTH), rowblk(1)),
            pl.BlockSpec((R, DN_WIDTH), rowblk(2)),
            pl.BlockSpec((R, LANES), rowblk(0)),
            pl.BlockSpec((R // G, 16, G), lambda b, r: (b * steps + r, 0, 0)),
        ],
        out_specs=[out_spec] * 5,
        out_shape=[out] * 5,
        compiler_params=_params("parallel", "parallel"),
        name="delta_prep",
    )(qkv, qkv, qkv, zg, gr)


def _delta_scan_kernel(uf, wf, qgf, kgf, atf, zgf, ub, wb, qgb, kgb, atb, zgb, of_ref, ob_ref, s_ref, *, batch):
    L, C = SCAN_CHUNK, SCAN_STEP_CHUNKS

    @pl.when(pl.program_id(0) == 0)
    def _():
        s_ref[...] = jnp.zeros_like(s_ref)

    dirs = ((uf, wf, qgf, kgf, atf, zgf, of_ref), (ub, wb, qgb, kgb, atb, zgb, ob_ref))

    chains = [(d, b, h) for d in range(N_DIR) for b in range(batch) for h in range(DN_HEADS)]
    state = {c: s_ref[c] for c in chains}
    for step in range(C):
        chunk = {0: step, 1: C - 1 - step}
        rows = {d: slice(chunk[d] * L, (chunk[d] + 1) * L) for d in range(N_DIR)}
        egt = {(d, b): jnp.exp(jnp.sum(dirs[d][5][b, rows[d], :], axis=0, keepdims=True))
               for d in range(N_DIR) for b in range(batch)}
        ws, v_new, av = {}, {}, {}
        for c in chains:
            d, b, h = c
            sl = slice(h * DN_HEAD_DIM, (h + 1) * DN_HEAD_DIM)
            wq = jnp.concatenate([dirs[d][1][b, rows[d], sl], dirs[d][2][b, rows[d], sl]], axis=0)
            ws[c] = _dot(wq, state[c].astype(BF16))
        for c in chains:
            d, b, h = c
            sl = slice(h * DN_HEAD_DIM, (h + 1) * DN_HEAD_DIM)
            v_new[c] = (dirs[d][0][b, rows[d], sl].astype(F32) - ws[c][:L]).astype(BF16)
        for c in chains:
            d, b, h = c
            at = dirs[d][4][b, rows[d], h * DN_HEAD_DIM:h * DN_HEAD_DIM + L]
            av[c] = _dot(at, v_new[c])
        for c in chains:
            d, b, h = c
            sl = slice(h * DN_HEAD_DIM, (h + 1) * DN_HEAD_DIM)
            cl = (N_DIR + d) * DN_HEADS + h
            dirs[d][6][b, rows[d], sl] = (ws[c][L:] + av[c]).astype(BF16)
            state[c] = state[c] * egt[d, b][:, cl:cl + 1] + _dot_tn(dirs[d][3][b, rows[d], sl], v_new[c])
    for c in chains:
        s_ref[c] = state[c]


def _scan_chunk_index(i, n, *, batch, n_chunks):
    b = i % batch
    backward = i // batch
    return b * n_chunks + n + backward * (n_chunks - 1 - 2 * n)


def _gate_layouts(zg, *, batch, seq):
    n_chunks = seq // SCAN_CHUNK
    g = zg[:, :16].reshape(batch * seq, 2, N_DIR, 4)
    cols = g.transpose(2, 0, 1, 3).reshape(N_DIR, batch * seq, 8)
    rows = cols.reshape(N_DIR, batch * n_chunks, SCAN_CHUNK, 8).transpose(0, 1, 3, 2)
    return cols, rows


def _delta_scan(qkv, zg, *, batch, seq):
    n = batch * seq
    T = SCAN_CHUNK * SCAN_STEP_CHUNKS
    steps = seq // T
    prepped = [t.reshape(N_DIR, batch, seq, DN_WIDTH) for t in _delta_prep(qkv, zg, batch=batch, seq=seq)]
    zg3 = zg.reshape(batch, seq, LANES)
    fwd = lambda s: s
    bwd = lambda s: steps - 1 - s
    specs = []
    for d, pos in ((0, fwd), (1, bwd)):
        specs += [pl.BlockSpec((None, batch, T, DN_WIDTH), lambda s, d=d, pos=pos: (d, 0, pos(s), 0))] * 5
        specs += [pl.BlockSpec((batch, T, LANES), lambda s, pos=pos: (0, pos(s), 0))]
    out = jax.ShapeDtypeStruct((batch, seq, DN_WIDTH), BF16)
    o_f, o_b = pl.pallas_call(
        functools.partial(_delta_scan_kernel, batch=batch),
        grid=(steps,),
        in_specs=specs,
        out_specs=[pl.BlockSpec((batch, T, DN_WIDTH), lambda s: (0, fwd(s), 0)),
                   pl.BlockSpec((batch, T, DN_WIDTH), lambda s: (0, bwd(s), 0))],
        out_shape=[out, out],
        scratch_shapes=[pltpu.VMEM((N_DIR, batch, DN_HEADS, DN_HEAD_DIM, DN_HEAD_DIM), F32)],
        compiler_params=_params("arbitrary"),
        name="delta_scan",
    )(*prepped, zg3, *prepped, zg3)
    return o_f.reshape(n, DN_WIDTH), o_b.reshape(n, DN_WIDTH)


ML_PAIRS = ML_HEADS // 2
ML_AUG = 2 * ML_V_DIM
ML_KV_ROWS = ML_PAIRS * 2 * ML_QK_DIM


def _lane_cols(cols, width):
    rows = cols[0].shape[0]
    lane = lax.broadcasted_iota(jnp.int32, (rows, width), 1)
    out = jnp.zeros((rows, width), F32)
    for j, c in enumerate(cols):
        out = jnp.where(lane == j, c, out)
    return out


def _mlstm_prep_kernel(q_ref, k_ref, v_ref, gc_ref, gr_ref, qs_ref, iv_ref, kv_ref, aux_ref):
    G, L = PREP_GROUP, SCAN_CHUNK
    row = lax.broadcasted_iota(jnp.int32, (G, G), 0)
    col = lax.broadcasted_iota(jnp.int32, (G, G), 1)
    shift = L.bit_length() - 1
    same01 = _as01((row >> shift) == (col >> shift))
    masks = [_group_masks(False), _group_masks(True)]
    cum_c = [_as01(m[0]) for m in masks]
    cum_r = [_as01(m[2]) for m in masks]
    lane = lax.broadcasted_iota(jnp.int32, (G, LANES), 1)
    first_head = lane < ML_QK_DIM
    lane_r = lax.broadcasted_iota(jnp.int32, (1, G), 1)
    row_c = lax.broadcasted_iota(jnp.int32, (G, 1), 0)
    ones_v = jnp.ones((G, ML_V_DIM), BF16)
    scale = ML_QK_DIM ** -0.5
    heads = range(ML_HEADS)
    chains = [(h, d) for h in heads for d in range(N_DIR)]

    def step(i, carry):
        groups = range(PREP_STEP_GROUPS)
        gidx = [i * PREP_STEP_GROUPS + g for g in groups]
        rows = [pl.ds(pl.multiple_of(x * G, G), G) for x in gidx]
        gc = [gc_ref[r, :] for r in rows]
        gr = [gr_ref[x] for x in gidx]
        bt_c = [_exact_dot01(same01, x) for x in gc]
        bt_r = [_exact_dot01_r(x, same01) for x in gr]
        bc_c = [[_exact_dot01(cum_c[d], x) for d in range(N_DIR)] for x in gc]
        bc_r = [[_exact_dot01_r(x, cum_r[d]) for d in range(N_DIR)] for x in gr]
        gps = [(g, p) for g in groups for p in range(ML_PAIRS)]
        ghs = [(g, h) for g in groups for h in heads]
        chains = [(g, h, d) for g, h in ghs for d in range(N_DIR)]
        qp = {(g, p): q_ref[rows[g], p * LANES:(p + 1) * LANES] * scale for g, p in gps}
        kp = {(g, p): k_ref[rows[g], p * LANES:(p + 1) * LANES] for g, p in gps}
        kp16 = {c: kp[c].astype(BF16) for c in gps}
        v16 = {(g, h): v_ref[rows[g], h * ML_V_DIM:(h + 1) * ML_V_DIM].astype(BF16) for g, h in ghs}
        for g, p in gps:
            qs_ref[rows[g], p * LANES:(p + 1) * LANES] = qp[g, p].astype(BF16)
        own = [first_head, jnp.logical_not(first_head)]
        scores = {(g, h): _dot_nt(jnp.where(own[h % 2], qp[g, h // 2], 0.0).astype(BF16), kp16[g, h // 2])
                  for g, h in ghs}
        col = lambda c: N_DIR * ML_HEADS + c[2] * ML_HEADS + c[1]
        b_c = {c: bc_c[c[0]][c[2]][:, col(c):col(c) + 1] for c in chains}
        b_t = {c: bt_c[c[0]][:, col(c):col(c) + 1] for c in chains}
        b_r = {c: bc_r[c[0]][c[2]][col(c):col(c) + 1, :] for c in chains}
        li_c = {c: gc[c[0]][:, col(c) - N_DIR * ML_HEADS:col(c) - N_DIR * ML_HEADS + 1] for c in chains}
        li_r = {c: gr[c[0]][col(c) - N_DIR * ML_HEADS:col(c) - N_DIR * ML_HEADS + 1, :] for c in chains}
        d_mat = {c: jnp.where(masks[c[2]][0], b_c[c] - b_r[c] + li_r[c], NEG_BIG) for c in chains}
        d_max = {c: jnp.max(d_mat[c], axis=-1, keepdims=True) for c in chains}
        pmat = {c: (jnp.exp(d_mat[c] - d_max[c]) * scores[c[:2]]).astype(BF16) for c in chains}
        w_end_r = {c: bt_r[c[0]][col(c):col(c) + 1, :] - b_r[c] + li_r[c] for c in chains}
        w_max = {}
        for c in chains:
            wm = jnp.full((G, 1), NEG_BIG, F32)
            for j in range(G // L):
                in_j = jnp.logical_and(lane_r >= j * L, lane_r < (j + 1) * L)
                wm_j = jnp.max(jnp.where(in_j, w_end_r[c], NEG_BIG), axis=-1, keepdims=True)
                wm = jnp.where(jnp.logical_and(row_c >= j * L, row_c < (j + 1) * L), wm_j, wm)
            w_max[c] = wm
        sw0 = {c: jnp.exp(b_t[c] - b_c[c] + li_c[c] - w_max[c]) for c in chains}
        iv = {c: _dot(pmat[c], jnp.concatenate([v16[c[:2]], ones_v], axis=-1)) for c in chains}
        for c in chains:
            g, h, d = c
            iv_ref[d, rows[g], h * ML_AUG:(h + 1) * ML_AUG] = iv[c].astype(BF16)
        for g in groups:
            for d in range(N_DIR):
                src = N_DIR * ML_HEADS + d * ML_HEADS
                aux_ref[d, rows[g], :LANES] = jnp.where(
                    lane < ML_HEADS, pltpu.roll(bc_c[g][d], LANES - src, 1),
                    jnp.where(lane < 2 * ML_HEADS, pltpu.roll(bt_c[g], LANES - src + ML_HEADS, 1), 0.0))
                aux_ref[d, rows[g], LANES:] = _lane_cols(
                    [d_max[g, h, d] for h in heads] + [w_max[g, h, d] for h in heads], LANES)
        gpd = [(g, p, d) for g, p in gps for d in range(N_DIR)]
        ks = {(g, p, d): (kp[g, p] * jnp.where(first_head, sw0[g, 2 * p, d], sw0[g, 2 * p + 1, d])).astype(BF16)
              for g, p, d in gpd}
        vcat = {(g, p): jnp.concatenate([v16[g, 2 * p], v16[g, 2 * p + 1], ones_v], axis=-1) for g, p in gps}
        pair_chunks = [(g, p, d, j) for g, p, d in gpd for j in range(G // L)]
        kv = [_dot_tn(ks[g, p, d][j * L:(j + 1) * L], vcat[g, p][j * L:(j + 1) * L]) for g, p, d, j in pair_chunks]
        for (g, p, d, j), t in zip(pair_chunks, kv):
            top = jnp.concatenate([t[:ML_QK_DIM, :ML_V_DIM], t[:ML_QK_DIM, 2 * ML_V_DIM:]], axis=-1)
            bot = jnp.concatenate([t[ML_QK_DIM:, ML_V_DIM:2 * ML_V_DIM], t[ML_QK_DIM:, 2 * ML_V_DIM:]], axis=-1)
            r0 = pl.multiple_of(gidx[g] * (G // L * ML_KV_ROWS) + (j * ML_PAIRS + p) * LANES, LANES)
            kv_ref[d, pl.ds(r0, LANES), :] = jnp.concatenate([top, bot], axis=0).astype(BF16)
        return carry

    lax.fori_loop(0, q_ref.shape[0] // (G * PREP_STEP_GROUPS), step, 0)


def _mlstm_prep(z, zg, *, batch, seq, q_col, k_col, v_col):
    n = batch * seq
    R, G, L = min(PREP_ROWS, seq), PREP_GROUP, SCAN_CHUNK
    steps = seq // R
    gr = zg[:, :16].reshape(n // G, G, 16).transpose(0, 2, 1)
    rowblk = lambda c: (lambda b, r: (b * steps + r, c))
    dirblk = lambda b, r: (0, b * steps + r, 0)
    kv_rows = ML_KV_ROWS // L
    return pl.pallas_call(
        _mlstm_prep_kernel,
        grid=(batch, steps),
        in_specs=[
            pl.BlockSpec((R, ML_QK_WIDTH), rowblk(q_col // ML_QK_WIDTH)),
            pl.BlockSpec((R, ML_QK_WIDTH), rowblk(k_col // ML_QK_WIDTH)),
            pl.BlockSpec((R, ML_V_WIDTH), rowblk(v_col // ML_V_WIDTH)),
            pl.BlockSpec((R, LANES), rowblk(0)),
            pl.BlockSpec((R // G, 16, G), lambda b, r: (b * steps + r, 0, 0)),
        ],
        out_specs=[
            pl.BlockSpec((R, ML_QK_WIDTH), rowblk(0)),
            pl.BlockSpec((N_DIR, R, ML_HEADS * ML_AUG), dirblk),
            pl.BlockSpec((N_DIR, R * kv_rows, ML_AUG), dirblk),
            pl.BlockSpec((N_DIR, R, 2 * LANES), dirblk),
        ],
        out_shape=[
            jax.ShapeDtypeStruct((n, ML_QK_WIDTH), BF16),
            jax.ShapeDtypeStruct((N_DIR, n, ML_HEADS * ML_AUG), BF16),
            jax.ShapeDtypeStruct((N_DIR, n * kv_rows, ML_AUG), BF16),
            jax.ShapeDtypeStruct((N_DIR, n, 2 * LANES), F32),
        ],
        compiler_params=_params("parallel", "parallel"),
        name="mlstm_prep",
    )(z, z, z, zg, gr)


def _mlstm_scan_kernel(qf, ivf, kvf, auxf, qb, ivb, kvb, auxb, of_ref, ob_ref, c_ref, m_ref, *, batch):
    L, C = SCAN_CHUNK, SCAN_STEP_CHUNKS

    @pl.when(pl.program_id(0) == 0)
    def _():
        c_ref[...] = jnp.zeros_like(c_ref)
        m_ref[...] = jnp.zeros_like(m_ref)

    dirs = ((qf, ivf, kvf, auxf, of_ref), (qb, ivb, kvb, auxb, ob_ref))
    lane = lax.broadcasted_iota(jnp.int32, (L, LANES), 1)
    keep = [jnp.where(lane < ML_QK_DIM, 1.0, 0.0).astype(BF16), jnp.where(lane < ML_QK_DIM, 0.0, 1.0).astype(BF16)]
    first_rows = lax.broadcasted_iota(jnp.int32, (2 * ML_QK_DIM, 1), 0) < ML_QK_DIM
    pair_chains = [(d, b, p) for d in range(N_DIR) for b in range(batch) for p in range(ML_PAIRS)]
    chains = [(d, b, h) for d in range(N_DIR) for b in range(batch) for h in range(ML_HEADS)]
    groups = [(d, b) for d in range(N_DIR) for b in range(batch)]
    H = ML_HEADS
    lane8 = lax.broadcasted_iota(jnp.int32, (SUBLANES, LANES), 1)
    sel_row = lax.broadcasted_iota(jnp.int32, (2 * LANES, LANES), 0) & (LANES - 1)
    lane_select = [_as01(sel_row == j) for j in range(2 * H)]
    state = {c: c_ref[c] for c in pair_chains}
    m_st = {g: m_ref[g] for g in groups}
    for step in range(C):
        chunk = {0: step, 1: C - 1 - step}
        rows = {d: slice(chunk[d] * L, (chunk[d] + 1) * L) for d in range(N_DIR)}
        qc = {}
        for c in pair_chains:
            d, b, p = c
            qp = dirs[d][0][b, rows[d], p * LANES:(p + 1) * LANES]
            qc[c] = _dot(jnp.concatenate([qp * keep[0], qp * keep[1]], axis=0), state[c].astype(BF16))
        w_prev, w_cur, floor = {}, {}, {}
        for g in groups:
            d, b = g
            aux = dirs[d][3][b, rows[d], :]
            x = aux[:, :LANES] + m_st[g][0:1, :]
            y = jnp.maximum(x, aux[:, LANES:])
            w_prev[g] = jnp.exp(x - y)
            w_cur[g] = jnp.exp(aux[:, LANES:] - y)
            floor[g] = jnp.exp(-y)
            y0 = jnp.broadcast_to(y[0:1, :], (SUBLANES, LANES))
            m_st[g] = jnp.where(lane8 < H, pltpu.roll(y0, LANES - H, 1), jnp.where(lane8 < 2 * H, y0, 0.0))
        tiles = [t[g] for g in groups for t in (w_prev, w_cur, floor)]
        per_row = _split_hi_lo(jnp.concatenate(tiles, axis=0))
        per_chunk = _split_hi_lo(jnp.concatenate([t[0:1, :] for t in tiles] + [tiles[0][0:SUBLANES, :]], axis=0))
        bc_row, bc_chunk = [], []
        for p in range(ML_PAIRS):
            both = _dot(per_row, jnp.concatenate([lane_select[2 * p], lane_select[2 * p + 1]], axis=-1))
            bc_row += [both[:, :LANES], both[:, LANES:]]
            both = _dot(per_chunk, jnp.concatenate([lane_select[H + 2 * p], lane_select[H + 2 * p + 1]], axis=-1))
            bc_chunk += [both[:, :LANES], both[:, LANES:]]
        numden = {}
        for c in chains:
            d, b, h = c
            r0 = (h % 2) * L
            t0 = 3 * groups.index((d, b)) * L
            wp = bc_row[h][t0:t0 + L, :]
            wc = bc_row[h][t0 + L:t0 + 2 * L, :]
            iv = dirs[d][1][b, rows[d], h * ML_AUG:(h + 1) * ML_AUG].astype(F32)
            qch = qc[d, b, h // 2][r0:r0 + L, :]
            numden[c] = jnp.concatenate([wp * qch[:, :ML_V_DIM] + wc * iv[:, :ML_V_DIM],
                                         wp * qch[:, ML_V_DIM:] + wc * iv[:, ML_V_DIM:]], axis=-1)
        for c in chains:
            d, b, h = c
            t0 = 3 * groups.index((d, b)) * L
            den = jnp.maximum(jnp.abs(numden[c][:, ML_V_DIM:]), bc_row[h][t0 + 2 * L:t0 + 3 * L, :])
            dirs[d][4][b, rows[d], h * ML_V_DIM:(h + 1) * ML_V_DIM] = (numden[c][:, :ML_V_DIM] / den).astype(BF16)
        for c in pair_chains:
            d, b, p = c
            r0 = (chunk[d] * ML_PAIRS + p) * LANES
            kv = dirs[d][2][b, r0:r0 + LANES, :].astype(F32)
            t0 = 3 * groups.index((d, b))
            cw = jnp.where(first_rows, bc_chunk[2 * p][t0:t0 + 1, :], bc_chunk[2 * p + 1][t0:t0 + 1, :])
            iw = jnp.where(first_rows, bc_chunk[2 * p][t0 + 1:t0 + 2, :], bc_chunk[2 * p + 1][t0 + 1:t0 + 2, :])
            state[c] = jnp.concatenate([cw * state[c][:, :ML_V_DIM] + iw * kv[:, :ML_V_DIM],
                                        cw * state[c][:, ML_V_DIM:] + iw * kv[:, ML_V_DIM:]], axis=-1)
    for c in pair_chains:
        c_ref[c] = state[c]
    for g in groups:
        m_ref[g] = m_st[g]


def _mlstm_scan(z, zg, *, batch, seq, q_col, k_col, v_col):
    n = batch * seq
    T = SCAN_CHUNK * SCAN_STEP_CHUNKS
    steps = seq // T
    kv_rows = ML_KV_ROWS // SCAN_CHUNK
    qs, iv, kv, aux = _mlstm_prep(z, zg, batch=batch, seq=seq, q_col=q_col, k_col=k_col, v_col=v_col)
    qs = qs.reshape(batch, seq, ML_QK_WIDTH)
    iv = iv.reshape(N_DIR, batch, seq, ML_HEADS * ML_AUG)
    kv = kv.reshape(N_DIR, batch, seq * kv_rows, ML_AUG)
    aux = aux.reshape(N_DIR, batch, seq, 2 * LANES)
    fwd = lambda s: s
    bwd = lambda s: steps - 1 - s
    specs = []
    for d, pos in ((0, fwd), (1, bwd)):
        specs += [
            pl.BlockSpec((batch, T, ML_QK_WIDTH), lambda s, pos=pos: (0, pos(s), 0)),
            pl.BlockSpec((None, batch, T, ML_HEADS * ML_AUG), lambda s, d=d, pos=pos: (d, 0, pos(s), 0)),
            pl.BlockSpec((None, batch, T * kv_rows, ML_AUG), lambda s, d=d, pos=pos: (d, 0, pos(s), 0)),
            pl.BlockSpec((None, batch, T, 2 * LANES), lambda s, d=d, pos=pos: (d, 0, pos(s), 0)),
        ]
    out = jax.ShapeDtypeStruct((batch, seq, ML_V_WIDTH), BF16)
    h_f, h_b = pl.pallas_call(
        functools.partial(_mlstm_scan_kernel, batch=batch),
        grid=(steps,),
        in_specs=specs,
        out_specs=[pl.BlockSpec((batch, T, ML_V_WIDTH), lambda s: (0, fwd(s), 0)),
                   pl.BlockSpec((batch, T, ML_V_WIDTH), lambda s: (0, bwd(s), 0))],
        out_shape=[out, out],
        scratch_shapes=[
            pltpu.VMEM((N_DIR, batch, ML_PAIRS, 2 * ML_QK_DIM, ML_AUG), F32),
            pltpu.VMEM((N_DIR, batch, SUBLANES, LANES), F32),
        ],
        compiler_params=_params("arbitrary"),
        name="mlstm_scan",
    )(qs, iv, kv, aux, qs, iv, kv, aux)
    return h_f.reshape(n, ML_V_WIDTH), h_b.reshape(n, ML_V_WIDTH)


def _head_rms_norm(x, g, n_heads, head_dim):
    parts = []
    for h in range(n_heads):
        xh = x[:, h * head_dim:(h + 1) * head_dim]
        ms = jnp.mean(xh * xh, axis=-1, keepdims=True)
        parts.append(xh * lax.rsqrt(ms + NORM_EPS) * g)
    return jnp.concatenate(parts, axis=-1)


def _layer_norm(x, g, b):
    mu = jnp.mean(x, axis=-1, keepdims=True)
    xc = x - mu
    var = jnp.mean(xc * xc, axis=-1, keepdims=True)
    return xc * lax.rsqrt(var + LN_EPS) * g + b


def _even_out_kernel(h_ref, c_ref, of_ref, ob_ref, gate_ref, lng_ref, lnb_ref, ng_ref, wa_ref, wb_ref, o_ref):
    ya = _layer_norm(c_ref[...], lng_ref[...], lnb_ref[...])
    ya = ya * jax.nn.sigmoid(ya)
    o = _head_rms_norm(of_ref[...].astype(F32) + ob_ref[...].astype(F32), ng_ref[...], DN_HEADS, DN_HEAD_DIM)
    gate = gate_ref[...]
    o = o * (gate * jax.nn.sigmoid(gate))
    o_ref[...] = h_ref[...] + _dot(ya.astype(BF16), wa_ref[...]) + _dot(o.astype(BF16), wb_ref[...])


def _even_out(h, conv, o_fwd, o_bwd, z, ln_g, ln_b, norm_g, w_out, *, gate_col):
    n, d = h.shape
    tm = min(ROW_TILE, n)
    gb = gate_col // DN_WIDTH
    row = lambda i: (i, 0)
    return pl.pallas_call(
        _even_out_kernel,
        grid=(n // tm,),
        in_specs=[
            pl.BlockSpec((tm, d), row),
            pl.BlockSpec((tm, CONV_CH), row),
            pl.BlockSpec((tm, DN_WIDTH), row),
            pl.BlockSpec((tm, DN_WIDTH), row),
            pl.BlockSpec((tm, DN_WIDTH), lambda i: (i, gb)),
            _resident((1, CONV_CH)),
            _resident((1, CONV_CH)),
            _resident((1, DN_HEAD_DIM)),
            _resident((CONV_CH, d)),
            _resident((DN_WIDTH, d)),
        ],
        out_specs=pl.BlockSpec((tm, d), row),
        out_shape=jax.ShapeDtypeStruct((n, d), F32),
        compiler_params=_params("parallel"),
        name="even_out",
    )(h, conv, o_fwd, o_bwd, z, ln_g.reshape(1, -1), ln_b.reshape(1, -1), norm_g.reshape(1, -1),
      w_out[:CONV_CH].astype(BF16), w_out[CONV_CH:].astype(BF16))


def _odd_out_kernel(h_ref, u_ref, vp_ref, hf_ref, hb_ref, op_ref, lng_ref, lnb_ref, sgw_ref, sgb_ref, ng_ref,
                    wa_ref, wb_ref, o_ref):
    tm = h_ref.shape[0]
    u = jax.nn.gelu(u_ref[...])
    vv = _layer_norm(jax.nn.gelu(vp_ref[...]), lng_ref[...], lnb_ref[...]).astype(BF16)
    sgb = sgb_ref[...]
    rows = []
    for c in range(tm // SG_CHUNK):
        parts = []
        for g in range(SG_GROUPS):
            blk = vv[c * SG_CHUNK:(c + 1) * SG_CHUNK, g * SG_GROUP_DIM:(g + 1) * SG_GROUP_DIM]
            parts.append(_dot(sgw_ref[g], blk) + sgb[:, g:g + 1])
        rows.append(jnp.concatenate(parts, axis=-1))
    yc = u * jnp.concatenate(rows, axis=0)
    hd = _head_rms_norm(hf_ref[...].astype(F32) + hb_ref[...].astype(F32), ng_ref[...], ML_HEADS, ML_V_DIM)
    hd = hd * jax.nn.sigmoid(op_ref[...])
    o_ref[...] = h_ref[...] + _dot(yc.astype(BF16), wa_ref[...]) + _dot(hd.astype(BF16), wb_ref[...])


def _odd_out(h, z, h_fwd, h_bwd, ln_g, ln_b, sg_w, sg_b, norm_g, w_out, *, u_col, v_col, o_col):
    n, d = h.shape
    tm = min(ROW_TILE, n)
    row = lambda i: (i, 0)
    return pl.pallas_call(
        _odd_out_kernel,
        grid=(n // tm,),
        in_specs=[
            pl.BlockSpec((tm, d), row),
            pl.BlockSpec((tm, SG_WIDTH), lambda i: (i, u_col // SG_WIDTH)),
            pl.BlockSpec((tm, SG_WIDTH), lambda i: (i, v_col // SG_WIDTH)),
            pl.BlockSpec((tm, ML_V_WIDTH), row),
            pl.BlockSpec((tm, ML_V_WIDTH), row),
            pl.BlockSpec((tm, ML_V_WIDTH), lambda i: (i, o_col // ML_V_WIDTH)),
            _resident((1, SG_WIDTH)),
            _resident((1, SG_WIDTH)),
            _resident((SG_GROUPS, SG_CHUNK, SG_CHUNK)),
            _resident((SG_CHUNK, SG_GROUPS)),
            _resident((1, ML_V_DIM)),
            _resident((SG_WIDTH, d)),
            _resident((ML_V_WIDTH, d)),
        ],
        out_specs=pl.BlockSpec((tm, d), row),
        out_shape=jax.ShapeDtypeStruct((n, d), F32),
        compiler_params=_params("parallel"),
        name="odd_out",
    )(h, z, z, h_fwd, h_bwd, z, ln_g.reshape(1, -1), ln_b.reshape(1, -1), sg_w.astype(BF16), sg_b.T,
      norm_g.reshape(1, -1), w_out[:SG_WIDTH].astype(BF16), w_out[SG_WIDTH:].astype(BF16))


def _ffn_kernel(h_ref, g_ref, w1_ref, w3_ref, w2_ref, fg_ref, o_ref, *, final_norm):
    x = h_ref[...]
    ms = jnp.mean(x * x, axis=-1, keepdims=True)
    hn = (x * lax.rsqrt(ms + NORM_EPS) * g_ref[...]).astype(BF16)
    a = _dot(hn, w1_ref[...])
    b = _dot(hn, w3_ref[...])
    y = x + _dot((a * jax.nn.sigmoid(a) * b).astype(BF16), w2_ref[...])
    if final_norm:
        ms = jnp.mean(y * y, axis=-1, keepdims=True)
        y = y * lax.rsqrt(ms + NORM_EPS) * fg_ref[...]
    o_ref[...] = y


def _ffn(h, g, w1, w3, w2, final_g, *, final_norm):
    n, d = h.shape
    f = w1.shape[1]
    tm = min(ROW_TILE, n)
    return pl.pallas_call(
        functools.partial(_ffn_kernel, final_norm=final_norm),
        grid=(n // tm,),
        in_specs=[
            pl.BlockSpec((tm, d), lambda i: (i, 0)),
            _resident((1, d)),
            _resident((d, f)),
            _resident((d, f)),
            _resident((f, d)),
            _resident((1, d)),
        ],
        out_specs=pl.BlockSpec((tm, d), lambda i: (i, 0)),
        out_shape=jax.ShapeDtypeStruct((n, d), F32),
        compiler_params=_params("parallel"),
        name="ffn_final" if final_norm else "ffn",
    )(h, g.reshape(1, d), w1.astype(BF16), w3.astype(BF16), w2.astype(BF16), final_g.reshape(1, d))


def _gate_weight(w_gate_cols):
    d, c = w_gate_cols.shape
    return jnp.zeros((d, LANES), F32).at[:, :c].set(w_gate_cols).astype(BF16)


def _gate_param_rows(first, second):
    t = jnp.zeros((SUBLANES, LANES), F32)
    if first is not None:
        t = t.at[0, first[0]:first[0] + 8].set(first[1].reshape(-1))
    if second is not None:
        t = t.at[1, second[0]:second[0] + 8].set(second[1].reshape(-1))
    return t


def _even_layer(h, j, p, *, batch, seq):
    main = 2 * CONV_CH + 4 * DN_WIDTH
    w_in = p["ev_w_in"][j]
    gate_params = jnp.zeros((SUBLANES, LANES), F32)
    gate_params = gate_params.at[0, 8:16].set(p["ev_dn_a_log"][j].reshape(-1))
    gate_params = gate_params.at[1, 8:16].set(p["ev_dn_dt_bias"][j].reshape(-1))
    z, zg = _in_proj(h, p["mix_norm_g"][2 * j], w_in[:, :main].astype(BF16), _gate_weight(w_in[:, main:]),
                     gate_params, even=True)
    conv = _conv_glu(z, p["ev_conv_w"][j], p["ev_conv_b"][j], batch=batch, seq=seq)
    qkv = _conv_qkv(z, p["ev_dn_conv_w"][j], batch=batch, seq=seq, col0=2 * CONV_CH)
    o_fwd, o_bwd = _delta_scan(qkv, zg, batch=batch, seq=seq)
    return _even_out(h, conv, o_fwd, o_bwd, z, p["ev_conv_ln_g"][j], p["ev_conv_ln_b"][j], p["ev_dn_norm_g"][j],
                     p["ev_w_out"][j], gate_col=2 * CONV_CH + 3 * DN_WIDTH)


def _odd_layer(h, j, p, *, batch, seq):
    main = 2 * SG_WIDTH + 2 * ML_QK_WIDTH + 2 * ML_V_WIDTH
    w_in = p["od_w_in"][j]
    gate_params = jnp.zeros((SUBLANES, LANES), F32)
    gate_params = gate_params.at[0, 0:8].set(p["od_ml_i_bias"][j].reshape(-1))
    gate_params = gate_params.at[1, 8:16].set(p["od_ml_f_bias"][j].reshape(-1))
    z, zg = _in_proj(h, p["mix_norm_g"][2 * j + 1], w_in[:, :main].astype(BF16), _gate_weight(w_in[:, main:]),
                     gate_params, even=False)
    q_col = 2 * SG_WIDTH
    k_col = q_col + ML_QK_WIDTH
    v_col = k_col + ML_QK_WIDTH
    o_col = v_col + ML_V_WIDTH
    h_fwd, h_bwd = _mlstm_scan(z, zg, batch=batch, seq=seq, q_col=q_col, k_col=k_col, v_col=v_col)
    return _odd_out(h, z, h_fwd, h_bwd, p["od_sg_ln_g"][j], p["od_sg_ln_b"][j], p["od_sg_w"][j], p["od_sg_b"][j],
                    p["od_ml_norm_g"][j], p["od_w_out"][j], u_col=0, v_col=SG_WIDTH, o_col=o_col)


def kernel(x, mix_norm_g, ev_w_in, ev_conv_w, ev_conv_b, ev_conv_ln_g, ev_conv_ln_b, ev_dn_conv_w, ev_dn_a_log, ev_dn_dt_bias, ev_dn_norm_g, ev_w_out, od_w_in, od_sg_ln_g, od_sg_ln_b, od_sg_w, od_sg_b, od_ml_i_bias, od_ml_f_bias, od_ml_norm_g, od_w_out, ffn_norm_g, ffn_w1, ffn_w3, ffn_w2, final_norm_g):
    p = dict(mix_norm_g=mix_norm_g, ev_w_in=ev_w_in, ev_conv_w=ev_conv_w, ev_conv_b=ev_conv_b,
             ev_conv_ln_g=ev_conv_ln_g, ev_conv_ln_b=ev_conv_ln_b, ev_dn_conv_w=ev_dn_conv_w,
             ev_dn_a_log=ev_dn_a_log, ev_dn_dt_bias=ev_dn_dt_bias, ev_dn_norm_g=ev_dn_norm_g, ev_w_out=ev_w_out,
             od_w_in=od_w_in, od_sg_ln_g=od_sg_ln_g, od_sg_ln_b=od_sg_ln_b, od_sg_w=od_sg_w, od_sg_b=od_sg_b,
             od_ml_i_bias=od_ml_i_bias, od_ml_f_bias=od_ml_f_bias, od_ml_norm_g=od_ml_norm_g, od_w_out=od_w_out)
    batch, seq, d = x.shape
    depth = mix_norm_g.shape[0]
    h = x.reshape(batch * seq, d)
    for layer in range(depth):
        j = layer // 2
        if layer % 2 == 0:
            h = _even_layer(h, j, p, batch=batch, seq=seq)
        else:
            h = _odd_layer(h, j, p, batch=batch, seq=seq)
        h = _ffn(h, ffn_norm_g[layer], ffn_w1[layer], ffn_w3[layer], ffn_w2[layer], final_norm_g,
                 final_norm=layer == depth - 1)
    return h.reshape(batch, seq, d)
```

```python
import functools

import jax
import jax.numpy as jnp
from jax import lax
from jax.experimental import pallas as pl
from jax.experimental.pallas import tpu as pltpu

NORM_EPS = 1e-6
LN_EPS = 1e-5
N_DIR = 2

CONV_CH = 512
DN_HEADS = 4
DN_HEAD_DIM = 128
DN_WIDTH = DN_HEADS * DN_HEAD_DIM
SG_GROUPS = 4
SG_GROUP_DIM = 128
SG_WIDTH = SG_GROUPS * SG_GROUP_DIM
SG_CHUNK = 128
ML_HEADS = 4
ML_QK_DIM = 64
ML_V_DIM = 128
ML_QK_WIDTH = ML_HEADS * ML_QK_DIM
ML_V_WIDTH = ML_HEADS * ML_V_DIM

LANES = 128
SUBLANES = 8
VMEM_LIMIT_BYTES = 56 * 1024 * 1024

SCAN_CHUNK = 64
SCAN_STEP_CHUNKS = 2
PREP_GROUP = 128
PREP_ROWS = 512
PREP_STEP_GROUPS = 2
ROW_TILE = 512
CONV_TILE = 128
QKV_CONV_TILES = 4
CONV_PAD = 16
NEG_BIG = -1e30

BF16 = jnp.bfloat16
F32 = jnp.float32


def _params(*sem):
    return pltpu.CompilerParams(dimension_semantics=sem, vmem_limit_bytes=VMEM_LIMIT_BYTES)


def _resident(shape):
    return pl.BlockSpec(shape, lambda *_: (0,) * len(shape), pipeline_mode=pl.Buffered(1))


def _dot(a, b):
    return jnp.dot(a, b, preferred_element_type=F32)


def _dot_nt(a, b):
    return lax.dot_general(a, b, (((1,), (1,)), ((), ())), preferred_element_type=F32)


def _dot_tn(a, b):
    return lax.dot_general(a, b, (((0,), (0,)), ((), ())), preferred_element_type=F32)


def _exact_dot01(t01, x):
    x1 = x.astype(BF16)
    r1 = x - x1.astype(F32)
    x2 = r1.astype(BF16)
    x3 = (r1 - x2.astype(F32)).astype(BF16)
    return _dot(t01, x1) + _dot(t01, x2) + _dot(t01, x3)


def _exact_dot01_r(x, t01):
    x1 = x.astype(BF16)
    r1 = x - x1.astype(F32)
    x2 = r1.astype(BF16)
    x3 = (r1 - x2.astype(F32)).astype(BF16)
    return _dot(x1, t01) + _dot(x2, t01) + _dot(x3, t01)


def _in_proj_kernel(h_ref, g_ref, w_ref, wg_ref, gp_ref, zm_ref, zg_ref, *, even):
    x = h_ref[...]
    ms = jnp.mean(x * x, axis=-1, keepdims=True)
    hn = (x * lax.rsqrt(ms + NORM_EPS) * g_ref[...]).astype(BF16)
    zm_ref[...] = _dot(hn, w_ref[...]).astype(zm_ref.dtype)
    zg = _dot(hn, wg_ref[...])
    p0 = gp_ref[0:1, :]
    p1 = gp_ref[1:2, :]
    lane = lax.broadcasted_iota(jnp.int32, zg.shape, 1)
    if even:
        first = jax.nn.sigmoid(zg)
        second = -jnp.exp(p0) * jax.nn.softplus(zg + p1)
    else:
        first = zg + p0
        second = jax.nn.log_sigmoid(zg + p1)
    zg_ref[...] = jnp.where(lane < 8, first, second)


def _in_proj(h, g, w_main, w_gate, gate_params, *, even):
    n, d = h.shape
    c = w_main.shape[1]
    tm = min(ROW_TILE, n)
    return pl.pallas_call(
        functools.partial(_in_proj_kernel, even=even),
        grid=(n // tm,),
        in_specs=[
            pl.BlockSpec((tm, d), lambda i: (i, 0)),
            _resident((1, d)),
            _resident((d, c)),
            _resident((d, LANES)),
            _resident((SUBLANES, LANES)),
        ],
        out_specs=[
            pl.BlockSpec((tm, c), lambda i: (i, 0)),
            pl.BlockSpec((tm, LANES), lambda i: (i, 0)),
        ],
        out_shape=[jax.ShapeDtypeStruct((n, c), BF16), jax.ShapeDtypeStruct((n, LANES), F32)],
        compiler_params=_params("parallel"),
        name="in_proj_even" if even else "in_proj_odd",
    )(h, g.reshape(1, d), w_main, w_gate, gate_params)


def _conv_taps(pad_ref, w_ref, o_ref, bias, *, seq, width, post, tiles_per_step=1):
    half = width // 2
    tiles_per_step = min(tiles_per_step, seq // CONV_TILE)
    step_rows = CONV_TILE * tiles_per_step

    def body(i, carry):
        t0 = pl.multiple_of(i * step_rows, step_rows)
        accs = []
        for s in range(tiles_per_step):
            acc = jnp.zeros((CONV_TILE, LANES), F32)
            for j in range(width):
                acc = acc + pad_ref[pl.ds(t0 + (s * CONV_TILE + CONV_PAD - half + j), CONV_TILE), :] * w_ref[j:j + 1, :]
            accs.append(acc if bias is None else acc + bias)
        for s, out in enumerate(post(accs)):
            o_ref[pl.ds(t0 + s * CONV_TILE, CONV_TILE), :] = out.astype(o_ref.dtype)
        return carry

    lax.fori_loop(0, seq // step_rows, body, 0)


def _fill_padded(pad_ref, x, seq):
    zeros = jnp.zeros((CONV_PAD, LANES), F32)
    pad_ref[0:CONV_PAD, :] = zeros
    pad_ref[CONV_PAD + seq:CONV_PAD + seq + CONV_PAD, :] = zeros
    pad_ref[CONV_PAD:CONV_PAD + seq, :] = x


def _conv_glu_kernel(av_ref, ag_ref, w_ref, b_ref, o_ref, pad_ref, *, seq, width):
    _fill_padded(pad_ref, av_ref[...].astype(F32) * jax.nn.sigmoid(ag_ref[...].astype(F32)), seq)
    _conv_taps(pad_ref, w_ref, o_ref, b_ref[...], seq=seq, width=width, post=lambda tiles: tiles)


def _conv_glu(z, conv_w, conv_b, *, batch, seq):
    width = conv_w.shape[0]
    nblk = CONV_CH // LANES
    return pl.pallas_call(
        functools.partial(_conv_glu_kernel, seq=seq, width=width),
        grid=(batch, nblk),
        in_specs=[
            pl.BlockSpec((seq, LANES), lambda b, c: (b, c)),
            pl.BlockSpec((seq, LANES), lambda b, c: (b, nblk + c)),
            pl.BlockSpec((width, LANES), lambda b, c: (0, c)),
            pl.BlockSpec((1, LANES), lambda b, c: (0, c)),
        ],
        out_specs=pl.BlockSpec((seq, LANES), lambda b, c: (b, c)),
        out_shape=jax.ShapeDtypeStruct((batch * seq, CONV_CH), BF16),
        scratch_shapes=[pltpu.VMEM((seq + 2 * CONV_PAD, LANES), F32)],
        compiler_params=_params("parallel", "parallel"),
        name="conv_glu",
    )(z, z, conv_w, conv_b.reshape(1, CONV_CH))


def _conv_qkv_kernel(x_ref, w_ref, o_ref, pad_ref, *, seq, width, n_norm_blocks):
    _fill_padded(pad_ref, x_ref[...].astype(F32), seq)
    normalise = pl.program_id(1) < n_norm_blocks

    def post(tiles):
        ys = [a * jax.nn.sigmoid(a) for a in tiles]
        sq = [jnp.sum(y * y, axis=-1, keepdims=True) for y in ys]
        inv = [lax.rsqrt(s + NORM_EPS) for s in sq]
        return [jnp.where(normalise, y * r, y) for y, r in zip(ys, inv)]

    _conv_taps(pad_ref, w_ref, o_ref, None, seq=seq, width=width, post=post, tiles_per_step=QKV_CONV_TILES)


def _conv_qkv(z, dn_conv_w, *, batch, seq, col0):
    width = dn_conv_w.shape[0]
    nblk = 3 * DN_WIDTH // LANES
    blk0 = col0 // LANES
    return pl.pallas_call(
        functools.partial(_conv_qkv_kernel, seq=seq, width=width, n_norm_blocks=2 * DN_HEADS),
        grid=(batch, nblk),
        in_specs=[
            pl.BlockSpec((seq, LANES), lambda b, c: (b, blk0 + c)),
            pl.BlockSpec((width, LANES), lambda b, c: (0, c)),
        ],
        out_specs=pl.BlockSpec((seq, LANES), lambda b, c: (b, c)),
        out_shape=jax.ShapeDtypeStruct((batch * seq, 3 * DN_WIDTH), BF16),
        scratch_shapes=[pltpu.VMEM((seq + 2 * CONV_PAD, LANES), F32)],
        compiler_params=_params("parallel", "parallel"),
        name="conv_qkv",
    )(z, dn_conv_w)


def _tri_masks(n, backward):
    row = lax.broadcasted_iota(jnp.int32, (n, n), 0)
    col = lax.broadcasted_iota(jnp.int32, (n, n), 1)
    d = jnp.where(backward, col - row, row - col)
    return row, col, d >= 0, d > 0, d <= 0


def _unit_tri_inverse(a, row, col, n=None):
    return _unit_tri_inverses([a], row, col, a.shape[0] if n is None else n)[0]


def _split_hi_lo(x):
    hi = x.astype(BF16)
    return jnp.concatenate([hi, (x - hi.astype(F32)).astype(BF16)], axis=-1)


def _bf16_all(mats):
    return [m.astype(BF16) for m in mats]


def _unit_tri_inverses(mats, row, col, n):
    eye = (row == col).astype(F32)
    same = (row >> 3) == (col >> 3)
    l8 = [jnp.where(same, a, 0.0) for a in mats]
    l8h = _bf16_all(l8)
    l2h = _bf16_all([_dot(p, p) for p in l8h])
    l4h = _bf16_all([_dot(p, p) for p in l2h])
    x = [eye - p for p in l8]
    x = [xi + _dot(xi.astype(BF16), p) for xi, p in zip(x, l2h)]
    x = [xi + _dot(xi.astype(BF16), p) for xi, p in zip(x, l4h)]
    shift = 3
    while (1 << shift) < n:
        same_next = (row >> (shift + 1)) == (col >> (shift + 1))
        off = jnp.logical_and(same_next, jnp.logical_not(same))
        ch = _bf16_all([jnp.where(off, a, 0.0) for a in mats])
        xh = _bf16_all(x)
        xch = _bf16_all([_dot(p, c) for p, c in zip(xh, ch)])
        x = [xi - _dot(p, q) for xi, p, q in zip(x, xch, xh)]
        same = same_next
        shift += 1
    return x


def _group_masks(backward):
    n = PREP_GROUP
    row = lax.broadcasted_iota(jnp.int32, (n, n), 0)
    col = lax.broadcasted_iota(jnp.int32, (n, n), 1)
    shift = SCAN_CHUNK.bit_length() - 1
    same = (row >> shift) == (col >> shift)
    d = (col - row) if backward else (row - col)
    land = jnp.logical_and
    return land(same, d >= 0), land(same, d > 0), land(same, d <= 0)


def _as01(mask):
    return jnp.where(mask, 1.0, 0.0).astype(BF16)


def _delta_prep_kernel(q_ref, k_ref, v_ref, gc_ref, gr_ref, u_ref, w_ref, qg_ref, kg_ref, at_ref):
    G, L = PREP_GROUP, SCAN_CHUNK
    row = lax.broadcasted_iota(jnp.int32, (G, G), 0)
    col = lax.broadcasted_iota(jnp.int32, (G, G), 1)
    shift = L.bit_length() - 1
    same01 = _as01((row >> shift) == (col >> shift))
    masks = [_group_masks(False), _group_masks(True)]
    cum_c = [_as01(m[0]) for m in masks]
    cum_r = [_as01(m[2]) for m in masks]
    fold = _as01((lax.broadcasted_iota(jnp.int32, (G, L), 0) & (L - 1)) == lax.broadcasted_iota(jnp.int32, (G, L), 1))
    pad = jnp.zeros((G, DN_HEAD_DIM - L), BF16)
    scale = DN_HEAD_DIM ** -0.5

    def step(i, carry):
        groups = range(PREP_STEP_GROUPS)
        rows = [pl.ds(pl.multiple_of((i * PREP_STEP_GROUPS + g) * G, G), G) for g in groups]
        gc = [gc_ref[r, :] for r in rows]
        gr = [gr_ref[i * PREP_STEP_GROUPS + g] for g in groups]
        g_tot = [_exact_dot01(same01, x) for x in gc]
        g_cum_c = [[_exact_dot01(cum_c[d], x) for d in range(N_DIR)] for x in gc]
        g_cum_r = [[_exact_dot01_r(x, cum_r[d]) for d in range(N_DIR)] for x in gr]
        heads = range(DN_HEADS)
        lanes = [slice(h * DN_HEAD_DIM, (h + 1) * DN_HEAD_DIM) for h in heads]
        gh = [(g, h) for g in groups for h in heads]
        chains = [(g, h, d) for g, h in gh for d in range(N_DIR)]
        k16 = {(g, h): k_ref[rows[g], lanes[h]] for g, h in gh}
        q = {(g, h): q_ref[rows[g], lanes[h]].astype(F32) for g, h in gh}
        k = {c: k16[c].astype(F32) for c in gh}
        kk = {c: _dot_nt(k16[c], k16[c]) for c in gh}
        qk = {c: _dot_nt((q[c] * scale).astype(BF16), k16[c]) for c in gh}
        beta, g_c, g_t, decay = {}, {}, {}, {}
        for c in chains:
            g, h, d = c
            cb = d * DN_HEADS + h
            cl = N_DIR * DN_HEADS + cb
            beta[c] = gc[g][:, cb:cb + 1]
            g_c[c] = g_cum_c[g][d][:, cl:cl + 1]
            g_t[c] = g_tot[g][:, cl:cl + 1]
            g_r = g_cum_r[g][d][cl:cl + 1, :]
            decay[c] = jnp.exp(jnp.where(masks[d][0], g_c[c] - g_r, NEG_BIG))
        a = [jnp.where(masks[c[2]][1], kk[c[:2]] * decay[c], 0.0) * beta[c] for c in chains]
        x = _unit_tri_inverses(a, row, col, L)
        eg = {c: jnp.exp(g_c[c]) for c in chains}
        rhs = [jnp.concatenate([v_ref[rows[c[0]], lanes[c[1]]].astype(F32) * beta[c], k[c[:2]] * (beta[c] * eg[c])],
                               axis=-1).astype(BF16) for c in chains]
        uw = [_dot(xi.astype(BF16), r) for xi, r in zip(x, rhs)]
        attn = [jnp.where(masks[c[2]][0], qk[c[:2]] * decay[c], 0.0).astype(BF16) for c in chains]
        at = [_dot(p, fold) for p in attn]
        for c, uwi, ati in zip(chains, uw, at):
            g, h, d = c
            r, sl = rows[g], lanes[h]
            u_ref[d, r, sl] = uwi[:, :DN_HEAD_DIM].astype(BF16)
            w_ref[d, r, sl] = uwi[:, DN_HEAD_DIM:].astype(BF16)
            qg_ref[d, r, sl] = (q[g, h] * (scale * eg[c])).astype(BF16)
            kg_ref[d, r, sl] = (k[g, h] * jnp.exp(g_t[c] - g_c[c])).astype(BF16)
            at_ref[d, r, sl] = jnp.concatenate([ati.astype(BF16), pad], axis=-1)
        return carry

    lax.fori_loop(0, q_ref.shape[0] // (G * PREP_STEP_GROUPS), step, 0)


def _delta_prep(qkv, zg, *, batch, seq):
    n = batch * seq
    R, G = min(PREP_ROWS, seq), PREP_GROUP
    steps = seq // R
    gr = zg[:, :16].reshape(n // G, G, 16).transpose(0, 2, 1)
    rowblk = lambda c: (lambda b, r: (b * steps + r, c))
    out = jax.ShapeDtypeStruct((N_DIR, n, DN_WIDTH), BF16)
    out_spec = pl.BlockSpec((N_DIR, R, DN_WIDTH), lambda b, r: (0, b * steps + r, 0))
    return pl.pallas_call(
        _delta_prep_kernel,
        grid=(batch, steps),
        in_specs=[
            pl.BlockSpec((R, DN_WIDTH), rowblk(0)),
            pl.BlockSpec((R, DN_WIDTH), rowblk(1)),
            pl.BlockSpec((R, DN_WIDTH), rowblk(2)),
            pl.BlockSpec((R, LANES), rowblk(0)),
            pl.BlockSpec((R // G, 16, G), lambda b, r: (b * steps + r, 0, 0)),
        ],
        out_specs=[out_spec] * 5,
        out_shape=[out] * 5,
        compiler_params=_params("parallel", "parallel"),
        name="delta_prep",
    )(qkv, qkv, qkv, zg, gr)


def _delta_scan_kernel(uf, wf, qgf, kgf, atf, zgf, ub, wb, qgb, kgb, atb, zgb, of_ref, ob_ref, s_ref, *, batch):
    L, C = SCAN_CHUNK, SCAN_STEP_CHUNKS

    @pl.when(pl.program_id(0) == 0)
    def _():
        s_ref[...] = jnp.zeros_like(s_ref)

    dirs = ((uf, wf, qgf, kgf, atf, zgf, of_ref), (ub, wb, qgb, kgb, atb, zgb, ob_ref))

    chains = [(d, b, h) for d in range(N_DIR) for b in range(batch) for h in range(DN_HEADS)]
    state = {c: s_ref[c] for c in chains}
    for step in range(C):
        chunk = {0: step, 1: C - 1 - step}
        rows = {d: slice(chunk[d] * L, (chunk[d] + 1) * L) for d in range(N_DIR)}
        egt = {(d, b): jnp.exp(jnp.sum(dirs[d][5][b, rows[d], :], axis=0, keepdims=True))
               for d in range(N_DIR) for b in range(batch)}
        ws, v_new, av = {}, {}, {}
        for c in chains:
            d, b, h = c
            sl = slice(h * DN_HEAD_DIM, (h + 1) * DN_HEAD_DIM)
            wq = jnp.concatenate([dirs[d][1][b, rows[d], sl], dirs[d][2][b, rows[d], sl]], axis=0)
            ws[c] = _dot(wq, state[c].astype(BF16))
        for c in chains:
            d, b, h = c
            sl = slice(h * DN_HEAD_DIM, (h + 1) * DN_HEAD_DIM)
            v_new[c] = (dirs[d][0][b, rows[d], sl].astype(F32) - ws[c][:L]).astype(BF16)
        for c in chains:
            d, b, h = c
            at = dirs[d][4][b, rows[d], h * DN_HEAD_DIM:h * DN_HEAD_DIM + L]
            av[c] = _dot(at, v_new[c])
        for c in chains:
            d, b, h = c
            sl = slice(h * DN_HEAD_DIM, (h + 1) * DN_HEAD_DIM)
            cl = (N_DIR + d) * DN_HEADS + h
            dirs[d][6][b, rows[d], sl] = (ws[c][L:] + av[c]).astype(BF16)
            state[c] = state[c] * egt[d, b][:, cl:cl + 1] + _dot_tn(dirs[d][3][b, rows[d], sl], v_new[c])
    for c in chains:
        s_ref[c] = state[c]


def _scan_chunk_index(i, n, *, batch, n_chunks):
    b = i % batch
    backward = i // batch
    return b * n_chunks + n + backward * (n_chunks - 1 - 2 * n)


def _gate_layouts(zg, *, batch, seq):
    n_chunks = seq // SCAN_CHUNK
    g = zg[:, :16].reshape(batch * seq, 2, N_DIR, 4)
    cols = g.transpose(2, 0, 1, 3).reshape(N_DIR, batch * seq, 8)
    rows = cols.reshape(N_DIR, batch * n_chunks, SCAN_CHUNK, 8).transpose(0, 1, 3, 2)
    return cols, rows


def _delta_scan(qkv, zg, *, batch, seq):
    n = batch * seq
    T = SCAN_CHUNK * SCAN_STEP_CHUNKS
    steps = seq // T
    prepped = [t.reshape(N_DIR, batch, seq, DN_WIDTH) for t in _delta_prep(qkv, zg, batch=batch, seq=seq)]
    zg3 = zg.reshape(batch, seq, LANES)
    fwd = lambda s: s
    bwd = lambda s: steps - 1 - s
    specs = []
    for d, pos in ((0, fwd), (1, bwd)):
        specs += [pl.BlockSpec((None, batch, T, DN_WIDTH), lambda s, d=d, pos=pos: (d, 0, pos(s), 0))] * 5
        specs += [pl.BlockSpec((batch, T, LANES), lambda s, pos=pos: (0, pos(s), 0))]
    out = jax.ShapeDtypeStruct((batch, seq, DN_WIDTH), BF16)
    o_f, o_b = pl.pallas_call(
        functools.partial(_delta_scan_kernel, batch=batch),
        grid=(steps,),
        in_specs=specs,
        out_specs=[pl.BlockSpec((batch, T, DN_WIDTH), lambda s: (0, fwd(s), 0)),
                   pl.BlockSpec((batch, T, DN_WIDTH), lambda s: (0, bwd(s), 0))],
        out_shape=[out, out],
        scratch_shapes=[pltpu.VMEM((N_DIR, batch, DN_HEADS, DN_HEAD_DIM, DN_HEAD_DIM), F32)],
        compiler_params=_params("arbitrary"),
        name="delta_scan",
    )(*prepped, zg3, *prepped, zg3)
    return o_f.reshape(n, DN_WIDTH), o_b.reshape(n, DN_WIDTH)


ML_PAIRS = ML_HEADS // 2
ML_AUG = 2 * ML_V_DIM
ML_KV_ROWS = ML_PAIRS * 2 * ML_QK_DIM


def _lane_cols(cols, width):
    rows = cols[0].shape[0]
    lane = lax.broadcasted_iota(jnp.int32, (rows, width), 1)
    out = jnp.zeros((rows, width), F32)
    for j, c in enumerate(cols):
        out = jnp.where(lane == j, c, out)
    return out


def _mlstm_prep_kernel(q_ref, k_ref, v_ref, gc_ref, gr_ref, qs_ref, iv_ref, kv_ref, aux_ref):
    G, L = PREP_GROUP, SCAN_CHUNK
    row = lax.broadcasted_iota(jnp.int32, (G, G), 0)
    col = lax.broadcasted_iota(jnp.int32, (G, G), 1)
    shift = L.bit_length() - 1
    same01 = _as01((row >> shift) == (col >> shift))
    masks = [_group_masks(False), _group_masks(True)]
    cum_c = [_as01(m[0]) for m in masks]
    cum_r = [_as01(m[2]) for m in masks]
    lane = lax.broadcasted_iota(jnp.int32, (G, LANES), 1)
    first_head = lane < ML_QK_DIM
    lane_r = lax.broadcasted_iota(jnp.int32, (1, G), 1)
    row_c = lax.broadcasted_iota(jnp.int32, (G, 1), 0)
    ones_v = jnp.ones((G, ML_V_DIM), BF16)
    scale = ML_QK_DIM ** -0.5
    heads = range(ML_HEADS)
    chains = [(h, d) for h in heads for d in range(N_DIR)]

    def step(i, carry):
        groups = range(PREP_STEP_GROUPS)
        gidx = [i * PREP_STEP_GROUPS + g for g in groups]
        rows = [pl.ds(pl.multiple_of(x * G, G), G) for x in gidx]
        gc = [gc_ref[r, :] for r in rows]
        gr = [gr_ref[x] for x in gidx]
        bt_c = [_exact_dot01(same01, x) for x in gc]
        bt_r = [_exact_dot01_r(x, same01) for x in gr]
        bc_c = [[_exact_dot01(cum_c[d], x) for d in range(N_DIR)] for x in gc]
        bc_r = [[_exact_dot01_r(x, cum_r[d]) for d in range(N_DIR)] for x in gr]
        gps = [(g, p) for g in groups for p in range(ML_PAIRS)]
        ghs = [(g, h) for g in groups for h in heads]
        chains = [(g, h, d) for g, h in ghs for d in range(N_DIR)]
        qp = {(g, p): q_ref[rows[g], p * LANES:(p + 1) * LANES].astype(F32) * scale for g, p in gps}
        kp16 = {(g, p): k_ref[rows[g], p * LANES:(p + 1) * LANES] for g, p in gps}
        kp = {c: kp16[c].astype(F32) for c in gps}
        v16 = {(g, h): v_ref[rows[g], h * ML_V_DIM:(h + 1) * ML_V_DIM].astype(BF16) for g, h in ghs}
        for g, p in gps:
            qs_ref[rows[g], p * LANES:(p + 1) * LANES] = qp[g, p].astype(BF16)
        own = [first_head, jnp.logical_not(first_head)]
        scores = {(g, h): _dot_nt(jnp.where(own[h % 2], qp[g, h // 2], 0.0).astype(BF16), kp16[g, h // 2])
                  for g, h in ghs}
        col = lambda c: N_DIR * ML_HEADS + c[2] * ML_HEADS + c[1]
        b_c = {c: bc_c[c[0]][c[2]][:, col(c):col(c) + 1] for c in chains}
        b_t = {c: bt_c[c[0]][:, col(c):col(c) + 1] for c in chains}
        b_r = {c: bc_r[c[0]][c[2]][col(c):col(c) + 1, :] for c in chains}
        li_c = {c: gc[c[0]][:, col(c) - N_DIR * ML_HEADS:col(c) - N_DIR * ML_HEADS + 1] for c in chains}
        li_r = {c: gr[c[0]][col(c) - N_DIR * ML_HEADS:col(c) - N_DIR * ML_HEADS + 1, :] for c in chains}
        d_mat = {c: jnp.where(masks[c[2]][0], b_c[c] - b_r[c] + li_r[c], NEG_BIG) for c in chains}
        d_max = {c: jnp.max(d_mat[c], axis=-1, keepdims=True) for c in chains}
        pmat = {c: (jnp.exp(d_mat[c] - d_max[c]) * scores[c[:2]]).astype(BF16) for c in chains}
        w_end_r = {c: bt_r[c[0]][col(c):col(c) + 1, :] - b_r[c] + li_r[c] for c in chains}
        w_max = {}
        for c in chains:
            wm = jnp.full((G, 1), NEG_BIG, F32)
            for j in range(G // L):
                in_j = jnp.logical_and(lane_r >= j * L, lane_r < (j + 1) * L)
                wm_j = jnp.max(jnp.where(in_j, w_end_r[c], NEG_BIG), axis=-1, keepdims=True)
                wm = jnp.where(jnp.logical_and(row_c >= j * L, row_c < (j + 1) * L), wm_j, wm)
            w_max[c] = wm
        sw0 = {c: jnp.exp(b_t[c] - b_c[c] + li_c[c] - w_max[c]) for c in chains}
        iv = {c: _dot(pmat[c], jnp.concatenate([v16[c[:2]], ones_v], axis=-1)) for c in chains}
        for c in chains:
            g, h, d = c
            iv_ref[d, rows[g], h * ML_AUG:(h + 1) * ML_AUG] = iv[c].astype(BF16)
        for g in groups:
            for d in range(N_DIR):
                src = N_DIR * ML_HEADS + d * ML_HEADS
                aux_ref[d, rows[g], :LANES] = jnp.where(
                    lane < ML_HEADS, pltpu.roll(bc_c[g][d], LANES - src, 1),
                    jnp.where(lane < 2 * ML_HEADS, pltpu.roll(bt_c[g], LANES - src + ML_HEADS, 1), 0.0))
                aux_ref[d, rows[g], LANES:] = _lane_cols(
                    [d_max[g, h, d] for h in heads] + [w_max[g, h, d] for h in heads], LANES)
        gpd = [(g, p, d) for g, p in gps for d in range(N_DIR)]
        ks = {(g, p, d): (kp[g, p] * jnp.where(first_head, sw0[g, 2 * p, d], sw0[g, 2 * p + 1, d])).astype(BF16)
              for g, p, d in gpd}
        vcat = {(g, p): jnp.concatenate([v16[g, 2 * p], v16[g, 2 * p + 1], ones_v], axis=-1) for g, p in gps}
        pair_chunks = [(g, p, d, j) for g, p, d in gpd for j in range(G // L)]
        kv = [_dot_tn(ks[g, p, d][j * L:(j + 1) * L], vcat[g, p][j * L:(j + 1) * L]) for g, p, d, j in pair_chunks]
        for (g, p, d, j), t in zip(pair_chunks, kv):
            top = jnp.concatenate([t[:ML_QK_DIM, :ML_V_DIM], t[:ML_QK_DIM, 2 * ML_V_DIM:]], axis=-1)
            bot = jnp.concatenate([t[ML_QK_DIM:, ML_V_DIM:2 * ML_V_DIM], t[ML_QK_DIM:, 2 * ML_V_DIM:]], axis=-1)
            r0 = pl.multiple_of(gidx[g] * (G // L * ML_KV_ROWS) + (j * ML_PAIRS + p) * LANES, LANES)
            kv_ref[d, pl.ds(r0, LANES), :] = jnp.concatenate([top, bot], axis=0).astype(BF16)
        return carry

    lax.fori_loop(0, q_ref.shape[0] // (G * PREP_STEP_GROUPS), step, 0)


def _mlstm_prep(z, zg, *, batch, seq, q_col, k_col, v_col):
    n = batch * seq
    R, G, L = min(PREP_ROWS, seq), PREP_GROUP, SCAN_CHUNK
    steps = seq // R
    gr = zg[:, :16].reshape(n // G, G, 16).transpose(0, 2, 1)
    rowblk = lambda c: (lambda b, r: (b * steps + r, c))
    dirblk = lambda b, r: (0, b * steps + r, 0)
    kv_rows = ML_KV_ROWS // L
    return pl.pallas_call(
        _mlstm_prep_kernel,
        grid=(batch, steps),
        in_specs=[
            pl.BlockSpec((R, ML_QK_WIDTH), rowblk(q_col // ML_QK_WIDTH)),
            pl.BlockSpec((R, ML_QK_WIDTH), rowblk(k_col // ML_QK_WIDTH)),
            pl.BlockSpec((R, ML_V_WIDTH), rowblk(v_col // ML_V_WIDTH)),
            pl.BlockSpec((R, LANES), rowblk(0)),
            pl.BlockSpec((R // G, 16, G), lambda b, r: (b * steps + r, 0, 0)),
        ],
        out_specs=[
            pl.BlockSpec((R, ML_QK_WIDTH), rowblk(0)),
            pl.BlockSpec((N_DIR, R, ML_HEADS * ML_AUG), dirblk),
            pl.BlockSpec((N_DIR, R * kv_rows, ML_AUG), dirblk),
            pl.BlockSpec((N_DIR, R, 2 * LANES), dirblk),
        ],
        out_shape=[
            jax.ShapeDtypeStruct((n, ML_QK_WIDTH), BF16),
            jax.ShapeDtypeStruct((N_DIR, n, ML_HEADS * ML_AUG), BF16),
            jax.ShapeDtypeStruct((N_DIR, n * kv_rows, ML_AUG), BF16),
            jax.ShapeDtypeStruct((N_DIR, n, 2 * LANES), F32),
        ],
        compiler_params=_params("parallel", "parallel"),
        name="mlstm_prep",
    )(z, z, z, zg, gr)


def _mlstm_scan_kernel(qf, ivf, kvf, auxf, qb, ivb, kvb, auxb, of_ref, ob_ref, c_ref, m_ref, *, batch):
    L, C = SCAN_CHUNK, SCAN_STEP_CHUNKS

    @pl.when(pl.program_id(0) == 0)
    def _():
        c_ref[...] = jnp.zeros_like(c_ref)
        m_ref[...] = jnp.zeros_like(m_ref)

    dirs = ((qf, ivf, kvf, auxf, of_ref), (qb, ivb, kvb, auxb, ob_ref))
    lane = lax.broadcasted_iota(jnp.int32, (L, LANES), 1)
    keep = [jnp.where(lane < ML_QK_DIM, 1.0, 0.0).astype(BF16), jnp.where(lane < ML_QK_DIM, 0.0, 1.0).astype(BF16)]
    first_rows = lax.broadcasted_iota(jnp.int32, (2 * ML_QK_DIM, 1), 0) < ML_QK_DIM
    pair_chains = [(d, b, p) for d in range(N_DIR) for b in range(batch) for p in range(ML_PAIRS)]
    chains = [(d, b, h) for d in range(N_DIR) for b in range(batch) for h in range(ML_HEADS)]
    groups = [(d, b) for d in range(N_DIR) for b in range(batch)]
    H = ML_HEADS
    lane8 = lax.broadcasted_iota(jnp.int32, (SUBLANES, LANES), 1)
    sel_row = lax.broadcasted_iota(jnp.int32, (2 * LANES, LANES), 0) & (LANES - 1)
    lane_select = [_as01(sel_row == j) for j in range(2 * H)]
    state = {c: c_ref[c] for c in pair_chains}
    m_st = {g: m_ref[g] for g in groups}
    for step in range(C):
        chunk = {0: step, 1: C - 1 - step}
        rows = {d: slice(chunk[d] * L, (chunk[d] + 1) * L) for d in range(N_DIR)}
        qc = {}
        for c in pair_chains:
            d, b, p = c
            qp = dirs[d][0][b, rows[d], p * LANES:(p + 1) * LANES]
            qc[c] = _dot(jnp.concatenate([qp * keep[0], qp * keep[1]], axis=0), state[c].astype(BF16))
        w_prev, w_cur, floor = {}, {}, {}
        for g in groups:
            d, b = g
            aux = dirs[d][3][b, rows[d], :]
            x = aux[:, :LANES] + m_st[g][0:1, :]
            y = jnp.maximum(x, aux[:, LANES:])
            w_prev[g] = jnp.exp(x - y)
            w_cur[g] = jnp.exp(aux[:, LANES:] - y)
            floor[g] = jnp.exp(-y)
            y0 = jnp.broadcast_to(y[0:1, :], (SUBLANES, LANES))
            m_st[g] = jnp.where(lane8 < H, pltpu.roll(y0, LANES - H, 1), jnp.where(lane8 < 2 * H, y0, 0.0))
        tiles = [t[g] for g in groups for t in (w_prev, w_cur, floor)]
        per_row = _split_hi_lo(jnp.concatenate(tiles, axis=0))
        per_chunk = _split_hi_lo(jnp.concatenate([t[0:1, :] for t in tiles] + [tiles[0][0:SUBLANES, :]], axis=0))
        bc_row, bc_chunk = [], []
        for p in range(ML_PAIRS):
            both = _dot(per_row, jnp.concatenate([lane_select[2 * p], lane_select[2 * p + 1]], axis=-1))
            bc_row += [both[:, :LANES], both[:, LANES:]]
            both = _dot(per_chunk, jnp.concatenate([lane_select[H + 2 * p], lane_select[H + 2 * p + 1]], axis=-1))
            bc_chunk += [both[:, :LANES], both[:, LANES:]]
        numden = {}
        for c in chains:
            d, b, h = c
            r0 = (h % 2) * L
            t0 = 3 * groups.index((d, b)) * L
            wp = bc_row[h][t0:t0 + L, :]
            wc = bc_row[h][t0 + L:t0 + 2 * L, :]
            iv = dirs[d][1][b, rows[d], h * ML_AUG:(h + 1) * ML_AUG].astype(F32)
            qch = qc[d, b, h // 2][r0:r0 + L, :]
            numden[c] = jnp.concatenate([wp * qch[:, :ML_V_DIM] + wc * iv[:, :ML_V_DIM],
                                         wp * qch[:, ML_V_DIM:] + wc * iv[:, ML_V_DIM:]], axis=-1)
        for c in chains:
            d, b, h = c
            t0 = 3 * groups.index((d, b)) * L
            den = jnp.maximum(jnp.abs(numden[c][:, ML_V_DIM:]), bc_row[h][t0 + 2 * L:t0 + 3 * L, :])
            dirs[d][4][b, rows[d], h * ML_V_DIM:(h + 1) * ML_V_DIM] = (numden[c][:, :ML_V_DIM] / den).astype(BF16)
        for c in pair_chains:
            d, b, p = c
            r0 = (chunk[d] * ML_PAIRS + p) * LANES
            kv = dirs[d][2][b, r0:r0 + LANES, :].astype(F32)
            t0 = 3 * groups.index((d, b))
            cw = jnp.where(first_rows, bc_chunk[2 * p][t0:t0 + 1, :], bc_chunk[2 * p + 1][t0:t0 + 1, :])
            iw = jnp.where(first_rows, bc_chunk[2 * p][t0 + 1:t0 + 2, :], bc_chunk[2 * p + 1][t0 + 1:t0 + 2, :])
            state[c] = jnp.concatenate([cw * state[c][:, :ML_V_DIM] + iw * kv[:, :ML_V_DIM],
                                        cw * state[c][:, ML_V_DIM:] + iw * kv[:, ML_V_DIM:]], axis=-1)
    for c in pair_chains:
        c_ref[c] = state[c]
    for g in groups:
        m_ref[g] = m_st[g]


def _mlstm_scan(z, zg, *, batch, seq, q_col, k_col, v_col):
    n = batch * seq
    T = SCAN_CHUNK * SCAN_STEP_CHUNKS
    steps = seq // T
    kv_rows = ML_KV_ROWS // SCAN_CHUNK
    qs, iv, kv, aux = _mlstm_prep(z, zg, batch=batch, seq=seq, q_col=q_col, k_col=k_col, v_col=v_col)
    qs = qs.reshape(batch, seq, ML_QK_WIDTH)
    iv = iv.reshape(N_DIR, batch, seq, ML_HEADS * ML_AUG)
    kv = kv.reshape(N_DIR, batch, seq * kv_rows, ML_AUG)
    aux = aux.reshape(N_DIR, batch, seq, 2 * LANES)
    fwd = lambda s: s
    bwd = lambda s: steps - 1 - s
    specs = []
    for d, pos in ((0, fwd), (1, bwd)):
        specs += [
            pl.BlockSpec((batch, T, ML_QK_WIDTH), lambda s, pos=pos: (0, pos(s), 0)),
            pl.BlockSpec((None, batch, T, ML_HEADS * ML_AUG), lambda s, d=d, pos=pos: (d, 0, pos(s), 0)),
            pl.BlockSpec((None, batch, T * kv_rows, ML_AUG), lambda s, d=d, pos=pos: (d, 0, pos(s), 0)),
            pl.BlockSpec((None, batch, T, 2 * LANES), lambda s, d=d, pos=pos: (d, 0, pos(s), 0)),
        ]
    out = jax.ShapeDtypeStruct((batch, seq, ML_V_WIDTH), BF16)
    h_f, h_b = pl.pallas_call(
        functools.partial(_mlstm_scan_kernel, batch=batch),
        grid=(steps,),
        in_specs=specs,
        out_specs=[pl.BlockSpec((batch, T, ML_V_WIDTH), lambda s: (0, fwd(s), 0)),
                   pl.BlockSpec((batch, T, ML_V_WIDTH), lambda s: (0, bwd(s), 0))],
        out_shape=[out, out],
        scratch_shapes=[
            pltpu.VMEM((N_DIR, batch, ML_PAIRS, 2 * ML_QK_DIM, ML_AUG), F32),
            pltpu.VMEM((N_DIR, batch, SUBLANES, LANES), F32),
        ],
        compiler_params=_params("arbitrary"),
        name="mlstm_scan",
    )(qs, iv, kv, aux, qs, iv, kv, aux)
    return h_f.reshape(n, ML_V_WIDTH), h_b.reshape(n, ML_V_WIDTH)


def _head_rms_norm(x, g, n_heads, head_dim):
    parts = []
    for h in range(n_heads):
        xh = x[:, h * head_dim:(h + 1) * head_dim]
        ms = jnp.mean(xh * xh, axis=-1, keepdims=True)
        parts.append(xh * lax.rsqrt(ms + NORM_EPS) * g)
    return jnp.concatenate(parts, axis=-1)


def _layer_norm(x, g, b):
    mu = jnp.mean(x, axis=-1, keepdims=True)
    xc = x - mu
    var = jnp.mean(xc * xc, axis=-1, keepdims=True)
    return xc * lax.rsqrt(var + LN_EPS) * g + b


def _swiglu_residual(x, g_ref, w1_ref, w3_ref, w2_ref, fg_ref, final_norm):
    ms = jnp.mean(x * x, axis=-1, keepdims=True)
    hn = (x * lax.rsqrt(ms + NORM_EPS) * g_ref[...]).astype(BF16)
    a = _dot(hn, w1_ref[...])
    b = _dot(hn, w3_ref[...])
    y = x + _dot((a * jax.nn.sigmoid(a) * b).astype(BF16), w2_ref[...])
    if final_norm:
        ms = jnp.mean(y * y, axis=-1, keepdims=True)
        y = y * lax.rsqrt(ms + NORM_EPS) * fg_ref[...]
    return y


def _ffn_operands(g, w1, w3, w2, final_g):
    d, f = w1.shape
    specs = [_resident((1, d)), _resident((d, f)), _resident((d, f)), _resident((f, d)), _resident((1, d))]
    return specs, (g.reshape(1, d), w1.astype(BF16), w3.astype(BF16), w2.astype(BF16), final_g.reshape(1, d))


def _even_block_kernel(h_ref, c_ref, of_ref, ob_ref, gate_ref, lng_ref, lnb_ref, ng_ref, wa_ref, wb_ref,
                       g_ref, w1_ref, w3_ref, w2_ref, fg_ref, o_ref, *, final_norm):
    ya = _layer_norm(c_ref[...].astype(F32), lng_ref[...], lnb_ref[...])
    ya = ya * jax.nn.sigmoid(ya)
    o = _head_rms_norm(of_ref[...].astype(F32) + ob_ref[...].astype(F32), ng_ref[...], DN_HEADS, DN_HEAD_DIM)
    gate = gate_ref[...].astype(F32)
    o = o * (gate * jax.nn.sigmoid(gate))
    x = h_ref[...] + _dot(ya.astype(BF16), wa_ref[...]) + _dot(o.astype(BF16), wb_ref[...])
    o_ref[...] = _swiglu_residual(x, g_ref, w1_ref, w3_ref, w2_ref, fg_ref, final_norm)


def _even_block(h, conv, o_fwd, o_bwd, z, ln_g, ln_b, norm_g, w_out, ffn, *, gate_col, final_norm):
    n, d = h.shape
    tm = min(ROW_TILE, n)
    gb = gate_col // DN_WIDTH
    row = lambda i: (i, 0)
    ffn_specs, ffn_args = _ffn_operands(*ffn)
    return pl.pallas_call(
        functools.partial(_even_block_kernel, final_norm=final_norm),
        grid=(n // tm,),
        in_specs=[
            pl.BlockSpec((tm, d), row),
            pl.BlockSpec((tm, CONV_CH), row),
            pl.BlockSpec((tm, DN_WIDTH), row),
            pl.BlockSpec((tm, DN_WIDTH), row),
            pl.BlockSpec((tm, DN_WIDTH), lambda i: (i, gb)),
            _resident((1, CONV_CH)),
            _resident((1, CONV_CH)),
            _resident((1, DN_HEAD_DIM)),
            _resident((CONV_CH, d)),
            _resident((DN_WIDTH, d)),
        ] + ffn_specs,
        out_specs=pl.BlockSpec((tm, d), row),
        out_shape=jax.ShapeDtypeStruct((n, d), F32),
        compiler_params=_params("parallel"),
        name="even_block",
    )(h, conv, o_fwd, o_bwd, z, ln_g.reshape(1, -1), ln_b.reshape(1, -1), norm_g.reshape(1, -1),
      w_out[:CONV_CH].astype(BF16), w_out[CONV_CH:].astype(BF16), *ffn_args)


def _odd_block_kernel(h_ref, u_ref, vp_ref, hf_ref, hb_ref, op_ref, lng_ref, lnb_ref, sgw_ref, sgb_ref, ng_ref,
                      wa_ref, wb_ref, g_ref, w1_ref, w3_ref, w2_ref, fg_ref, o_ref, *, final_norm):
    tm = h_ref.shape[0]
    u = jax.nn.gelu(u_ref[...].astype(F32))
    vv = _layer_norm(jax.nn.gelu(vp_ref[...].astype(F32)), lng_ref[...], lnb_ref[...]).astype(BF16)
    sgb = sgb_ref[...]
    rows = []
    for c in range(tm // SG_CHUNK):
        parts = []
        for g in range(SG_GROUPS):
            blk = vv[c * SG_CHUNK:(c + 1) * SG_CHUNK, g * SG_GROUP_DIM:(g + 1) * SG_GROUP_DIM]
            parts.append(_dot(sgw_ref[g], blk) + sgb[:, g:g + 1])
        rows.append(jnp.concatenate(parts, axis=-1))
    yc = u * jnp.concatenate(rows, axis=0)
    hd = _head_rms_norm(hf_ref[...].astype(F32) + hb_ref[...].astype(F32), ng_ref[...], ML_HEADS, ML_V_DIM)
    hd = hd * jax.nn.sigmoid(op_ref[...].astype(F32))
    x = h_ref[...] + _dot(yc.astype(BF16), wa_ref[...]) + _dot(hd.astype(BF16), wb_ref[...])
    o_ref[...] = _swiglu_residual(x, g_ref, w1_ref, w3_ref, w2_ref, fg_ref, final_norm)


def _odd_block(h, z, h_fwd, h_bwd, ln_g, ln_b, sg_w, sg_b, norm_g, w_out, ffn, *, u_col, v_col, o_col, final_norm):
    n, d = h.shape
    tm = min(ROW_TILE, n)
    row = lambda i: (i, 0)
    ffn_specs, ffn_args = _ffn_operands(*ffn)
    return pl.pallas_call(
        functools.partial(_odd_block_kernel, final_norm=final_norm),
        grid=(n // tm,),
        in_specs=[
            pl.BlockSpec((tm, d), row),
            pl.BlockSpec((tm, SG_WIDTH), lambda i: (i, u_col // SG_WIDTH)),
            pl.BlockSpec((tm, SG_WIDTH), lambda i: (i, v_col // SG_WIDTH)),
            pl.BlockSpec((tm, ML_V_WIDTH), row),
            pl.BlockSpec((tm, ML_V_WIDTH), row),
            pl.BlockSpec((tm, ML_V_WIDTH), lambda i: (i, o_col // ML_V_WIDTH)),
            _resident((1, SG_WIDTH)),
            _resident((1, SG_WIDTH)),
            _resident((SG_GROUPS, SG_CHUNK, SG_CHUNK)),
            _resident((SG_CHUNK, SG_GROUPS)),
            _resident((1, ML_V_DIM)),
            _resident((SG_WIDTH, d)),
            _resident((ML_V_WIDTH, d)),
        ] + ffn_specs,
        out_specs=pl.BlockSpec((tm, d), row),
        out_shape=jax.ShapeDtypeStruct((n, d), F32),
        compiler_params=_params("parallel"),
        name="odd_block",
    )(h, z, z, h_fwd, h_bwd, z, ln_g.reshape(1, -1), ln_b.reshape(1, -1), sg_w.astype(BF16), sg_b.T,
      norm_g.reshape(1, -1), w_out[:SG_WIDTH].astype(BF16), w_out[SG_WIDTH:].astype(BF16), *ffn_args)


def _gate_weight(w_gate_cols):
    d, c = w_gate_cols.shape
    return jnp.zeros((d, LANES), F32).at[:, :c].set(w_gate_cols).astype(BF16)


def _even_layer(h, j, p, ffn, *, batch, seq, final_norm):
    main = 2 * CONV_CH + 4 * DN_WIDTH
    w_in = p["ev_w_in"][j]
    gate_params = jnp.zeros((SUBLANES, LANES), F32)
    gate_params = gate_params.at[0, 8:16].set(p["ev_dn_a_log"][j].reshape(-1))
    gate_params = gate_params.at[1, 8:16].set(p["ev_dn_dt_bias"][j].reshape(-1))
    z, zg = _in_proj(h, p["mix_norm_g"][2 * j], w_in[:, :main].astype(BF16), _gate_weight(w_in[:, main:]),
                     gate_params, even=True)
    conv = _conv_glu(z, p["ev_conv_w"][j], p["ev_conv_b"][j], batch=batch, seq=seq)
    qkv = _conv_qkv(z, p["ev_dn_conv_w"][j], batch=batch, seq=seq, col0=2 * CONV_CH)
    o_fwd, o_bwd = _delta_scan(qkv, zg, batch=batch, seq=seq)
    return _even_block(h, conv, o_fwd, o_bwd, z, p["ev_conv_ln_g"][j], p["ev_conv_ln_b"][j], p["ev_dn_norm_g"][j],
                       p["ev_w_out"][j], ffn, gate_col=2 * CONV_CH + 3 * DN_WIDTH, final_norm=final_norm)


def _odd_layer(h, j, p, ffn, *, batch, seq, final_norm):
    main = 2 * SG_WIDTH + 2 * ML_QK_WIDTH + 2 * ML_V_WIDTH
    w_in = p["od_w_in"][j]
    gate_params = jnp.zeros((SUBLANES, LANES), F32)
    gate_params = gate_params.at[0, 0:8].set(p["od_ml_i_bias"][j].reshape(-1))
    gate_params = gate_params.at[1, 8:16].set(p["od_ml_f_bias"][j].reshape(-1))
    z, zg = _in_proj(h, p["mix_norm_g"][2 * j + 1], w_in[:, :main].astype(BF16), _gate_weight(w_in[:, main:]),
                     gate_params, even=False)
    q_col = 2 * SG_WIDTH
    k_col = q_col + ML_QK_WIDTH
    v_col = k_col + ML_QK_WIDTH
    o_col = v_col + ML_V_WIDTH
    h_fwd, h_bwd = _mlstm_scan(z, zg, batch=batch, seq=seq, q_col=q_col, k_col=k_col, v_col=v_col)
    return _odd_block(h, z, h_fwd, h_bwd, p["od_sg_ln_g"][j], p["od_sg_ln_b"][j], p["od_sg_w"][j], p["od_sg_b"][j],
                      p["od_ml_norm_g"][j], p["od_w_out"][j], ffn, u_col=0, v_col=SG_WIDTH, o_col=o_col,
                      final_norm=final_norm)


def kernel(x, mix_norm_g, ev_w_in, ev_conv_w, ev_conv_b, ev_conv_ln_g, ev_conv_ln_b, ev_dn_conv_w, ev_dn_a_log, ev_dn_dt_bias, ev_dn_norm_g, ev_w_out, od_w_in, od_sg_ln_g, od_sg_ln_b, od_sg_w, od_sg_b, od_ml_i_bias, od_ml_f_bias, od_ml_norm_g, od_w_out, ffn_norm_g, ffn_w1, ffn_w3, ffn_w2, final_norm_g):
    p = dict(mix_norm_g=mix_norm_g, ev_w_in=ev_w_in, ev_conv_w=ev_conv_w, ev_conv_b=ev_conv_b,
             ev_conv_ln_g=ev_conv_ln_g, ev_conv_ln_b=ev_conv_ln_b, ev_dn_conv_w=ev_dn_conv_w,
             ev_dn_a_log=ev_dn_a_log, ev_dn_dt_bias=ev_dn_dt_bias, ev_dn_norm_g=ev_dn_norm_g, ev_w_out=ev_w_out,
             od_w_in=od_w_in, od_sg_ln_g=od_sg_ln_g, od_sg_ln_b=od_sg_ln_b, od_sg_w=od_sg_w, od_sg_b=od_sg_b,
             od_ml_i_bias=od_ml_i_bias, od_ml_f_bias=od_ml_f_bias, od_ml_norm_g=od_ml_norm_g, od_w_out=od_w_out)
    batch, seq, d = x.shape
    depth = mix_norm_g.shape[0]
    h = x.reshape(batch * seq, d)
    for layer in range(depth):
        ffn = (ffn_norm_g[layer], ffn_w1[layer], ffn_w3[layer], ffn_w2[layer], final_norm_g)
        mixer_layer = _even_layer if layer % 2 == 0 else _odd_layer
        h = mixer_layer(h, layer // 2, p, ffn, batch=batch, seq=seq, final_norm=layer == depth - 1)
    return h.reshape(batch, seq, d)
```

```python
import functools

import jax
import jax.numpy as jnp
from jax import lax
from jax.experimental import pallas as pl
from jax.experimental.pallas import tpu as pltpu

NORM_EPS = 1e-6
LN_EPS = 1e-5
N_DIR = 2

CONV_CH = 512
DN_HEADS = 4
DN_HEAD_DIM = 128
DN_WIDTH = DN_HEADS * DN_HEAD_DIM
SG_GROUPS = 4
SG_GROUP_DIM = 128
SG_WIDTH = SG_GROUPS * SG_GROUP_DIM
SG_CHUNK = 128
ML_HEADS = 4
ML_QK_DIM = 64
ML_V_DIM = 128
ML_QK_WIDTH = ML_HEADS * ML_QK_DIM
ML_V_WIDTH = ML_HEADS * ML_V_DIM

LANES = 128
SUBLANES = 8
VMEM_LIMIT_BYTES = 56 * 1024 * 1024

SCAN_CHUNK = 64
SCAN_STEP_CHUNKS = 2
PREP_GROUP = 128
PREP_ROWS = 512
PREP_STEP_GROUPS = 2
ROW_TILE = 512
IN_ROW_TILE = 1024
CONV_TILE = 128
QKV_CONV_TILES = 4
CONV_PAD = 16
NEG_BIG = -1e30

BF16 = jnp.bfloat16
F32 = jnp.float32


def _params(*sem):
    return pltpu.CompilerParams(dimension_semantics=sem, vmem_limit_bytes=VMEM_LIMIT_BYTES)


def _resident(shape):
    return pl.BlockSpec(shape, lambda *_: (0,) * len(shape), pipeline_mode=pl.Buffered(1))


def _dot(a, b):
    return jnp.dot(a, b, preferred_element_type=F32)


def _dot_nt(a, b):
    return lax.dot_general(a, b, (((1,), (1,)), ((), ())), preferred_element_type=F32)


def _dot_tn(a, b):
    return lax.dot_general(a, b, (((0,), (0,)), ((), ())), preferred_element_type=F32)


def _exact_dot01(t01, x):
    x1 = x.astype(BF16)
    r1 = x - x1.astype(F32)
    x2 = r1.astype(BF16)
    x3 = (r1 - x2.astype(F32)).astype(BF16)
    return _dot(t01, x1) + _dot(t01, x2) + _dot(t01, x3)


def _exact_dot01_r(x, t01):
    x1 = x.astype(BF16)
    r1 = x - x1.astype(F32)
    x2 = r1.astype(BF16)
    x3 = (r1 - x2.astype(F32)).astype(BF16)
    return _dot(x1, t01) + _dot(x2, t01) + _dot(x3, t01)


def _in_proj_kernel(h_ref, g_ref, w_ref, wg_ref, gp_ref, zm_ref, zg_ref, zgt_ref, *, even):
    x = h_ref[...]
    ms = jnp.mean(x * x, axis=-1, keepdims=True)
    hn = (x * lax.rsqrt(ms + NORM_EPS) * g_ref[...]).astype(BF16)
    zm_ref[...] = _dot(hn, w_ref[...]).astype(zm_ref.dtype)
    zg = _dot(hn, wg_ref[...])
    p0 = gp_ref[0:1, :]
    p1 = gp_ref[1:2, :]
    lane = lax.broadcasted_iota(jnp.int32, zg.shape, 1)
    if even:
        first = jax.nn.sigmoid(zg)
        second = -jnp.exp(p0) * jax.nn.softplus(zg + p1)
    else:
        first = zg + p0
        second = jax.nn.log_sigmoid(zg + p1)
    gates = jnp.where(lane < 8, first, second)
    zg_ref[...] = gates
    for s in range(gates.shape[0] // PREP_GROUP):
        zgt_ref[s] = gates[s * PREP_GROUP:(s + 1) * PREP_GROUP, :].T[:2 * SUBLANES, :]


def _in_proj(h, g, w_main, w_gate, gate_params, *, even):
    n, d = h.shape
    c = w_main.shape[1]
    tm = min(IN_ROW_TILE, n)
    return pl.pallas_call(
        functools.partial(_in_proj_kernel, even=even),
        grid=(n // tm,),
        in_specs=[
            pl.BlockSpec((tm, d), lambda i: (i, 0)),
            _resident((1, d)),
            _resident((d, c)),
            _resident((d, LANES)),
            _resident((SUBLANES, LANES)),
        ],
        out_specs=[
            pl.BlockSpec((tm, c), lambda i: (i, 0)),
            pl.BlockSpec((tm, LANES), lambda i: (i, 0)),
            pl.BlockSpec((tm // PREP_GROUP, 2 * SUBLANES, PREP_GROUP), lambda i: (i, 0, 0)),
        ],
        out_shape=[jax.ShapeDtypeStruct((n, c), BF16), jax.ShapeDtypeStruct((n, LANES), F32),
                   jax.ShapeDtypeStruct((n // PREP_GROUP, 2 * SUBLANES, PREP_GROUP), F32)],
        compiler_params=_params("parallel"),
        name="in_proj_even" if even else "in_proj_odd",
    )(h, g.reshape(1, d), w_main, w_gate, gate_params)


def _conv_taps(pad_ref, w_ref, o_ref, bias, *, seq, width, post, tiles_per_step=1):
    half = width // 2
    tiles_per_step = min(tiles_per_step, seq // CONV_TILE)
    step_rows = CONV_TILE * tiles_per_step

    def body(i, carry):
        t0 = pl.multiple_of(i * step_rows, step_rows)
        accs = []
        for s in range(tiles_per_step):
            acc = jnp.zeros((CONV_TILE, LANES), F32)
            for j in range(width):
                acc = acc + pad_ref[pl.ds(t0 + (s * CONV_TILE + CONV_PAD - half + j), CONV_TILE), :] * w_ref[j:j + 1, :]
            accs.append(acc if bias is None else acc + bias)
        for s, out in enumerate(post(accs)):
            o_ref[pl.ds(t0 + s * CONV_TILE, CONV_TILE), :] = out.astype(o_ref.dtype)
        return carry

    lax.fori_loop(0, seq // step_rows, body, 0)


def _fill_padded(pad_ref, x, seq):
    zeros = jnp.zeros((CONV_PAD, LANES), F32)
    pad_ref[0:CONV_PAD, :] = zeros
    pad_ref[CONV_PAD + seq:CONV_PAD + seq + CONV_PAD, :] = zeros
    pad_ref[CONV_PAD:CONV_PAD + seq, :] = x


def _conv_glu_kernel(av_ref, ag_ref, w_ref, b_ref, o_ref, pad_ref, *, seq, width):
    _fill_padded(pad_ref, av_ref[...].astype(F32) * jax.nn.sigmoid(ag_ref[...].astype(F32)), seq)
    _conv_taps(pad_ref, w_ref, o_ref, b_ref[...], seq=seq, width=width, post=lambda tiles: tiles)


def _conv_glu(z, conv_w, conv_b, *, batch, seq):
    width = conv_w.shape[0]
    nblk = CONV_CH // LANES
    return pl.pallas_call(
        functools.partial(_conv_glu_kernel, seq=seq, width=width),
        grid=(batch, nblk),
        in_specs=[
            pl.BlockSpec((seq, LANES), lambda b, c: (b, c)),
            pl.BlockSpec((seq, LANES), lambda b, c: (b, nblk + c)),
            pl.BlockSpec((width, LANES), lambda b, c: (0, c)),
            pl.BlockSpec((1, LANES), lambda b, c: (0, c)),
        ],
        out_specs=pl.BlockSpec((seq, LANES), lambda b, c: (b, c)),
        out_shape=jax.ShapeDtypeStruct((batch * seq, CONV_CH), BF16),
        scratch_shapes=[pltpu.VMEM((seq + 2 * CONV_PAD, LANES), F32)],
        compiler_params=_params("parallel", "parallel"),
        name="conv_glu",
    )(z, z, conv_w, conv_b.reshape(1, CONV_CH))


def _conv_qkv_kernel(x_ref, w_ref, o_ref, pad_ref, *, seq, width, n_norm_blocks):
    _fill_padded(pad_ref, x_ref[...].astype(F32), seq)
    normalise = pl.program_id(1) < n_norm_blocks

    def post(tiles):
        ys = [a * jax.nn.sigmoid(a) for a in tiles]
        sq = [jnp.sum(y * y, axis=-1, keepdims=True) for y in ys]
        inv = [lax.rsqrt(s + NORM_EPS) for s in sq]
        return [jnp.where(normalise, y * r, y) for y, r in zip(ys, inv)]

    _conv_taps(pad_ref, w_ref, o_ref, None, seq=seq, width=width, post=post, tiles_per_step=QKV_CONV_TILES)


def _conv_qkv(z, dn_conv_w, *, batch, seq, col0):
    width = dn_conv_w.shape[0]
    nblk = 3 * DN_WIDTH // LANES
    blk0 = col0 // LANES
    return pl.pallas_call(
        functools.partial(_conv_qkv_kernel, seq=seq, width=width, n_norm_blocks=2 * DN_HEADS),
        grid=(batch, nblk),
        in_specs=[
            pl.BlockSpec((seq, LANES), lambda b, c: (b, blk0 + c)),
            pl.BlockSpec((width, LANES), lambda b, c: (0, c)),
        ],
        out_specs=pl.BlockSpec((seq, LANES), lambda b, c: (b, c)),
        out_shape=jax.ShapeDtypeStruct((batch * seq, 3 * DN_WIDTH), BF16),
        scratch_shapes=[pltpu.VMEM((seq + 2 * CONV_PAD, LANES), F32)],
        compiler_params=_params("parallel", "parallel"),
        name="conv_qkv",
    )(z, dn_conv_w)


def _tri_masks(n, backward):
    row = lax.broadcasted_iota(jnp.int32, (n, n), 0)
    col = lax.broadcasted_iota(jnp.int32, (n, n), 1)
    d = jnp.where(backward, col - row, row - col)
    return row, col, d >= 0, d > 0, d <= 0


def _unit_tri_inverse(a, row, col, n=None):
    return _unit_tri_inverses([a], row, col, a.shape[0] if n is None else n)[0]


def _split_hi_lo(x):
    hi = x.astype(BF16)
    return jnp.concatenate([hi, (x - hi.astype(F32)).astype(BF16)], axis=-1)


def _bf16_all(mats):
    return [m.astype(BF16) for m in mats]


def _unit_tri_inverses(mats, row, col, n):
    eye = (row == col).astype(F32)
    same = (row >> 3) == (col >> 3)
    l8 = [jnp.where(same, a, 0.0) for a in mats]
    l8h = _bf16_all(l8)
    l2h = _bf16_all([_dot(p, p) for p in l8h])
    l4h = _bf16_all([_dot(p, p) for p in l2h])
    x = [eye - p for p in l8]
    x = [xi + _dot(xi.astype(BF16), p) for xi, p in zip(x, l2h)]
    x = [xi + _dot(xi.astype(BF16), p) for xi, p in zip(x, l4h)]
    shift = 3
    while (1 << shift) < n:
        same_next = (row >> (shift + 1)) == (col >> (shift + 1))
        off = jnp.logical_and(same_next, jnp.logical_not(same))
        ch = _bf16_all([jnp.where(off, a, 0.0) for a in mats])
        xh = _bf16_all(x)
        xch = _bf16_all([_dot(p, c) for p, c in zip(xh, ch)])
        x = [xi - _dot(p, q) for xi, p, q in zip(x, xch, xh)]
        same = same_next
        shift += 1
    return x


def _group_masks(backward):
    n = PREP_GROUP
    row = lax.broadcasted_iota(jnp.int32, (n, n), 0)
    col = lax.broadcasted_iota(jnp.int32, (n, n), 1)
    shift = SCAN_CHUNK.bit_length() - 1
    same = (row >> shift) == (col >> shift)
    d = (col - row) if backward else (row - col)
    land = jnp.logical_and
    return land(same, d >= 0), land(same, d > 0), land(same, d <= 0)


def _as01(mask):
    return jnp.where(mask, 1.0, 0.0).astype(BF16)


def _delta_prep_kernel(q_ref, k_ref, v_ref, gc_ref, gr_ref, u_ref, w_ref, qg_ref, kg_ref, at_ref):
    G, L = PREP_GROUP, SCAN_CHUNK
    row = lax.broadcasted_iota(jnp.int32, (G, G), 0)
    col = lax.broadcasted_iota(jnp.int32, (G, G), 1)
    shift = L.bit_length() - 1
    same01 = _as01((row >> shift) == (col >> shift))
    masks = [_group_masks(False), _group_masks(True)]
    cum_c = [_as01(m[0]) for m in masks]
    cum_r = [_as01(m[2]) for m in masks]
    fold = _as01((lax.broadcasted_iota(jnp.int32, (G, L), 0) & (L - 1)) == lax.broadcasted_iota(jnp.int32, (G, L), 1))
    pad = jnp.zeros((G, DN_HEAD_DIM - L), BF16)
    scale = DN_HEAD_DIM ** -0.5

    def step(i, carry):
        groups = range(PREP_STEP_GROUPS)
        rows = [pl.ds(pl.multiple_of((i * PREP_STEP_GROUPS + g) * G, G), G) for g in groups]
        gc = [gc_ref[r, :] for r in rows]
        gr = [gr_ref[i * PREP_STEP_GROUPS + g] for g in groups]
        g_tot = [_exact_dot01(same01, x) for x in gc]
        g_cum_c = [[_exact_dot01(cum_c[d], x) for d in range(N_DIR)] for x in gc]
        g_cum_r = [[_exact_dot01_r(x, cum_r[d]) for d in range(N_DIR)] for x in gr]
        heads = range(DN_HEADS)
        lanes = [slice(h * DN_HEAD_DIM, (h + 1) * DN_HEAD_DIM) for h in heads]
        gh = [(g, h) for g in groups for h in heads]
        chains = [(g, h, d) for g, h in gh for d in range(N_DIR)]
        k16 = {(g, h): k_ref[rows[g], lanes[h]] for g, h in gh}
        q = {(g, h): q_ref[rows[g], lanes[h]].astype(F32) for g, h in gh}
        k = {c: k16[c].astype(F32) for c in gh}
        kk = {c: _dot_nt(k16[c], k16[c]) for c in gh}
        qk = {c: _dot_nt((q[c] * scale).astype(BF16), k16[c]) for c in gh}
        beta, g_c, g_t, decay = {}, {}, {}, {}
        for c in chains:
            g, h, d = c
            cb = d * DN_HEADS + h
            cl = N_DIR * DN_HEADS + cb
            beta[c] = gc[g][:, cb:cb + 1]
            g_c[c] = g_cum_c[g][d][:, cl:cl + 1]
            g_t[c] = g_tot[g][:, cl:cl + 1]
            g_r = g_cum_r[g][d][cl:cl + 1, :]
            decay[c] = jnp.exp(jnp.where(masks[d][0], g_c[c] - g_r, NEG_BIG))
        a = [jnp.where(masks[c[2]][1], kk[c[:2]] * decay[c], 0.0) * beta[c] for c in chains]
        x = _unit_tri_inverses(a, row, col, L)
        eg = {c: jnp.exp(g_c[c]) for c in chains}
        rhs = [jnp.concatenate([v_ref[rows[c[0]], lanes[c[1]]].astype(F32) * beta[c], k[c[:2]] * (beta[c] * eg[c])],
                               axis=-1).astype(BF16) for c in chains]
        uw = [_dot(xi.astype(BF16), r) for xi, r in zip(x, rhs)]
        attn = [jnp.where(masks[c[2]][0], qk[c[:2]] * decay[c], 0.0).astype(BF16) for c in chains]
        at = [_dot(p, fold) for p in attn]
        for c, uwi, ati in zip(chains, uw, at):
            g, h, d = c
            r, sl = rows[g], lanes[h]
            u_ref[d, r, sl] = uwi[:, :DN_HEAD_DIM].astype(BF16)
            w_ref[d, r, sl] = uwi[:, DN_HEAD_DIM:].astype(BF16)
            qg_ref[d, r, sl] = (q[g, h] * (scale * eg[c])).astype(BF16)
            kg_ref[d, r, sl] = (k[g, h] * jnp.exp(g_t[c] - g_c[c])).astype(BF16)
            at_ref[d, r, sl] = jnp.concatenate([ati.astype(BF16), pad], axis=-1)
        return carry

    lax.fori_loop(0, q_ref.shape[0] // (G * PREP_STEP_GROUPS), step, 0)


def _delta_prep(qkv, zg, gr, *, batch, seq):
    n = batch * seq
    R, G = min(PREP_ROWS, seq), PREP_GROUP
    steps = seq // R
    rowblk = lambda c: (lambda b, r: (b * steps + r, c))
    out = jax.ShapeDtypeStruct((N_DIR, n, DN_WIDTH), BF16)
    out_spec = pl.BlockSpec((N_DIR, R, DN_WIDTH), lambda b, r: (0, b * steps + r, 0))
    return pl.pallas_call(
        _delta_prep_kernel,
        grid=(batch, steps),
        in_specs=[
            pl.BlockSpec((R, DN_WIDTH), rowblk(0)),
            pl.BlockSpec((R, DN_WIDTH), rowblk(1)),
            pl.BlockSpec((R, DN_WIDTH), rowblk(2)),
            pl.BlockSpec((R, LANES), rowblk(0)),
            pl.BlockSpec((R // G, 16, G), lambda b, r: (b * steps + r, 0, 0)),
        ],
        out_specs=[out_spec] * 5,
        out_shape=[out] * 5,
        compiler_params=_params("parallel", "parallel"),
        name="delta_prep",
    )(qkv, qkv, qkv, zg, gr)


def _delta_scan_kernel(uf, wf, qgf, kgf, atf, zgf, ub, wb, qgb, kgb, atb, zgb, of_ref, ob_ref, s_ref, *, batch):
    L, C = SCAN_CHUNK, SCAN_STEP_CHUNKS

    @pl.when(pl.program_id(0) == 0)
    def _():
        s_ref[...] = jnp.zeros_like(s_ref)

    dirs = ((uf, wf, qgf, kgf, atf, zgf, of_ref), (ub, wb, qgb, kgb, atb, zgb, ob_ref))

    chains = [(d, b, h) for d in range(N_DIR) for b in range(batch) for h in range(DN_HEADS)]
    state = {c: s_ref[c] for c in chains}
    for step in range(C):
        chunk = {0: step, 1: C - 1 - step}
        rows = {d: slice(chunk[d] * L, (chunk[d] + 1) * L) for d in range(N_DIR)}
        egt = {(d, b): jnp.exp(jnp.sum(dirs[d][5][b, rows[d], :], axis=0, keepdims=True))
               for d in range(N_DIR) for b in range(batch)}
        ws, v_new, av = {}, {}, {}
        for c in chains:
            d, b, h = c
            sl = slice(h * DN_HEAD_DIM, (h + 1) * DN_HEAD_DIM)
            wq = jnp.concatenate([dirs[d][1][b, rows[d], sl], dirs[d][2][b, rows[d], sl]], axis=0)
            ws[c] = _dot(wq, state[c].astype(BF16))
        for c in chains:
            d, b, h = c
            sl = slice(h * DN_HEAD_DIM, (h + 1) * DN_HEAD_DIM)
            v_new[c] = (dirs[d][0][b, rows[d], sl].astype(F32) - ws[c][:L]).astype(BF16)
        for c in chains:
            d, b, h = c
            at = dirs[d][4][b, rows[d], h * DN_HEAD_DIM:h * DN_HEAD_DIM + L]
            av[c] = _dot(at, v_new[c])
        for c in chains:
            d, b, h = c
            sl = slice(h * DN_HEAD_DIM, (h + 1) * DN_HEAD_DIM)
            cl = (N_DIR + d) * DN_HEADS + h
            dirs[d][6][b, rows[d], sl] = (ws[c][L:] + av[c]).astype(BF16)
            state[c] = state[c] * egt[d, b][:, cl:cl + 1] + _dot_tn(dirs[d][3][b, rows[d], sl], v_new[c])
    for c in chains:
        s_ref[c] = state[c]


def _scan_chunk_index(i, n, *, batch, n_chunks):
    b = i % batch
    backward = i // batch
    return b * n_chunks + n + backward * (n_chunks - 1 - 2 * n)


def _gate_layouts(zg, *, batch, seq):
    n_chunks = seq // SCAN_CHUNK
    g = zg[:, :16].reshape(batch * seq, 2, N_DIR, 4)
    cols = g.transpose(2, 0, 1, 3).reshape(N_DIR, batch * seq, 8)
    rows = cols.reshape(N_DIR, batch * n_chunks, SCAN_CHUNK, 8).transpose(0, 1, 3, 2)
    return cols, rows


def _delta_scan(qkv, zg, gr, *, batch, seq):
    n = batch * seq
    T = SCAN_CHUNK * SCAN_STEP_CHUNKS
    steps = seq // T
    prepped = [t.reshape(N_DIR, batch, seq, DN_WIDTH) for t in _delta_prep(qkv, zg, gr, batch=batch, seq=seq)]
    zg3 = zg.reshape(batch, seq, LANES)
    fwd = lambda s: s
    bwd = lambda s: steps - 1 - s
    specs = []
    for d, pos in ((0, fwd), (1, bwd)):
        specs += [pl.BlockSpec((None, batch, T, DN_WIDTH), lambda s, d=d, pos=pos: (d, 0, pos(s), 0))] * 5
        specs += [pl.BlockSpec((batch, T, LANES), lambda s, pos=pos: (0, pos(s), 0))]
    out = jax.ShapeDtypeStruct((batch, seq, DN_WIDTH), BF16)
    o_f, o_b = pl.pallas_call(
        functools.partial(_delta_scan_kernel, batch=batch),
        grid=(steps,),
        in_specs=specs,
        out_specs=[pl.BlockSpec((batch, T, DN_WIDTH), lambda s: (0, fwd(s), 0)),
                   pl.BlockSpec((batch, T, DN_WIDTH), lambda s: (0, bwd(s), 0))],
        out_shape=[out, out],
        scratch_shapes=[pltpu.VMEM((N_DIR, batch, DN_HEADS, DN_HEAD_DIM, DN_HEAD_DIM), F32)],
        compiler_params=_params("arbitrary"),
        name="delta_scan",
    )(*prepped, zg3, *prepped, zg3)
    return o_f.reshape(n, DN_WIDTH), o_b.reshape(n, DN_WIDTH)


ML_PAIRS = ML_HEADS // 2
ML_AUG = 2 * ML_V_DIM
ML_KV_ROWS = ML_PAIRS * 2 * ML_QK_DIM


def _lane_cols(cols, width):
    rows = cols[0].shape[0]
    lane = lax.broadcasted_iota(jnp.int32, (rows, width), 1)
    out = jnp.zeros((rows, width), F32)
    for j, c in enumerate(cols):
        out = jnp.where(lane == j, c, out)
    return out


def _mlstm_prep_kernel(q_ref, k_ref, v_ref, gc_ref, gr_ref, qs_ref, iv_ref, kv_ref, aux_ref):
    G, L = PREP_GROUP, SCAN_CHUNK
    row = lax.broadcasted_iota(jnp.int32, (G, G), 0)
    col = lax.broadcasted_iota(jnp.int32, (G, G), 1)
    shift = L.bit_length() - 1
    same01 = _as01((row >> shift) == (col >> shift))
    masks = [_group_masks(False), _group_masks(True)]
    cum_c = [_as01(m[0]) for m in masks]
    cum_r = [_as01(m[2]) for m in masks]
    lane = lax.broadcasted_iota(jnp.int32, (G, LANES), 1)
    first_head = lane < ML_QK_DIM
    lane_r = lax.broadcasted_iota(jnp.int32, (1, G), 1)
    row_c = lax.broadcasted_iota(jnp.int32, (G, 1), 0)
    ones_v = jnp.ones((G, ML_V_DIM), BF16)
    scale = ML_QK_DIM ** -0.5
    heads = range(ML_HEADS)
    chains = [(h, d) for h in heads for d in range(N_DIR)]

    def step(i, carry):
        groups = range(PREP_STEP_GROUPS)
        gidx = [i * PREP_STEP_GROUPS + g for g in groups]
        rows = [pl.ds(pl.multiple_of(x * G, G), G) for x in gidx]
        gc = [gc_ref[r, :] for r in rows]
        gr = [gr_ref[x] for x in gidx]
        bt_c = [_exact_dot01(same01, x) for x in gc]
        bt_r = [_exact_dot01_r(x, same01) for x in gr]
        bc_c = [[_exact_dot01(cum_c[d], x) for d in range(N_DIR)] for x in gc]
        bc_r = [[_exact_dot01_r(x, cum_r[d]) for d in range(N_DIR)] for x in gr]
        gps = [(g, p) for g in groups for p in range(ML_PAIRS)]
        ghs = [(g, h) for g in groups for h in heads]
        chains = [(g, h, d) for g, h in ghs for d in range(N_DIR)]
        qp = {(g, p): q_ref[rows[g], p * LANES:(p + 1) * LANES].astype(F32) * scale for g, p in gps}
        kp16 = {(g, p): k_ref[rows[g], p * LANES:(p + 1) * LANES] for g, p in gps}
        kp = {c: kp16[c].astype(F32) for c in gps}
        v16 = {(g, h): v_ref[rows[g], h * ML_V_DIM:(h + 1) * ML_V_DIM].astype(BF16) for g, h in ghs}
        for g, p in gps:
            qs_ref[rows[g], p * LANES:(p + 1) * LANES] = qp[g, p].astype(BF16)
        own = [first_head, jnp.logical_not(first_head)]
        scores = {(g, h): _dot_nt(jnp.where(own[h % 2], qp[g, h // 2], 0.0).astype(BF16), kp16[g, h // 2])
                  for g, h in ghs}
        col = lambda c: N_DIR * ML_HEADS + c[2] * ML_HEADS + c[1]
        b_c = {c: bc_c[c[0]][c[2]][:, col(c):col(c) + 1] for c in chains}
        b_t = {c: bt_c[c[0]][:, col(c):col(c) + 1] for c in chains}
        b_r = {c: bc_r[c[0]][c[2]][col(c):col(c) + 1, :] for c in chains}
        li_c = {c: gc[c[0]][:, col(c) - N_DIR * ML_HEADS:col(c) - N_DIR * ML_HEADS + 1] for c in chains}
        li_r = {c: gr[c[0]][col(c) - N_DIR * ML_HEADS:col(c) - N_DIR * ML_HEADS + 1, :] for c in chains}
        d_mat = {c: jnp.where(masks[c[2]][0], b_c[c] - b_r[c] + li_r[c], NEG_BIG) for c in chains}
        d_max = {c: jnp.max(d_mat[c], axis=-1, keepdims=True) for c in chains}
        pmat = {c: (jnp.exp(d_mat[c] - d_max[c]) * scores[c[:2]]).astype(BF16) for c in chains}
        w_end_r = {c: bt_r[c[0]][col(c):col(c) + 1, :] - b_r[c] + li_r[c] for c in chains}
        w_max = {}
        for c in chains:
            wm = jnp.full((G, 1), NEG_BIG, F32)
            for j in range(G // L):
                in_j = jnp.logical_and(lane_r >= j * L, lane_r < (j + 1) * L)
                wm_j = jnp.max(jnp.where(in_j, w_end_r[c], NEG_BIG), axis=-1, keepdims=True)
                wm = jnp.where(jnp.logical_and(row_c >= j * L, row_c < (j + 1) * L), wm_j, wm)
            w_max[c] = wm
        sw0 = {c: jnp.exp(b_t[c] - b_c[c] + li_c[c] - w_max[c]) for c in chains}
        iv = {c: _dot(pmat[c], jnp.concatenate([v16[c[:2]], ones_v], axis=-1)) for c in chains}
        for c in chains:
            g, h, d = c
            iv_ref[d, rows[g], h * ML_AUG:(h + 1) * ML_AUG] = iv[c].astype(BF16)
        for g in groups:
            for d in range(N_DIR):
                src = N_DIR * ML_HEADS + d * ML_HEADS
                aux_ref[d, rows[g], :LANES] = jnp.where(
                    lane < ML_HEADS, pltpu.roll(bc_c[g][d], LANES - src, 1),
                    jnp.where(lane < 2 * ML_HEADS, pltpu.roll(bt_c[g], LANES - src + ML_HEADS, 1), 0.0))
                aux_ref[d, rows[g], LANES:] = _lane_cols(
                    [d_max[g, h, d] for h in heads] + [w_max[g, h, d] for h in heads], LANES)
        gpd = [(g, p, d) for g, p in gps for d in range(N_DIR)]
        ks = {(g, p, d): (kp[g, p] * jnp.where(first_head, sw0[g, 2 * p, d], sw0[g, 2 * p + 1, d])).astype(BF16)
              for g, p, d in gpd}
        vcat = {(g, p): jnp.concatenate([v16[g, 2 * p], v16[g, 2 * p + 1], ones_v], axis=-1) for g, p in gps}
        pair_chunks = [(g, p, d, j) for g, p, d in gpd for j in range(G // L)]
        kv = [_dot_tn(ks[g, p, d][j * L:(j + 1) * L], vcat[g, p][j * L:(j + 1) * L]) for g, p, d, j in pair_chunks]
        for (g, p, d, j), t in zip(pair_chunks, kv):
            top = jnp.concatenate([t[:ML_QK_DIM, :ML_V_DIM], t[:ML_QK_DIM, 2 * ML_V_DIM:]], axis=-1)
            bot = jnp.concatenate([t[ML_QK_DIM:, ML_V_DIM:2 * ML_V_DIM], t[ML_QK_DIM:, 2 * ML_V_DIM:]], axis=-1)
            r0 = pl.multiple_of(gidx[g] * (G // L * ML_KV_ROWS) + (j * ML_PAIRS + p) * LANES, LANES)
            kv_ref[d, pl.ds(r0, LANES), :] = jnp.concatenate([top, bot], axis=0).astype(BF16)
        return carry

    lax.fori_loop(0, q_ref.shape[0] // (G * PREP_STEP_GROUPS), step, 0)


def _mlstm_prep(z, zg, gr, *, batch, seq, q_col, k_col, v_col):
    n = batch * seq
    R, G, L = min(PREP_ROWS, seq), PREP_GROUP, SCAN_CHUNK
    steps = seq // R
    rowblk = lambda c: (lambda b, r: (b * steps + r, c))
    dirblk = lambda b, r: (0, b * steps + r, 0)
    kv_rows = ML_KV_ROWS // L
    return pl.pallas_call(
        _mlstm_prep_kernel,
        grid=(batch, steps),
        in_specs=[
            pl.BlockSpec((R, ML_QK_WIDTH), rowblk(q_col // ML_QK_WIDTH)),
            pl.BlockSpec((R, ML_QK_WIDTH), rowblk(k_col // ML_QK_WIDTH)),
            pl.BlockSpec((R, ML_V_WIDTH), rowblk(v_col // ML_V_WIDTH)),
            pl.BlockSpec((R, LANES), rowblk(0)),
            pl.BlockSpec((R // G, 16, G), lambda b, r: (b * steps + r, 0, 0)),
        ],
        out_specs=[
            pl.BlockSpec((R, ML_QK_WIDTH), rowblk(0)),
            pl.BlockSpec((N_DIR, R, ML_HEADS * ML_AUG), dirblk),
            pl.BlockSpec((N_DIR, R * kv_rows, ML_AUG), dirblk),
            pl.BlockSpec((N_DIR, R, 2 * LANES), dirblk),
        ],
        out_shape=[
            jax.ShapeDtypeStruct((n, ML_QK_WIDTH), BF16),
            jax.ShapeDtypeStruct((N_DIR, n, ML_HEADS * ML_AUG), BF16),
            jax.ShapeDtypeStruct((N_DIR, n * kv_rows, ML_AUG), BF16),
            jax.ShapeDtypeStruct((N_DIR, n, 2 * LANES), F32),
        ],
        compiler_params=_params("parallel", "parallel"),
        name="mlstm_prep",
    )(z, z, z, zg, gr)


def _mlstm_scan_kernel(qf, ivf, kvf, auxf, qb, ivb, kvb, auxb, of_ref, ob_ref, c_ref, m_ref, *, batch):
    L, C = SCAN_CHUNK, SCAN_STEP_CHUNKS

    @pl.when(pl.program_id(0) == 0)
    def _():
        c_ref[...] = jnp.zeros_like(c_ref)
        m_ref[...] = jnp.zeros_like(m_ref)

    dirs = ((qf, ivf, kvf, auxf, of_ref), (qb, ivb, kvb, auxb, ob_ref))
    lane = lax.broadcasted_iota(jnp.int32, (L, LANES), 1)
    keep = [jnp.where(lane < ML_QK_DIM, 1.0, 0.0).astype(BF16), jnp.where(lane < ML_QK_DIM, 0.0, 1.0).astype(BF16)]
    first_rows = lax.broadcasted_iota(jnp.int32, (2 * ML_QK_DIM, 1), 0) < ML_QK_DIM
    pair_chains = [(d, b, p) for d in range(N_DIR) for b in range(batch) for p in range(ML_PAIRS)]
    chains = [(d, b, h) for d in range(N_DIR) for b in range(batch) for h in range(ML_HEADS)]
    groups = [(d, b) for d in range(N_DIR) for b in range(batch)]
    H = ML_HEADS
    lane8 = lax.broadcasted_iota(jnp.int32, (SUBLANES, LANES), 1)
    sel_row = lax.broadcasted_iota(jnp.int32, (2 * LANES, LANES), 0) & (LANES - 1)
    lane_select = [_as01(sel_row == j) for j in range(2 * H)]
    state = {c: c_ref[c] for c in pair_chains}
    m_st = {g: m_ref[g] for g in groups}
    for step in range(C):
        chunk = {0: step, 1: C - 1 - step}
        rows = {d: slice(chunk[d] * L, (chunk[d] + 1) * L) for d in range(N_DIR)}
        qc = {}
        for c in pair_chains:
            d, b, p = c
            qp = dirs[d][0][b, rows[d], p * LANES:(p + 1) * LANES]
            qc[c] = _dot(jnp.concatenate([qp * keep[0], qp * keep[1]], axis=0), state[c].astype(BF16))
        w_prev, w_cur, floor = {}, {}, {}
        for g in groups:
            d, b = g
            aux = dirs[d][3][b, rows[d], :]
            x = aux[:, :LANES] + m_st[g][0:1, :]
            y = jnp.maximum(x, aux[:, LANES:])
            w_prev[g] = jnp.exp(x - y)
            w_cur[g] = jnp.exp(aux[:, LANES:] - y)
            floor[g] = jnp.exp(-y)
            y0 = jnp.broadcast_to(y[0:1, :], (SUBLANES, LANES))
            m_st[g] = jnp.where(lane8 < H, pltpu.roll(y0, LANES - H, 1), jnp.where(lane8 < 2 * H, y0, 0.0))
        tiles = [t[g] for g in groups for t in (w_prev, w_cur, floor)]
        per_row = _split_hi_lo(jnp.concatenate(tiles, axis=0))
        per_chunk = _split_hi_lo(jnp.concatenate([t[0:1, :] for t in tiles] + [tiles[0][0:SUBLANES, :]], axis=0))
        bc_row, bc_chunk = [], []
        for p in range(ML_PAIRS):
            both = _dot(per_row, jnp.concatenate([lane_select[2 * p], lane_select[2 * p + 1]], axis=-1))
            bc_row += [both[:, :LANES], both[:, LANES:]]
            both = _dot(per_chunk, jnp.concatenate([lane_select[H + 2 * p], lane_select[H + 2 * p + 1]], axis=-1))
            bc_chunk += [both[:, :LANES], both[:, LANES:]]
        numden = {}
        for c in chains:
            d, b, h = c
            r0 = (h % 2) * L
            t0 = 3 * groups.index((d, b)) * L
            wp = bc_row[h][t0:t0 + L, :]
            wc = bc_row[h][t0 + L:t0 + 2 * L, :]
            iv = dirs[d][1][b, rows[d], h * ML_AUG:(h + 1) * ML_AUG].astype(F32)
            qch = qc[d, b, h // 2][r0:r0 + L, :]
            numden[c] = jnp.concatenate([wp * qch[:, :ML_V_DIM] + wc * iv[:, :ML_V_DIM],
                                         wp * qch[:, ML_V_DIM:] + wc * iv[:, ML_V_DIM:]], axis=-1)
        for c in chains:
            d, b, h = c
            t0 = 3 * groups.index((d, b)) * L
            den = jnp.maximum(jnp.abs(numden[c][:, ML_V_DIM:]), bc_row[h][t0 + 2 * L:t0 + 3 * L, :])
            dirs[d][4][b, rows[d], h * ML_V_DIM:(h + 1) * ML_V_DIM] = (numden[c][:, :ML_V_DIM] / den).astype(BF16)
        for c in pair_chains:
            d, b, p = c
            r0 = (chunk[d] * ML_PAIRS + p) * LANES
            kv = dirs[d][2][b, r0:r0 + LANES, :].astype(F32)
            t0 = 3 * groups.index((d, b))
            cw = jnp.where(first_rows, bc_chunk[2 * p][t0:t0 + 1, :], bc_chunk[2 * p + 1][t0:t0 + 1, :])
            iw = jnp.where(first_rows, bc_chunk[2 * p][t0 + 1:t0 + 2, :], bc_chunk[2 * p + 1][t0 + 1:t0 + 2, :])
            state[c] = jnp.concatenate([cw * state[c][:, :ML_V_DIM] + iw * kv[:, :ML_V_DIM],
                                        cw * state[c][:, ML_V_DIM:] + iw * kv[:, ML_V_DIM:]], axis=-1)
    for c in pair_chains:
        c_ref[c] = state[c]
    for g in groups:
        m_ref[g] = m_st[g]


def _mlstm_scan(z, zg, gr, *, batch, seq, q_col, k_col, v_col):
    n = batch * seq
    T = SCAN_CHUNK * SCAN_STEP_CHUNKS
    steps = seq // T
    kv_rows = ML_KV_ROWS // SCAN_CHUNK
    qs, iv, kv, aux = _mlstm_prep(z, zg, gr, batch=batch, seq=seq, q_col=q_col, k_col=k_col, v_col=v_col)
    qs = qs.reshape(batch, seq, ML_QK_WIDTH)
    iv = iv.reshape(N_DIR, batch, seq, ML_HEADS * ML_AUG)
    kv = kv.reshape(N_DIR, batch, seq * kv_rows, ML_AUG)
    aux = aux.reshape(N_DIR, batch, seq, 2 * LANES)
    fwd = lambda s: s
    bwd = lambda s: steps - 1 - s
    specs = []
    for d, pos in ((0, fwd), (1, bwd)):
        specs += [
            pl.BlockSpec((batch, T, ML_QK_WIDTH), lambda s, pos=pos: (0, pos(s), 0)),
            pl.BlockSpec((None, batch, T, ML_HEADS * ML_AUG), lambda s, d=d, pos=pos: (d, 0, pos(s), 0)),
            pl.BlockSpec((None, batch, T * kv_rows, ML_AUG), lambda s, d=d, pos=pos: (d, 0, pos(s), 0)),
            pl.BlockSpec((None, batch, T, 2 * LANES), lambda s, d=d, pos=pos: (d, 0, pos(s), 0)),
        ]
    out = jax.ShapeDtypeStruct((batch, seq, ML_V_WIDTH), BF16)
    h_f, h_b = pl.pallas_call(
        functools.partial(_mlstm_scan_kernel, batch=batch),
        grid=(steps,),
        in_specs=specs,
        out_specs=[pl.BlockSpec((batch, T, ML_V_WIDTH), lambda s: (0, fwd(s), 0)),
                   pl.BlockSpec((batch, T, ML_V_WIDTH), lambda s: (0, bwd(s), 0))],
        out_shape=[out, out],
        scratch_shapes=[
            pltpu.VMEM((N_DIR, batch, ML_PAIRS, 2 * ML_QK_DIM, ML_AUG), F32),
            pltpu.VMEM((N_DIR, batch, SUBLANES, LANES), F32),
        ],
        compiler_params=_params("arbitrary"),
        name="mlstm_scan",
    )(qs, iv, kv, aux, qs, iv, kv, aux)
    return h_f.reshape(n, ML_V_WIDTH), h_b.reshape(n, ML_V_WIDTH)


def _head_rms_norm(x, g, n_heads, head_dim):
    parts = []
    for h in range(n_heads):
        xh = x[:, h * head_dim:(h + 1) * head_dim]
        ms = jnp.mean(xh * xh, axis=-1, keepdims=True)
        parts.append(xh * lax.rsqrt(ms + NORM_EPS) * g)
    return jnp.concatenate(parts, axis=-1)


def _layer_norm(x, g, b):
    mu = jnp.mean(x, axis=-1, keepdims=True)
    xc = x - mu
    var = jnp.mean(xc * xc, axis=-1, keepdims=True)
    return xc * lax.rsqrt(var + LN_EPS) * g + b


def _swiglu_residual(x, g_ref, w1_ref, w3_ref, w2_ref, fg_ref, final_norm):
    ms = jnp.mean(x * x, axis=-1, keepdims=True)
    hn = (x * lax.rsqrt(ms + NORM_EPS) * g_ref[...]).astype(BF16)
    a = _dot(hn, w1_ref[...])
    b = _dot(hn, w3_ref[...])
    y = x + _dot((a * jax.nn.sigmoid(a) * b).astype(BF16), w2_ref[...])
    if final_norm:
        ms = jnp.mean(y * y, axis=-1, keepdims=True)
        y = y * lax.rsqrt(ms + NORM_EPS) * fg_ref[...]
    return y


def _ffn_operands(g, w1, w3, w2, final_g):
    d, f = w1.shape
    specs = [_resident((1, d)), _resident((d, f)), _resident((d, f)), _resident((f, d)), _resident((1, d))]
    return specs, (g.reshape(1, d), w1.astype(BF16), w3.astype(BF16), w2.astype(BF16), final_g.reshape(1, d))


def _even_block_kernel(h_ref, c_ref, of_ref, ob_ref, gate_ref, lng_ref, lnb_ref, ng_ref, wa_ref, wb_ref,
                       g_ref, w1_ref, w3_ref, w2_ref, fg_ref, o_ref, *, final_norm):
    ya = _layer_norm(c_ref[...].astype(F32), lng_ref[...], lnb_ref[...])
    ya = ya * jax.nn.sigmoid(ya)
    o = _head_rms_norm(of_ref[...].astype(F32) + ob_ref[...].astype(F32), ng_ref[...], DN_HEADS, DN_HEAD_DIM)
    gate = gate_ref[...].astype(F32)
    o = o * (gate * jax.nn.sigmoid(gate))
    x = h_ref[...] + _dot(ya.astype(BF16), wa_ref[...]) + _dot(o.astype(BF16), wb_ref[...])
    o_ref[...] = _swiglu_residual(x, g_ref, w1_ref, w3_ref, w2_ref, fg_ref, final_norm)


def _even_block(h, conv, o_fwd, o_bwd, z, ln_g, ln_b, norm_g, w_out, ffn, *, gate_col, final_norm):
    n, d = h.shape
    tm = min(ROW_TILE, n)
    gb = gate_col // DN_WIDTH
    row = lambda i: (i, 0)
    ffn_specs, ffn_args = _ffn_operands(*ffn)
    return pl.pallas_call(
        functools.partial(_even_block_kernel, final_norm=final_norm),
        grid=(n // tm,),
        in_specs=[
            pl.BlockSpec((tm, d), row),
            pl.BlockSpec((tm, CONV_CH), row),
            pl.BlockSpec((tm, DN_WIDTH), row),
            pl.BlockSpec((tm, DN_WIDTH), row),
            pl.BlockSpec((tm, DN_WIDTH), lambda i: (i, gb)),
            _resident((1, CONV_CH)),
            _resident((1, CONV_CH)),
            _resident((1, DN_HEAD_DIM)),
            _resident((CONV_CH, d)),
            _resident((DN_WIDTH, d)),
        ] + ffn_specs,
        out_specs=pl.BlockSpec((tm, d), row),
        out_shape=jax.ShapeDtypeStruct((n, d), F32),
        compiler_params=_params("parallel"),
        name="even_block",
    )(h, conv, o_fwd, o_bwd, z, ln_g.reshape(1, -1), ln_b.reshape(1, -1), norm_g.reshape(1, -1),
      w_out[:CONV_CH].astype(BF16), w_out[CONV_CH:].astype(BF16), *ffn_args)


def _odd_block_kernel(h_ref, u_ref, vp_ref, hf_ref, hb_ref, op_ref, lng_ref, lnb_ref, sgw_ref, sgb_ref, ng_ref,
                      wa_ref, wb_ref, g_ref, w1_ref, w3_ref, w2_ref, fg_ref, o_ref, *, final_norm):
    tm = h_ref.shape[0]
    u = jax.nn.gelu(u_ref[...].astype(F32))
    vv = _layer_norm(jax.nn.gelu(vp_ref[...].astype(F32)), lng_ref[...], lnb_ref[...]).astype(BF16)
    sgb = sgb_ref[...]
    rows = []
    for c in range(tm // SG_CHUNK):
        parts = []
        for g in range(SG_GROUPS):
            blk = vv[c * SG_CHUNK:(c + 1) * SG_CHUNK, g * SG_GROUP_DIM:(g + 1) * SG_GROUP_DIM]
            parts.append(_dot(sgw_ref[g], blk) + sgb[:, g:g + 1])
        rows.append(jnp.concatenate(parts, axis=-1))
    yc = u * jnp.concatenate(rows, axis=0)
    hd = _head_rms_norm(hf_ref[...].astype(F32) + hb_ref[...].astype(F32), ng_ref[...], ML_HEADS, ML_V_DIM)
    hd = hd * jax.nn.sigmoid(op_ref[...].astype(F32))
    x = h_ref[...] + _dot(yc.astype(BF16), wa_ref[...]) + _dot(hd.astype(BF16), wb_ref[...])
    o_ref[...] = _swiglu_residual(x, g_ref, w1_ref, w3_ref, w2_ref, fg_ref, final_norm)


def _odd_block(h, z, h_fwd, h_bwd, ln_g, ln_b, sg_w, sg_b, norm_g, w_out, ffn, *, u_col, v_col, o_col, final_norm):
    n, d = h.shape
    tm = min(ROW_TILE, n)
    row = lambda i: (i, 0)
    ffn_specs, ffn_args = _ffn_operands(*ffn)
    return pl.pallas_call(
        functools.partial(_odd_block_kernel, final_norm=final_norm),
        grid=(n // tm,),
        in_specs=[
            pl.BlockSpec((tm, d), row),
            pl.BlockSpec((tm, SG_WIDTH), lambda i: (i, u_col // SG_WIDTH)),
            pl.BlockSpec((tm, SG_WIDTH), lambda i: (i, v_col // SG_WIDTH)),
            pl.BlockSpec((tm, ML_V_WIDTH), row),
            pl.BlockSpec((tm, ML_V_WIDTH), row),
            pl.BlockSpec((tm, ML_V_WIDTH), lambda i: (i, o_col // ML_V_WIDTH)),
            _resident((1, SG_WIDTH)),
            _resident((1, SG_WIDTH)),
            _resident((SG_GROUPS, SG_CHUNK, SG_CHUNK)),
            _resident((SG_CHUNK, SG_GROUPS)),
            _resident((1, ML_V_DIM)),
            _resident((SG_WIDTH, d)),
            _resident((ML_V_WIDTH, d)),
        ] + ffn_specs,
        out_specs=pl.BlockSpec((tm, d), row),
        out_shape=jax.ShapeDtypeStruct((n, d), F32),
        compiler_params=_params("parallel"),
        name="odd_block",
    )(h, z, z, h_fwd, h_bwd, z, ln_g.reshape(1, -1), ln_b.reshape(1, -1), sg_w.astype(BF16), sg_b.T,
      norm_g.reshape(1, -1), w_out[:SG_WIDTH].astype(BF16), w_out[SG_WIDTH:].astype(BF16), *ffn_args)


def _gate_weight(w_gate_cols):
    d, c = w_gate_cols.shape
    return jnp.zeros((d, LANES), F32).at[:, :c].set(w_gate_cols).astype(BF16)


def _even_layer(h, j, p, ffn, *, batch, seq, final_norm):
    main = 2 * CONV_CH + 4 * DN_WIDTH
    w_in = p["ev_w_in"][j]
    gate_params = jnp.zeros((SUBLANES, LANES), F32)
    gate_params = gate_params.at[0, 8:16].set(p["ev_dn_a_log"][j].reshape(-1))
    gate_params = gate_params.at[1, 8:16].set(p["ev_dn_dt_bias"][j].reshape(-1))
    z, zg, gr = _in_proj(h, p["mix_norm_g"][2 * j],w_in[:, :main].astype(BF16), _gate_weight(w_in[:, main:]),
                     gate_params, even=True)
    conv = _conv_glu(z, p["ev_conv_w"][j], p["ev_conv_b"][j], batch=batch, seq=seq)
    qkv = _conv_qkv(z, p["ev_dn_conv_w"][j], batch=batch, seq=seq, col0=2 * CONV_CH)
    o_fwd, o_bwd = _delta_scan(qkv, zg, gr, batch=batch, seq=seq)
    return _even_block(h, conv, o_fwd, o_bwd, z, p["ev_conv_ln_g"][j], p["ev_conv_ln_b"][j], p["ev_dn_norm_g"][j],
                       p["ev_w_out"][j], ffn, gate_col=2 * CONV_CH + 3 * DN_WIDTH, final_norm=final_norm)


def _odd_layer(h, j, p, ffn, *, batch, seq, final_norm):
    main = 2 * SG_WIDTH + 2 * ML_QK_WIDTH + 2 * ML_V_WIDTH
    w_in = p["od_w_in"][j]
    gate_params = jnp.zeros((SUBLANES, LANES), F32)
    gate_params = gate_params.at[0, 0:8].set(p["od_ml_i_bias"][j].reshape(-1))
    gate_params = gate_params.at[1, 8:16].set(p["od_ml_f_bias"][j].reshape(-1))
    z, zg, gr = _in_proj(h, p["mix_norm_g"][2 * j + 1],w_in[:, :main].astype(BF16), _gate_weight(w_in[:, main:]),
                     gate_params, even=False)
    q_col = 2 * SG_WIDTH
    k_col = q_col + ML_QK_WIDTH
    v_col = k_col + ML_QK_WIDTH
    o_col = v_col + ML_V_WIDTH
    h_fwd, h_bwd = _mlstm_scan(z, zg, gr, batch=batch, seq=seq, q_col=q_col, k_col=k_col, v_col=v_col)
    return _odd_block(h, z, h_fwd, h_bwd, p["od_sg_ln_g"][j], p["od_sg_ln_b"][j], p["od_sg_w"][j], p["od_sg_b"][j],
                      p["od_ml_norm_g"][j], p["od_w_out"][j], ffn, u_col=0, v_col=SG_WIDTH, o_col=o_col,
                      final_norm=final_norm)


def kernel(x, mix_norm_g, ev_w_in, ev_conv_w, ev_conv_b, ev_conv_ln_g, ev_conv_ln_b, ev_dn_conv_w, ev_dn_a_log, ev_dn_dt_bias, ev_dn_norm_g, ev_w_out, od_w_in, od_sg_ln_g, od_sg_ln_b, od_sg_w, od_sg_b, od_ml_i_bias, od_ml_f_bias, od_ml_norm_g, od_w_out, ffn_norm_g, ffn_w1, ffn_w3, ffn_w2, final_norm_g):
    p = dict(mix_norm_g=mix_norm_g, ev_w_in=ev_w_in, ev_conv_w=ev_conv_w, ev_conv_b=ev_conv_b,
             ev_conv_ln_g=ev_conv_ln_g, ev_conv_ln_b=ev_conv_ln_b, ev_dn_conv_w=ev_dn_conv_w,
             ev_dn_a_log=ev_dn_a_log, ev_dn_dt_bias=ev_dn_dt_bias, ev_dn_norm_g=ev_dn_norm_g, ev_w_out=ev_w_out,
             od_w_in=od_w_in, od_sg_ln_g=od_sg_ln_g, od_sg_ln_b=od_sg_ln_b, od_sg_w=od_sg_w, od_sg_b=od_sg_b,
             od_ml_i_bias=od_ml_i_bias, od_ml_f_bias=od_ml_f_bias, od_ml_norm_g=od_ml_norm_g, od_w_out=od_w_out)
    batch, seq, d = x.shape
    depth = mix_norm_g.shape[0]
    h = x.reshape(batch * seq, d)
    for layer in range(depth):
        ffn = (ffn_norm_g[layer], ffn_w1[layer], ffn_w3[layer], ffn_w2[layer], final_norm_g)
        mixer_layer = _even_layer if layer % 2 == 0 else _odd_layer
        h = mixer_layer(h, layer // 2, p, ffn, batch=batch, seq=seq, final_norm=layer == depth - 1)
    return h.reshape(batch, seq, d)
```

```python
import functools

import jax
import jax.numpy as jnp
from jax import lax
from jax.experimental import pallas as pl
from jax.experimental.pallas import tpu as pltpu

NORM_EPS = 1e-6
LN_EPS = 1e-5
N_DIR = 2

CONV_CH = 512
DN_HEADS = 4
DN_HEAD_DIM = 128
DN_WIDTH = DN_HEADS * DN_HEAD_DIM
SG_GROUPS = 4
SG_GROUP_DIM = 128
SG_WIDTH = SG_GROUPS * SG_GROUP_DIM
SG_CHUNK = 128
ML_HEADS = 4
ML_QK_DIM = 64
ML_V_DIM = 128
ML_QK_WIDTH = ML_HEADS * ML_QK_DIM
ML_V_WIDTH = ML_HEADS * ML_V_DIM

LANES = 128
SUBLANES = 8
VMEM_LIMIT_BYTES = 56 * 1024 * 1024

SCAN_CHUNK = 64
SCAN_STEP_CHUNKS = 2
PREP_GROUP = 128
PREP_ROWS = 512
DN_PREP_STEP_GROUPS = 2
ML_PREP_STEP_GROUPS = 4
ROW_TILE = 512
IN_ROW_TILE = 1024
CONV_TILE = 128
QKV_CONV_TILES = 4
CONV_PAD = 16
NEG_BIG = -1e30

BF16 = jnp.bfloat16
F32 = jnp.float32


def _params(*sem):
    return pltpu.CompilerParams(dimension_semantics=sem, vmem_limit_bytes=VMEM_LIMIT_BYTES)


def _resident(shape):
    return pl.BlockSpec(shape, lambda *_: (0,) * len(shape), pipeline_mode=pl.Buffered(1))


def _dot(a, b):
    return jnp.dot(a, b, preferred_element_type=F32)


def _dot_nt(a, b):
    return lax.dot_general(a, b, (((1,), (1,)), ((), ())), preferred_element_type=F32)


def _dot_tn(a, b):
    return lax.dot_general(a, b, (((0,), (0,)), ((), ())), preferred_element_type=F32)


def _exact_dot01(t01, x):
    x1 = x.astype(BF16)
    r1 = x - x1.astype(F32)
    x2 = r1.astype(BF16)
    x3 = (r1 - x2.astype(F32)).astype(BF16)
    return _dot(t01, x1) + _dot(t01, x2) + _dot(t01, x3)


def _exact_dot01_r(x, t01):
    x1 = x.astype(BF16)
    r1 = x - x1.astype(F32)
    x2 = r1.astype(BF16)
    x3 = (r1 - x2.astype(F32)).astype(BF16)
    return _dot(x1, t01) + _dot(x2, t01) + _dot(x3, t01)


def _in_proj_kernel(h_ref, g_ref, w_ref, wg_ref, gp_ref, zm_ref, zg_ref, zgt_ref, *, even):
    x = h_ref[...]
    ms = jnp.mean(x * x, axis=-1, keepdims=True)
    hn = (x * lax.rsqrt(ms + NORM_EPS) * g_ref[...]).astype(BF16)
    zm_ref[...] = _dot(hn, w_ref[...]).astype(zm_ref.dtype)
    zg = _dot(hn, wg_ref[...])
    p0 = gp_ref[0:1, :]
    p1 = gp_ref[1:2, :]
    lane = lax.broadcasted_iota(jnp.int32, zg.shape, 1)
    if even:
        first = jax.nn.sigmoid(zg)
        second = -jnp.exp(p0) * jax.nn.softplus(zg + p1)
    else:
        first = zg + p0
        second = jax.nn.log_sigmoid(zg + p1)
    gates = jnp.where(lane < 8, first, second)
    zg_ref[...] = gates
    for s in range(gates.shape[0] // PREP_GROUP):
        zgt_ref[s] = gates[s * PREP_GROUP:(s + 1) * PREP_GROUP, :].T[:2 * SUBLANES, :]


def _in_proj(h, g, w_main, w_gate, gate_params, *, even):
    n, d = h.shape
    c = w_main.shape[1]
    tm = min(IN_ROW_TILE, n)
    return pl.pallas_call(
        functools.partial(_in_proj_kernel, even=even),
        grid=(n // tm,),
        in_specs=[
            pl.BlockSpec((tm, d), lambda i: (i, 0)),
            _resident((1, d)),
            _resident((d, c)),
            _resident((d, LANES)),
            _resident((SUBLANES, LANES)),
        ],
        out_specs=[
            pl.BlockSpec((tm, c), lambda i: (i, 0)),
            pl.BlockSpec((tm, LANES), lambda i: (i, 0)),
            pl.BlockSpec((tm // PREP_GROUP, 2 * SUBLANES, PREP_GROUP), lambda i: (i, 0, 0)),
        ],
        out_shape=[jax.ShapeDtypeStruct((n, c), BF16), jax.ShapeDtypeStruct((n, LANES), F32),
                   jax.ShapeDtypeStruct((n // PREP_GROUP, 2 * SUBLANES, PREP_GROUP), F32)],
        compiler_params=_params("parallel"),
        name="in_proj_even" if even else "in_proj_odd",
    )(h, g.reshape(1, d), w_main, w_gate, gate_params)


def _conv_taps(pad_ref, w_ref, o_ref, bias, *, seq, width, post, tiles_per_step=1):
    half = width // 2
    tiles_per_step = min(tiles_per_step, seq // CONV_TILE)
    step_rows = CONV_TILE * tiles_per_step

    def body(i, carry):
        t0 = pl.multiple_of(i * step_rows, step_rows)
        accs = []
        for s in range(tiles_per_step):
            acc = jnp.zeros((CONV_TILE, LANES), F32)
            for j in range(width):
                acc = acc + pad_ref[pl.ds(t0 + (s * CONV_TILE + CONV_PAD - half + j), CONV_TILE), :] * w_ref[j:j + 1, :]
            accs.append(acc if bias is None else acc + bias)
        for s, out in enumerate(post(accs)):
            o_ref[pl.ds(t0 + s * CONV_TILE, CONV_TILE), :] = out.astype(o_ref.dtype)
        return carry

    lax.fori_loop(0, seq // step_rows, body, 0)


def _fill_padded(pad_ref, x, seq):
    zeros = jnp.zeros((CONV_PAD, LANES), F32)
    pad_ref[0:CONV_PAD, :] = zeros
    pad_ref[CONV_PAD + seq:CONV_PAD + seq + CONV_PAD, :] = zeros
    pad_ref[CONV_PAD:CONV_PAD + seq, :] = x


def _conv_glu_kernel(av_ref, ag_ref, w_ref, b_ref, o_ref, pad_ref, *, seq, width):
    _fill_padded(pad_ref, av_ref[...].astype(F32) * jax.nn.sigmoid(ag_ref[...].astype(F32)), seq)
    _conv_taps(pad_ref, w_ref, o_ref, b_ref[...], seq=seq, width=width, post=lambda tiles: tiles)


def _conv_glu(z, conv_w, conv_b, *, batch, seq):
    width = conv_w.shape[0]
    nblk = CONV_CH // LANES
    return pl.pallas_call(
        functools.partial(_conv_glu_kernel, seq=seq, width=width),
        grid=(batch, nblk),
        in_specs=[
            pl.BlockSpec((seq, LANES), lambda b, c: (b, c)),
            pl.BlockSpec((seq, LANES), lambda b, c: (b, nblk + c)),
            pl.BlockSpec((width, LANES), lambda b, c: (0, c)),
            pl.BlockSpec((1, LANES), lambda b, c: (0, c)),
        ],
        out_specs=pl.BlockSpec((seq, LANES), lambda b, c: (b, c)),
        out_shape=jax.ShapeDtypeStruct((batch * seq, CONV_CH), BF16),
        scratch_shapes=[pltpu.VMEM((seq + 2 * CONV_PAD, LANES), F32)],
        compiler_params=_params("parallel", "parallel"),
        name="conv_glu",
    )(z, z, conv_w, conv_b.reshape(1, CONV_CH))


def _conv_qkv_kernel(x_ref, w_ref, o_ref, pad_ref, *, seq, width, n_norm_blocks):
    _fill_padded(pad_ref, x_ref[...].astype(F32), seq)
    normalise = pl.program_id(1) < n_norm_blocks

    def post(tiles):
        ys = [a * jax.nn.sigmoid(a) for a in tiles]
        sq = [jnp.sum(y * y, axis=-1, keepdims=True) for y in ys]
        inv = [lax.rsqrt(s + NORM_EPS) for s in sq]
        return [jnp.where(normalise, y * r, y) for y, r in zip(ys, inv)]

    _conv_taps(pad_ref, w_ref, o_ref, None, seq=seq, width=width, post=post, tiles_per_step=QKV_CONV_TILES)


def _conv_qkv(z, dn_conv_w, *, batch, seq, col0):
    width = dn_conv_w.shape[0]
    nblk = 3 * DN_WIDTH // LANES
    blk0 = col0 // LANES
    return pl.pallas_call(
        functools.partial(_conv_qkv_kernel, seq=seq, width=width, n_norm_blocks=2 * DN_HEADS),
        grid=(batch, nblk),
        in_specs=[
            pl.BlockSpec((seq, LANES), lambda b, c: (b, blk0 + c)),
            pl.BlockSpec((width, LANES), lambda b, c: (0, c)),
        ],
        out_specs=pl.BlockSpec((seq, LANES), lambda b, c: (b, c)),
        out_shape=jax.ShapeDtypeStruct((batch * seq, 3 * DN_WIDTH), BF16),
        scratch_shapes=[pltpu.VMEM((seq + 2 * CONV_PAD, LANES), F32)],
        compiler_params=_params("parallel", "parallel"),
        name="conv_qkv",
    )(z, dn_conv_w)


def _tri_masks(n, backward):
    row = lax.broadcasted_iota(jnp.int32, (n, n), 0)
    col = lax.broadcasted_iota(jnp.int32, (n, n), 1)
    d = jnp.where(backward, col - row, row - col)
    return row, col, d >= 0, d > 0, d <= 0


def _unit_tri_inverse(a, row, col, n=None):
    return _unit_tri_inverses([a], row, col, a.shape[0] if n is None else n)[0]


def _split_hi_lo(x):
    hi = x.astype(BF16)
    return jnp.concatenate([hi, (x - hi.astype(F32)).astype(BF16)], axis=-1)


def _bf16_all(mats):
    return [m.astype(BF16) for m in mats]


def _unit_tri_inverses(mats, row, col, n):
    eye = (row == col).astype(F32)
    same = (row >> 3) == (col >> 3)
    l8 = [jnp.where(same, a, 0.0) for a in mats]
    l8h = _bf16_all(l8)
    l2h = _bf16_all([_dot(p, p) for p in l8h])
    l4h = _bf16_all([_dot(p, p) for p in l2h])
    x = [eye - p for p in l8]
    x = [xi + _dot(xi.astype(BF16), p) for xi, p in zip(x, l2h)]
    x = [xi + _dot(xi.astype(BF16), p) for xi, p in zip(x, l4h)]
    shift = 3
    while (1 << shift) < n:
        same_next = (row >> (shift + 1)) == (col >> (shift + 1))
        off = jnp.logical_and(same_next, jnp.logical_not(same))
        ch = _bf16_all([jnp.where(off, a, 0.0) for a in mats])
        xh = _bf16_all(x)
        xch = _bf16_all([_dot(p, c) for p, c in zip(xh, ch)])
        x = [xi - _dot(p, q) for xi, p, q in zip(x, xch, xh)]
        same = same_next
        shift += 1
    return x


def _group_masks(backward):
    n = PREP_GROUP
    row = lax.broadcasted_iota(jnp.int32, (n, n), 0)
    col = lax.broadcasted_iota(jnp.int32, (n, n), 1)
    shift = SCAN_CHUNK.bit_length() - 1
    same = (row >> shift) == (col >> shift)
    d = (col - row) if backward else (row - col)
    land = jnp.logical_and
    return land(same, d >= 0), land(same, d > 0), land(same, d <= 0)


def _as01(mask):
    return jnp.where(mask, 1.0, 0.0).astype(BF16)


def _delta_prep_kernel(q_ref, k_ref, v_ref, gc_ref, gr_ref, u_ref, w_ref, qg_ref, kg_ref, at_ref):
    G, L = PREP_GROUP, SCAN_CHUNK
    row = lax.broadcasted_iota(jnp.int32, (G, G), 0)
    col = lax.broadcasted_iota(jnp.int32, (G, G), 1)
    shift = L.bit_length() - 1
    same01 = _as01((row >> shift) == (col >> shift))
    masks = [_group_masks(False), _group_masks(True)]
    cum_c = [_as01(m[0]) for m in masks]
    cum_r = [_as01(m[2]) for m in masks]
    fold = _as01((lax.broadcasted_iota(jnp.int32, (G, L), 0) & (L - 1)) == lax.broadcasted_iota(jnp.int32, (G, L), 1))
    pad = jnp.zeros((G, DN_HEAD_DIM - L), BF16)
    scale = DN_HEAD_DIM ** -0.5

    step_groups = min(DN_PREP_STEP_GROUPS, q_ref.shape[0] // G)

    def step(i, carry):
        groups = range(step_groups)
        rows = [pl.ds(pl.multiple_of((i * step_groups + g) * G, G), G) for g in groups]
        gc = [gc_ref[r, :] for r in rows]
        gr = [gr_ref[i * step_groups + g] for g in groups]
        g_tot = [_exact_dot01(same01, x) for x in gc]
        g_cum_c = [[_exact_dot01(cum_c[d], x) for d in range(N_DIR)] for x in gc]
        g_cum_r = [[_exact_dot01_r(x, cum_r[d]) for d in range(N_DIR)] for x in gr]
        heads = range(DN_HEADS)
        lanes = [slice(h * DN_HEAD_DIM, (h + 1) * DN_HEAD_DIM) for h in heads]
        gh = [(g, h) for g in groups for h in heads]
        chains = [(g, h, d) for g, h in gh for d in range(N_DIR)]
        k16 = {(g, h): k_ref[rows[g], lanes[h]] for g, h in gh}
        q = {(g, h): q_ref[rows[g], lanes[h]].astype(F32) for g, h in gh}
        k = {c: k16[c].astype(F32) for c in gh}
        kk = {c: _dot_nt(k16[c], k16[c]) for c in gh}
        qk = {c: _dot_nt((q[c] * scale).astype(BF16), k16[c]) for c in gh}
        beta, g_c, g_t, decay = {}, {}, {}, {}
        for c in chains:
            g, h, d = c
            cb = d * DN_HEADS + h
            cl = N_DIR * DN_HEADS + cb
            beta[c] = gc[g][:, cb:cb + 1]
            g_c[c] = g_cum_c[g][d][:, cl:cl + 1]
            g_t[c] = g_tot[g][:, cl:cl + 1]
            g_r = g_cum_r[g][d][cl:cl + 1, :]
            decay[c] = jnp.exp(jnp.where(masks[d][0], g_c[c] - g_r, NEG_BIG))
        a = [jnp.where(masks[c[2]][1], kk[c[:2]] * decay[c], 0.0) * beta[c] for c in chains]
        x = _unit_tri_inverses(a, row, col, L)
        eg = {c: jnp.exp(g_c[c]) for c in chains}
        rhs = [jnp.concatenate([v_ref[rows[c[0]], lanes[c[1]]].astype(F32) * beta[c], k[c[:2]] * (beta[c] * eg[c])],
                               axis=-1).astype(BF16) for c in chains]
        uw = [_dot(xi.astype(BF16), r) for xi, r in zip(x, rhs)]
        attn = [jnp.where(masks[c[2]][0], qk[c[:2]] * decay[c], 0.0).astype(BF16) for c in chains]
        at = [_dot(p, fold) for p in attn]
        for c, uwi, ati in zip(chains, uw, at):
            g, h, d = c
            r, sl = rows[g], lanes[h]
            u_ref[d, r, sl] = uwi[:, :DN_HEAD_DIM].astype(BF16)
            w_ref[d, r, sl] = uwi[:, DN_HEAD_DIM:].astype(BF16)
            qg_ref[d, r, sl] = (q[g, h] * (scale * eg[c])).astype(BF16)
            kg_ref[d, r, sl] = (k[g, h] * jnp.exp(g_t[c] - g_c[c])).astype(BF16)
            at_ref[d, r, sl] = jnp.concatenate([ati.astype(BF16), pad], axis=-1)
        return carry

    lax.fori_loop(0, q_ref.shape[0] // (G * step_groups), step, 0)


def _delta_prep(qkv, zg, gr, *, batch, seq):
    n = batch * seq
    R, G = min(PREP_ROWS, seq), PREP_GROUP
    steps = seq // R
    rowblk = lambda c: (lambda b, r: (b * steps + r, c))
    out = jax.ShapeDtypeStruct((N_DIR, n, DN_WIDTH), BF16)
    out_spec = pl.BlockSpec((N_DIR, R, DN_WIDTH), lambda b, r: (0, b * steps + r, 0))
    return pl.pallas_call(
        _delta_prep_kernel,
        grid=(batch, steps),
        in_specs=[
            pl.BlockSpec((R, DN_WIDTH), rowblk(0)),
            pl.BlockSpec((R, DN_WIDTH), rowblk(1)),
            pl.BlockSpec((R, DN_WIDTH), rowblk(2)),
            pl.BlockSpec((R, LANES), rowblk(0)),
            pl.BlockSpec((R // G, 16, G), lambda b, r: (b * steps + r, 0, 0)),
        ],
        out_specs=[out_spec] * 5,
        out_shape=[out] * 5,
        compiler_params=_params("parallel", "parallel"),
        name="delta_prep",
    )(qkv, qkv, qkv, zg, gr)


def _delta_scan_kernel(uf, wf, qgf, kgf, atf, zgf, ub, wb, qgb, kgb, atb, zgb, of_ref, ob_ref, s_ref, *, batch):
    L, C = SCAN_CHUNK, SCAN_STEP_CHUNKS

    @pl.when(pl.program_id(0) == 0)
    def _():
        s_ref[...] = jnp.zeros_like(s_ref)

    dirs = ((uf, wf, qgf, kgf, atf, zgf, of_ref), (ub, wb, qgb, kgb, atb, zgb, ob_ref))

    chains = [(d, b, h) for d in range(N_DIR) for b in range(batch) for h in range(DN_HEADS)]
    state = {c: s_ref[c] for c in chains}
    for step in range(C):
        chunk = {0: step, 1: C - 1 - step}
        rows = {d: slice(chunk[d] * L, (chunk[d] + 1) * L) for d in range(N_DIR)}
        egt = {(d, b): jnp.exp(jnp.sum(dirs[d][5][b, rows[d], :], axis=0, keepdims=True))
               for d in range(N_DIR) for b in range(batch)}
        ws, v_new, av = {}, {}, {}
        for c in chains:
            d, b, h = c
            sl = slice(h * DN_HEAD_DIM, (h + 1) * DN_HEAD_DIM)
            wq = jnp.concatenate([dirs[d][1][b, rows[d], sl], dirs[d][2][b, rows[d], sl]], axis=0)
            ws[c] = _dot(wq, state[c].astype(BF16))
        for c in chains:
            d, b, h = c
            sl = slice(h * DN_HEAD_DIM, (h + 1) * DN_HEAD_DIM)
            v_new[c] = (dirs[d][0][b, rows[d], sl].astype(F32) - ws[c][:L]).astype(BF16)
        for c in chains:
            d, b, h = c
            at = dirs[d][4][b, rows[d], h * DN_HEAD_DIM:h * DN_HEAD_DIM + L]
            av[c] = _dot(at, v_new[c])
        for c in chains:
            d, b, h = c
            sl = slice(h * DN_HEAD_DIM, (h + 1) * DN_HEAD_DIM)
            cl = (N_DIR + d) * DN_HEADS + h
            dirs[d][6][b, rows[d], sl] = (ws[c][L:] + av[c]).astype(BF16)
            state[c] = state[c] * egt[d, b][:, cl:cl + 1] + _dot_tn(dirs[d][3][b, rows[d], sl], v_new[c])
    for c in chains:
        s_ref[c] = state[c]


def _scan_chunk_index(i, n, *, batch, n_chunks):
    b = i % batch
    backward = i // batch
    return b * n_chunks + n + backward * (n_chunks - 1 - 2 * n)


def _gate_layouts(zg, *, batch, seq):
    n_chunks = seq // SCAN_CHUNK
    g = zg[:, :16].reshape(batch * seq, 2, N_DIR, 4)
    cols = g.transpose(2, 0, 1, 3).reshape(N_DIR, batch * seq, 8)
    rows = cols.reshape(N_DIR, batch * n_chunks, SCAN_CHUNK, 8).transpose(0, 1, 3, 2)
    return cols, rows


def _delta_scan(qkv, zg, gr, *, batch, seq):
    n = batch * seq
    T = SCAN_CHUNK * SCAN_STEP_CHUNKS
    steps = seq // T
    prepped = [t.reshape(N_DIR, batch, seq, DN_WIDTH) for t in _delta_prep(qkv, zg, gr, batch=batch, seq=seq)]
    zg3 = zg.reshape(batch, seq, LANES)
    fwd = lambda s: s
    bwd = lambda s: steps - 1 - s
    specs = []
    for d, pos in ((0, fwd), (1, bwd)):
        specs += [pl.BlockSpec((None, batch, T, DN_WIDTH), lambda s, d=d, pos=pos: (d, 0, pos(s), 0))] * 5
        specs += [pl.BlockSpec((batch, T, LANES), lambda s, pos=pos: (0, pos(s), 0))]
    out = jax.ShapeDtypeStruct((batch, seq, DN_WIDTH), BF16)
    o_f, o_b = pl.pallas_call(
        functools.partial(_delta_scan_kernel, batch=batch),
        grid=(steps,),
        in_specs=specs,
        out_specs=[pl.BlockSpec((batch, T, DN_WIDTH), lambda s: (0, fwd(s), 0)),
                   pl.BlockSpec((batch, T, DN_WIDTH), lambda s: (0, bwd(s), 0))],
        out_shape=[out, out],
        scratch_shapes=[pltpu.VMEM((N_DIR, batch, DN_HEADS, DN_HEAD_DIM, DN_HEAD_DIM), F32)],
        compiler_params=_params("arbitrary"),
        name="delta_scan",
    )(*prepped, zg3, *prepped, zg3)
    return o_f.reshape(n, DN_WIDTH), o_b.reshape(n, DN_WIDTH)


ML_PAIRS = ML_HEADS // 2
ML_AUG = 2 * ML_V_DIM
ML_KV_ROWS = ML_PAIRS * 2 * ML_QK_DIM


def _lane_cols(cols, width):
    rows = cols[0].shape[0]
    lane = lax.broadcasted_iota(jnp.int32, (rows, width), 1)
    out = jnp.zeros((rows, width), F32)
    for j, c in enumerate(cols):
        out = jnp.where(lane == j, c, out)
    return out


def _mlstm_prep_kernel(q_ref, k_ref, v_ref, gc_ref, gr_ref, qs_ref, iv_ref, kv_ref, aux_ref):
    G, L = PREP_GROUP, SCAN_CHUNK
    row = lax.broadcasted_iota(jnp.int32, (G, G), 0)
    col = lax.broadcasted_iota(jnp.int32, (G, G), 1)
    shift = L.bit_length() - 1
    same01 = _as01((row >> shift) == (col >> shift))
    masks = [_group_masks(False), _group_masks(True)]
    cum_c = [_as01(m[0]) for m in masks]
    cum_r = [_as01(m[2]) for m in masks]
    lane = lax.broadcasted_iota(jnp.int32, (G, LANES), 1)
    first_head = lane < ML_QK_DIM
    lane_r = lax.broadcasted_iota(jnp.int32, (1, G), 1)
    row_c = lax.broadcasted_iota(jnp.int32, (G, 1), 0)
    sel_row = lax.broadcasted_iota(jnp.int32, (2 * LANES, G), 0) & (LANES - 1)
    gate_lane = lambda h, d: N_DIR * ML_HEADS + d * ML_HEADS + h
    pick_pair = {(p, d): jnp.concatenate([_as01(sel_row == gate_lane(2 * p, d)),
                                          _as01(sel_row == gate_lane(2 * p + 1, d))], axis=-1)
                 for p in range(ML_PAIRS) for d in range(N_DIR)}
    ones_v = jnp.ones((G, ML_V_DIM), BF16)
    scale = ML_QK_DIM ** -0.5
    heads = range(ML_HEADS)
    chains = [(h, d) for h in heads for d in range(N_DIR)]

    step_groups = min(ML_PREP_STEP_GROUPS, q_ref.shape[0] // G)

    def step(i, carry):
        groups = range(step_groups)
        gidx = [i * step_groups + g for g in groups]
        rows = [pl.ds(pl.multiple_of(x * G, G), G) for x in gidx]
        gc = [gc_ref[r, :] for r in rows]
        gr = [gr_ref[x] for x in gidx]
        bt_c = [_exact_dot01(same01, x) for x in gc]
        bt_r = [_exact_dot01_r(x, same01) for x in gr]
        bc_c = [[_exact_dot01(cum_c[d], x) for d in range(N_DIR)] for x in gc]
        bc_r = [[_exact_dot01_r(x, cum_r[d]) for d in range(N_DIR)] for x in gr]
        gps = [(g, p) for g in groups for p in range(ML_PAIRS)]
        ghs = [(g, h) for g in groups for h in heads]
        chains = [(g, h, d) for g, h in ghs for d in range(N_DIR)]
        qp = {(g, p): q_ref[rows[g], p * LANES:(p + 1) * LANES].astype(F32) * scale for g, p in gps}
        kp16 = {(g, p): k_ref[rows[g], p * LANES:(p + 1) * LANES] for g, p in gps}
        kp = {c: kp16[c].astype(F32) for c in gps}
        v16 = {(g, h): v_ref[rows[g], h * ML_V_DIM:(h + 1) * ML_V_DIM].astype(BF16) for g, h in ghs}
        for g, p in gps:
            qs_ref[rows[g], p * LANES:(p + 1) * LANES] = qp[g, p].astype(BF16)
        own = [first_head, jnp.logical_not(first_head)]
        scores = {(g, h): _dot_nt(jnp.where(own[h % 2], qp[g, h // 2], 0.0).astype(BF16), kp16[g, h // 2])
                  for g, h in ghs}
        col = lambda c: N_DIR * ML_HEADS + c[2] * ML_HEADS + c[1]
        b_c = {c: bc_c[c[0]][c[2]][:, col(c):col(c) + 1] for c in chains}
        b_t = {c: bt_c[c[0]][:, col(c):col(c) + 1] for c in chains}
        b_r = {c: bc_r[c[0]][c[2]][col(c):col(c) + 1, :] for c in chains}
        li_c = {c: gc[c[0]][:, col(c) - N_DIR * ML_HEADS:col(c) - N_DIR * ML_HEADS + 1] for c in chains}
        li_r = {c: gr[c[0]][col(c) - N_DIR * ML_HEADS:col(c) - N_DIR * ML_HEADS + 1, :] for c in chains}
        split = {(g, d): _split_hi_lo(bc_c[g][d]) for g in groups for d in range(N_DIR)}
        b_cb = {}
        for g, p in gps:
            for d in range(N_DIR):
                both = _dot(split[g, d], pick_pair[p, d])
                b_cb[g, 2 * p, d], b_cb[g, 2 * p + 1, d] = both[:, :G], both[:, G:]
        d_mat = {c: jnp.where(masks[c[2]][0], b_cb[c] - b_r[c] + li_r[c], NEG_BIG) for c in chains}
        d_max = {c: jnp.max(d_mat[c], axis=-1, keepdims=True) for c in chains}
        pmat = {c: (jnp.exp(d_mat[c] - d_max[c]) * scores[c[:2]]).astype(BF16) for c in chains}
        w_end_r = {c: bt_r[c[0]][col(c):col(c) + 1, :] - b_r[c] + li_r[c] for c in chains}
        w_max = {}
        for c in chains:
            wm = jnp.full((G, 1), NEG_BIG, F32)
            for j in range(G // L):
                in_j = jnp.logical_and(lane_r >= j * L, lane_r < (j + 1) * L)
                wm_j = jnp.max(jnp.where(in_j, w_end_r[c], NEG_BIG), axis=-1, keepdims=True)
                wm = jnp.where(jnp.logical_and(row_c >= j * L, row_c < (j + 1) * L), wm_j, wm)
            w_max[c] = wm
        sw0 = {c: jnp.exp(b_t[c] - b_c[c] + li_c[c] - w_max[c]) for c in chains}
        iv = {c: _dot(pmat[c], jnp.concatenate([v16[c[:2]], ones_v], axis=-1)) for c in chains}
        for c in chains:
            g, h, d = c
            iv_ref[d, rows[g], h * ML_AUG:(h + 1) * ML_AUG] = iv[c].astype(BF16)
        for g in groups:
            for d in range(N_DIR):
                src = N_DIR * ML_HEADS + d * ML_HEADS
                aux_ref[d, rows[g], :LANES] = jnp.where(
                    lane < ML_HEADS, pltpu.roll(bc_c[g][d], LANES - src, 1),
                    jnp.where(lane < 2 * ML_HEADS, pltpu.roll(bt_c[g], LANES - src + ML_HEADS, 1), 0.0))
                aux_ref[d, rows[g], LANES:] = _lane_cols(
                    [d_max[g, h, d] for h in heads] + [w_max[g, h, d] for h in heads], LANES)
        gpd = [(g, p, d) for g, p in gps for d in range(N_DIR)]
        ks = {(g, p, d): (kp[g, p] * jnp.where(first_head, sw0[g, 2 * p, d], sw0[g, 2 * p + 1, d])).astype(BF16)
              for g, p, d in gpd}
        vcat = {(g, p): jnp.concatenate([v16[g, 2 * p], v16[g, 2 * p + 1], ones_v], axis=-1) for g, p in gps}
        pair_chunks = [(g, p, d, j) for g, p, d in gpd for j in range(G // L)]
        kv = [_dot_tn(ks[g, p, d][j * L:(j + 1) * L], vcat[g, p][j * L:(j + 1) * L]) for g, p, d, j in pair_chunks]
        for (g, p, d, j), t in zip(pair_chunks, kv):
            top = jnp.concatenate([t[:ML_QK_DIM, :ML_V_DIM], t[:ML_QK_DIM, 2 * ML_V_DIM:]], axis=-1)
            bot = jnp.concatenate([t[ML_QK_DIM:, ML_V_DIM:2 * ML_V_DIM], t[ML_QK_DIM:, 2 * ML_V_DIM:]], axis=-1)
            r0 = pl.multiple_of(gidx[g] * (G // L * ML_KV_ROWS) + (j * ML_PAIRS + p) * LANES, LANES)
            kv_ref[d, pl.ds(r0, LANES), :] = jnp.concatenate([top, bot], axis=0).astype(BF16)
        return carry

    lax.fori_loop(0, q_ref.shape[0] // (G * step_groups), step, 0)


def _mlstm_prep(z, zg, gr, *, batch, seq, q_col, k_col, v_col):
    n = batch * seq
    R, G, L = min(PREP_ROWS, seq), PREP_GROUP, SCAN_CHUNK
    steps = seq // R
    rowblk = lambda c: (lambda b, r: (b * steps + r, c))
    dirblk = lambda b, r: (0, b * steps + r, 0)
    kv_rows = ML_KV_ROWS // L
    return pl.pallas_call(
        _mlstm_prep_kernel,
        grid=(batch, steps),
        in_specs=[
            pl.BlockSpec((R, ML_QK_WIDTH), rowblk(q_col // ML_QK_WIDTH)),
            pl.BlockSpec((R, ML_QK_WIDTH), rowblk(k_col // ML_QK_WIDTH)),
            pl.BlockSpec((R, ML_V_WIDTH), rowblk(v_col // ML_V_WIDTH)),
            pl.BlockSpec((R, LANES), rowblk(0)),
            pl.BlockSpec((R // G, 16, G), lambda b, r: (b * steps + r, 0, 0)),
        ],
        out_specs=[
            pl.BlockSpec((R, ML_QK_WIDTH), rowblk(0)),
            pl.BlockSpec((N_DIR, R, ML_HEADS * ML_AUG), dirblk),
            pl.BlockSpec((N_DIR, R * kv_rows, ML_AUG), dirblk),
            pl.BlockSpec((N_DIR, R, 2 * LANES), dirblk),
        ],
        out_shape=[
            jax.ShapeDtypeStruct((n, ML_QK_WIDTH), BF16),
            jax.ShapeDtypeStruct((N_DIR, n, ML_HEADS * ML_AUG), BF16),
            jax.ShapeDtypeStruct((N_DIR, n * kv_rows, ML_AUG), BF16),
            jax.ShapeDtypeStruct((N_DIR, n, 2 * LANES), F32),
        ],
        compiler_params=_params("parallel", "parallel"),
        name="mlstm_prep",
    )(z, z, z, zg, gr)


def _mlstm_scan_kernel(qf, ivf, kvf, auxf, qb, ivb, kvb, auxb, of_ref, ob_ref, c_ref, m_ref, *, batch):
    L, C = SCAN_CHUNK, SCAN_STEP_CHUNKS

    @pl.when(pl.program_id(0) == 0)
    def _():
        c_ref[...] = jnp.zeros_like(c_ref)
        m_ref[...] = jnp.zeros_like(m_ref)

    dirs = ((qf, ivf, kvf, auxf, of_ref), (qb, ivb, kvb, auxb, ob_ref))
    lane = lax.broadcasted_iota(jnp.int32, (L, LANES), 1)
    keep = [jnp.where(lane < ML_QK_DIM, 1.0, 0.0).astype(BF16), jnp.where(lane < ML_QK_DIM, 0.0, 1.0).astype(BF16)]
    first_rows = lax.broadcasted_iota(jnp.int32, (2 * ML_QK_DIM, 1), 0) < ML_QK_DIM
    pair_chains = [(d, b, p) for d in range(N_DIR) for b in range(batch) for p in range(ML_PAIRS)]
    chains = [(d, b, h) for d in range(N_DIR) for b in range(batch) for h in range(ML_HEADS)]
    groups = [(d, b) for d in range(N_DIR) for b in range(batch)]
    H = ML_HEADS
    lane8 = lax.broadcasted_iota(jnp.int32, (SUBLANES, LANES), 1)
    sel_row = lax.broadcasted_iota(jnp.int32, (2 * LANES, LANES), 0) & (LANES - 1)
    lane_select = [_as01(sel_row == j) for j in range(2 * H)]
    state = {c: c_ref[c] for c in pair_chains}
    m_st = {g: m_ref[g] for g in groups}
    for step in range(C):
        chunk = {0: step, 1: C - 1 - step}
        rows = {d: slice(chunk[d] * L, (chunk[d] + 1) * L) for d in range(N_DIR)}
        qc = {}
        for c in pair_chains:
            d, b, p = c
            qp = dirs[d][0][b, rows[d], p * LANES:(p + 1) * LANES]
            qc[c] = _dot(jnp.concatenate([qp * keep[0], qp * keep[1]], axis=0), state[c].astype(BF16))
        w_prev, w_cur, floor = {}, {}, {}
        for g in groups:
            d, b = g
            aux = dirs[d][3][b, rows[d], :]
            x = aux[:, :LANES] + m_st[g][0:1, :]
            y = jnp.maximum(x, aux[:, LANES:])
            w_prev[g] = jnp.exp(x - y)
            w_cur[g] = jnp.exp(aux[:, LANES:] - y)
            floor[g] = jnp.exp(-y)
            y0 = jnp.broadcast_to(y[0:1, :], (SUBLANES, LANES))
            m_st[g] = jnp.where(lane8 < H, pltpu.roll(y0, LANES - H, 1), jnp.where(lane8 < 2 * H, y0, 0.0))
        tiles = [t[g] for g in groups for t in (w_prev, w_cur, floor)]
        per_row = _split_hi_lo(jnp.concatenate(tiles, axis=0))
        per_chunk = _split_hi_lo(jnp.concatenate([t[0:1, :] for t in tiles] + [tiles[0][0:SUBLANES, :]], axis=0))
        bc_row, bc_chunk = [], []
        for p in range(ML_PAIRS):
            both = _dot(per_row, jnp.concatenate([lane_select[2 * p], lane_select[2 * p + 1]], axis=-1))
            bc_row += [both[:, :LANES], both[:, LANES:]]
            both = _dot(per_chunk, jnp.concatenate([lane_select[H + 2 * p], lane_select[H + 2 * p + 1]], axis=-1))
            bc_chunk += [both[:, :LANES], both[:, LANES:]]
        numden = {}
        for c in chains:
            d, b, h = c
            r0 = (h % 2) * L
            t0 = 3 * groups.index((d, b)) * L
            wp = bc_row[h][t0:t0 + L, :]
            wc = bc_row[h][t0 + L:t0 + 2 * L, :]
            iv = dirs[d][1][b, rows[d], h * ML_AUG:(h + 1) * ML_AUG].astype(F32)
            qch = qc[d, b, h // 2][r0:r0 + L, :]
            numden[c] = jnp.concatenate([wp * qch[:, :ML_V_DIM] + wc * iv[:, :ML_V_DIM],
                                         wp * qch[:, ML_V_DIM:] + wc * iv[:, ML_V_DIM:]], axis=-1)
        for c in chains:
            d, b, h = c
            t0 = 3 * groups.index((d, b)) * L
            den = jnp.maximum(jnp.abs(numden[c][:, ML_V_DIM:]), bc_row[h][t0 + 2 * L:t0 + 3 * L, :])
            dirs[d][4][b, rows[d], h * ML_V_DIM:(h + 1) * ML_V_DIM] = (numden[c][:, :ML_V_DIM] / den).astype(BF16)
        for c in pair_chains:
            d, b, p = c
            r0 = (chunk[d] * ML_PAIRS + p) * LANES
            kv = dirs[d][2][b, r0:r0 + LANES, :].astype(F32)
            t0 = 3 * groups.index((d, b))
            cw = jnp.where(first_rows, bc_chunk[2 * p][t0:t0 + 1, :], bc_chunk[2 * p + 1][t0:t0 + 1, :])
            iw = jnp.where(first_rows, bc_chunk[2 * p][t0 + 1:t0 + 2, :], bc_chunk[2 * p + 1][t0 + 1:t0 + 2, :])
            state[c] = jnp.concatenate([cw * state[c][:, :ML_V_DIM] + iw * kv[:, :ML_V_DIM],
                                        cw * state[c][:, ML_V_DIM:] + iw * kv[:, ML_V_DIM:]], axis=-1)
    for c in pair_chains:
        c_ref[c] = state[c]
    for g in groups:
        m_ref[g] = m_st[g]


def _mlstm_scan(z, zg, gr, *, batch, seq, q_col, k_col, v_col):
    n = batch * seq
    T = SCAN_CHUNK * SCAN_STEP_CHUNKS
    steps = seq // T
    kv_rows = ML_KV_ROWS // SCAN_CHUNK
    qs, iv, kv, aux = _mlstm_prep(z, zg, gr, batch=batch, seq=seq, q_col=q_col, k_col=k_col, v_col=v_col)
    qs = qs.reshape(batch, seq, ML_QK_WIDTH)
    iv = iv.reshape(N_DIR, batch, seq, ML_HEADS * ML_AUG)
    kv = kv.reshape(N_DIR, batch, seq * kv_rows, ML_AUG)
    aux = aux.reshape(N_DIR, batch, seq, 2 * LANES)
    fwd = lambda s: s
    bwd = lambda s: steps - 1 - s
    specs = []
    for d, pos in ((0, fwd), (1, bwd)):
        specs += [
            pl.BlockSpec((batch, T, ML_QK_WIDTH), lambda s, pos=pos: (0, pos(s), 0)),
            pl.BlockSpec((None, batch, T, ML_HEADS * ML_AUG), lambda s, d=d, pos=pos: (d, 0, pos(s), 0)),
            pl.BlockSpec((None, batch, T * kv_rows, ML_AUG), lambda s, d=d, pos=pos: (d, 0, pos(s), 0)),
            pl.BlockSpec((None, batch, T, 2 * LANES), lambda s, d=d, pos=pos: (d, 0, pos(s), 0)),
        ]
    out = jax.ShapeDtypeStruct((batch, seq, ML_V_WIDTH), BF16)
    h_f, h_b = pl.pallas_call(
        functools.partial(_mlstm_scan_kernel, batch=batch),
        grid=(steps,),
        in_specs=specs,
        out_specs=[pl.BlockSpec((batch, T, ML_V_WIDTH), lambda s: (0, fwd(s), 0)),
                   pl.BlockSpec((batch, T, ML_V_WIDTH), lambda s: (0, bwd(s), 0))],
        out_shape=[out, out],
        scratch_shapes=[
            pltpu.VMEM((N_DIR, batch, ML_PAIRS, 2 * ML_QK_DIM, ML_AUG), F32),
            pltpu.VMEM((N_DIR, batch, SUBLANES, LANES), F32),
        ],
        compiler_params=_params("arbitrary"),
        name="mlstm_scan",
    )(qs, iv, kv, aux, qs, iv, kv, aux)
    return h_f.reshape(n, ML_V_WIDTH), h_b.reshape(n, ML_V_WIDTH)


def _head_rms_norm(x, g, n_heads, head_dim):
    parts = []
    for h in range(n_heads):
        xh = x[:, h * head_dim:(h + 1) * head_dim]
        ms = jnp.mean(xh * xh, axis=-1, keepdims=True)
        parts.append(xh * lax.rsqrt(ms + NORM_EPS) * g)
    return jnp.concatenate(parts, axis=-1)


def _layer_norm(x, g, b):
    mu = jnp.mean(x, axis=-1, keepdims=True)
    xc = x - mu
    var = jnp.mean(xc * xc, axis=-1, keepdims=True)
    return xc * lax.rsqrt(var + LN_EPS) * g + b


def _swiglu_residual(x, g_ref, w1_ref, w3_ref, w2_ref, fg_ref, final_norm):
    ms = jnp.mean(x * x, axis=-1, keepdims=True)
    hn = (x * lax.rsqrt(ms + NORM_EPS) * g_ref[...]).astype(BF16)
    a = _dot(hn, w1_ref[...])
    b = _dot(hn, w3_ref[...])
    y = x + _dot((a * jax.nn.sigmoid(a) * b).astype(BF16), w2_ref[...])
    if final_norm:
        ms = jnp.mean(y * y, axis=-1, keepdims=True)
        y = y * lax.rsqrt(ms + NORM_EPS) * fg_ref[...]
    return y


def _ffn_operands(g, w1, w3, w2, final_g):
    d, f = w1.shape
    specs = [_resident((1, d)), _resident((d, f)), _resident((d, f)), _resident((f, d)), _resident((1, d))]
    return specs, (g.reshape(1, d), w1.astype(BF16), w3.astype(BF16), w2.astype(BF16), final_g.reshape(1, d))


def _even_block_kernel(h_ref, c_ref, of_ref, ob_ref, gate_ref, lng_ref, lnb_ref, ng_ref, wa_ref, wb_ref,
                       g_ref, w1_ref, w3_ref, w2_ref, fg_ref, o_ref, *, final_norm):
    ya = _layer_norm(c_ref[...].astype(F32), lng_ref[...], lnb_ref[...])
    ya = ya * jax.nn.sigmoid(ya)
    o = _head_rms_norm(of_ref[...].astype(F32) + ob_ref[...].astype(F32), ng_ref[...], DN_HEADS, DN_HEAD_DIM)
    gate = gate_ref[...].astype(F32)
    o = o * (gate * jax.nn.sigmoid(gate))
    x = h_ref[...] + _dot(ya.astype(BF16), wa_ref[...]) + _dot(o.astype(BF16), wb_ref[...])
    o_ref[...] = _swiglu_residual(x, g_ref, w1_ref, w3_ref, w2_ref, fg_ref, final_norm)


def _even_block(h, conv, o_fwd, o_bwd, z, ln_g, ln_b, norm_g, w_out, ffn, *, gate_col, final_norm):
    n, d = h.shape
    tm = min(ROW_TILE, n)
    gb = gate_col // DN_WIDTH
    row = lambda i: (i, 0)
    ffn_specs, ffn_args = _ffn_operands(*ffn)
    return pl.pallas_call(
        functools.partial(_even_block_kernel, final_norm=final_norm),
        grid=(n // tm,),
        in_specs=[
            pl.BlockSpec((tm, d), row),
            pl.BlockSpec((tm, CONV_CH), row),
            pl.BlockSpec((tm, DN_WIDTH), row),
            pl.BlockSpec((tm, DN_WIDTH), row),
            pl.BlockSpec((tm, DN_WIDTH), lambda i: (i, gb)),
            _resident((1, CONV_CH)),
            _resident((1, CONV_CH)),
            _resident((1, DN_HEAD_DIM)),
            _resident((CONV_CH, d)),
            _resident((DN_WIDTH, d)),
        ] + ffn_specs,
        out_specs=pl.BlockSpec((tm, d), row),
        out_shape=jax.ShapeDtypeStruct((n, d), F32),
        compiler_params=_params("parallel"),
        name="even_block",
    )(h, conv, o_fwd, o_bwd, z, ln_g.reshape(1, -1), ln_b.reshape(1, -1), norm_g.reshape(1, -1),
      w_out[:CONV_CH].astype(BF16), w_out[CONV_CH:].astype(BF16), *ffn_args)


def _odd_block_kernel(h_ref, u_ref, vp_ref, hf_ref, hb_ref, op_ref, lng_ref, lnb_ref, sgw_ref, sgb_ref, ng_ref,
                      wa_ref, wb_ref, g_ref, w1_ref, w3_ref, w2_ref, fg_ref, o_ref, *, final_norm):
    tm = h_ref.shape[0]
    u = jax.nn.gelu(u_ref[...].astype(F32))
    vv = _layer_norm(jax.nn.gelu(vp_ref[...].astype(F32)), lng_ref[...], lnb_ref[...]).astype(BF16)
    sgb = sgb_ref[...]
    rows = []
    for c in range(tm // SG_CHUNK):
        parts = []
        for g in range(SG_GROUPS):
            blk = vv[c * SG_CHUNK:(c + 1) * SG_CHUNK, g * SG_GROUP_DIM:(g + 1) * SG_GROUP_DIM]
            parts.append(_dot(sgw_ref[g], blk) + sgb[:, g:g + 1])
        rows.append(jnp.concatenate(parts, axis=-1))
    yc = u * jnp.concatenate(rows, axis=0)
    hd = _head_rms_norm(hf_ref[...].astype(F32) + hb_ref[...].astype(F32), ng_ref[...], ML_HEADS, ML_V_DIM)
    hd = hd * jax.nn.sigmoid(op_ref[...].astype(F32))
    x = h_ref[...] + _dot(yc.astype(BF16), wa_ref[...]) + _dot(hd.astype(BF16), wb_ref[...])
    o_ref[...] = _swiglu_residual(x, g_ref, w1_ref, w3_ref, w2_ref, fg_ref, final_norm)


def _odd_block(h, z, h_fwd, h_bwd, ln_g, ln_b, sg_w, sg_b, norm_g, w_out, ffn, *, u_col, v_col, o_col, final_norm):
    n, d = h.shape
    tm = min(ROW_TILE, n)
    row = lambda i: (i, 0)
    ffn_specs, ffn_args = _ffn_operands(*ffn)
    return pl.pallas_call(
        functools.partial(_odd_block_kernel, final_norm=final_norm),
        grid=(n // tm,),
        in_specs=[
            pl.BlockSpec((tm, d), row),
            pl.BlockSpec((tm, SG_WIDTH), lambda i: (i, u_col // SG_WIDTH)),
            pl.BlockSpec((tm, SG_WIDTH), lambda i: (i, v_col // SG_WIDTH)),
            pl.BlockSpec((tm, ML_V_WIDTH), row),
            pl.BlockSpec((tm, ML_V_WIDTH), row),
            pl.BlockSpec((tm, ML_V_WIDTH), lambda i: (i, o_col // ML_V_WIDTH)),
            _resident((1, SG_WIDTH)),
            _resident((1, SG_WIDTH)),
            _resident((SG_GROUPS, SG_CHUNK, SG_CHUNK)),
            _resident((SG_CHUNK, SG_GROUPS)),
            _resident((1, ML_V_DIM)),
            _resident((SG_WIDTH, d)),
            _resident((ML_V_WIDTH, d)),
        ] + ffn_specs,
        out_specs=pl.BlockSpec((tm, d), row),
        out_shape=jax.ShapeDtypeStruct((n, d), F32),
        compiler_params=_params("parallel"),
        name="odd_block",
    )(h, z, z, h_fwd, h_bwd, z, ln_g.reshape(1, -1), ln_b.reshape(1, -1), sg_w.astype(BF16), sg_b.T,
      norm_g.reshape(1, -1), w_out[:SG_WIDTH].astype(BF16), w_out[SG_WIDTH:].astype(BF16), *ffn_args)


def _gate_weight(w_gate_cols):
    d, c = w_gate_cols.shape
    return jnp.zeros((d, LANES), F32).at[:, :c].set(w_gate_cols).astype(BF16)


def _even_layer(h, j, p, ffn, *, batch, seq, final_norm):
    main = 2 * CONV_CH + 4 * DN_WIDTH
    w_in = p["ev_w_in"][j]
    gate_params = jnp.zeros((SUBLANES, LANES), F32)
    gate_params = gate_params.at[0, 8:16].set(p["ev_dn_a_log"][j].reshape(-1))
    gate_params = gate_params.at[1, 8:16].set(p["ev_dn_dt_bias"][j].reshape(-1))
    z, zg, gr = _in_proj(h, p["mix_norm_g"][2 * j],w_in[:, :main].astype(BF16), _gate_weight(w_in[:, main:]),
                     gate_params, even=True)
    conv = _conv_glu(z, p["ev_conv_w"][j], p["ev_conv_b"][j], batch=batch, seq=seq)
    qkv = _conv_qkv(z, p["ev_dn_conv_w"][j], batch=batch, seq=seq, col0=2 * CONV_CH)
    o_fwd, o_bwd = _delta_scan(qkv, zg, gr, batch=batch, seq=seq)
    return _even_block(h, conv, o_fwd, o_bwd, z, p["ev_conv_ln_g"][j], p["ev_conv_ln_b"][j], p["ev_dn_norm_g"][j],
                       p["ev_w_out"][j], ffn, gate_col=2 * CONV_CH + 3 * DN_WIDTH, final_norm=final_norm)


def _odd_layer(h, j, p, ffn, *, batch, seq, final_norm):
    main = 2 * SG_WIDTH + 2 * ML_QK_WIDTH + 2 * ML_V_WIDTH
    w_in = p["od_w_in"][j]
    gate_params = jnp.zeros((SUBLANES, LANES), F32)
    gate_params = gate_params.at[0, 0:8].set(p["od_ml_i_bias"][j].reshape(-1))
    gate_params = gate_params.at[1, 8:16].set(p["od_ml_f_bias"][j].reshape(-1))
    z, zg, gr = _in_proj(h, p["mix_norm_g"][2 * j + 1],w_in[:, :main].astype(BF16), _gate_weight(w_in[:, main:]),
                     gate_params, even=False)
    q_col = 2 * SG_WIDTH
    k_col = q_col + ML_QK_WIDTH
    v_col = k_col + ML_QK_WIDTH
    o_col = v_col + ML_V_WIDTH
    h_fwd, h_bwd = _mlstm_scan(z, zg, gr, batch=batch, seq=seq, q_col=q_col, k_col=k_col, v_col=v_col)
    return _odd_block(h, z, h_fwd, h_bwd, p["od_sg_ln_g"][j], p["od_sg_ln_b"][j], p["od_sg_w"][j], p["od_sg_b"][j],
                      p["od_ml_norm_g"][j], p["od_w_out"][j], ffn, u_col=0, v_col=SG_WIDTH, o_col=o_col,
                      final_norm=final_norm)


def kernel(x, mix_norm_g, ev_w_in, ev_conv_w, ev_conv_b, ev_conv_ln_g, ev_conv_ln_b, ev_dn_conv_w, ev_dn_a_log, ev_dn_dt_bias, ev_dn_norm_g, ev_w_out, od_w_in, od_sg_ln_g, od_sg_ln_b, od_sg_w, od_sg_b, od_ml_i_bias, od_ml_f_bias, od_ml_norm_g, od_w_out, ffn_norm_g, ffn_w1, ffn_w3, ffn_w2, final_norm_g):
    p = dict(mix_norm_g=mix_norm_g, ev_w_in=ev_w_in, ev_conv_w=ev_conv_w, ev_conv_b=ev_conv_b,
             ev_conv_ln_g=ev_conv_ln_g, ev_conv_ln_b=ev_conv_ln_b, ev_dn_conv_w=ev_dn_conv_w,
             ev_dn_a_log=ev_dn_a_log, ev_dn_dt_bias=ev_dn_dt_bias, ev_dn_norm_g=ev_dn_norm_g, ev_w_out=ev_w_out,
             od_w_in=od_w_in, od_sg_ln_g=od_sg_ln_g, od_sg_ln_b=od_sg_ln_b, od_sg_w=od_sg_w, od_sg_b=od_sg_b,
             od_ml_i_bias=od_ml_i_bias, od_ml_f_bias=od_ml_f_bias, od_ml_norm_g=od_ml_norm_g, od_w_out=od_w_out)
    batch, seq, d = x.shape
    depth = mix_norm_g.shape[0]
    h = x.reshape(batch * seq, d)
    for layer in range(depth):
        ffn = (ffn_norm_g[layer], ffn_w1[layer], ffn_w3[layer], ffn_w2[layer], final_norm_g)
        mixer_layer = _even_layer if layer % 2 == 0 else _odd_layer
        h = mixer_layer(h, layer // 2, p, ffn, batch=batch, seq=seq, final_norm=layer == depth - 1)
    return h.reshape(batch, seq, d)
```

```python
import functools

import jax
import jax.numpy as jnp
from jax import lax
from jax.experimental import pallas as pl
from jax.experimental.pallas import tpu as pltpu

NORM_EPS = 1e-6
LN_EPS = 1e-5
N_DIR = 2

CONV_CH = 512
DN_HEADS = 4
DN_HEAD_DIM = 128
DN_WIDTH = DN_HEADS * DN_HEAD_DIM
SG_GROUPS = 4
SG_GROUP_DIM = 128
SG_WIDTH = SG_GROUPS * SG_GROUP_DIM
SG_CHUNK = 128
ML_HEADS = 4
ML_QK_DIM = 64
ML_V_DIM = 128
ML_QK_WIDTH = ML_HEADS * ML_QK_DIM
ML_V_WIDTH = ML_HEADS * ML_V_DIM

LANES = 128
SUBLANES = 8
VMEM_LIMIT_BYTES = 56 * 1024 * 1024

SCAN_CHUNK = 64
SCAN_STEP_CHUNKS = 2
PREP_GROUP = 128
PREP_ROWS = 512
DN_PREP_STEP_GROUPS = 2
ML_PREP_STEP_GROUPS = 4
ROW_TILE = 512
IN_ROW_TILE = 1024
CONV_TILE = 128
QKV_CONV_TILES = 4
CONV_PAD = 16
NEG_BIG = -1e30

BF16 = jnp.bfloat16
F32 = jnp.float32


def _params(*sem):
    return pltpu.CompilerParams(dimension_semantics=sem, vmem_limit_bytes=VMEM_LIMIT_BYTES)


def _resident(shape, block_index=None):
    index = (0,) * len(shape) if block_index is None else tuple(block_index)
    return pl.BlockSpec(shape, lambda *_: index, pipeline_mode=pl.Buffered(1))


def _dot(a, b):
    return jnp.dot(a, b, preferred_element_type=F32)


def _dot_nt(a, b):
    return lax.dot_general(a, b, (((1,), (1,)), ((), ())), preferred_element_type=F32)


def _dot_tn(a, b):
    return lax.dot_general(a, b, (((0,), (0,)), ((), ())), preferred_element_type=F32)


def _exact_dot01(t01, x):
    x1 = x.astype(BF16)
    r1 = x - x1.astype(F32)
    x2 = r1.astype(BF16)
    x3 = (r1 - x2.astype(F32)).astype(BF16)
    return _dot(t01, x1) + _dot(t01, x2) + _dot(t01, x3)


def _exact_dot01_r(x, t01):
    x1 = x.astype(BF16)
    r1 = x - x1.astype(F32)
    x2 = r1.astype(BF16)
    x3 = (r1 - x2.astype(F32)).astype(BF16)
    return _dot(x1, t01) + _dot(x2, t01) + _dot(x3, t01)


def _in_proj_kernel(h_ref, g_ref, w_ref, wg_ref, gp_ref, zm_ref, zg_ref, zgt_ref, *, even):
    x = h_ref[...]
    ms = jnp.mean(x * x, axis=-1, keepdims=True)
    hn = (x * lax.rsqrt(ms + NORM_EPS) * g_ref[...]).astype(BF16)
    zm_ref[...] = _dot(hn, w_ref[...]).astype(zm_ref.dtype)
    zg = _dot(hn, wg_ref[...])
    p0 = gp_ref[0:1, :]
    p1 = gp_ref[1:2, :]
    lane = lax.broadcasted_iota(jnp.int32, zg.shape, 1)
    if even:
        first = jax.nn.sigmoid(zg)
        second = -jnp.exp(p0) * jax.nn.softplus(zg + p1)
    else:
        first = zg + p0
        second = jax.nn.log_sigmoid(zg + p1)
    gates = jnp.where(lane < 8, first, second)
    zg_ref[...] = gates
    for s in range(gates.shape[0] // PREP_GROUP):
        zgt_ref[s] = gates[s * PREP_GROUP:(s + 1) * PREP_GROUP, :].T[:2 * SUBLANES, :]


def _in_proj(h, g, w_in, c, w_gate, gate_params, *, even):
    n, d = h.shape
    assert c % LANES == 0
    tm = min(IN_ROW_TILE, n)
    return pl.pallas_call(
        functools.partial(_in_proj_kernel, even=even),
        grid=(n // tm,),
        in_specs=[
            pl.BlockSpec((tm, d), lambda i: (i, 0)),
            _resident((1, d)),
            _resident((d, c)),
            _resident((d, LANES)),
            _resident((SUBLANES, LANES)),
        ],
        out_specs=[
            pl.BlockSpec((tm, c), lambda i: (i, 0)),
            pl.BlockSpec((tm, LANES), lambda i: (i, 0)),
            pl.BlockSpec((tm // PREP_GROUP, 2 * SUBLANES, PREP_GROUP), lambda i: (i, 0, 0)),
        ],
        out_shape=[jax.ShapeDtypeStruct((n, c), BF16), jax.ShapeDtypeStruct((n, LANES), F32),
                   jax.ShapeDtypeStruct((n // PREP_GROUP, 2 * SUBLANES, PREP_GROUP), F32)],
        compiler_params=_params("parallel"),
        name="in_proj_even" if even else "in_proj_odd",
    )(h, g.reshape(1, d), w_in, w_gate, gate_params)


def _conv_taps(pad_ref, w_ref, o_ref, bias, *, seq, width, post, tiles_per_step=1):
    half = width // 2
    tiles_per_step = min(tiles_per_step, seq // CONV_TILE)
    step_rows = CONV_TILE * tiles_per_step

    def body(i, carry):
        t0 = pl.multiple_of(i * step_rows, step_rows)
        accs = []
        for s in range(tiles_per_step):
            acc = jnp.zeros((CONV_TILE, LANES), F32)
            for j in range(width):
                acc = acc + pad_ref[pl.ds(t0 + (s * CONV_TILE + CONV_PAD - half + j), CONV_TILE), :] * w_ref[j:j + 1, :]
            accs.append(acc if bias is None else acc + bias)
        for s, out in enumerate(post(accs)):
            o_ref[pl.ds(t0 + s * CONV_TILE, CONV_TILE), :] = out.astype(o_ref.dtype)
        return carry

    lax.fori_loop(0, seq // step_rows, body, 0)


def _fill_padded(pad_ref, x, seq):
    zeros = jnp.zeros((CONV_PAD, LANES), F32)
    pad_ref[0:CONV_PAD, :] = zeros
    pad_ref[CONV_PAD + seq:CONV_PAD + seq + CONV_PAD, :] = zeros
    pad_ref[CONV_PAD:CONV_PAD + seq, :] = x


def _conv_glu_kernel(av_ref, ag_ref, w_ref, b_ref, o_ref, pad_ref, *, seq, width):
    _fill_padded(pad_ref, av_ref[...].astype(F32) * jax.nn.sigmoid(ag_ref[...].astype(F32)), seq)
    _conv_taps(pad_ref, w_ref, o_ref, b_ref[...], seq=seq, width=width, post=lambda tiles: tiles)


def _conv_glu(z, conv_w, conv_b, *, batch, seq):
    width = conv_w.shape[0]
    nblk = CONV_CH // LANES
    return pl.pallas_call(
        functools.partial(_conv_glu_kernel, seq=seq, width=width),
        grid=(batch, nblk),
        in_specs=[
            pl.BlockSpec((seq, LANES), lambda b, c: (b, c)),
            pl.BlockSpec((seq, LANES), lambda b, c: (b, nblk + c)),
            pl.BlockSpec((width, LANES), lambda b, c: (0, c)),
            pl.BlockSpec((1, LANES), lambda b, c: (0, c)),
        ],
        out_specs=pl.BlockSpec((seq, LANES), lambda b, c: (b, c)),
        out_shape=jax.ShapeDtypeStruct((batch * seq, CONV_CH), BF16),
        scratch_shapes=[pltpu.VMEM((seq + 2 * CONV_PAD, LANES), F32)],
        compiler_params=_params("parallel", "parallel"),
        name="conv_glu",
    )(z, z, conv_w, conv_b.reshape(1, CONV_CH))


def _conv_qkv_kernel(x_ref, w_ref, o_ref, pad_ref, *, seq, width, n_norm_blocks):
    _fill_padded(pad_ref, x_ref[...].astype(F32), seq)
    normalise = pl.program_id(1) < n_norm_blocks

    def post(tiles):
        ys = [a * jax.nn.sigmoid(a) for a in tiles]
        sq = [jnp.sum(y * y, axis=-1, keepdims=True) for y in ys]
        inv = [lax.rsqrt(s + NORM_EPS) for s in sq]
        return [jnp.where(normalise, y * r, y) for y, r in zip(ys, inv)]

    _conv_taps(pad_ref, w_ref, o_ref, None, seq=seq, width=width, post=post, tiles_per_step=QKV_CONV_TILES)


def _conv_qkv(z, dn_conv_w, *, batch, seq, col0):
    width = dn_conv_w.shape[0]
    nblk = 3 * DN_WIDTH // LANES
    blk0 = col0 // LANES
    return pl.pallas_call(
        functools.partial(_conv_qkv_kernel, seq=seq, width=width, n_norm_blocks=2 * DN_HEADS),
        grid=(batch, nblk),
        in_specs=[
            pl.BlockSpec((seq, LANES), lambda b, c: (b, blk0 + c)),
            pl.BlockSpec((width, LANES), lambda b, c: (0, c)),
        ],
        out_specs=pl.BlockSpec((seq, LANES), lambda b, c: (b, c)),
        out_shape=jax.ShapeDtypeStruct((batch * seq, 3 * DN_WIDTH), BF16),
        scratch_shapes=[pltpu.VMEM((seq + 2 * CONV_PAD, LANES), F32)],
        compiler_params=_params("parallel", "parallel"),
        name="conv_qkv",
    )(z, dn_conv_w)


def _split_hi_lo(x):
    hi = x.astype(BF16)
    return jnp.concatenate([hi, (x - hi.astype(F32)).astype(BF16)], axis=-1)


def _bf16_all(mats):
    return [m.astype(BF16) for m in mats]


def _unit_tri_inverses(mats, row, col, n):
    eye = (row == col).astype(F32)
    same = (row >> 3) == (col >> 3)
    l8 = [jnp.where(same, a, 0.0) for a in mats]
    l8h = _bf16_all(l8)
    l2h = _bf16_all([_dot(p, p) for p in l8h])
    l4h = _bf16_all([_dot(p, p) for p in l2h])
    x = [eye - p for p in l8]
    x = [xi + _dot(xi.astype(BF16), p) for xi, p in zip(x, l2h)]
    x = [xi + _dot(xi.astype(BF16), p) for xi, p in zip(x, l4h)]
    shift = 3
    while (1 << shift) < n:
        same_next = (row >> (shift + 1)) == (col >> (shift + 1))
        off = jnp.logical_and(same_next, jnp.logical_not(same))
        ch = _bf16_all([jnp.where(off, a, 0.0) for a in mats])
        xh = _bf16_all(x)
        xch = _bf16_all([_dot(p, c) for p, c in zip(xh, ch)])
        x = [xi - _dot(p, q) for xi, p, q in zip(x, xch, xh)]
        same = same_next
        shift += 1
    return x


def _group_masks(backward):
    n = PREP_GROUP
    row = lax.broadcasted_iota(jnp.int32, (n, n), 0)
    col = lax.broadcasted_iota(jnp.int32, (n, n), 1)
    shift = SCAN_CHUNK.bit_length() - 1
    same = (row >> shift) == (col >> shift)
    d = (col - row) if backward else (row - col)
    land = jnp.logical_and
    return land(same, d >= 0), land(same, d > 0), land(same, d <= 0)


def _as01(mask):
    return jnp.where(mask, 1.0, 0.0).astype(BF16)


def _delta_prep_kernel(q_ref, k_ref, v_ref, gc_ref, gr_ref, u_ref, w_ref, qg_ref, kg_ref, at_ref):
    G, L = PREP_GROUP, SCAN_CHUNK
    row = lax.broadcasted_iota(jnp.int32, (G, G), 0)
    col = lax.broadcasted_iota(jnp.int32, (G, G), 1)
    shift = L.bit_length() - 1
    same01 = _as01((row >> shift) == (col >> shift))
    masks = [_group_masks(False), _group_masks(True)]
    cum_c = [_as01(m[0]) for m in masks]
    cum_r = [_as01(m[2]) for m in masks]
    fold = _as01((lax.broadcasted_iota(jnp.int32, (G, L), 0) & (L - 1)) == lax.broadcasted_iota(jnp.int32, (G, L), 1))
    pad = jnp.zeros((G, DN_HEAD_DIM - L), BF16)
    scale = DN_HEAD_DIM ** -0.5

    step_groups = min(DN_PREP_STEP_GROUPS, q_ref.shape[0] // G)

    def step(i, carry):
        groups = range(step_groups)
        rows = [pl.ds(pl.multiple_of((i * step_groups + g) * G, G), G) for g in groups]
        gc = [gc_ref[r, :] for r in rows]
        gr = [gr_ref[i * step_groups + g] for g in groups]
        g_tot = [_exact_dot01(same01, x) for x in gc]
        g_cum_c = [[_exact_dot01(cum_c[d], x) for d in range(N_DIR)] for x in gc]
        g_cum_r = [[_exact_dot01_r(x, cum_r[d]) for d in range(N_DIR)] for x in gr]
        heads = range(DN_HEADS)
        lanes = [slice(h * DN_HEAD_DIM, (h + 1) * DN_HEAD_DIM) for h in heads]
        gh = [(g, h) for g in groups for h in heads]
        chains = [(g, h, d) for g, h in gh for d in range(N_DIR)]
        k16 = {(g, h): k_ref[rows[g], lanes[h]] for g, h in gh}
        q = {(g, h): q_ref[rows[g], lanes[h]].astype(F32) for g, h in gh}
        k = {c: k16[c].astype(F32) for c in gh}
        kk = {c: _dot_nt(k16[c], k16[c]) for c in gh}
        qk = {c: _dot_nt((q[c] * scale).astype(BF16), k16[c]) for c in gh}
        beta, g_c, g_t, decay = {}, {}, {}, {}
        for c in chains:
            g, h, d = c
            cb = d * DN_HEADS + h
            cl = N_DIR * DN_HEADS + cb
            beta[c] = gc[g][:, cb:cb + 1]
            g_c[c] = g_cum_c[g][d][:, cl:cl + 1]
            g_t[c] = g_tot[g][:, cl:cl + 1]
            g_r = g_cum_r[g][d][cl:cl + 1, :]
            decay[c] = jnp.exp(jnp.where(masks[d][0], g_c[c] - g_r, NEG_BIG))
        a = [jnp.where(masks[c[2]][1], kk[c[:2]] * decay[c], 0.0) * beta[c] for c in chains]
        x = _unit_tri_inverses(a, row, col, L)
        eg = {c: jnp.exp(g_c[c]) for c in chains}
        rhs = [jnp.concatenate([v_ref[rows[c[0]], lanes[c[1]]].astype(F32) * beta[c], k[c[:2]] * (beta[c] * eg[c])],
                               axis=-1).astype(BF16) for c in chains]
        uw = [_dot(xi.astype(BF16), r) for xi, r in zip(x, rhs)]
        attn = [jnp.where(masks[c[2]][0], qk[c[:2]] * decay[c], 0.0).astype(BF16) for c in chains]
        at = [_dot(p, fold) for p in attn]
        for c, uwi, ati in zip(chains, uw, at):
            g, h, d = c
            r, sl = rows[g], lanes[h]
            u_ref[d, r, sl] = uwi[:, :DN_HEAD_DIM].astype(BF16)
            w_ref[d, r, sl] = uwi[:, DN_HEAD_DIM:].astype(BF16)
            qg_ref[d, r, sl] = (q[g, h] * (scale * eg[c])).astype(BF16)
            kg_ref[d, r, sl] = (k[g, h] * jnp.exp(g_t[c] - g_c[c])).astype(BF16)
            at_ref[d, r, sl] = jnp.concatenate([ati.astype(BF16), pad], axis=-1)
        return carry

    lax.fori_loop(0, q_ref.shape[0] // (G * step_groups), step, 0)


def _delta_prep(qkv, zg, gr, *, batch, seq):
    n = batch * seq
    R, G = min(PREP_ROWS, seq), PREP_GROUP
    steps = seq // R
    rowblk = lambda c: (lambda b, r: (b * steps + r, c))
    out = jax.ShapeDtypeStruct((N_DIR, n, DN_WIDTH), BF16)
    out_spec = pl.BlockSpec((N_DIR, R, DN_WIDTH), lambda b, r: (0, b * steps + r, 0))
    return pl.pallas_call(
        _delta_prep_kernel,
        grid=(batch, steps),
        in_specs=[
            pl.BlockSpec((R, DN_WIDTH), rowblk(0)),
            pl.BlockSpec((R, DN_WIDTH), rowblk(1)),
            pl.BlockSpec((R, DN_WIDTH), rowblk(2)),
            pl.BlockSpec((R, LANES), rowblk(0)),
            pl.BlockSpec((R // G, 16, G), lambda b, r: (b * steps + r, 0, 0)),
        ],
        out_specs=[out_spec] * 5,
        out_shape=[out] * 5,
        compiler_params=_params("parallel", "parallel"),
        name="delta_prep",
    )(qkv, qkv, qkv, zg, gr)


def _delta_scan_kernel(uf, wf, qgf, kgf, atf, zgf, ub, wb, qgb, kgb, atb, zgb, of_ref, ob_ref, s_ref, *, batch):
    L, C = SCAN_CHUNK, SCAN_STEP_CHUNKS

    @pl.when(pl.program_id(0) == 0)
    def _():
        s_ref[...] = jnp.zeros_like(s_ref)

    dirs = ((uf, wf, qgf, kgf, atf, zgf, of_ref), (ub, wb, qgb, kgb, atb, zgb, ob_ref))

    chains = [(d, b, h) for d in range(N_DIR) for b in range(batch) for h in range(DN_HEADS)]
    state = {c: s_ref[c] for c in chains}
    for step in range(C):
        chunk = {0: step, 1: C - 1 - step}
        rows = {d: slice(chunk[d] * L, (chunk[d] + 1) * L) for d in range(N_DIR)}
        egt = {(d, b): jnp.exp(jnp.sum(dirs[d][5][b, rows[d], :], axis=0, keepdims=True))
               for d in range(N_DIR) for b in range(batch)}
        ws, v_new, av = {}, {}, {}
        for c in chains:
            d, b, h = c
            sl = slice(h * DN_HEAD_DIM, (h + 1) * DN_HEAD_DIM)
            wq = jnp.concatenate([dirs[d][1][b, rows[d], sl], dirs[d][2][b, rows[d], sl]], axis=0)
            ws[c] = _dot(wq, state[c].astype(BF16))
        for c in chains:
            d, b, h = c
            sl = slice(h * DN_HEAD_DIM, (h + 1) * DN_HEAD_DIM)
            v_new[c] = (dirs[d][0][b, rows[d], sl].astype(F32) - ws[c][:L]).astype(BF16)
        for c in chains:
            d, b, h = c
            at = dirs[d][4][b, rows[d], h * DN_HEAD_DIM:h * DN_HEAD_DIM + L]
            av[c] = _dot(at, v_new[c])
        for c in chains:
            d, b, h = c
            sl = slice(h * DN_HEAD_DIM, (h + 1) * DN_HEAD_DIM)
            cl = (N_DIR + d) * DN_HEADS + h
            dirs[d][6][b, rows[d], sl] = (ws[c][L:] + av[c]).astype(BF16)
            state[c] = state[c] * egt[d, b][:, cl:cl + 1] + _dot_tn(dirs[d][3][b, rows[d], sl], v_new[c])
    for c in chains:
        s_ref[c] = state[c]


def _delta_scan(qkv, zg, gr, *, batch, seq):
    n = batch * seq
    T = SCAN_CHUNK * SCAN_STEP_CHUNKS
    steps = seq // T
    prepped = [t.reshape(N_DIR, batch, seq, DN_WIDTH) for t in _delta_prep(qkv, zg, gr, batch=batch, seq=seq)]
    zg3 = zg.reshape(batch, seq, LANES)
    fwd = lambda s: s
    bwd = lambda s: steps - 1 - s
    specs = []
    for d, pos in ((0, fwd), (1, bwd)):
        specs += [pl.BlockSpec((None, batch, T, DN_WIDTH), lambda s, d=d, pos=pos: (d, 0, pos(s), 0))] * 5
        specs += [pl.BlockSpec((batch, T, LANES), lambda s, pos=pos: (0, pos(s), 0))]
    out = jax.ShapeDtypeStruct((batch, seq, DN_WIDTH), BF16)
    o_f, o_b = pl.pallas_call(
        functools.partial(_delta_scan_kernel, batch=batch),
        grid=(steps,),
        in_specs=specs,
        out_specs=[pl.BlockSpec((batch, T, DN_WIDTH), lambda s: (0, fwd(s), 0)),
                   pl.BlockSpec((batch, T, DN_WIDTH), lambda s: (0, bwd(s), 0))],
        out_shape=[out, out],
        scratch_shapes=[pltpu.VMEM((N_DIR, batch, DN_HEADS, DN_HEAD_DIM, DN_HEAD_DIM), F32)],
        compiler_params=_params("arbitrary"),
        name="delta_scan",
    )(*prepped, zg3, *prepped, zg3)
    return o_f.reshape(n, DN_WIDTH), o_b.reshape(n, DN_WIDTH)


ML_PAIRS = ML_HEADS // 2
ML_AUG = 2 * ML_V_DIM
ML_KV_ROWS = ML_PAIRS * 2 * ML_QK_DIM


def _lane_cols(cols, width):
    rows = cols[0].shape[0]
    lane = lax.broadcasted_iota(jnp.int32, (rows, width), 1)
    out = jnp.zeros((rows, width), F32)
    for j, c in enumerate(cols):
        out = jnp.where(lane == j, c, out)
    return out


def _mlstm_prep_kernel(q_ref, k_ref, v_ref, gc_ref, gr_ref, qs_ref, iv_ref, kv_ref, aux_ref):
    G, L = PREP_GROUP, SCAN_CHUNK
    row = lax.broadcasted_iota(jnp.int32, (G, G), 0)
    col = lax.broadcasted_iota(jnp.int32, (G, G), 1)
    shift = L.bit_length() - 1
    same01 = _as01((row >> shift) == (col >> shift))
    masks = [_group_masks(False), _group_masks(True)]
    cum_c = [_as01(m[0]) for m in masks]
    cum_r = [_as01(m[2]) for m in masks]
    lane = lax.broadcasted_iota(jnp.int32, (G, LANES), 1)
    first_head = lane < ML_QK_DIM
    lane_r = lax.broadcasted_iota(jnp.int32, (1, G), 1)
    row_c = lax.broadcasted_iota(jnp.int32, (G, 1), 0)
    sel_row = lax.broadcasted_iota(jnp.int32, (2 * LANES, G), 0) & (LANES - 1)
    gate_lane = lambda h, d: N_DIR * ML_HEADS + d * ML_HEADS + h
    pick_pair = {(p, d): jnp.concatenate([_as01(sel_row == gate_lane(2 * p, d)),
                                          _as01(sel_row == gate_lane(2 * p + 1, d))], axis=-1)
                 for p in range(ML_PAIRS) for d in range(N_DIR)}
    ones_v = jnp.ones((G, ML_V_DIM), BF16)
    scale = ML_QK_DIM ** -0.5
    heads = range(ML_HEADS)
    chains = [(h, d) for h in heads for d in range(N_DIR)]

    step_groups = min(ML_PREP_STEP_GROUPS, q_ref.shape[0] // G)

    def step(i, carry):
        groups = range(step_groups)
        gidx = [i * step_groups + g for g in groups]
        rows = [pl.ds(pl.multiple_of(x * G, G), G) for x in gidx]
        gc = [gc_ref[r, :] for r in rows]
        gr = [gr_ref[x] for x in gidx]
        bt_c = [_exact_dot01(same01, x) for x in gc]
        bt_r = [_exact_dot01_r(x, same01) for x in gr]
        bc_c = [[_exact_dot01(cum_c[d], x) for d in range(N_DIR)] for x in gc]
        bc_r = [[_exact_dot01_r(x, cum_r[d]) for d in range(N_DIR)] for x in gr]
        gps = [(g, p) for g in groups for p in range(ML_PAIRS)]
        ghs = [(g, h) for g in groups for h in heads]
        chains = [(g, h, d) for g, h in ghs for d in range(N_DIR)]
        qp = {(g, p): q_ref[rows[g], p * LANES:(p + 1) * LANES].astype(F32) * scale for g, p in gps}
        kp16 = {(g, p): k_ref[rows[g], p * LANES:(p + 1) * LANES] for g, p in gps}
        kp = {c: kp16[c].astype(F32) for c in gps}
        v16 = {(g, h): v_ref[rows[g], h * ML_V_DIM:(h + 1) * ML_V_DIM].astype(BF16) for g, h in ghs}
        for g, p in gps:
            qs_ref[rows[g], p * LANES:(p + 1) * LANES] = qp[g, p].astype(BF16)
        own = [first_head, jnp.logical_not(first_head)]
        scores = {(g, h): _dot_nt(jnp.where(own[h % 2], qp[g, h // 2], 0.0).astype(BF16), kp16[g, h // 2])
                  for g, h in ghs}
        col = lambda c: N_DIR * ML_HEADS + c[2] * ML_HEADS + c[1]
        b_c = {c: bc_c[c[0]][c[2]][:, col(c):col(c) + 1] for c in chains}
        b_t = {c: bt_c[c[0]][:, col(c):col(c) + 1] for c in chains}
        b_r = {c: bc_r[c[0]][c[2]][col(c):col(c) + 1, :] for c in chains}
        li_c = {c: gc[c[0]][:, col(c) - N_DIR * ML_HEADS:col(c) - N_DIR * ML_HEADS + 1] for c in chains}
        li_r = {c: gr[c[0]][col(c) - N_DIR * ML_HEADS:col(c) - N_DIR * ML_HEADS + 1, :] for c in chains}
        split = {(g, d): _split_hi_lo(bc_c[g][d]) for g in groups for d in range(N_DIR)}
        b_cb = {}
        for g, p in gps:
            for d in range(N_DIR):
                both = _dot(split[g, d], pick_pair[p, d])
                b_cb[g, 2 * p, d], b_cb[g, 2 * p + 1, d] = both[:, :G], both[:, G:]
        d_mat = {c: jnp.where(masks[c[2]][0], b_cb[c] - b_r[c] + li_r[c], NEG_BIG) for c in chains}
        d_max = {c: jnp.max(d_mat[c], axis=-1, keepdims=True) for c in chains}
        pmat = {c: (jnp.exp(d_mat[c] - d_max[c]) * scores[c[:2]]).astype(BF16) for c in chains}
        w_end_r = {c: bt_r[c[0]][col(c):col(c) + 1, :] - b_r[c] + li_r[c] for c in chains}
        w_max = {}
        for c in chains:
            wm = jnp.full((G, 1), NEG_BIG, F32)
            for j in range(G // L):
                in_j = jnp.logical_and(lane_r >= j * L, lane_r < (j + 1) * L)
                wm_j = jnp.max(jnp.where(in_j, w_end_r[c], NEG_BIG), axis=-1, keepdims=True)
                wm = jnp.where(jnp.logical_and(row_c >= j * L, row_c < (j + 1) * L), wm_j, wm)
            w_max[c] = wm
        sw0 = {c: jnp.exp(b_t[c] - b_c[c] + li_c[c] - w_max[c]) for c in chains}
        iv = {c: _dot(pmat[c], jnp.concatenate([v16[c[:2]], ones_v], axis=-1)) for c in chains}
        for c in chains:
            g, h, d = c
            iv_ref[d, rows[g], h * ML_AUG:(h + 1) * ML_AUG] = iv[c].astype(BF16)
        for g in groups:
            for d in range(N_DIR):
                src = N_DIR * ML_HEADS + d * ML_HEADS
                aux_ref[d, rows[g], :LANES] = jnp.where(
                    lane < ML_HEADS, pltpu.roll(bc_c[g][d], LANES - src, 1),
                    jnp.where(lane < 2 * ML_HEADS, pltpu.roll(bt_c[g], LANES - src + ML_HEADS, 1), 0.0))
                aux_ref[d, rows[g], LANES:] = _lane_cols(
                    [d_max[g, h, d] for h in heads] + [w_max[g, h, d] for h in heads], LANES)
        gpd = [(g, p, d) for g, p in gps for d in range(N_DIR)]
        ks = {(g, p, d): (kp[g, p] * jnp.where(first_head, sw0[g, 2 * p, d], sw0[g, 2 * p + 1, d])).astype(BF16)
              for g, p, d in gpd}
        vcat = {(g, p): jnp.concatenate([v16[g, 2 * p], v16[g, 2 * p + 1], ones_v], axis=-1) for g, p in gps}
        pair_chunks = [(g, p, d, j) for g, p, d in gpd for j in range(G // L)]
        kv = [_dot_tn(ks[g, p, d][j * L:(j + 1) * L], vcat[g, p][j * L:(j + 1) * L]) for g, p, d, j in pair_chunks]
        for (g, p, d, j), t in zip(pair_chunks, kv):
            top = jnp.concatenate([t[:ML_QK_DIM, :ML_V_DIM], t[:ML_QK_DIM, 2 * ML_V_DIM:]], axis=-1)
            bot = jnp.concatenate([t[ML_QK_DIM:, ML_V_DIM:2 * ML_V_DIM], t[ML_QK_DIM:, 2 * ML_V_DIM:]], axis=-1)
            r0 = pl.multiple_of(gidx[g] * (G // L * ML_KV_ROWS) + (j * ML_PAIRS + p) * LANES, LANES)
            kv_ref[d, pl.ds(r0, LANES), :] = jnp.concatenate([top, bot], axis=0).astype(BF16)
        return carry

    lax.fori_loop(0, q_ref.shape[0] // (G * step_groups), step, 0)


def _mlstm_prep(z, zg, gr, *, batch, seq, q_col, k_col, v_col):
    n = batch * seq
    R, G, L = min(PREP_ROWS, seq), PREP_GROUP, SCAN_CHUNK
    steps = seq // R
    rowblk = lambda c: (lambda b, r: (b * steps + r, c))
    dirblk = lambda b, r: (0, b * steps + r, 0)
    kv_rows = ML_KV_ROWS // L
    return pl.pallas_call(
        _mlstm_prep_kernel,
        grid=(batch, steps),
        in_specs=[
            pl.BlockSpec((R, ML_QK_WIDTH), rowblk(q_col // ML_QK_WIDTH)),
            pl.BlockSpec((R, ML_QK_WIDTH), rowblk(k_col // ML_QK_WIDTH)),
            pl.BlockSpec((R, ML_V_WIDTH), rowblk(v_col // ML_V_WIDTH)),
            pl.BlockSpec((R, LANES), rowblk(0)),
            pl.BlockSpec((R // G, 16, G), lambda b, r: (b * steps + r, 0, 0)),
        ],
        out_specs=[
            pl.BlockSpec((R, ML_QK_WIDTH), rowblk(0)),
            pl.BlockSpec((N_DIR, R, ML_HEADS * ML_AUG), dirblk),
            pl.BlockSpec((N_DIR, R * kv_rows, ML_AUG), dirblk),
            pl.BlockSpec((N_DIR, R, 2 * LANES), dirblk),
        ],
        out_shape=[
            jax.ShapeDtypeStruct((n, ML_QK_WIDTH), BF16),
            jax.ShapeDtypeStruct((N_DIR, n, ML_HEADS * ML_AUG), BF16),
            jax.ShapeDtypeStruct((N_DIR, n * kv_rows, ML_AUG), BF16),
            jax.ShapeDtypeStruct((N_DIR, n, 2 * LANES), F32),
        ],
        compiler_params=_params("parallel", "parallel"),
        name="mlstm_prep",
    )(z, z, z, zg, gr)


def _mlstm_scan_kernel(qf, ivf, kvf, auxf, qb, ivb, kvb, auxb, of_ref, ob_ref, c_ref, m_ref, *, batch):
    L, C = SCAN_CHUNK, SCAN_STEP_CHUNKS

    @pl.when(pl.program_id(0) == 0)
    def _():
        c_ref[...] = jnp.zeros_like(c_ref)
        m_ref[...] = jnp.zeros_like(m_ref)

    dirs = ((qf, ivf, kvf, auxf, of_ref), (qb, ivb, kvb, auxb, ob_ref))
    lane = lax.broadcasted_iota(jnp.int32, (L, LANES), 1)
    keep = [jnp.where(lane < ML_QK_DIM, 1.0, 0.0).astype(BF16), jnp.where(lane < ML_QK_DIM, 0.0, 1.0).astype(BF16)]
    first_rows = lax.broadcasted_iota(jnp.int32, (2 * ML_QK_DIM, 1), 0) < ML_QK_DIM
    pair_chains = [(d, b, p) for d in range(N_DIR) for b in range(batch) for p in range(ML_PAIRS)]
    chains = [(d, b, h) for d in range(N_DIR) for b in range(batch) for h in range(ML_HEADS)]
    groups = [(d, b) for d in range(N_DIR) for b in range(batch)]
    H = ML_HEADS
    lane8 = lax.broadcasted_iota(jnp.int32, (SUBLANES, LANES), 1)
    sel_row = lax.broadcasted_iota(jnp.int32, (2 * LANES, LANES), 0) & (LANES - 1)
    lane_select = [_as01(sel_row == j) for j in range(2 * H)]
    state = {c: c_ref[c] for c in pair_chains}
    m_st = {g: m_ref[g] for g in groups}
    for step in range(C):
        chunk = {0: step, 1: C - 1 - step}
        rows = {d: slice(chunk[d] * L, (chunk[d] + 1) * L) for d in range(N_DIR)}
        qc = {}
        for c in pair_chains:
            d, b, p = c
            qp = dirs[d][0][b, rows[d], p * LANES:(p + 1) * LANES]
            qc[c] = _dot(jnp.concatenate([qp * keep[0], qp * keep[1]], axis=0), state[c].astype(BF16))
        w_prev, w_cur, floor = {}, {}, {}
        for g in groups:
            d, b = g
            aux = dirs[d][3][b, rows[d], :]
            x = aux[:, :LANES] + m_st[g][0:1, :]
            y = jnp.maximum(x, aux[:, LANES:])
            w_prev[g] = jnp.exp(x - y)
            w_cur[g] = jnp.exp(aux[:, LANES:] - y)
            floor[g] = jnp.exp(-y)
            y0 = jnp.broadcast_to(y[0:1, :], (SUBLANES, LANES))
            m_st[g] = jnp.where(lane8 < H, pltpu.roll(y0, LANES - H, 1), jnp.where(lane8 < 2 * H, y0, 0.0))
        tiles = [t[g] for g in groups for t in (w_prev, w_cur, floor)]
        per_row = _split_hi_lo(jnp.concatenate(tiles, axis=0))
        per_chunk = _split_hi_lo(jnp.concatenate([t[0:1, :] for t in tiles] + [tiles[0][0:SUBLANES, :]], axis=0))
        bc_row, bc_chunk = [], []
        for p in range(ML_PAIRS):
            both = _dot(per_row, jnp.concatenate([lane_select[2 * p], lane_select[2 * p + 1]], axis=-1))
            bc_row += [both[:, :LANES], both[:, LANES:]]
            both = _dot(per_chunk, jnp.concatenate([lane_select[H + 2 * p], lane_select[H + 2 * p + 1]], axis=-1))
            bc_chunk += [both[:, :LANES], both[:, LANES:]]
        numden = {}
        for c in chains:
            d, b, h = c
            r0 = (h % 2) * L
            t0 = 3 * groups.index((d, b)) * L
            wp = bc_row[h][t0:t0 + L, :]
            wc = bc_row[h][t0 + L:t0 + 2 * L, :]
            iv = dirs[d][1][b, rows[d], h * ML_AUG:(h + 1) * ML_AUG].astype(F32)
            qch = qc[d, b, h // 2][r0:r0 + L, :]
            numden[c] = jnp.concatenate([wp * qch[:, :ML_V_DIM] + wc * iv[:, :ML_V_DIM],
                                         wp * qch[:, ML_V_DIM:] + wc * iv[:, ML_V_DIM:]], axis=-1)
        for c in chains:
            d, b, h = c
            t0 = 3 * groups.index((d, b)) * L
            den = jnp.maximum(jnp.abs(numden[c][:, ML_V_DIM:]), bc_row[h][t0 + 2 * L:t0 + 3 * L, :])
            dirs[d][4][b, rows[d], h * ML_V_DIM:(h + 1) * ML_V_DIM] = (numden[c][:, :ML_V_DIM] / den).astype(BF16)
        for c in pair_chains:
            d, b, p = c
            r0 = (chunk[d] * ML_PAIRS + p) * LANES
            kv = dirs[d][2][b, r0:r0 + LANES, :].astype(F32)
            t0 = 3 * groups.index((d, b))
            cw = jnp.where(first_rows, bc_chunk[2 * p][t0:t0 + 1, :], bc_chunk[2 * p + 1][t0:t0 + 1, :])
            iw = jnp.where(first_rows, bc_chunk[2 * p][t0 + 1:t0 + 2, :], bc_chunk[2 * p + 1][t0 + 1:t0 + 2, :])
            state[c] = jnp.concatenate([cw * state[c][:, :ML_V_DIM] + iw * kv[:, :ML_V_DIM],
                                        cw * state[c][:, ML_V_DIM:] + iw * kv[:, ML_V_DIM:]], axis=-1)
    for c in pair_chains:
        c_ref[c] = state[c]
    for g in groups:
        m_ref[g] = m_st[g]


def _mlstm_scan(z, zg, gr, *, batch, seq, q_col, k_col, v_col):
    n = batch * seq
    T = SCAN_CHUNK * SCAN_STEP_CHUNKS
    steps = seq // T
    kv_rows = ML_KV_ROWS // SCAN_CHUNK
    qs, iv, kv, aux = _mlstm_prep(z, zg, gr, batch=batch, seq=seq, q_col=q_col, k_col=k_col, v_col=v_col)
    qs = qs.reshape(batch, seq, ML_QK_WIDTH)
    iv = iv.reshape(N_DIR, batch, seq, ML_HEADS * ML_AUG)
    kv = kv.reshape(N_DIR, batch, seq * kv_rows, ML_AUG)
    aux = aux.reshape(N_DIR, batch, seq, 2 * LANES)
    fwd = lambda s: s
    bwd = lambda s: steps - 1 - s
    specs = []
    for d, pos in ((0, fwd), (1, bwd)):
        specs += [
            pl.BlockSpec((batch, T, ML_QK_WIDTH), lambda s, pos=pos: (0, pos(s), 0)),
            pl.BlockSpec((None, batch, T, ML_HEADS * ML_AUG), lambda s, d=d, pos=pos: (d, 0, pos(s), 0)),
            pl.BlockSpec((None, batch, T * kv_rows, ML_AUG), lambda s, d=d, pos=pos: (d, 0, pos(s), 0)),
            pl.BlockSpec((None, batch, T, 2 * LANES), lambda s, d=d, pos=pos: (d, 0, pos(s), 0)),
        ]
    out = jax.ShapeDtypeStruct((batch, seq, ML_V_WIDTH), BF16)
    h_f, h_b = pl.pallas_call(
        functools.partial(_mlstm_scan_kernel, batch=batch),
        grid=(steps,),
        in_specs=specs,
        out_specs=[pl.BlockSpec((batch, T, ML_V_WIDTH), lambda s: (0, fwd(s), 0)),
                   pl.BlockSpec((batch, T, ML_V_WIDTH), lambda s: (0, bwd(s), 0))],
        out_shape=[out, out],
        scratch_shapes=[
            pltpu.VMEM((N_DIR, batch, ML_PAIRS, 2 * ML_QK_DIM, ML_AUG), F32),
            pltpu.VMEM((N_DIR, batch, SUBLANES, LANES), F32),
        ],
        compiler_params=_params("arbitrary"),
        name="mlstm_scan",
    )(qs, iv, kv, aux, qs, iv, kv, aux)
    return h_f.reshape(n, ML_V_WIDTH), h_b.reshape(n, ML_V_WIDTH)


def _head_rms_norm(x, g, n_heads, head_dim):
    parts = []
    for h in range(n_heads):
        xh = x[:, h * head_dim:(h + 1) * head_dim]
        ms = jnp.mean(xh * xh, axis=-1, keepdims=True)
        parts.append(xh * lax.rsqrt(ms + NORM_EPS) * g)
    return jnp.concatenate(parts, axis=-1)


def _layer_norm(x, g, b):
    mu = jnp.mean(x, axis=-1, keepdims=True)
    xc = x - mu
    var = jnp.mean(xc * xc, axis=-1, keepdims=True)
    return xc * lax.rsqrt(var + LN_EPS) * g + b


def _swiglu_residual(x, g_ref, w1_ref, w3_ref, w2_ref, fg_ref, final_norm):
    ms = jnp.mean(x * x, axis=-1, keepdims=True)
    hn = (x * lax.rsqrt(ms + NORM_EPS) * g_ref[...]).astype(BF16)
    a = _dot(hn, w1_ref[...])
    b = _dot(hn, w3_ref[...])
    y = x + _dot((a * jax.nn.sigmoid(a) * b).astype(BF16), w2_ref[...])
    if final_norm:
        ms = jnp.mean(y * y, axis=-1, keepdims=True)
        y = y * lax.rsqrt(ms + NORM_EPS) * fg_ref[...]
    return y


def _ffn_operands(g, w1, w3, w2, final_g):
    d, f = w1.shape
    specs = [_resident((1, d)), _resident((d, f)), _resident((d, f)), _resident((f, d)), _resident((1, d))]
    return specs, (g.reshape(1, d), w1.astype(BF16), w3.astype(BF16), w2.astype(BF16), final_g.reshape(1, d))


def _even_block_kernel(h_ref, c_ref, of_ref, ob_ref, gate_ref, lng_ref, lnb_ref, ng_ref, wa_ref, wb_ref,
                       g_ref, w1_ref, w3_ref, w2_ref, fg_ref, o_ref, *, final_norm):
    ya = _layer_norm(c_ref[...].astype(F32), lng_ref[...], lnb_ref[...])
    ya = ya * jax.nn.sigmoid(ya)
    o = _head_rms_norm(of_ref[...].astype(F32) + ob_ref[...].astype(F32), ng_ref[...], DN_HEADS, DN_HEAD_DIM)
    gate = gate_ref[...].astype(F32)
    o = o * (gate * jax.nn.sigmoid(gate))
    x = h_ref[...] + _dot(ya.astype(BF16), wa_ref[...]) + _dot(o.astype(BF16), wb_ref[...])
    o_ref[...] = _swiglu_residual(x, g_ref, w1_ref, w3_ref, w2_ref, fg_ref, final_norm)


def _even_block(h, conv, o_fwd, o_bwd, z, ln_g, ln_b, norm_g, w_out, ffn, *, gate_col, final_norm):
    n, d = h.shape
    tm = min(ROW_TILE, n)
    gb = gate_col // DN_WIDTH
    row = lambda i: (i, 0)
    ffn_specs, ffn_args = _ffn_operands(*ffn)
    return pl.pallas_call(
        functools.partial(_even_block_kernel, final_norm=final_norm),
        grid=(n // tm,),
        in_specs=[
            pl.BlockSpec((tm, d), row),
            pl.BlockSpec((tm, CONV_CH), row),
            pl.BlockSpec((tm, DN_WIDTH), row),
            pl.BlockSpec((tm, DN_WIDTH), row),
            pl.BlockSpec((tm, DN_WIDTH), lambda i: (i, gb)),
            _resident((1, CONV_CH)),
            _resident((1, CONV_CH)),
            _resident((1, DN_HEAD_DIM)),
            _resident((CONV_CH, d)),
            _resident((DN_WIDTH, d), (CONV_CH // DN_WIDTH, 0)),
        ] + ffn_specs,
        out_specs=pl.BlockSpec((tm, d), row),
        out_shape=jax.ShapeDtypeStruct((n, d), F32),
        compiler_params=_params("parallel"),
        name="even_block",
    )(h, conv, o_fwd, o_bwd, z, ln_g.reshape(1, -1), ln_b.reshape(1, -1), norm_g.reshape(1, -1),
      w_out.astype(BF16), w_out.astype(BF16), *ffn_args)


def _odd_block_kernel(h_ref, u_ref, vp_ref, hf_ref, hb_ref, op_ref, lng_ref, lnb_ref, sgw_ref, sgb_ref, ng_ref,
                      wa_ref, wb_ref, g_ref, w1_ref, w3_ref, w2_ref, fg_ref, o_ref, *, final_norm):
    tm = h_ref.shape[0]
    u = jax.nn.gelu(u_ref[...].astype(F32))
    vv = _layer_norm(jax.nn.gelu(vp_ref[...].astype(F32)), lng_ref[...], lnb_ref[...]).astype(BF16)
    sgb = sgb_ref[...]
    rows = []
    for c in range(tm // SG_CHUNK):
        parts = []
        for g in range(SG_GROUPS):
            blk = vv[c * SG_CHUNK:(c + 1) * SG_CHUNK, g * SG_GROUP_DIM:(g + 1) * SG_GROUP_DIM]
            parts.append(_dot(sgw_ref[g], blk) + sgb[:, g:g + 1])
        rows.append(jnp.concatenate(parts, axis=-1))
    yc = u * jnp.concatenate(rows, axis=0)
    hd = _head_rms_norm(hf_ref[...].astype(F32) + hb_ref[...].astype(F32), ng_ref[...], ML_HEADS, ML_V_DIM)
    hd = hd * jax.nn.sigmoid(op_ref[...].astype(F32))
    x = h_ref[...] + _dot(yc.astype(BF16), wa_ref[...]) + _dot(hd.astype(BF16), wb_ref[...])
    o_ref[...] = _swiglu_residual(x, g_ref, w1_ref, w3_ref, w2_ref, fg_ref, final_norm)


def _odd_block(h, z, h_fwd, h_bwd, ln_g, ln_b, sg_w, sg_b, norm_g, w_out, ffn, *, u_col, v_col, o_col, final_norm):
    n, d = h.shape
    tm = min(ROW_TILE, n)
    row = lambda i: (i, 0)
    ffn_specs, ffn_args = _ffn_operands(*ffn)
    return pl.pallas_call(
        functools.partial(_odd_block_kernel, final_norm=final_norm),
        grid=(n // tm,),
        in_specs=[
            pl.BlockSpec((tm, d), row),
            pl.BlockSpec((tm, SG_WIDTH), lambda i: (i, u_col // SG_WIDTH)),
            pl.BlockSpec((tm, SG_WIDTH), lambda i: (i, v_col // SG_WIDTH)),
            pl.BlockSpec((tm, ML_V_WIDTH), row),
            pl.BlockSpec((tm, ML_V_WIDTH), row),
            pl.BlockSpec((tm, ML_V_WIDTH), lambda i: (i, o_col // ML_V_WIDTH)),
            _resident((1, SG_WIDTH)),
            _resident((1, SG_WIDTH)),
            _resident((SG_GROUPS, SG_CHUNK, SG_CHUNK)),
            _resident((SG_CHUNK, SG_GROUPS)),
            _resident((1, ML_V_DIM)),
            _resident((SG_WIDTH, d)),
            _resident((ML_V_WIDTH, d), (SG_WIDTH // ML_V_WIDTH, 0)),
        ] + ffn_specs,
        out_specs=pl.BlockSpec((tm, d), row),
        out_shape=jax.ShapeDtypeStruct((n, d), F32),
        compiler_params=_params("parallel"),
        name="odd_block",
    )(h, z, z, h_fwd, h_bwd, z, ln_g.reshape(1, -1), ln_b.reshape(1, -1), sg_w.astype(BF16), sg_b.T,
      norm_g.reshape(1, -1), w_out.astype(BF16), w_out.astype(BF16), *ffn_args)


def _gate_weight(w_gate_cols):
    d, c = w_gate_cols.shape
    return jnp.zeros((d, LANES), F32).at[:, :c].set(w_gate_cols).astype(BF16)


def _even_layer(h, j, p, ffn, *, batch, seq, final_norm):
    main = 2 * CONV_CH + 4 * DN_WIDTH
    w_in = p["ev_w_in"][j]
    gate_params = jnp.zeros((SUBLANES, LANES), F32)
    gate_params = gate_params.at[0, 8:16].set(p["ev_dn_a_log"][j].reshape(-1))
    gate_params = gate_params.at[1, 8:16].set(p["ev_dn_dt_bias"][j].reshape(-1))
    z, zg, gr = _in_proj(h, p["mix_norm_g"][2 * j], w_in.astype(BF16), main, _gate_weight(w_in[:, main:]),
                         gate_params, even=True)
    conv = _conv_glu(z, p["ev_conv_w"][j], p["ev_conv_b"][j], batch=batch, seq=seq)
    qkv = _conv_qkv(z, p["ev_dn_conv_w"][j], batch=batch, seq=seq, col0=2 * CONV_CH)
    o_fwd, o_bwd = _delta_scan(qkv, zg, gr, batch=batch, seq=seq)
    return _even_block(h, conv, o_fwd, o_bwd, z, p["ev_conv_ln_g"][j], p["ev_conv_ln_b"][j], p["ev_dn_norm_g"][j],
                       p["ev_w_out"][j], ffn, gate_col=2 * CONV_CH + 3 * DN_WIDTH, final_norm=final_norm)


def _odd_layer(h, j, p, ffn, *, batch, seq, final_norm):
    main = 2 * SG_WIDTH + 2 * ML_QK_WIDTH + 2 * ML_V_WIDTH
    w_in = p["od_w_in"][j]
    gate_params = jnp.zeros((SUBLANES, LANES), F32)
    gate_params = gate_params.at[0, 0:8].set(p["od_ml_i_bias"][j].reshape(-1))
    gate_params = gate_params.at[1, 8:16].set(p["od_ml_f_bias"][j].reshape(-1))
    z, zg, gr = _in_proj(h, p["mix_norm_g"][2 * j + 1], w_in.astype(BF16), main, _gate_weight(w_in[:, main:]),
                         gate_params, even=False)
    q_col = 2 * SG_WIDTH
    k_col = q_col + ML_QK_WIDTH
    v_col = k_col + ML_QK_WIDTH
    o_col = v_col + ML_V_WIDTH
    h_fwd, h_bwd = _mlstm_scan(z, zg, gr, batch=batch, seq=seq, q_col=q_col, k_col=k_col, v_col=v_col)
    return _odd_block(h, z, h_fwd, h_bwd, p["od_sg_ln_g"][j], p["od_sg_ln_b"][j], p["od_sg_w"][j], p["od_sg_b"][j],
                      p["od_ml_norm_g"][j], p["od_w_out"][j], ffn, u_col=0, v_col=SG_WIDTH, o_col=o_col,
                      final_norm=final_norm)


def kernel(x, mix_norm_g, ev_w_in, ev_conv_w, ev_conv_b, ev_conv_ln_g, ev_conv_ln_b, ev_dn_conv_w, ev_dn_a_log, ev_dn_dt_bias, ev_dn_norm_g, ev_w_out, od_w_in, od_sg_ln_g, od_sg_ln_b, od_sg_w, od_sg_b, od_ml_i_bias, od_ml_f_bias, od_ml_norm_g, od_w_out, ffn_norm_g, ffn_w1, ffn_w3, ffn_w2, final_norm_g):
    p = dict(mix_norm_g=mix_norm_g, ev_w_in=ev_w_in, ev_conv_w=ev_conv_w, ev_conv_b=ev_conv_b,
             ev_conv_ln_g=ev_conv_ln_g, ev_conv_ln_b=ev_conv_ln_b, ev_dn_conv_w=ev_dn_conv_w,
             ev_dn_a_log=ev_dn_a_log, ev_dn_dt_bias=ev_dn_dt_bias, ev_dn_norm_g=ev_dn_norm_g, ev_w_out=ev_w_out,
             od_w_in=od_w_in, od_sg_ln_g=od_sg_ln_g, od_sg_ln_b=od_sg_ln_b, od_sg_w=od_sg_w, od_sg_b=od_sg_b,
             od_ml_i_bias=od_ml_i_bias, od_ml_f_bias=od_ml_f_bias, od_ml_norm_g=od_ml_norm_g, od_w_out=od_w_out)
    batch, seq, d = x.shape
    depth = mix_norm_g.shape[0]
    h = x.reshape(batch * seq, d)
    for layer in range(depth):
        ffn = (ffn_norm_g[layer], ffn_w1[layer], ffn_w3[layer], ffn_w2[layer], final_norm_g)
        mixer_layer = _even_layer if layer % 2 == 0 else _odd_layer
        h = mixer_layer(h, layer // 2, p, ffn, batch=batch, seq=seq, final_norm=layer == depth - 1)
    return h.reshape(batch, seq, d)
```

```python
import functools

import jax
import jax.numpy as jnp
from jax import lax
from jax.experimental import pallas as pl
from jax.experimental.pallas import tpu as pltpu

NORM_EPS = 1e-6
LN_EPS = 1e-5
N_DIR = 2

CONV_CH = 512
DN_HEADS = 4
DN_HEAD_DIM = 128
DN_WIDTH = DN_HEADS * DN_HEAD_DIM
SG_GROUPS = 4
SG_GROUP_DIM = 128
SG_WIDTH = SG_GROUPS * SG_GROUP_DIM
SG_CHUNK = 128
ML_HEADS = 4
ML_QK_DIM = 64
ML_V_DIM = 128
ML_QK_WIDTH = ML_HEADS * ML_QK_DIM
ML_V_WIDTH = ML_HEADS * ML_V_DIM

LANES = 128
SUBLANES = 8
VMEM_LIMIT_BYTES = 56 * 1024 * 1024

SCAN_CHUNK = 64
SCAN_STEP_CHUNKS = 2
PREP_GROUP = 128
PREP_ROWS = 512
DN_PREP_STEP_GROUPS = 2
ML_PREP_STEP_GROUPS = 4
ROW_TILE = 512
IN_ROW_TILE = 1024
CONV_TILE = 128
QKV_CONV_TILES = 4
CONV_PAD = 16
NEG_BIG = -1e30

BF16 = jnp.bfloat16
F32 = jnp.float32


def _params(*sem):
    return pltpu.CompilerParams(dimension_semantics=sem, vmem_limit_bytes=VMEM_LIMIT_BYTES)


def _resident(shape, block_index=None):
    index = (0,) * len(shape) if block_index is None else tuple(block_index)
    return pl.BlockSpec(shape, lambda *_: index, pipeline_mode=pl.Buffered(1))


def _dot(a, b):
    return jnp.dot(a, b, preferred_element_type=F32)


def _dot_nt(a, b):
    return lax.dot_general(a, b, (((1,), (1,)), ((), ())), preferred_element_type=F32)


def _dot_tn(a, b):
    return lax.dot_general(a, b, (((0,), (0,)), ((), ())), preferred_element_type=F32)


def _exact_dot01(t01, x):
    x1 = x.astype(BF16)
    r1 = x - x1.astype(F32)
    x2 = r1.astype(BF16)
    x3 = (r1 - x2.astype(F32)).astype(BF16)
    return _dot(t01, x1) + _dot(t01, x2) + _dot(t01, x3)


def _exact_dot01_r(x, t01):
    x1 = x.astype(BF16)
    r1 = x - x1.astype(F32)
    x2 = r1.astype(BF16)
    x3 = (r1 - x2.astype(F32)).astype(BF16)
    return _dot(x1, t01) + _dot(x2, t01) + _dot(x3, t01)


def _in_proj_kernel(h_ref, g_ref, w_ref, wg_ref, gp_ref, zm_ref, zg_ref, zgt_ref, *, even):
    x = h_ref[...]
    ms = jnp.mean(x * x, axis=-1, keepdims=True)
    hn = (x * lax.rsqrt(ms + NORM_EPS) * g_ref[...]).astype(BF16)
    zm_ref[...] = _dot(hn, w_ref[...]).astype(zm_ref.dtype)
    zg = _dot(hn, wg_ref[...])
    p0 = gp_ref[0:1, :]
    p1 = gp_ref[1:2, :]
    lane = lax.broadcasted_iota(jnp.int32, zg.shape, 1)
    if even:
        first = jax.nn.sigmoid(zg)
        second = -jnp.exp(p0) * jax.nn.softplus(zg + p1)
    else:
        first = zg + p0
        second = jax.nn.log_sigmoid(zg + p1)
    gates = jnp.where(lane < 8, first, second)
    zg_ref[...] = gates
    for s in range(gates.shape[0] // PREP_GROUP):
        zgt_ref[s] = gates[s * PREP_GROUP:(s + 1) * PREP_GROUP, :].T[:2 * SUBLANES, :]


def _in_proj(h, g, w_in, c, w_gate, gate_params, *, even):
    n, d = h.shape
    assert c % LANES == 0
    tm = min(IN_ROW_TILE, n)
    return pl.pallas_call(
        functools.partial(_in_proj_kernel, even=even),
        grid=(n // tm,),
        in_specs=[
            pl.BlockSpec((tm, d), lambda i: (i, 0)),
            _resident((1, d)),
            _resident((d, c)),
            _resident((d, LANES)),
            _resident((SUBLANES, LANES)),
        ],
        out_specs=[
            pl.BlockSpec((tm, c), lambda i: (i, 0)),
            pl.BlockSpec((tm, LANES), lambda i: (i, 0)),
            pl.BlockSpec((tm // PREP_GROUP, 2 * SUBLANES, PREP_GROUP), lambda i: (i, 0, 0)),
        ],
        out_shape=[jax.ShapeDtypeStruct((n, c), BF16), jax.ShapeDtypeStruct((n, LANES), F32),
                   jax.ShapeDtypeStruct((n // PREP_GROUP, 2 * SUBLANES, PREP_GROUP), F32)],
        compiler_params=_params("parallel"),
        name="in_proj_even" if even else "in_proj_odd",
    )(h, g.reshape(1, d), w_in, w_gate, gate_params)


def _conv_taps(pad_ref, w_ref, o_ref, bias, *, seq, width, post, tiles_per_step=1):
    half = width // 2
    tiles_per_step = min(tiles_per_step, seq // CONV_TILE)
    step_rows = CONV_TILE * tiles_per_step

    def body(i, carry):
        t0 = pl.multiple_of(i * step_rows, step_rows)
        accs = []
        for s in range(tiles_per_step):
            acc = jnp.zeros((CONV_TILE, LANES), F32)
            for j in range(width):
                acc = acc + pad_ref[pl.ds(t0 + (s * CONV_TILE + CONV_PAD - half + j), CONV_TILE), :] * w_ref[j:j + 1, :]
            accs.append(acc if bias is None else acc + bias)
        for s, out in enumerate(post(accs)):
            o_ref[pl.ds(t0 + s * CONV_TILE, CONV_TILE), :] = out.astype(o_ref.dtype)
        return carry

    lax.fori_loop(0, seq // step_rows, body, 0)


def _fill_padded(pad_ref, x, seq):
    zeros = jnp.zeros((CONV_PAD, LANES), F32)
    pad_ref[0:CONV_PAD, :] = zeros
    pad_ref[CONV_PAD + seq:CONV_PAD + seq + CONV_PAD, :] = zeros
    pad_ref[CONV_PAD:CONV_PAD + seq, :] = x


def _conv_glu_kernel(av_ref, ag_ref, w_ref, b_ref, o_ref, pad_ref, *, seq, width):
    _fill_padded(pad_ref, av_ref[...].astype(F32) * jax.nn.sigmoid(ag_ref[...].astype(F32)), seq)
    _conv_taps(pad_ref, w_ref, o_ref, b_ref[...], seq=seq, width=width, post=lambda tiles: tiles)


def _conv_glu(z, conv_w, conv_b, *, batch, seq):
    width = conv_w.shape[0]
    nblk = CONV_CH // LANES
    return pl.pallas_call(
        functools.partial(_conv_glu_kernel, seq=seq, width=width),
        grid=(batch, nblk),
        in_specs=[
            pl.BlockSpec((seq, LANES), lambda b, c: (b, c)),
            pl.BlockSpec((seq, LANES), lambda b, c: (b, nblk + c)),
            pl.BlockSpec((width, LANES), lambda b, c: (0, c)),
            pl.BlockSpec((1, LANES), lambda b, c: (0, c)),
        ],
        out_specs=pl.BlockSpec((seq, LANES), lambda b, c: (b, c)),
        out_shape=jax.ShapeDtypeStruct((batch * seq, CONV_CH), BF16),
        scratch_shapes=[pltpu.VMEM((seq + 2 * CONV_PAD, LANES), F32)],
        compiler_params=_params("parallel", "parallel"),
        name="conv_glu",
    )(z, z, conv_w, conv_b.reshape(1, CONV_CH))


def _conv_qkv_kernel(x_ref, w_ref, o_ref, pad_ref, *, seq, width, n_norm_blocks):
    _fill_padded(pad_ref, x_ref[...].astype(F32), seq)
    normalise = pl.program_id(1) < n_norm_blocks

    def post(tiles):
        ys = [a * jax.nn.sigmoid(a) for a in tiles]
        sq = [jnp.sum(y * y, axis=-1, keepdims=True) for y in ys]
        inv = [lax.rsqrt(s + NORM_EPS) for s in sq]
        return [jnp.where(normalise, y * r, y) for y, r in zip(ys, inv)]

    _conv_taps(pad_ref, w_ref, o_ref, None, seq=seq, width=width, post=post, tiles_per_step=QKV_CONV_TILES)


def _conv_qkv(z, dn_conv_w, *, batch, seq, col0):
    width = dn_conv_w.shape[0]
    nblk = 3 * DN_WIDTH // LANES
    blk0 = col0 // LANES
    return pl.pallas_call(
        functools.partial(_conv_qkv_kernel, seq=seq, width=width, n_norm_blocks=2 * DN_HEADS),
        grid=(batch, nblk),
        in_specs=[
            pl.BlockSpec((seq, LANES), lambda b, c: (b, blk0 + c)),
            pl.BlockSpec((width, LANES), lambda b, c: (0, c)),
        ],
        out_specs=pl.BlockSpec((seq, LANES), lambda b, c: (b, c)),
        out_shape=jax.ShapeDtypeStruct((batch * seq, 3 * DN_WIDTH), BF16),
        scratch_shapes=[pltpu.VMEM((seq + 2 * CONV_PAD, LANES), F32)],
        compiler_params=_params("parallel", "parallel"),
        name="conv_qkv",
    )(z, dn_conv_w)


def _split_hi_lo(x):
    hi = x.astype(BF16)
    return jnp.concatenate([hi, (x - hi.astype(F32)).astype(BF16)], axis=-1)


def _bf16_all(mats):
    return [m.astype(BF16) for m in mats]


def _unit_tri_inverses(mats, row, col, n):
    eye = (row == col).astype(F32)
    same = (row >> 3) == (col >> 3)
    l8 = [jnp.where(same, a, 0.0) for a in mats]
    l8h = _bf16_all(l8)
    l2h = _bf16_all([_dot(p, p) for p in l8h])
    l4h = _bf16_all([_dot(p, p) for p in l2h])
    x = [eye - p for p in l8]
    x = [xi + _dot(xi.astype(BF16), p) for xi, p in zip(x, l2h)]
    x = [xi + _dot(xi.astype(BF16), p) for xi, p in zip(x, l4h)]
    shift = 3
    while (1 << shift) < n:
        same_next = (row >> (shift + 1)) == (col >> (shift + 1))
        off = jnp.logical_and(same_next, jnp.logical_not(same))
        ch = _bf16_all([jnp.where(off, a, 0.0) for a in mats])
        xh = _bf16_all(x)
        xch = _bf16_all([_dot(p, c) for p, c in zip(xh, ch)])
        x = [xi - _dot(p, q) for xi, p, q in zip(x, xch, xh)]
        same = same_next
        shift += 1
    return x


def _group_masks(backward):
    n = PREP_GROUP
    row = lax.broadcasted_iota(jnp.int32, (n, n), 0)
    col = lax.broadcasted_iota(jnp.int32, (n, n), 1)
    shift = SCAN_CHUNK.bit_length() - 1
    same = (row >> shift) == (col >> shift)
    d = (col - row) if backward else (row - col)
    land = jnp.logical_and
    return land(same, d >= 0), land(same, d > 0), land(same, d <= 0)


def _as01(mask):
    return jnp.where(mask, 1.0, 0.0).astype(BF16)


def _delta_prep_kernel(q_ref, k_ref, v_ref, gc_ref, gr_ref, u_ref, w_ref, qg_ref, kg_ref, at_ref):
    G, L = PREP_GROUP, SCAN_CHUNK
    row = lax.broadcasted_iota(jnp.int32, (G, G), 0)
    col = lax.broadcasted_iota(jnp.int32, (G, G), 1)
    shift = L.bit_length() - 1
    same01 = _as01((row >> shift) == (col >> shift))
    masks = [_group_masks(False), _group_masks(True)]
    cum_c = [_as01(m[0]) for m in masks]
    cum_r = [_as01(m[2]) for m in masks]
    fold = _as01((lax.broadcasted_iota(jnp.int32, (G, L), 0) & (L - 1)) == lax.broadcasted_iota(jnp.int32, (G, L), 1))
    pad = jnp.zeros((G, DN_HEAD_DIM - L), BF16)
    scale = DN_HEAD_DIM ** -0.5

    step_groups = min(DN_PREP_STEP_GROUPS, q_ref.shape[0] // G)

    def step(i, carry):
        groups = range(step_groups)
        rows = [pl.ds(pl.multiple_of((i * step_groups + g) * G, G), G) for g in groups]
        gc = [gc_ref[r, :] for r in rows]
        gr = [gr_ref[i * step_groups + g] for g in groups]
        g_tot = [_exact_dot01(same01, x) for x in gc]
        g_cum_c = [[_exact_dot01(cum_c[d], x) for d in range(N_DIR)] for x in gc]
        g_cum_r = [[_exact_dot01_r(x, cum_r[d]) for d in range(N_DIR)] for x in gr]
        heads = range(DN_HEADS)
        lanes = [slice(h * DN_HEAD_DIM, (h + 1) * DN_HEAD_DIM) for h in heads]
        gh = [(g, h) for g in groups for h in heads]
        chains = [(g, h, d) for g, h in gh for d in range(N_DIR)]
        k16 = {(g, h): k_ref[rows[g], lanes[h]] for g, h in gh}
        q = {(g, h): q_ref[rows[g], lanes[h]].astype(F32) for g, h in gh}
        k = {c: k16[c].astype(F32) for c in gh}
        kk = {c: _dot_nt(k16[c], k16[c]) for c in gh}
        qk = {c: _dot_nt((q[c] * scale).astype(BF16), k16[c]) for c in gh}
        beta, g_c, g_t, decay = {}, {}, {}, {}
        for c in chains:
            g, h, d = c
            cb = d * DN_HEADS + h
            cl = N_DIR * DN_HEADS + cb
            beta[c] = gc[g][:, cb:cb + 1]
            g_c[c] = g_cum_c[g][d][:, cl:cl + 1]
            g_t[c] = g_tot[g][:, cl:cl + 1]
            g_r = g_cum_r[g][d][cl:cl + 1, :]
            decay[c] = jnp.exp(jnp.where(masks[d][0], g_c[c] - g_r, NEG_BIG))
        a = [jnp.where(masks[c[2]][1], kk[c[:2]] * decay[c], 0.0) * beta[c] for c in chains]
        x = _unit_tri_inverses(a, row, col, L)
        eg = {c: jnp.exp(g_c[c]) for c in chains}
        rhs = [jnp.concatenate([v_ref[rows[c[0]], lanes[c[1]]].astype(F32) * beta[c], k[c[:2]] * (beta[c] * eg[c])],
                               axis=-1).astype(BF16) for c in chains]
        uw = [_dot(xi.astype(BF16), r) for xi, r in zip(x, rhs)]
        attn = [jnp.where(masks[c[2]][0], qk[c[:2]] * decay[c], 0.0).astype(BF16) for c in chains]
        at = [_dot(p, fold) for p in attn]
        for c, uwi, ati in zip(chains, uw, at):
            g, h, d = c
            r, sl = rows[g], lanes[h]
            u_ref[d, r, sl] = uwi[:, :DN_HEAD_DIM].astype(BF16)
            w_ref[d, r, sl] = uwi[:, DN_HEAD_DIM:].astype(BF16)
            qg_ref[d, r, sl] = (q[g, h] * (scale * eg[c])).astype(BF16)
            kg_ref[d, r, sl] = (k[g, h] * jnp.exp(g_t[c] - g_c[c])).astype(BF16)
            at_ref[d, r, sl] = jnp.concatenate([ati.astype(BF16), pad], axis=-1)
        return carry

    lax.fori_loop(0, q_ref.shape[0] // (G * step_groups), step, 0)


def _delta_prep(qkv, zg, gr, *, batch, seq):
    n = batch * seq
    R, G = min(PREP_ROWS, seq), PREP_GROUP
    steps = seq // R
    rowblk = lambda c: (lambda b, r: (b * steps + r, c))
    out = jax.ShapeDtypeStruct((N_DIR, n, DN_WIDTH), BF16)
    out_spec = pl.BlockSpec((N_DIR, R, DN_WIDTH), lambda b, r: (0, b * steps + r, 0))
    return pl.pallas_call(
        _delta_prep_kernel,
        grid=(batch, steps),
        in_specs=[
            pl.BlockSpec((R, DN_WIDTH), rowblk(0)),
            pl.BlockSpec((R, DN_WIDTH), rowblk(1)),
            pl.BlockSpec((R, DN_WIDTH), rowblk(2)),
            pl.BlockSpec((R, LANES), rowblk(0)),
            pl.BlockSpec((R // G, 16, G), lambda b, r: (b * steps + r, 0, 0)),
        ],
        out_specs=[out_spec] * 5,
        out_shape=[out] * 5,
        compiler_params=_params("parallel", "parallel"),
        name="delta_prep",
    )(qkv, qkv, qkv, zg, gr)


def _delta_scan_kernel(uf, wf, qgf, kgf, atf, zgf, ub, wb, qgb, kgb, atb, zgb, of_ref, ob_ref, s_ref, *, batch):
    L, C = SCAN_CHUNK, SCAN_STEP_CHUNKS

    @pl.when(pl.program_id(0) == 0)
    def _():
        s_ref[...] = jnp.zeros_like(s_ref)

    dirs = ((uf, wf, qgf, kgf, atf, zgf, of_ref), (ub, wb, qgb, kgb, atb, zgb, ob_ref))

    chains = [(d, b, h) for d in range(N_DIR) for b in range(batch) for h in range(DN_HEADS)]
    state = {c: s_ref[c] for c in chains}
    for step in range(C):
        chunk = {0: step, 1: C - 1 - step}
        rows = {d: slice(chunk[d] * L, (chunk[d] + 1) * L) for d in range(N_DIR)}
        egt = {(d, b): jnp.exp(jnp.sum(dirs[d][5][b, rows[d], :], axis=0, keepdims=True))
               for d in range(N_DIR) for b in range(batch)}
        ws, v_new, av = {}, {}, {}
        for c in chains:
            d, b, h = c
            sl = slice(h * DN_HEAD_DIM, (h + 1) * DN_HEAD_DIM)
            wq = jnp.concatenate([dirs[d][1][b, rows[d], sl], dirs[d][2][b, rows[d], sl]], axis=0)
            ws[c] = _dot(wq, state[c].astype(BF16))
        for c in chains:
            d, b, h = c
            sl = slice(h * DN_HEAD_DIM, (h + 1) * DN_HEAD_DIM)
            v_new[c] = (dirs[d][0][b, rows[d], sl].astype(F32) - ws[c][:L]).astype(BF16)
        for c in chains:
            d, b, h = c
            at = dirs[d][4][b, rows[d], h * DN_HEAD_DIM:h * DN_HEAD_DIM + L]
            av[c] = _dot(at, v_new[c])
        for c in chains:
            d, b, h = c
            sl = slice(h * DN_HEAD_DIM, (h + 1) * DN_HEAD_DIM)
            cl = (N_DIR + d) * DN_HEADS + h
            dirs[d][6][b, rows[d], sl] = (ws[c][L:] + av[c]).astype(BF16)
            state[c] = state[c] * egt[d, b][:, cl:cl + 1] + _dot_tn(dirs[d][3][b, rows[d], sl], v_new[c])
    for c in chains:
        s_ref[c] = state[c]


def _delta_scan(qkv, zg, gr, *, batch, seq):
    n = batch * seq
    T = SCAN_CHUNK * SCAN_STEP_CHUNKS
    steps = seq // T
    prepped = [t.reshape(N_DIR, batch, seq, DN_WIDTH) for t in _delta_prep(qkv, zg, gr, batch=batch, seq=seq)]
    zg3 = zg.reshape(batch, seq, LANES)
    fwd = lambda s: s
    bwd = lambda s: steps - 1 - s
    specs = []
    for d, pos in ((0, fwd), (1, bwd)):
        specs += [pl.BlockSpec((None, batch, T, DN_WIDTH), lambda s, d=d, pos=pos: (d, 0, pos(s), 0))] * 5
        specs += [pl.BlockSpec((batch, T, LANES), lambda s, pos=pos: (0, pos(s), 0))]
    out = jax.ShapeDtypeStruct((batch, seq, DN_WIDTH), BF16)
    o_f, o_b = pl.pallas_call(
        functools.partial(_delta_scan_kernel, batch=batch),
        grid=(steps,),
        in_specs=specs,
        out_specs=[pl.BlockSpec((batch, T, DN_WIDTH), lambda s: (0, fwd(s), 0)),
                   pl.BlockSpec((batch, T, DN_WIDTH), lambda s: (0, bwd(s), 0))],
        out_shape=[out, out],
        scratch_shapes=[pltpu.VMEM((N_DIR, batch, DN_HEADS, DN_HEAD_DIM, DN_HEAD_DIM), F32)],
        compiler_params=_params("arbitrary"),
        name="delta_scan",
    )(*prepped, zg3, *prepped, zg3)
    return o_f.reshape(n, DN_WIDTH), o_b.reshape(n, DN_WIDTH)


ML_PAIRS = ML_HEADS // 2
ML_AUG = 2 * ML_V_DIM
ML_KV_ROWS = ML_PAIRS * 2 * ML_QK_DIM


def _lane_cols(cols, width):
    rows = cols[0].shape[0]
    lane = lax.broadcasted_iota(jnp.int32, (rows, width), 1)
    out = jnp.zeros((rows, width), F32)
    for j, c in enumerate(cols):
        out = jnp.where(lane == j, c, out)
    return out


def _mlstm_prep_kernel(q_ref, k_ref, v_ref, gc_ref, gr_ref, qs_ref, iv_ref, kv_ref, aux_ref):
    G, L = PREP_GROUP, SCAN_CHUNK
    row = lax.broadcasted_iota(jnp.int32, (G, G), 0)
    col = lax.broadcasted_iota(jnp.int32, (G, G), 1)
    shift = L.bit_length() - 1
    same01 = _as01((row >> shift) == (col >> shift))
    masks = [_group_masks(False), _group_masks(True)]
    cum_c = [_as01(m[0]) for m in masks]
    cum_r = [_as01(m[2]) for m in masks]
    lane = lax.broadcasted_iota(jnp.int32, (G, LANES), 1)
    first_head = lane < ML_QK_DIM
    lane_r = lax.broadcasted_iota(jnp.int32, (1, G), 1)
    row_c = lax.broadcasted_iota(jnp.int32, (G, 1), 0)
    sel_row = lax.broadcasted_iota(jnp.int32, (2 * LANES, G), 0) & (LANES - 1)
    gate_lane = lambda h, d: N_DIR * ML_HEADS + d * ML_HEADS + h
    pick_pair = {(p, d): jnp.concatenate([_as01(sel_row == gate_lane(2 * p, d)),
                                          _as01(sel_row == gate_lane(2 * p + 1, d))], axis=-1)
                 for p in range(ML_PAIRS) for d in range(N_DIR)}
    ones_v = jnp.ones((G, ML_V_DIM), BF16)
    scale = ML_QK_DIM ** -0.5
    heads = range(ML_HEADS)
    chains = [(h, d) for h in heads for d in range(N_DIR)]

    step_groups = min(ML_PREP_STEP_GROUPS, q_ref.shape[0] // G)

    def step(i, carry):
        groups = range(step_groups)
        gidx = [i * step_groups + g for g in groups]
        rows = [pl.ds(pl.multiple_of(x * G, G), G) for x in gidx]
        gc = [gc_ref[r, :] for r in rows]
        gr = [gr_ref[x] for x in gidx]
        bt_c = [_exact_dot01(same01, x) for x in gc]
        bt_r = [_exact_dot01_r(x, same01) for x in gr]
        bc_c = [[_exact_dot01(cum_c[d], x) for d in range(N_DIR)] for x in gc]
        bc_r = [[_exact_dot01_r(x, cum_r[d]) for d in range(N_DIR)] for x in gr]
        gps = [(g, p) for g in groups for p in range(ML_PAIRS)]
        ghs = [(g, h) for g in groups for h in heads]
        chains = [(g, h, d) for g, h in ghs for d in range(N_DIR)]
        qp = {(g, p): q_ref[rows[g], p * LANES:(p + 1) * LANES].astype(F32) * scale for g, p in gps}
        kp16 = {(g, p): k_ref[rows[g], p * LANES:(p + 1) * LANES] for g, p in gps}
        kp = {c: kp16[c].astype(F32) for c in gps}
        v16 = {(g, h): v_ref[rows[g], h * ML_V_DIM:(h + 1) * ML_V_DIM].astype(BF16) for g, h in ghs}
        for g, p in gps:
            qs_ref[rows[g], p * LANES:(p + 1) * LANES] = qp[g, p].astype(BF16)
        own = [first_head, jnp.logical_not(first_head)]
        scores = {(g, h): _dot_nt(jnp.where(own[h % 2], qp[g, h // 2], 0.0).astype(BF16), kp16[g, h // 2])
                  for g, h in ghs}
        col = lambda c: N_DIR * ML_HEADS + c[2] * ML_HEADS + c[1]
        b_c = {c: bc_c[c[0]][c[2]][:, col(c):col(c) + 1] for c in chains}
        b_t = {c: bt_c[c[0]][:, col(c):col(c) + 1] for c in chains}
        b_r = {c: bc_r[c[0]][c[2]][col(c):col(c) + 1, :] for c in chains}
        li_c = {c: gc[c[0]][:, col(c) - N_DIR * ML_HEADS:col(c) - N_DIR * ML_HEADS + 1] for c in chains}
        li_r = {c: gr[c[0]][col(c) - N_DIR * ML_HEADS:col(c) - N_DIR * ML_HEADS + 1, :] for c in chains}
        split = {(g, d): _split_hi_lo(bc_c[g][d]) for g in groups for d in range(N_DIR)}
        b_cb = {}
        for g, p in gps:
            for d in range(N_DIR):
                both = _dot(split[g, d], pick_pair[p, d])
                b_cb[g, 2 * p, d], b_cb[g, 2 * p + 1, d] = both[:, :G], both[:, G:]
        d_mat = {c: jnp.where(masks[c[2]][0], b_cb[c] - b_r[c] + li_r[c], NEG_BIG) for c in chains}
        d_max = {c: jnp.max(d_mat[c], axis=-1, keepdims=True) for c in chains}
        pmat = {c: (jnp.exp(d_mat[c] - d_max[c]) * scores[c[:2]]).astype(BF16) for c in chains}
        w_end_r = {c: bt_r[c[0]][col(c):col(c) + 1, :] - b_r[c] + li_r[c] for c in chains}
        w_max = {}
        for c in chains:
            wm = jnp.full((G, 1), NEG_BIG, F32)
            for j in range(G // L):
                in_j = jnp.logical_and(lane_r >= j * L, lane_r < (j + 1) * L)
                wm_j = jnp.max(jnp.where(in_j, w_end_r[c], NEG_BIG), axis=-1, keepdims=True)
                wm = jnp.where(jnp.logical_and(row_c >= j * L, row_c < (j + 1) * L), wm_j, wm)
            w_max[c] = wm
        sw0 = {c: jnp.exp(b_t[c] - b_c[c] + li_c[c] - w_max[c]) for c in chains}
        iv = {c: _dot(pmat[c], jnp.concatenate([v16[c[:2]], ones_v], axis=-1)) for c in chains}
        for c in chains:
            g, h, d = c
            iv_ref[d, rows[g], h * ML_AUG:(h + 1) * ML_AUG] = iv[c].astype(BF16)
        for g in groups:
            for d in range(N_DIR):
                src = N_DIR * ML_HEADS + d * ML_HEADS
                aux_ref[d, rows[g], :LANES] = jnp.where(
                    lane < ML_HEADS, pltpu.roll(bc_c[g][d], LANES - src, 1),
                    jnp.where(lane < 2 * ML_HEADS, pltpu.roll(bt_c[g], LANES - src + ML_HEADS, 1), 0.0))
                aux_ref[d, rows[g], LANES:] = _lane_cols(
                    [d_max[g, h, d] for h in heads] + [w_max[g, h, d] for h in heads], LANES)
        gpd = [(g, p, d) for g, p in gps for d in range(N_DIR)]
        ks = {(g, p, d): (kp[g, p] * jnp.where(first_head, sw0[g, 2 * p, d], sw0[g, 2 * p + 1, d])).astype(BF16)
              for g, p, d in gpd}
        vcat = {(g, p): jnp.concatenate([v16[g, 2 * p], v16[g, 2 * p + 1], ones_v], axis=-1) for g, p in gps}
        pair_chunks = [(g, p, d, j) for g, p, d in gpd for j in range(G // L)]
        kv = [_dot_tn(ks[g, p, d][j * L:(j + 1) * L], vcat[g, p][j * L:(j + 1) * L]) for g, p, d, j in pair_chunks]
        for (g, p, d, j), t in zip(pair_chunks, kv):
            top = jnp.concatenate([t[:ML_QK_DIM, :ML_V_DIM], t[:ML_QK_DIM, 2 * ML_V_DIM:]], axis=-1)
            bot = jnp.concatenate([t[ML_QK_DIM:, ML_V_DIM:2 * ML_V_DIM], t[ML_QK_DIM:, 2 * ML_V_DIM:]], axis=-1)
            r0 = pl.multiple_of(gidx[g] * (G // L * ML_KV_ROWS) + (j * ML_PAIRS + p) * LANES, LANES)
            kv_ref[d, pl.ds(r0, LANES), :] = jnp.concatenate([top, bot], axis=0).astype(BF16)
        return carry

    lax.fori_loop(0, q_ref.shape[0] // (G * step_groups), step, 0)


def _mlstm_prep(z, zg, gr, *, batch, seq, q_col, k_col, v_col):
    n = batch * seq
    R, G, L = min(PREP_ROWS, seq), PREP_GROUP, SCAN_CHUNK
    steps = seq // R
    rowblk = lambda c: (lambda b, r: (b * steps + r, c))
    dirblk = lambda b, r: (0, b * steps + r, 0)
    kv_rows = ML_KV_ROWS // L
    return pl.pallas_call(
        _mlstm_prep_kernel,
        grid=(batch, steps),
        in_specs=[
            pl.BlockSpec((R, ML_QK_WIDTH), rowblk(q_col // ML_QK_WIDTH)),
            pl.BlockSpec((R, ML_QK_WIDTH), rowblk(k_col // ML_QK_WIDTH)),
            pl.BlockSpec((R, ML_V_WIDTH), rowblk(v_col // ML_V_WIDTH)),
            pl.BlockSpec((R, LANES), rowblk(0)),
            pl.BlockSpec((R // G, 16, G), lambda b, r: (b * steps + r, 0, 0)),
        ],
        out_specs=[
            pl.BlockSpec((R, ML_QK_WIDTH), rowblk(0)),
            pl.BlockSpec((N_DIR, R, ML_HEADS * ML_AUG), dirblk),
            pl.BlockSpec((N_DIR, R * kv_rows, ML_AUG), dirblk),
            pl.BlockSpec((N_DIR, R, 2 * LANES), dirblk),
        ],
        out_shape=[
            jax.ShapeDtypeStruct((n, ML_QK_WIDTH), BF16),
            jax.ShapeDtypeStruct((N_DIR, n, ML_HEADS * ML_AUG), BF16),
            jax.ShapeDtypeStruct((N_DIR, n * kv_rows, ML_AUG), BF16),
            jax.ShapeDtypeStruct((N_DIR, n, 2 * LANES), F32),
        ],
        compiler_params=_params("parallel", "parallel"),
        name="mlstm_prep",
    )(z, z, z, zg, gr)


def _mlstm_scan_kernel(qf, ivf, kvf, auxf, qb, ivb, kvb, auxb, of_ref, ob_ref, c_ref, m_ref, *, batch):
    L, C = SCAN_CHUNK, SCAN_STEP_CHUNKS

    @pl.when(pl.program_id(0) == 0)
    def _():
        c_ref[...] = jnp.zeros_like(c_ref)
        m_ref[...] = jnp.zeros_like(m_ref)

    dirs = ((qf, ivf, kvf, auxf, of_ref), (qb, ivb, kvb, auxb, ob_ref))
    lane = lax.broadcasted_iota(jnp.int32, (L, LANES), 1)
    keep = [jnp.where(lane < ML_QK_DIM, 1.0, 0.0).astype(BF16), jnp.where(lane < ML_QK_DIM, 0.0, 1.0).astype(BF16)]
    first_rows = lax.broadcasted_iota(jnp.int32, (2 * ML_QK_DIM, 1), 0) < ML_QK_DIM
    pair_chains = [(d, b, p) for d in range(N_DIR) for b in range(batch) for p in range(ML_PAIRS)]
    chains = [(d, b, h) for d in range(N_DIR) for b in range(batch) for h in range(ML_HEADS)]
    groups = [(d, b) for d in range(N_DIR) for b in range(batch)]
    H = ML_HEADS
    lane8 = lax.broadcasted_iota(jnp.int32, (SUBLANES, LANES), 1)
    sel_row = lax.broadcasted_iota(jnp.int32, (2 * LANES, LANES), 0) & (LANES - 1)
    lane_select = [_as01(sel_row == j) for j in range(2 * H)]
    state = {c: c_ref[c] for c in pair_chains}
    m_st = {g: m_ref[g] for g in groups}
    for step in range(C):
        chunk = {0: step, 1: C - 1 - step}
        rows = {d: slice(chunk[d] * L, (chunk[d] + 1) * L) for d in range(N_DIR)}
        qc = {}
        for c in pair_chains:
            d, b, p = c
            qp = dirs[d][0][b, rows[d], p * LANES:(p + 1) * LANES]
            qc[c] = _dot(jnp.concatenate([qp * keep[0], qp * keep[1]], axis=0), state[c].astype(BF16))
        w_prev, w_cur, floor = {}, {}, {}
        for g in groups:
            d, b = g
            aux = dirs[d][3][b, rows[d], :]
            x = aux[:, :LANES] + m_st[g][0:1, :]
            y = jnp.maximum(x, aux[:, LANES:])
            w_prev[g] = jnp.exp(x - y)
            w_cur[g] = jnp.exp(aux[:, LANES:] - y)
            floor[g] = jnp.exp(-y)
            y0 = jnp.broadcast_to(y[0:1, :], (SUBLANES, LANES))
            m_st[g] = jnp.where(lane8 < H, pltpu.roll(y0, LANES - H, 1), jnp.where(lane8 < 2 * H, y0, 0.0))
        tiles = [t[g] for g in groups for t in (w_prev, w_cur, floor)]
        per_row = _split_hi_lo(jnp.concatenate(tiles, axis=0))
        per_chunk = _split_hi_lo(jnp.concatenate([t[0:1, :] for t in tiles] + [tiles[0][0:SUBLANES, :]], axis=0))
        bc_row, bc_chunk = [], []
        for p in range(ML_PAIRS):
            both = _dot(per_row, jnp.concatenate([lane_select[2 * p], lane_select[2 * p + 1]], axis=-1))
            bc_row += [both[:, :LANES], both[:, LANES:]]
            both = _dot(per_chunk, jnp.concatenate([lane_select[H + 2 * p], lane_select[H + 2 * p + 1]], axis=-1))
            bc_chunk += [both[:, :LANES], both[:, LANES:]]
        numden = {}
        for c in chains:
            d, b, h = c
            r0 = (h % 2) * L
            t0 = 3 * groups.index((d, b)) * L
            wp = bc_row[h][t0:t0 + L, :]
            wc = bc_row[h][t0 + L:t0 + 2 * L, :]
            iv = dirs[d][1][b, rows[d], h * ML_AUG:(h + 1) * ML_AUG].astype(F32)
            qch = qc[d, b, h // 2][r0:r0 + L, :]
            numden[c] = jnp.concatenate([wp * qch[:, :ML_V_DIM] + wc * iv[:, :ML_V_DIM],
                                         wp * qch[:, ML_V_DIM:] + wc * iv[:, ML_V_DIM:]], axis=-1)
        for c in chains:
            d, b, h = c
            t0 = 3 * groups.index((d, b)) * L
            den = jnp.maximum(jnp.abs(numden[c][:, ML_V_DIM:]), bc_row[h][t0 + 2 * L:t0 + 3 * L, :])
            dirs[d][4][b, rows[d], h * ML_V_DIM:(h + 1) * ML_V_DIM] = (numden[c][:, :ML_V_DIM] / den).astype(BF16)
        for c in pair_chains:
            d, b, p = c
            r0 = (chunk[d] * ML_PAIRS + p) * LANES
            kv = dirs[d][2][b, r0:r0 + LANES, :].astype(F32)
            t0 = 3 * groups.index((d, b))
            cw = jnp.where(first_rows, bc_chunk[2 * p][t0:t0 + 1, :], bc_chunk[2 * p + 1][t0:t0 + 1, :])
            iw = jnp.where(first_rows, bc_chunk[2 * p][t0 + 1:t0 + 2, :], bc_chunk[2 * p + 1][t0 + 1:t0 + 2, :])
            state[c] = jnp.concatenate([cw * state[c][:, :ML_V_DIM] + iw * kv[:, :ML_V_DIM],
                                        cw * state[c][:, ML_V_DIM:] + iw * kv[:, ML_V_DIM:]], axis=-1)
    for c in pair_chains:
        c_ref[c] = state[c]
    for g in groups:
        m_ref[g] = m_st[g]


def _mlstm_scan(z, zg, gr, *, batch, seq, q_col, k_col, v_col):
    n = batch * seq
    T = SCAN_CHUNK * SCAN_STEP_CHUNKS
    steps = seq // T
    kv_rows = ML_KV_ROWS // SCAN_CHUNK
    qs, iv, kv, aux = _mlstm_prep(z, zg, gr, batch=batch, seq=seq, q_col=q_col, k_col=k_col, v_col=v_col)
    qs = qs.reshape(batch, seq, ML_QK_WIDTH)
    iv = iv.reshape(N_DIR, batch, seq, ML_HEADS * ML_AUG)
    kv = kv.reshape(N_DIR, batch, seq * kv_rows, ML_AUG)
    aux = aux.reshape(N_DIR, batch, seq, 2 * LANES)
    fwd = lambda s: s
    bwd = lambda s: steps - 1 - s
    specs = []
    for d, pos in ((0, fwd), (1, bwd)):
        specs += [
            pl.BlockSpec((batch, T, ML_QK_WIDTH), lambda s, pos=pos: (0, pos(s), 0)),
            pl.BlockSpec((None, batch, T, ML_HEADS * ML_AUG), lambda s, d=d, pos=pos: (d, 0, pos(s), 0)),
            pl.BlockSpec((None, batch, T * kv_rows, ML_AUG), lambda s, d=d, pos=pos: (d, 0, pos(s), 0)),
            pl.BlockSpec((None, batch, T, 2 * LANES), lambda s, d=d, pos=pos: (d, 0, pos(s), 0)),
        ]
    out = jax.ShapeDtypeStruct((batch, seq, ML_V_WIDTH), BF16)
    h_f, h_b = pl.pallas_call(
        functools.partial(_mlstm_scan_kernel, batch=batch),
        grid=(steps,),
        in_specs=specs,
        out_specs=[pl.BlockSpec((batch, T, ML_V_WIDTH), lambda s: (0, fwd(s), 0)),
                   pl.BlockSpec((batch, T, ML_V_WIDTH), lambda s: (0, bwd(s), 0))],
        out_shape=[out, out],
        scratch_shapes=[
            pltpu.VMEM((N_DIR, batch, ML_PAIRS, 2 * ML_QK_DIM, ML_AUG), F32),
            pltpu.VMEM((N_DIR, batch, SUBLANES, LANES), F32),
        ],
        compiler_params=_params("arbitrary"),
        name="mlstm_scan",
    )(qs, iv, kv, aux, qs, iv, kv, aux)
    return h_f.reshape(n, ML_V_WIDTH), h_b.reshape(n, ML_V_WIDTH)


def _head_rms_norm(x, g, n_heads, head_dim):
    parts = []
    for h in range(n_heads):
        xh = x[:, h * head_dim:(h + 1) * head_dim]
        ms = jnp.mean(xh * xh, axis=-1, keepdims=True)
        parts.append(xh * lax.rsqrt(ms + NORM_EPS) * g)
    return jnp.concatenate(parts, axis=-1)


def _layer_norm(x, g, b):
    mu = jnp.mean(x, axis=-1, keepdims=True)
    xc = x - mu
    var = jnp.mean(xc * xc, axis=-1, keepdims=True)
    return xc * lax.rsqrt(var + LN_EPS) * g + b


def _swiglu_residual(x, g_ref, w1_ref, w3_ref, w2_ref, fg_ref, final_norm):
    ms = jnp.mean(x * x, axis=-1, keepdims=True)
    hn = (x * lax.rsqrt(ms + NORM_EPS) * g_ref[...]).astype(BF16)
    a = _dot(hn, w1_ref[...])
    b = _dot(hn, w3_ref[...])
    y = x + _dot((a * jax.nn.sigmoid(a) * b).astype(BF16), w2_ref[...])
    if final_norm:
        ms = jnp.mean(y * y, axis=-1, keepdims=True)
        y = y * lax.rsqrt(ms + NORM_EPS) * fg_ref[...]
    return y


def _ffn_operands(layer, g, w1, w3, w2, final_g):
    _, d, f = w1.shape
    pick = (layer, 0, 0)
    specs = [_resident((1, d)), _resident((None, d, f), pick), _resident((None, d, f), pick),
             _resident((None, f, d), pick), _resident((1, d))]
    return specs, (g.reshape(1, d), w1.astype(BF16), w3.astype(BF16), w2.astype(BF16), final_g.reshape(1, d))


def _even_block_kernel(h_ref, c_ref, of_ref, ob_ref, gate_ref, lng_ref, lnb_ref, ng_ref, wa_ref, wb_ref,
                       g_ref, w1_ref, w3_ref, w2_ref, fg_ref, o_ref, *, final_norm):
    ya = _layer_norm(c_ref[...].astype(F32), lng_ref[...], lnb_ref[...])
    ya = ya * jax.nn.sigmoid(ya)
    o = _head_rms_norm(of_ref[...].astype(F32) + ob_ref[...].astype(F32), ng_ref[...], DN_HEADS, DN_HEAD_DIM)
    gate = gate_ref[...].astype(F32)
    o = o * (gate * jax.nn.sigmoid(gate))
    x = h_ref[...] + _dot(ya.astype(BF16), wa_ref[...]) + _dot(o.astype(BF16), wb_ref[...])
    o_ref[...] = _swiglu_residual(x, g_ref, w1_ref, w3_ref, w2_ref, fg_ref, final_norm)


def _even_block(h, conv, o_fwd, o_bwd, z, ln_g, ln_b, norm_g, w_out, ffn, *, gate_col, final_norm):
    n, d = h.shape
    tm = min(ROW_TILE, n)
    gb = gate_col // DN_WIDTH
    row = lambda i: (i, 0)
    ffn_specs, ffn_args = _ffn_operands(*ffn)
    return pl.pallas_call(
        functools.partial(_even_block_kernel, final_norm=final_norm),
        grid=(n // tm,),
        in_specs=[
            pl.BlockSpec((tm, d), row),
            pl.BlockSpec((tm, CONV_CH), row),
            pl.BlockSpec((tm, DN_WIDTH), row),
            pl.BlockSpec((tm, DN_WIDTH), row),
            pl.BlockSpec((tm, DN_WIDTH), lambda i: (i, gb)),
            _resident((1, CONV_CH)),
            _resident((1, CONV_CH)),
            _resident((1, DN_HEAD_DIM)),
            _resident((CONV_CH, d)),
            _resident((DN_WIDTH, d), (CONV_CH // DN_WIDTH, 0)),
        ] + ffn_specs,
        out_specs=pl.BlockSpec((tm, d), row),
        out_shape=jax.ShapeDtypeStruct((n, d), F32),
        compiler_params=_params("parallel"),
        name="even_block",
    )(h, conv, o_fwd, o_bwd, z, ln_g.reshape(1, -1), ln_b.reshape(1, -1), norm_g.reshape(1, -1),
      w_out.astype(BF16), w_out.astype(BF16), *ffn_args)


def _odd_block_kernel(h_ref, u_ref, vp_ref, hf_ref, hb_ref, op_ref, lng_ref, lnb_ref, sgw_ref, sgb_ref, ng_ref,
                      wa_ref, wb_ref, g_ref, w1_ref, w3_ref, w2_ref, fg_ref, o_ref, *, final_norm):
    tm = h_ref.shape[0]
    u = jax.nn.gelu(u_ref[...].astype(F32))
    vv = _layer_norm(jax.nn.gelu(vp_ref[...].astype(F32)), lng_ref[...], lnb_ref[...]).astype(BF16)
    sgb = sgb_ref[...]
    rows = []
    for c in range(tm // SG_CHUNK):
        parts = []
        for g in range(SG_GROUPS):
            blk = vv[c * SG_CHUNK:(c + 1) * SG_CHUNK, g * SG_GROUP_DIM:(g + 1) * SG_GROUP_DIM]
            parts.append(_dot(sgw_ref[g], blk) + sgb[:, g:g + 1])
        rows.append(jnp.concatenate(parts, axis=-1))
    yc = u * jnp.concatenate(rows, axis=0)
    hd = _head_rms_norm(hf_ref[...].astype(F32) + hb_ref[...].astype(F32), ng_ref[...], ML_HEADS, ML_V_DIM)
    hd = hd * jax.nn.sigmoid(op_ref[...].astype(F32))
    x = h_ref[...] + _dot(yc.astype(BF16), wa_ref[...]) + _dot(hd.astype(BF16), wb_ref[...])
    o_ref[...] = _swiglu_residual(x, g_ref, w1_ref, w3_ref, w2_ref, fg_ref, final_norm)


def _odd_block(h, z, h_fwd, h_bwd, ln_g, ln_b, sg_w, sg_b, norm_g, w_out, ffn, *, u_col, v_col, o_col, final_norm):
    n, d = h.shape
    tm = min(ROW_TILE, n)
    row = lambda i: (i, 0)
    ffn_specs, ffn_args = _ffn_operands(*ffn)
    return pl.pallas_call(
        functools.partial(_odd_block_kernel, final_norm=final_norm),
        grid=(n // tm,),
        in_specs=[
            pl.BlockSpec((tm, d), row),
            pl.BlockSpec((tm, SG_WIDTH), lambda i: (i, u_col // SG_WIDTH)),
            pl.BlockSpec((tm, SG_WIDTH), lambda i: (i, v_col // SG_WIDTH)),
            pl.BlockSpec((tm, ML_V_WIDTH), row),
            pl.BlockSpec((tm, ML_V_WIDTH), row),
            pl.BlockSpec((tm, ML_V_WIDTH), lambda i: (i, o_col // ML_V_WIDTH)),
            _resident((1, SG_WIDTH)),
            _resident((1, SG_WIDTH)),
            _resident((SG_GROUPS, SG_CHUNK, SG_CHUNK)),
            _resident((SG_CHUNK, SG_GROUPS)),
            _resident((1, ML_V_DIM)),
            _resident((SG_WIDTH, d)),
            _resident((ML_V_WIDTH, d), (SG_WIDTH // ML_V_WIDTH, 0)),
        ] + ffn_specs,
        out_specs=pl.BlockSpec((tm, d), row),
        out_shape=jax.ShapeDtypeStruct((n, d), F32),
        compiler_params=_params("parallel"),
        name="odd_block",
    )(h, z, z, h_fwd, h_bwd, z, ln_g.reshape(1, -1), ln_b.reshape(1, -1), sg_w.astype(BF16), sg_b.T,
      norm_g.reshape(1, -1), w_out.astype(BF16), w_out.astype(BF16), *ffn_args)


def _gate_weight(w_gate_cols):
    d, c = w_gate_cols.shape
    return jnp.zeros((d, LANES), F32).at[:, :c].set(w_gate_cols).astype(BF16)


def _even_layer(h, j, p, ffn, *, batch, seq, final_norm):
    main = 2 * CONV_CH + 4 * DN_WIDTH
    w_in = p["ev_w_in"][j]
    gate_params = jnp.zeros((SUBLANES, LANES), F32)
    gate_params = gate_params.at[0, 8:16].set(p["ev_dn_a_log"][j].reshape(-1))
    gate_params = gate_params.at[1, 8:16].set(p["ev_dn_dt_bias"][j].reshape(-1))
    z, zg, gr = _in_proj(h, p["mix_norm_g"][2 * j], w_in.astype(BF16), main, _gate_weight(w_in[:, main:]),
                         gate_params, even=True)
    conv = _conv_glu(z, p["ev_conv_w"][j], p["ev_conv_b"][j], batch=batch, seq=seq)
    qkv = _conv_qkv(z, p["ev_dn_conv_w"][j], batch=batch, seq=seq, col0=2 * CONV_CH)
    o_fwd, o_bwd = _delta_scan(qkv, zg, gr, batch=batch, seq=seq)
    return _even_block(h, conv, o_fwd, o_bwd, z, p["ev_conv_ln_g"][j], p["ev_conv_ln_b"][j], p["ev_dn_norm_g"][j],
                       p["ev_w_out"][j], ffn, gate_col=2 * CONV_CH + 3 * DN_WIDTH, final_norm=final_norm)


def _odd_layer(h, j, p, ffn, *, batch, seq, final_norm):
    main = 2 * SG_WIDTH + 2 * ML_QK_WIDTH + 2 * ML_V_WIDTH
    w_in = p["od_w_in"][j]
    gate_params = jnp.zeros((SUBLANES, LANES), F32)
    gate_params = gate_params.at[0, 0:8].set(p["od_ml_i_bias"][j].reshape(-1))
    gate_params = gate_params.at[1, 8:16].set(p["od_ml_f_bias"][j].reshape(-1))
    z, zg, gr = _in_proj(h, p["mix_norm_g"][2 * j + 1], w_in.astype(BF16), main, _gate_weight(w_in[:, main:]),
                         gate_params, even=False)
    q_col = 2 * SG_WIDTH
    k_col = q_col + ML_QK_WIDTH
    v_col = k_col + ML_QK_WIDTH
    o_col = v_col + ML_V_WIDTH
    h_fwd, h_bwd = _mlstm_scan(z, zg, gr, batch=batch, seq=seq, q_col=q_col, k_col=k_col, v_col=v_col)
    return _odd_block(h, z, h_fwd, h_bwd, p["od_sg_ln_g"][j], p["od_sg_ln_b"][j], p["od_sg_w"][j], p["od_sg_b"][j],
                      p["od_ml_norm_g"][j], p["od_w_out"][j], ffn, u_col=0, v_col=SG_WIDTH, o_col=o_col,
                      final_norm=final_norm)


def kernel(x, mix_norm_g, ev_w_in, ev_conv_w, ev_conv_b, ev_conv_ln_g, ev_conv_ln_b, ev_dn_conv_w, ev_dn_a_log, ev_dn_dt_bias, ev_dn_norm_g, ev_w_out, od_w_in, od_sg_ln_g, od_sg_ln_b, od_sg_w, od_sg_b, od_ml_i_bias, od_ml_f_bias, od_ml_norm_g, od_w_out, ffn_norm_g, ffn_w1, ffn_w3, ffn_w2, final_norm_g):
    p = dict(mix_norm_g=mix_norm_g, ev_w_in=ev_w_in, ev_conv_w=ev_conv_w, ev_conv_b=ev_conv_b,
             ev_conv_ln_g=ev_conv_ln_g, ev_conv_ln_b=ev_conv_ln_b, ev_dn_conv_w=ev_dn_conv_w,
             ev_dn_a_log=ev_dn_a_log, ev_dn_dt_bias=ev_dn_dt_bias, ev_dn_norm_g=ev_dn_norm_g, ev_w_out=ev_w_out,
             od_w_in=od_w_in, od_sg_ln_g=od_sg_ln_g, od_sg_ln_b=od_sg_ln_b, od_sg_w=od_sg_w, od_sg_b=od_sg_b,
             od_ml_i_bias=od_ml_i_bias, od_ml_f_bias=od_ml_f_bias, od_ml_norm_g=od_ml_norm_g, od_w_out=od_w_out)
    batch, seq, d = x.shape
    depth = mix_norm_g.shape[0]
    h = x.reshape(batch * seq, d)
    for layer in range(depth):
        ffn = (layer, ffn_norm_g[layer], ffn_w1, ffn_w3, ffn_w2, final_norm_g)
        mixer_layer = _even_layer if layer % 2 == 0 else _odd_layer
        h = mixer_layer(h, layer // 2, p, ffn, batch=batch, seq=seq, final_norm=layer == depth - 1)
    return h.reshape(batch, seq, d)
```

```python
import functools

import jax
import jax.numpy as jnp
from jax import lax
from jax.experimental import pallas as pl
from jax.experimental.pallas import tpu as pltpu

NORM_EPS = 1e-6
LN_EPS = 1e-5
N_DIR = 2

CONV_CH = 512
DN_HEADS = 4
DN_HEAD_DIM = 128
DN_WIDTH = DN_HEADS * DN_HEAD_DIM
SG_GROUPS = 4
SG_GROUP_DIM = 128
SG_WIDTH = SG_GROUPS * SG_GROUP_DIM
SG_CHUNK = 128
ML_HEADS = 4
ML_QK_DIM = 64
ML_V_DIM = 128
ML_QK_WIDTH = ML_HEADS * ML_QK_DIM
ML_V_WIDTH = ML_HEADS * ML_V_DIM

LANES = 128
SUBLANES = 8
VMEM_LIMIT_BYTES = 56 * 1024 * 1024

SCAN_CHUNK = 64
SCAN_STEP_CHUNKS = 4
PREP_GROUP = 128
PREP_ROWS = 512
DN_PREP_STEP_GROUPS = 2
ML_PREP_STEP_GROUPS = 4
ROW_TILE = 512
IN_ROW_TILE = 1024
CONV_TILE = 128
QKV_CONV_TILES = 8
CONV_PAD = 16
NEG_BIG = -1e30

BF16 = jnp.bfloat16
F32 = jnp.float32


def _params(*sem):
    return pltpu.CompilerParams(dimension_semantics=sem, vmem_limit_bytes=VMEM_LIMIT_BYTES)


def _resident(shape, block_index=None):
    index = (0,) * len(shape) if block_index is None else tuple(block_index)
    return pl.BlockSpec(shape, lambda *_: index, pipeline_mode=pl.Buffered(1))


def _dot(a, b):
    return jnp.dot(a, b, preferred_element_type=F32)


def _dot_nt(a, b):
    return lax.dot_general(a, b, (((1,), (1,)), ((), ())), preferred_element_type=F32)


def _dot_tn(a, b):
    return lax.dot_general(a, b, (((0,), (0,)), ((), ())), preferred_element_type=F32)


def _exact_dot01(t01, x):
    x1 = x.astype(BF16)
    r1 = x - x1.astype(F32)
    x2 = r1.astype(BF16)
    x3 = (r1 - x2.astype(F32)).astype(BF16)
    return _dot(t01, x1) + _dot(t01, x2) + _dot(t01, x3)


def _exact_dot01_r(x, t01):
    x1 = x.astype(BF16)
    r1 = x - x1.astype(F32)
    x2 = r1.astype(BF16)
    x3 = (r1 - x2.astype(F32)).astype(BF16)
    return _dot(x1, t01) + _dot(x2, t01) + _dot(x3, t01)


def _in_proj_kernel(h_ref, g_ref, w_ref, wg_ref, gp_ref, zm_ref, zg_ref, zgt_ref, *, even):
    x = h_ref[...]
    ms = jnp.mean(x * x, axis=-1, keepdims=True)
    hn = (x * lax.rsqrt(ms + NORM_EPS) * g_ref[...]).astype(BF16)
    zm_ref[...] = _dot(hn, w_ref[...]).astype(zm_ref.dtype)
    zg = _dot(hn, wg_ref[...])
    p0 = gp_ref[0:1, :]
    p1 = gp_ref[1:2, :]
    lane = lax.broadcasted_iota(jnp.int32, zg.shape, 1)
    if even:
        first = jax.nn.sigmoid(zg)
        second = -jnp.exp(p0) * jax.nn.softplus(zg + p1)
    else:
        first = zg + p0
        second = jax.nn.log_sigmoid(zg + p1)
    gates = jnp.where(lane < 8, first, second)
    zg_ref[...] = gates
    for s in range(gates.shape[0] // PREP_GROUP):
        zgt_ref[s] = gates[s * PREP_GROUP:(s + 1) * PREP_GROUP, :].T[:2 * SUBLANES, :]


def _in_proj(h, g, w_in, c, w_gate, gate_params, *, even):
    n, d = h.shape
    assert c % LANES == 0
    tm = min(IN_ROW_TILE, n)
    return pl.pallas_call(
        functools.partial(_in_proj_kernel, even=even),
        grid=(n // tm,),
        in_specs=[
            pl.BlockSpec((tm, d), lambda i: (i, 0)),
            _resident((1, d)),
            _resident((d, c)),
            _resident((d, LANES)),
            _resident((SUBLANES, LANES)),
        ],
        out_specs=[
            pl.BlockSpec((tm, c), lambda i: (i, 0)),
            pl.BlockSpec((tm, LANES), lambda i: (i, 0)),
            pl.BlockSpec((tm // PREP_GROUP, 2 * SUBLANES, PREP_GROUP), lambda i: (i, 0, 0)),
        ],
        out_shape=[jax.ShapeDtypeStruct((n, c), BF16), jax.ShapeDtypeStruct((n, LANES), F32),
                   jax.ShapeDtypeStruct((n // PREP_GROUP, 2 * SUBLANES, PREP_GROUP), F32)],
        compiler_params=_params("parallel"),
        name="in_proj_even" if even else "in_proj_odd",
    )(h, g.reshape(1, d), w_in, w_gate, gate_params)


def _conv_taps(pad_ref, w_ref, o_ref, bias, *, seq, width, post, tiles_per_step=1):
    half = width // 2
    tiles_per_step = min(tiles_per_step, seq // CONV_TILE)
    step_rows = CONV_TILE * tiles_per_step

    def body(i, carry):
        t0 = pl.multiple_of(i * step_rows, step_rows)
        accs = []
        for s in range(tiles_per_step):
            acc = jnp.zeros((CONV_TILE, LANES), F32)
            for j in range(width):
                acc = acc + pad_ref[pl.ds(t0 + (s * CONV_TILE + CONV_PAD - half + j), CONV_TILE), :] * w_ref[j:j + 1, :]
            accs.append(acc if bias is None else acc + bias)
        for s, out in enumerate(post(accs)):
            o_ref[pl.ds(t0 + s * CONV_TILE, CONV_TILE), :] = out.astype(o_ref.dtype)
        return carry

    lax.fori_loop(0, seq // step_rows, body, 0)


def _fill_padded(pad_ref, x, seq):
    zeros = jnp.zeros((CONV_PAD, LANES), F32)
    pad_ref[0:CONV_PAD, :] = zeros
    pad_ref[CONV_PAD + seq:CONV_PAD + seq + CONV_PAD, :] = zeros
    pad_ref[CONV_PAD:CONV_PAD + seq, :] = x


def _conv_glu_kernel(av_ref, ag_ref, w_ref, b_ref, o_ref, pad_ref, *, seq, width):
    _fill_padded(pad_ref, av_ref[...].astype(F32) * jax.nn.sigmoid(ag_ref[...].astype(F32)), seq)
    _conv_taps(pad_ref, w_ref, o_ref, b_ref[...], seq=seq, width=width, post=lambda tiles: tiles)


def _conv_glu(z, conv_w, conv_b, *, batch, seq):
    width = conv_w.shape[0]
    nblk = CONV_CH // LANES
    return pl.pallas_call(
        functools.partial(_conv_glu_kernel, seq=seq, width=width),
        grid=(batch, nblk),
        in_specs=[
            pl.BlockSpec((seq, LANES), lambda b, c: (b, c)),
            pl.BlockSpec((seq, LANES), lambda b, c: (b, nblk + c)),
            pl.BlockSpec((width, LANES), lambda b, c: (0, c)),
            pl.BlockSpec((1, LANES), lambda b, c: (0, c)),
        ],
        out_specs=pl.BlockSpec((seq, LANES), lambda b, c: (b, c)),
        out_shape=jax.ShapeDtypeStruct((batch * seq, CONV_CH), BF16),
        scratch_shapes=[pltpu.VMEM((seq + 2 * CONV_PAD, LANES), F32)],
        compiler_params=_params("parallel", "parallel"),
        name="conv_glu",
    )(z, z, conv_w, conv_b.reshape(1, CONV_CH))


def _conv_qkv_kernel(x_ref, w_ref, o_ref, pad_ref, *, seq, width, n_norm_blocks):
    _fill_padded(pad_ref, x_ref[...].astype(F32), seq)
    normalise = pl.program_id(1) < n_norm_blocks

    def post(tiles):
        ys = [a * jax.nn.sigmoid(a) for a in tiles]
        sq = [jnp.sum(y * y, axis=-1, keepdims=True) for y in ys]
        inv = [lax.rsqrt(s + NORM_EPS) for s in sq]
        return [jnp.where(normalise, y * r, y) for y, r in zip(ys, inv)]

    _conv_taps(pad_ref, w_ref, o_ref, None, seq=seq, width=width, post=post, tiles_per_step=QKV_CONV_TILES)


def _conv_qkv(z, dn_conv_w, *, batch, seq, col0):
    width = dn_conv_w.shape[0]
    nblk = 3 * DN_WIDTH // LANES
    blk0 = col0 // LANES
    return pl.pallas_call(
        functools.partial(_conv_qkv_kernel, seq=seq, width=width, n_norm_blocks=2 * DN_HEADS),
        grid=(batch, nblk),
        in_specs=[
            pl.BlockSpec((seq, LANES), lambda b, c: (b, blk0 + c)),
            pl.BlockSpec((width, LANES), lambda b, c: (0, c)),
        ],
        out_specs=pl.BlockSpec((seq, LANES), lambda b, c: (b, c)),
        out_shape=jax.ShapeDtypeStruct((batch * seq, 3 * DN_WIDTH), BF16),
        scratch_shapes=[pltpu.VMEM((seq + 2 * CONV_PAD, LANES), F32)],
        compiler_params=_params("parallel", "parallel"),
        name="conv_qkv",
    )(z, dn_conv_w)


def _split_hi_lo(x):
    hi = x.astype(BF16)
    return jnp.concatenate([hi, (x - hi.astype(F32)).astype(BF16)], axis=-1)


def _bf16_all(mats):
    return [m.astype(BF16) for m in mats]


def _unit_tri_inverses(mats, row, col, n):
    eye = (row == col).astype(F32)
    same = (row >> 3) == (col >> 3)
    l8 = [jnp.where(same, a, 0.0) for a in mats]
    l8h = _bf16_all(l8)
    l2h = _bf16_all([_dot(p, p) for p in l8h])
    l4h = _bf16_all([_dot(p, p) for p in l2h])
    x = [eye - p for p in l8]
    x = [xi + _dot(xi.astype(BF16), p) for xi, p in zip(x, l2h)]
    x = [xi + _dot(xi.astype(BF16), p) for xi, p in zip(x, l4h)]
    shift = 3
    while (1 << shift) < n:
        same_next = (row >> (shift + 1)) == (col >> (shift + 1))
        off = jnp.logical_and(same_next, jnp.logical_not(same))
        ch = _bf16_all([jnp.where(off, a, 0.0) for a in mats])
        xh = _bf16_all(x)
        xch = _bf16_all([_dot(p, c) for p, c in zip(xh, ch)])
        x = [xi - _dot(p, q) for xi, p, q in zip(x, xch, xh)]
        same = same_next
        shift += 1
    return x


def _group_masks(backward):
    n = PREP_GROUP
    row = lax.broadcasted_iota(jnp.int32, (n, n), 0)
    col = lax.broadcasted_iota(jnp.int32, (n, n), 1)
    shift = SCAN_CHUNK.bit_length() - 1
    same = (row >> shift) == (col >> shift)
    d = (col - row) if backward else (row - col)
    land = jnp.logical_and
    return land(same, d >= 0), land(same, d > 0), land(same, d <= 0)


def _as01(mask):
    return jnp.where(mask, 1.0, 0.0).astype(BF16)


def _delta_prep_kernel(q_ref, k_ref, v_ref, gc_ref, gr_ref, u_ref, w_ref, qg_ref, kg_ref, at_ref):
    G, L = PREP_GROUP, SCAN_CHUNK
    row = lax.broadcasted_iota(jnp.int32, (G, G), 0)
    col = lax.broadcasted_iota(jnp.int32, (G, G), 1)
    shift = L.bit_length() - 1
    same01 = _as01((row >> shift) == (col >> shift))
    masks = [_group_masks(False), _group_masks(True)]
    cum_c = [_as01(m[0]) for m in masks]
    cum_r = [_as01(m[2]) for m in masks]
    fold = _as01((lax.broadcasted_iota(jnp.int32, (G, L), 0) & (L - 1)) == lax.broadcasted_iota(jnp.int32, (G, L), 1))
    pad = jnp.zeros((G, DN_HEAD_DIM - L), BF16)
    scale = DN_HEAD_DIM ** -0.5

    step_groups = min(DN_PREP_STEP_GROUPS, q_ref.shape[0] // G)

    def step(i, carry):
        groups = range(step_groups)
        rows = [pl.ds(pl.multiple_of((i * step_groups + g) * G, G), G) for g in groups]
        gc = [gc_ref[r, :] for r in rows]
        gr = [gr_ref[i * step_groups + g] for g in groups]
        g_tot = [_exact_dot01(same01, x) for x in gc]
        g_cum_c = [[_exact_dot01(cum_c[d], x) for d in range(N_DIR)] for x in gc]
        g_cum_r = [[_exact_dot01_r(x, cum_r[d]) for d in range(N_DIR)] for x in gr]
        heads = range(DN_HEADS)
        lanes = [slice(h * DN_HEAD_DIM, (h + 1) * DN_HEAD_DIM) for h in heads]
        gh = [(g, h) for g in groups for h in heads]
        chains = [(g, h, d) for g, h in gh for d in range(N_DIR)]
        k16 = {(g, h): k_ref[rows[g], lanes[h]] for g, h in gh}
        q = {(g, h): q_ref[rows[g], lanes[h]].astype(F32) for g, h in gh}
        k = {c: k16[c].astype(F32) for c in gh}
        kk = {c: _dot_nt(k16[c], k16[c]) for c in gh}
        qk = {c: _dot_nt((q[c] * scale).astype(BF16), k16[c]) for c in gh}
        beta, g_c, g_t, decay = {}, {}, {}, {}
        for c in chains:
            g, h, d = c
            cb = d * DN_HEADS + h
            cl = N_DIR * DN_HEADS + cb
            beta[c] = gc[g][:, cb:cb + 1]
            g_c[c] = g_cum_c[g][d][:, cl:cl + 1]
            g_t[c] = g_tot[g][:, cl:cl + 1]
            g_r = g_cum_r[g][d][cl:cl + 1, :]
            decay[c] = jnp.exp(jnp.where(masks[d][0], g_c[c] - g_r, NEG_BIG))
        a = [jnp.where(masks[c[2]][1], kk[c[:2]] * decay[c], 0.0) * beta[c] for c in chains]
        x = _unit_tri_inverses(a, row, col, L)
        eg = {c: jnp.exp(g_c[c]) for c in chains}
        rhs = [jnp.concatenate([v_ref[rows[c[0]], lanes[c[1]]].astype(F32) * beta[c], k[c[:2]] * (beta[c] * eg[c])],
                               axis=-1).astype(BF16) for c in chains]
        uw = [_dot(xi.astype(BF16), r) for xi, r in zip(x, rhs)]
        attn = [jnp.where(masks[c[2]][0], qk[c[:2]] * decay[c], 0.0).astype(BF16) for c in chains]
        at = [_dot(p, fold) for p in attn]
        for c, uwi, ati in zip(chains, uw, at):
            g, h, d = c
            r, sl = rows[g], lanes[h]
            u_ref[d, r, sl] = uwi[:, :DN_HEAD_DIM].astype(BF16)
            w_ref[d, r, sl] = uwi[:, DN_HEAD_DIM:].astype(BF16)
            qg_ref[d, r, sl] = (q[g, h] * (scale * eg[c])).astype(BF16)
            kg_ref[d, r, sl] = (k[g, h] * jnp.exp(g_t[c] - g_c[c])).astype(BF16)
            at_ref[d, r, sl] = jnp.concatenate([ati.astype(BF16), pad], axis=-1)
        return carry

    lax.fori_loop(0, q_ref.shape[0] // (G * step_groups), step, 0)


def _delta_prep(qkv, zg, gr, *, batch, seq):
    n = batch * seq
    R, G = min(PREP_ROWS, seq), PREP_GROUP
    steps = seq // R
    rowblk = lambda c: (lambda b, r: (b * steps + r, c))
    out = jax.ShapeDtypeStruct((N_DIR, n, DN_WIDTH), BF16)
    out_spec = pl.BlockSpec((N_DIR, R, DN_WIDTH), lambda b, r: (0, b * steps + r, 0))
    return pl.pallas_call(
        _delta_prep_kernel,
        grid=(batch, steps),
        in_specs=[
            pl.BlockSpec((R, DN_WIDTH), rowblk(0)),
            pl.BlockSpec((R, DN_WIDTH), rowblk(1)),
            pl.BlockSpec((R, DN_WIDTH), rowblk(2)),
            pl.BlockSpec((R, LANES), rowblk(0)),
            pl.BlockSpec((R // G, 16, G), lambda b, r: (b * steps + r, 0, 0)),
        ],
        out_specs=[out_spec] * 5,
        out_shape=[out] * 5,
        compiler_params=_params("parallel", "parallel"),
        name="delta_prep",
    )(qkv, qkv, qkv, zg, gr)


def _delta_scan_kernel(uf, wf, qgf, kgf, atf, zgf, ub, wb, qgb, kgb, atb, zgb, of_ref, ob_ref, s_ref, *, batch):
    L, C = SCAN_CHUNK, SCAN_STEP_CHUNKS

    @pl.when(pl.program_id(0) == 0)
    def _():
        s_ref[...] = jnp.zeros_like(s_ref)

    dirs = ((uf, wf, qgf, kgf, atf, zgf, of_ref), (ub, wb, qgb, kgb, atb, zgb, ob_ref))

    chains = [(d, b, h) for d in range(N_DIR) for b in range(batch) for h in range(DN_HEADS)]
    state = {c: s_ref[c] for c in chains}
    for step in range(C):
        chunk = {0: step, 1: C - 1 - step}
        rows = {d: slice(chunk[d] * L, (chunk[d] + 1) * L) for d in range(N_DIR)}
        egt = {(d, b): jnp.exp(jnp.sum(dirs[d][5][b, rows[d], :], axis=0, keepdims=True))
               for d in range(N_DIR) for b in range(batch)}
        ws, v_new, av = {}, {}, {}
        for c in chains:
            d, b, h = c
            sl = slice(h * DN_HEAD_DIM, (h + 1) * DN_HEAD_DIM)
            wq = jnp.concatenate([dirs[d][1][b, rows[d], sl], dirs[d][2][b, rows[d], sl]], axis=0)
            ws[c] = _dot(wq, state[c].astype(BF16))
        for c in chains:
            d, b, h = c
            sl = slice(h * DN_HEAD_DIM, (h + 1) * DN_HEAD_DIM)
            v_new[c] = (dirs[d][0][b, rows[d], sl].astype(F32) - ws[c][:L]).astype(BF16)
        for c in chains:
            d, b, h = c
            at = dirs[d][4][b, rows[d], h * DN_HEAD_DIM:h * DN_HEAD_DIM + L]
            av[c] = _dot(at, v_new[c])
        for c in chains:
            d, b, h = c
            sl = slice(h * DN_HEAD_DIM, (h + 1) * DN_HEAD_DIM)
            cl = (N_DIR + d) * DN_HEADS + h
            dirs[d][6][b, rows[d], sl] = (ws[c][L:] + av[c]).astype(BF16)
            state[c] = state[c] * egt[d, b][:, cl:cl + 1] + _dot_tn(dirs[d][3][b, rows[d], sl], v_new[c])
    for c in chains:
        s_ref[c] = state[c]


def _delta_scan(qkv, zg, gr, *, batch, seq):
    n = batch * seq
    T = SCAN_CHUNK * SCAN_STEP_CHUNKS
    steps = seq // T
    prepped = [t.reshape(N_DIR, batch, seq, DN_WIDTH) for t in _delta_prep(qkv, zg, gr, batch=batch, seq=seq)]
    zg3 = zg.reshape(batch, seq, LANES)
    fwd = lambda s: s
    bwd = lambda s: steps - 1 - s
    specs = []
    for d, pos in ((0, fwd), (1, bwd)):
        specs += [pl.BlockSpec((None, batch, T, DN_WIDTH), lambda s, d=d, pos=pos: (d, 0, pos(s), 0))] * 5
        specs += [pl.BlockSpec((batch, T, LANES), lambda s, pos=pos: (0, pos(s), 0))]
    out = jax.ShapeDtypeStruct((batch, seq, DN_WIDTH), BF16)
    o_f, o_b = pl.pallas_call(
        functools.partial(_delta_scan_kernel, batch=batch),
        grid=(steps,),
        in_specs=specs,
        out_specs=[pl.BlockSpec((batch, T, DN_WIDTH), lambda s: (0, fwd(s), 0)),
                   pl.BlockSpec((batch, T, DN_WIDTH), lambda s: (0, bwd(s), 0))],
        out_shape=[out, out],
        scratch_shapes=[pltpu.VMEM((N_DIR, batch, DN_HEADS, DN_HEAD_DIM, DN_HEAD_DIM), F32)],
        compiler_params=_params("arbitrary"),
        name="delta_scan",
    )(*prepped, zg3, *prepped, zg3)
    return o_f.reshape(n, DN_WIDTH), o_b.reshape(n, DN_WIDTH)


ML_PAIRS = ML_HEADS // 2
ML_AUG = 2 * ML_V_DIM
ML_KV_ROWS = ML_PAIRS * 2 * ML_QK_DIM


def _lane_cols(cols, width):
    rows = cols[0].shape[0]
    lane = lax.broadcasted_iota(jnp.int32, (rows, width), 1)
    out = jnp.zeros((rows, width), F32)
    for j, c in enumerate(cols):
        out = jnp.where(lane == j, c, out)
    return out


def _mlstm_prep_kernel(q_ref, k_ref, v_ref, gc_ref, gr_ref, qs_ref, iv_ref, kv_ref, aux_ref):
    G, L = PREP_GROUP, SCAN_CHUNK
    row = lax.broadcasted_iota(jnp.int32, (G, G), 0)
    col = lax.broadcasted_iota(jnp.int32, (G, G), 1)
    shift = L.bit_length() - 1
    same01 = _as01((row >> shift) == (col >> shift))
    masks = [_group_masks(False), _group_masks(True)]
    cum_c = [_as01(m[0]) for m in masks]
    cum_r = [_as01(m[2]) for m in masks]
    lane = lax.broadcasted_iota(jnp.int32, (G, LANES), 1)
    first_head = lane < ML_QK_DIM
    lane_r = lax.broadcasted_iota(jnp.int32, (1, G), 1)
    row_c = lax.broadcasted_iota(jnp.int32, (G, 1), 0)
    sel_row = lax.broadcasted_iota(jnp.int32, (2 * LANES, G), 0) & (LANES - 1)
    gate_lane = lambda h, d: N_DIR * ML_HEADS + d * ML_HEADS + h
    pick_pair = {(p, d): jnp.concatenate([_as01(sel_row == gate_lane(2 * p, d)),
                                          _as01(sel_row == gate_lane(2 * p + 1, d))], axis=-1)
                 for p in range(ML_PAIRS) for d in range(N_DIR)}
    ones_v = jnp.ones((G, ML_V_DIM), BF16)
    scale = ML_QK_DIM ** -0.5
    heads = range(ML_HEADS)
    chains = [(h, d) for h in heads for d in range(N_DIR)]

    step_groups = min(ML_PREP_STEP_GROUPS, q_ref.shape[0] // G)

    def step(i, carry):
        groups = range(step_groups)
        gidx = [i * step_groups + g for g in groups]
        rows = [pl.ds(pl.multiple_of(x * G, G), G) for x in gidx]
        gc = [gc_ref[r, :] for r in rows]
        gr = [gr_ref[x] for x in gidx]
        bt_c = [_exact_dot01(same01, x) for x in gc]
        bt_r = [_exact_dot01_r(x, same01) for x in gr]
        bc_c = [[_exact_dot01(cum_c[d], x) for d in range(N_DIR)] for x in gc]
        bc_r = [[_exact_dot01_r(x, cum_r[d]) for d in range(N_DIR)] for x in gr]
        gps = [(g, p) for g in groups for p in range(ML_PAIRS)]
        ghs = [(g, h) for g in groups for h in heads]
        chains = [(g, h, d) for g, h in ghs for d in range(N_DIR)]
        qp = {(g, p): q_ref[rows[g], p * LANES:(p + 1) * LANES].astype(F32) * scale for g, p in gps}
        kp16 = {(g, p): k_ref[rows[g], p * LANES:(p + 1) * LANES] for g, p in gps}
        kp = {c: kp16[c].astype(F32) for c in gps}
        v16 = {(g, h): v_ref[rows[g], h * ML_V_DIM:(h + 1) * ML_V_DIM].astype(BF16) for g, h in ghs}
        for g, p in gps:
            qs_ref[rows[g], p * LANES:(p + 1) * LANES] = qp[g, p].astype(BF16)
        own = [first_head, jnp.logical_not(first_head)]
        scores = {(g, h): _dot_nt(jnp.where(own[h % 2], qp[g, h // 2], 0.0).astype(BF16), kp16[g, h // 2])
                  for g, h in ghs}
        col = lambda c: N_DIR * ML_HEADS + c[2] * ML_HEADS + c[1]
        b_c = {c: bc_c[c[0]][c[2]][:, col(c):col(c) + 1] for c in chains}
        b_t = {c: bt_c[c[0]][:, col(c):col(c) + 1] for c in chains}
        b_r = {c: bc_r[c[0]][c[2]][col(c):col(c) + 1, :] for c in chains}
        li_c = {c: gc[c[0]][:, col(c) - N_DIR * ML_HEADS:col(c) - N_DIR * ML_HEADS + 1] for c in chains}
        li_r = {c: gr[c[0]][col(c) - N_DIR * ML_HEADS:col(c) - N_DIR * ML_HEADS + 1, :] for c in chains}
        split = {(g, d): _split_hi_lo(bc_c[g][d]) for g in groups for d in range(N_DIR)}
        b_cb = {}
        for g, p in gps:
            for d in range(N_DIR):
                both = _dot(split[g, d], pick_pair[p, d])
                b_cb[g, 2 * p, d], b_cb[g, 2 * p + 1, d] = both[:, :G], both[:, G:]
        d_mat = {c: jnp.where(masks[c[2]][0], b_cb[c] - b_r[c] + li_r[c], NEG_BIG) for c in chains}
        d_max = {c: jnp.max(d_mat[c], axis=-1, keepdims=True) for c in chains}
        pmat = {c: (jnp.exp(d_mat[c] - d_max[c]) * scores[c[:2]]).astype(BF16) for c in chains}
        w_end_r = {c: bt_r[c[0]][col(c):col(c) + 1, :] - b_r[c] + li_r[c] for c in chains}
        w_max = {}
        for c in chains:
            wm = jnp.full((G, 1), NEG_BIG, F32)
            for j in range(G // L):
                in_j = jnp.logical_and(lane_r >= j * L, lane_r < (j + 1) * L)
                wm_j = jnp.max(jnp.where(in_j, w_end_r[c], NEG_BIG), axis=-1, keepdims=True)
                wm = jnp.where(jnp.logical_and(row_c >= j * L, row_c < (j + 1) * L), wm_j, wm)
            w_max[c] = wm
        sw0 = {c: jnp.exp(b_t[c] - b_c[c] + li_c[c] - w_max[c]) for c in chains}
        iv = {c: _dot(pmat[c], jnp.concatenate([v16[c[:2]], ones_v], axis=-1)) for c in chains}
        for c in chains:
            g, h, d = c
            iv_ref[d, rows[g], h * ML_AUG:(h + 1) * ML_AUG] = iv[c].astype(BF16)
        for g in groups:
            for d in range(N_DIR):
                src = N_DIR * ML_HEADS + d * ML_HEADS
                aux_ref[d, rows[g], :LANES] = jnp.where(
                    lane < ML_HEADS, pltpu.roll(bc_c[g][d], LANES - src, 1),
                    jnp.where(lane < 2 * ML_HEADS, pltpu.roll(bt_c[g], LANES - src + ML_HEADS, 1), 0.0))
                aux_ref[d, rows[g], LANES:] = _lane_cols(
                    [d_max[g, h, d] for h in heads] + [w_max[g, h, d] for h in heads], LANES)
        gpd = [(g, p, d) for g, p in gps for d in range(N_DIR)]
        ks = {(g, p, d): (kp[g, p] * jnp.where(first_head, sw0[g, 2 * p, d], sw0[g, 2 * p + 1, d])).astype(BF16)
              for g, p, d in gpd}
        vcat = {(g, p): jnp.concatenate([v16[g, 2 * p], v16[g, 2 * p + 1], ones_v], axis=-1) for g, p in gps}
        pair_chunks = [(g, p, d, j) for g, p, d in gpd for j in range(G // L)]
        kv = [_dot_tn(ks[g, p, d][j * L:(j + 1) * L], vcat[g, p][j * L:(j + 1) * L]) for g, p, d, j in pair_chunks]
        for (g, p, d, j), t in zip(pair_chunks, kv):
            top = jnp.concatenate([t[:ML_QK_DIM, :ML_V_DIM], t[:ML_QK_DIM, 2 * ML_V_DIM:]], axis=-1)
            bot = jnp.concatenate([t[ML_QK_DIM:, ML_V_DIM:2 * ML_V_DIM], t[ML_QK_DIM:, 2 * ML_V_DIM:]], axis=-1)
            r0 = pl.multiple_of(gidx[g] * (G // L * ML_KV_ROWS) + (j * ML_PAIRS + p) * LANES, LANES)
            kv_ref[d, pl.ds(r0, LANES), :] = jnp.concatenate([top, bot], axis=0).astype(BF16)
        return carry

    lax.fori_loop(0, q_ref.shape[0] // (G * step_groups), step, 0)


def _mlstm_prep(z, zg, gr, *, batch, seq, q_col, k_col, v_col):
    n = batch * seq
    R, G, L = min(PREP_ROWS, seq), PREP_GROUP, SCAN_CHUNK
    steps = seq // R
    rowblk = lambda c: (lambda b, r: (b * steps + r, c))
    dirblk = lambda b, r: (0, b * steps + r, 0)
    kv_rows = ML_KV_ROWS // L
    return pl.pallas_call(
        _mlstm_prep_kernel,
        grid=(batch, steps),
        in_specs=[
            pl.BlockSpec((R, ML_QK_WIDTH), rowblk(q_col // ML_QK_WIDTH)),
            pl.BlockSpec((R, ML_QK_WIDTH), rowblk(k_col // ML_QK_WIDTH)),
            pl.BlockSpec((R, ML_V_WIDTH), rowblk(v_col // ML_V_WIDTH)),
            pl.BlockSpec((R, LANES), rowblk(0)),
            pl.BlockSpec((R // G, 16, G), lambda b, r: (b * steps + r, 0, 0)),
        ],
        out_specs=[
            pl.BlockSpec((R, ML_QK_WIDTH), rowblk(0)),
            pl.BlockSpec((N_DIR, R, ML_HEADS * ML_AUG), dirblk),
            pl.BlockSpec((N_DIR, R * kv_rows, ML_AUG), dirblk),
            pl.BlockSpec((N_DIR, R, 2 * LANES), dirblk),
        ],
        out_shape=[
            jax.ShapeDtypeStruct((n, ML_QK_WIDTH), BF16),
            jax.ShapeDtypeStruct((N_DIR, n, ML_HEADS * ML_AUG), BF16),
            jax.ShapeDtypeStruct((N_DIR, n * kv_rows, ML_AUG), BF16),
            jax.ShapeDtypeStruct((N_DIR, n, 2 * LANES), F32),
        ],
        compiler_params=_params("parallel", "parallel"),
        name="mlstm_prep",
    )(z, z, z, zg, gr)


def _mlstm_scan_kernel(qf, ivf, kvf, auxf, qb, ivb, kvb, auxb, of_ref, ob_ref, c_ref, m_ref, *, batch):
    L, C = SCAN_CHUNK, SCAN_STEP_CHUNKS

    @pl.when(pl.program_id(0) == 0)
    def _():
        c_ref[...] = jnp.zeros_like(c_ref)
        m_ref[...] = jnp.zeros_like(m_ref)

    dirs = ((qf, ivf, kvf, auxf, of_ref), (qb, ivb, kvb, auxb, ob_ref))
    lane = lax.broadcasted_iota(jnp.int32, (L, LANES), 1)
    keep = [jnp.where(lane < ML_QK_DIM, 1.0, 0.0).astype(BF16), jnp.where(lane < ML_QK_DIM, 0.0, 1.0).astype(BF16)]
    first_rows = lax.broadcasted_iota(jnp.int32, (2 * ML_QK_DIM, 1), 0) < ML_QK_DIM
    pair_chains = [(d, b, p) for d in range(N_DIR) for b in range(batch) for p in range(ML_PAIRS)]
    chains = [(d, b, h) for d in range(N_DIR) for b in range(batch) for h in range(ML_HEADS)]
    groups = [(d, b) for d in range(N_DIR) for b in range(batch)]
    H = ML_HEADS
    lane8 = lax.broadcasted_iota(jnp.int32, (SUBLANES, LANES), 1)
    sel_row = lax.broadcasted_iota(jnp.int32, (2 * LANES, LANES), 0) & (LANES - 1)
    lane_select = [_as01(sel_row == j) for j in range(2 * H)]
    state = {c: c_ref[c] for c in pair_chains}
    m_st = {g: m_ref[g] for g in groups}
    for step in range(C):
        chunk = {0: step, 1: C - 1 - step}
        rows = {d: slice(chunk[d] * L, (chunk[d] + 1) * L) for d in range(N_DIR)}
        qc = {}
        for c in pair_chains:
            d, b, p = c
            qp = dirs[d][0][b, rows[d], p * LANES:(p + 1) * LANES]
            qc[c] = _dot(jnp.concatenate([qp * keep[0], qp * keep[1]], axis=0), state[c].astype(BF16))
        w_prev, w_cur, floor = {}, {}, {}
        for g in groups:
            d, b = g
            aux = dirs[d][3][b, rows[d], :]
            x = aux[:, :LANES] + m_st[g][0:1, :]
            y = jnp.maximum(x, aux[:, LANES:])
            w_prev[g] = jnp.exp(x - y)
            w_cur[g] = jnp.exp(aux[:, LANES:] - y)
            floor[g] = jnp.exp(-y)
            y0 = jnp.broadcast_to(y[0:1, :], (SUBLANES, LANES))
            m_st[g] = jnp.where(lane8 < H, pltpu.roll(y0, LANES - H, 1), jnp.where(lane8 < 2 * H, y0, 0.0))
        tiles = [t[g] for g in groups for t in (w_prev, w_cur, floor)]
        per_row = _split_hi_lo(jnp.concatenate(tiles, axis=0))
        per_chunk = _split_hi_lo(jnp.concatenate([t[0:1, :] for t in tiles] + [tiles[0][0:SUBLANES, :]], axis=0))
        bc_row, bc_chunk = [], []
        for p in range(ML_PAIRS):
            both = _dot(per_row, jnp.concatenate([lane_select[2 * p], lane_select[2 * p + 1]], axis=-1))
            bc_row += [both[:, :LANES], both[:, LANES:]]
            both = _dot(per_chunk, jnp.concatenate([lane_select[H + 2 * p], lane_select[H + 2 * p + 1]], axis=-1))
            bc_chunk += [both[:, :LANES], both[:, LANES:]]
        numden = {}
        for c in chains:
            d, b, h = c
            r0 = (h % 2) * L
            t0 = 3 * groups.index((d, b)) * L
            wp = bc_row[h][t0:t0 + L, :]
            wc = bc_row[h][t0 + L:t0 + 2 * L, :]
            iv = dirs[d][1][b, rows[d], h * ML_AUG:(h + 1) * ML_AUG].astype(F32)
            qch = qc[d, b, h // 2][r0:r0 + L, :]
            numden[c] = jnp.concatenate([wp * qch[:, :ML_V_DIM] + wc * iv[:, :ML_V_DIM],
                                         wp * qch[:, ML_V_DIM:] + wc * iv[:, ML_V_DIM:]], axis=-1)
        for c in chains:
            d, b, h = c
            t0 = 3 * groups.index((d, b)) * L
            den = jnp.maximum(jnp.abs(numden[c][:, ML_V_DIM:]), bc_row[h][t0 + 2 * L:t0 + 3 * L, :])
            dirs[d][4][b, rows[d], h * ML_V_DIM:(h + 1) * ML_V_DIM] = (numden[c][:, :ML_V_DIM] / den).astype(BF16)
        for c in pair_chains:
            d, b, p = c
            r0 = (chunk[d] * ML_PAIRS + p) * LANES
            kv = dirs[d][2][b, r0:r0 + LANES, :].astype(F32)
            t0 = 3 * groups.index((d, b))
            cw = jnp.where(first_rows, bc_chunk[2 * p][t0:t0 + 1, :], bc_chunk[2 * p + 1][t0:t0 + 1, :])
            iw = jnp.where(first_rows, bc_chunk[2 * p][t0 + 1:t0 + 2, :], bc_chunk[2 * p + 1][t0 + 1:t0 + 2, :])
            state[c] = jnp.concatenate([cw * state[c][:, :ML_V_DIM] + iw * kv[:, :ML_V_DIM],
                                        cw * state[c][:, ML_V_DIM:] + iw * kv[:, ML_V_DIM:]], axis=-1)
    for c in pair_chains:
        c_ref[c] = state[c]
    for g in groups:
        m_ref[g] = m_st[g]


def _mlstm_scan(z, zg, gr, *, batch, seq, q_col, k_col, v_col):
    n = batch * seq
    T = SCAN_CHUNK * SCAN_STEP_CHUNKS
    steps = seq // T
    kv_rows = ML_KV_ROWS // SCAN_CHUNK
    qs, iv, kv, aux = _mlstm_prep(z, zg, gr, batch=batch, seq=seq, q_col=q_col, k_col=k_col, v_col=v_col)
    qs = qs.reshape(batch, seq, ML_QK_WIDTH)
    iv = iv.reshape(N_DIR, batch, seq, ML_HEADS * ML_AUG)
    kv = kv.reshape(N_DIR, batch, seq * kv_rows, ML_AUG)
    aux = aux.reshape(N_DIR, batch, seq, 2 * LANES)
    fwd = lambda s: s
    bwd = lambda s: steps - 1 - s
    specs = []
    for d, pos in ((0, fwd), (1, bwd)):
        specs += [
            pl.BlockSpec((batch, T, ML_QK_WIDTH), lambda s, pos=pos: (0, pos(s), 0)),
            pl.BlockSpec((None, batch, T, ML_HEADS * ML_AUG), lambda s, d=d, pos=pos: (d, 0, pos(s), 0)),
            pl.BlockSpec((None, batch, T * kv_rows, ML_AUG), lambda s, d=d, pos=pos: (d, 0, pos(s), 0)),
            pl.BlockSpec((None, batch, T, 2 * LANES), lambda s, d=d, pos=pos: (d, 0, pos(s), 0)),
        ]
    out = jax.ShapeDtypeStruct((batch, seq, ML_V_WIDTH), BF16)
    h_f, h_b = pl.pallas_call(
        functools.partial(_mlstm_scan_kernel, batch=batch),
        grid=(steps,),
        in_specs=specs,
        out_specs=[pl.BlockSpec((batch, T, ML_V_WIDTH), lambda s: (0, fwd(s), 0)),
                   pl.BlockSpec((batch, T, ML_V_WIDTH), lambda s: (0, bwd(s), 0))],
        out_shape=[out, out],
        scratch_shapes=[
            pltpu.VMEM((N_DIR, batch, ML_PAIRS, 2 * ML_QK_DIM, ML_AUG), F32),
            pltpu.VMEM((N_DIR, batch, SUBLANES, LANES), F32),
        ],
        compiler_params=_params("arbitrary"),
        name="mlstm_scan",
    )(qs, iv, kv, aux, qs, iv, kv, aux)
    return h_f.reshape(n, ML_V_WIDTH), h_b.reshape(n, ML_V_WIDTH)


def _head_rms_norm(x, g, n_heads, head_dim):
    parts = []
    for h in range(n_heads):
        xh = x[:, h * head_dim:(h + 1) * head_dim]
        ms = jnp.mean(xh * xh, axis=-1, keepdims=True)
        parts.append(xh * lax.rsqrt(ms + NORM_EPS) * g)
    return jnp.concatenate(parts, axis=-1)


def _layer_norm(x, g, b):
    mu = jnp.mean(x, axis=-1, keepdims=True)
    xc = x - mu
    var = jnp.mean(xc * xc, axis=-1, keepdims=True)
    return xc * lax.rsqrt(var + LN_EPS) * g + b


def _swiglu_residual(x, g_ref, w1_ref, w3_ref, w2_ref, fg_ref, final_norm):
    ms = jnp.mean(x * x, axis=-1, keepdims=True)
    hn = (x * lax.rsqrt(ms + NORM_EPS) * g_ref[...]).astype(BF16)
    a = _dot(hn, w1_ref[...])
    b = _dot(hn, w3_ref[...])
    y = x + _dot((a * jax.nn.sigmoid(a) * b).astype(BF16), w2_ref[...])
    if final_norm:
        ms = jnp.mean(y * y, axis=-1, keepdims=True)
        y = y * lax.rsqrt(ms + NORM_EPS) * fg_ref[...]
    return y


def _ffn_operands(layer, g, w1, w3, w2, final_g):
    _, d, f = w1.shape
    pick = (layer, 0, 0)
    specs = [_resident((1, d)), _resident((None, d, f), pick), _resident((None, d, f), pick),
             _resident((None, f, d), pick), _resident((1, d))]
    return specs, (g.reshape(1, d), w1.astype(BF16), w3.astype(BF16), w2.astype(BF16), final_g.reshape(1, d))


def _even_block_kernel(h_ref, c_ref, of_ref, ob_ref, gate_ref, lng_ref, lnb_ref, ng_ref, wa_ref, wb_ref,
                       g_ref, w1_ref, w3_ref, w2_ref, fg_ref, o_ref, *, final_norm):
    ya = _layer_norm(c_ref[...].astype(F32), lng_ref[...], lnb_ref[...])
    ya = ya * jax.nn.sigmoid(ya)
    o = _head_rms_norm(of_ref[...].astype(F32) + ob_ref[...].astype(F32), ng_ref[...], DN_HEADS, DN_HEAD_DIM)
    gate = gate_ref[...].astype(F32)
    o = o * (gate * jax.nn.sigmoid(gate))
    x = h_ref[...] + _dot(ya.astype(BF16), wa_ref[...]) + _dot(o.astype(BF16), wb_ref[...])
    o_ref[...] = _swiglu_residual(x, g_ref, w1_ref, w3_ref, w2_ref, fg_ref, final_norm)


def _even_block(h, conv, o_fwd, o_bwd, z, ln_g, ln_b, norm_g, w_out, ffn, *, gate_col, final_norm):
    n, d = h.shape
    tm = min(ROW_TILE, n)
    gb = gate_col // DN_WIDTH
    row = lambda i: (i, 0)
    ffn_specs, ffn_args = _ffn_operands(*ffn)
    return pl.pallas_call(
        functools.partial(_even_block_kernel, final_norm=final_norm),
        grid=(n // tm,),
        in_specs=[
            pl.BlockSpec((tm, d), row),
            pl.BlockSpec((tm, CONV_CH), row),
            pl.BlockSpec((tm, DN_WIDTH), row),
            pl.BlockSpec((tm, DN_WIDTH), row),
            pl.BlockSpec((tm, DN_WIDTH), lambda i: (i, gb)),
            _resident((1, CONV_CH)),
            _resident((1, CONV_CH)),
            _resident((1, DN_HEAD_DIM)),
            _resident((CONV_CH, d)),
            _resident((DN_WIDTH, d), (CONV_CH // DN_WIDTH, 0)),
        ] + ffn_specs,
        out_specs=pl.BlockSpec((tm, d), row),
        out_shape=jax.ShapeDtypeStruct((n, d), F32),
        compiler_params=_params("parallel"),
        name="even_block",
    )(h, conv, o_fwd, o_bwd, z, ln_g.reshape(1, -1), ln_b.reshape(1, -1), norm_g.reshape(1, -1),
      w_out.astype(BF16), w_out.astype(BF16), *ffn_args)


def _odd_block_kernel(h_ref, u_ref, vp_ref, hf_ref, hb_ref, op_ref, lng_ref, lnb_ref, sgw_ref, sgb_ref, ng_ref,
                      wa_ref, wb_ref, g_ref, w1_ref, w3_ref, w2_ref, fg_ref, o_ref, *, final_norm):
    tm = h_ref.shape[0]
    u = jax.nn.gelu(u_ref[...].astype(F32))
    vv = _layer_norm(jax.nn.gelu(vp_ref[...].astype(F32)), lng_ref[...], lnb_ref[...]).astype(BF16)
    sgb = sgb_ref[...]
    rows = []
    for c in range(tm // SG_CHUNK):
        parts = []
        for g in range(SG_GROUPS):
            blk = vv[c * SG_CHUNK:(c + 1) * SG_CHUNK, g * SG_GROUP_DIM:(g + 1) * SG_GROUP_DIM]
            parts.append(_dot(sgw_ref[g], blk) + sgb[:, g:g + 1])
        rows.append(jnp.concatenate(parts, axis=-1))
    yc = u * jnp.concatenate(rows, axis=0)
    hd = _head_rms_norm(hf_ref[...].astype(F32) + hb_ref[...].astype(F32), ng_ref[...], ML_HEADS, ML_V_DIM)
    hd = hd * jax.nn.sigmoid(op_ref[...].astype(F32))
    x = h_ref[...] + _dot(yc.astype(BF16), wa_ref[...]) + _dot(hd.astype(BF16), wb_ref[...])
    o_ref[...] = _swiglu_residual(x, g_ref, w1_ref, w3_ref, w2_ref, fg_ref, final_norm)


def _odd_block(h, z, h_fwd, h_bwd, ln_g, ln_b, sg_w, sg_b, norm_g, w_out, ffn, *, u_col, v_col, o_col, final_norm):
    n, d = h.shape
    tm = min(ROW_TILE, n)
    row = lambda i: (i, 0)
    ffn_specs, ffn_args = _ffn_operands(*ffn)
    return pl.pallas_call(
        functools.partial(_odd_block_kernel, final_norm=final_norm),
        grid=(n // tm,),
        in_specs=[
            pl.BlockSpec((tm, d), row),
            pl.BlockSpec((tm, SG_WIDTH), lambda i: (i, u_col // SG_WIDTH)),
            pl.BlockSpec((tm, SG_WIDTH), lambda i: (i, v_col // SG_WIDTH)),
            pl.BlockSpec((tm, ML_V_WIDTH), row),
            pl.BlockSpec((tm, ML_V_WIDTH), row),
            pl.BlockSpec((tm, ML_V_WIDTH), lambda i: (i, o_col // ML_V_WIDTH)),
            _resident((1, SG_WIDTH)),
            _resident((1, SG_WIDTH)),
            _resident((SG_GROUPS, SG_CHUNK, SG_CHUNK)),
            _resident((SG_CHUNK, SG_GROUPS)),
            _resident((1, ML_V_DIM)),
            _resident((SG_WIDTH, d)),
            _resident((ML_V_WIDTH, d), (SG_WIDTH // ML_V_WIDTH, 0)),
        ] + ffn_specs,
        out_specs=pl.BlockSpec((tm, d), row),
        out_shape=jax.ShapeDtypeStruct((n, d), F32),
        compiler_params=_params("parallel"),
        name="odd_block",
    )(h, z, z, h_fwd, h_bwd, z, ln_g.reshape(1, -1), ln_b.reshape(1, -1), sg_w.astype(BF16), sg_b.T,
      norm_g.reshape(1, -1), w_out.astype(BF16), w_out.astype(BF16), *ffn_args)


def _gate_weight(w_gate_cols):
    d, c = w_gate_cols.shape
    return jnp.zeros((d, LANES), F32).at[:, :c].set(w_gate_cols).astype(BF16)


def _even_layer(h, j, p, ffn, *, batch, seq, final_norm):
    main = 2 * CONV_CH + 4 * DN_WIDTH
    w_in = p["ev_w_in"][j]
    gate_params = jnp.zeros((SUBLANES, LANES), F32)
    gate_params = gate_params.at[0, 8:16].set(p["ev_dn_a_log"][j].reshape(-1))
    gate_params = gate_params.at[1, 8:16].set(p["ev_dn_dt_bias"][j].reshape(-1))
    z, zg, gr = _in_proj(h, p["mix_norm_g"][2 * j], w_in.astype(BF16), main, _gate_weight(w_in[:, main:]),
                         gate_params, even=True)
    conv = _conv_glu(z, p["ev_conv_w"][j], p["ev_conv_b"][j], batch=batch, seq=seq)
    qkv = _conv_qkv(z, p["ev_dn_conv_w"][j], batch=batch, seq=seq, col0=2 * CONV_CH)
    o_fwd, o_bwd = _delta_scan(qkv, zg, gr, batch=batch, seq=seq)
    return _even_block(h, conv, o_fwd, o_bwd, z, p["ev_conv_ln_g"][j], p["ev_conv_ln_b"][j], p["ev_dn_norm_g"][j],
                       p["ev_w_out"][j], ffn, gate_col=2 * CONV_CH + 3 * DN_WIDTH, final_norm=final_norm)


def _odd_layer(h, j, p, ffn, *, batch, seq, final_norm):
    main = 2 * SG_WIDTH + 2 * ML_QK_WIDTH + 2 * ML_V_WIDTH
    w_in = p["od_w_in"][j]
    gate_params = jnp.zeros((SUBLANES, LANES), F32)
    gate_params = gate_params.at[0, 0:8].set(p["od_ml_i_bias"][j].reshape(-1))
    gate_params = gate_params.at[1, 8:16].set(p["od_ml_f_bias"][j].reshape(-1))
    z, zg, gr = _in_proj(h, p["mix_norm_g"][2 * j + 1], w_in.astype(BF16), main, _gate_weight(w_in[:, main:]),
                         gate_params, even=False)
    q_col = 2 * SG_WIDTH
    k_col = q_col + ML_QK_WIDTH
    v_col = k_col + ML_QK_WIDTH
    o_col = v_col + ML_V_WIDTH
    h_fwd, h_bwd = _mlstm_scan(z, zg, gr, batch=batch, seq=seq, q_col=q_col, k_col=k_col, v_col=v_col)
    return _odd_block(h, z, h_fwd, h_bwd, p["od_sg_ln_g"][j], p["od_sg_ln_b"][j], p["od_sg_w"][j], p["od_sg_b"][j],
                      p["od_ml_norm_g"][j], p["od_w_out"][j], ffn, u_col=0, v_col=SG_WIDTH, o_col=o_col,
                      final_norm=final_norm)


def kernel(x, mix_norm_g, ev_w_in, ev_conv_w, ev_conv_b, ev_conv_ln_g, ev_conv_ln_b, ev_dn_conv_w, ev_dn_a_log, ev_dn_dt_bias, ev_dn_norm_g, ev_w_out, od_w_in, od_sg_ln_g, od_sg_ln_b, od_sg_w, od_sg_b, od_ml_i_bias, od_ml_f_bias, od_ml_norm_g, od_w_out, ffn_norm_g, ffn_w1, ffn_w3, ffn_w2, final_norm_g):
    p = dict(mix_norm_g=mix_norm_g, ev_w_in=ev_w_in, ev_conv_w=ev_conv_w, ev_conv_b=ev_conv_b,
             ev_conv_ln_g=ev_conv_ln_g, ev_conv_ln_b=ev_conv_ln_b, ev_dn_conv_w=ev_dn_conv_w,
             ev_dn_a_log=ev_dn_a_log, ev_dn_dt_bias=ev_dn_dt_bias, ev_dn_norm_g=ev_dn_norm_g, ev_w_out=ev_w_out,
             od_w_in=od_w_in, od_sg_ln_g=od_sg_ln_g, od_sg_ln_b=od_sg_ln_b, od_sg_w=od_sg_w, od_sg_b=od_sg_b,
             od_ml_i_bias=od_ml_i_bias, od_ml_f_bias=od_ml_f_bias, od_ml_norm_g=od_ml_norm_g, od_w_out=od_w_out)
    batch, seq, d = x.shape
    depth = mix_norm_g.shape[0]
    h = x.reshape(batch * seq, d)
    for layer in range(depth):
        ffn = (layer, ffn_norm_g[layer], ffn_w1, ffn_w3, ffn_w2, final_norm_g)
        mixer_layer = _even_layer if layer % 2 == 0 else _odd_layer
        h = mixer_layer(h, layer // 2, p, ffn, batch=batch, seq=seq, final_norm=layer == depth - 1)
    return h.reshape(batch, seq, d)
```

```python
import functools

import jax
import jax.numpy as jnp
from jax import lax
from jax.experimental import pallas as pl
from jax.experimental.pallas import tpu as pltpu

NORM_EPS = 1e-6
LN_EPS = 1e-5
N_DIR = 2

CONV_CH = 512
DN_HEADS = 4
DN_HEAD_DIM = 128
DN_WIDTH = DN_HEADS * DN_HEAD_DIM
SG_GROUPS = 4
SG_GROUP_DIM = 128
SG_WIDTH = SG_GROUPS * SG_GROUP_DIM
SG_CHUNK = 128
ML_HEADS = 4
ML_QK_DIM = 64
ML_V_DIM = 128
ML_QK_WIDTH = ML_HEADS * ML_QK_DIM
ML_V_WIDTH = ML_HEADS * ML_V_DIM

LANES = 128
SUBLANES = 8
VMEM_LIMIT_BYTES = 56 * 1024 * 1024

SCAN_CHUNK = 64
SCAN_STEP_CHUNKS = 4
PREP_GROUP = 128
PREP_ROWS = 512
DN_PREP_STEP_GROUPS = 2
ML_PREP_STEP_GROUPS = 4
ROW_TILE = 512
IN_ROW_TILE = 1024
CONV_TILE = 128
QKV_CONV_TILES = 16
GLU_CONV_TILES = 4
CONV_PAD = 16
NEG_BIG = -1e30

BF16 = jnp.bfloat16
F32 = jnp.float32


def _params(*sem):
    return pltpu.CompilerParams(dimension_semantics=sem, vmem_limit_bytes=VMEM_LIMIT_BYTES)


def _resident(shape, block_index=None):
    index = (0,) * len(shape) if block_index is None else tuple(block_index)
    return pl.BlockSpec(shape, lambda *_: index, pipeline_mode=pl.Buffered(1))


def _dot(a, b):
    return jnp.dot(a, b, preferred_element_type=F32)


def _dot_nt(a, b):
    return lax.dot_general(a, b, (((1,), (1,)), ((), ())), preferred_element_type=F32)


def _dot_tn(a, b):
    return lax.dot_general(a, b, (((0,), (0,)), ((), ())), preferred_element_type=F32)


def _exact_dot01(t01, x):
    x1 = x.astype(BF16)
    r1 = x - x1.astype(F32)
    x2 = r1.astype(BF16)
    x3 = (r1 - x2.astype(F32)).astype(BF16)
    return _dot(t01, x1) + _dot(t01, x2) + _dot(t01, x3)


def _exact_dot01_r(x, t01):
    x1 = x.astype(BF16)
    r1 = x - x1.astype(F32)
    x2 = r1.astype(BF16)
    x3 = (r1 - x2.astype(F32)).astype(BF16)
    return _dot(x1, t01) + _dot(x2, t01) + _dot(x3, t01)


def _in_proj_kernel(h_ref, g_ref, w_ref, wg_ref, gp_ref, zm_ref, zg_ref, zgt_ref, *, even):
    x = h_ref[...]
    ms = jnp.mean(x * x, axis=-1, keepdims=True)
    hn = (x * lax.rsqrt(ms + NORM_EPS) * g_ref[...]).astype(BF16)
    zm_ref[...] = _dot(hn, w_ref[...]).astype(zm_ref.dtype)
    zg = _dot(hn, wg_ref[...])
    p0 = gp_ref[0:1, :]
    p1 = gp_ref[1:2, :]
    lane = lax.broadcasted_iota(jnp.int32, zg.shape, 1)
    if even:
        first = jax.nn.sigmoid(zg)
        second = -jnp.exp(p0) * jax.nn.softplus(zg + p1)
    else:
        first = zg + p0
        second = jax.nn.log_sigmoid(zg + p1)
    gates = jnp.where(lane < 8, first, second)
    zg_ref[...] = gates
    for s in range(gates.shape[0] // PREP_GROUP):
        zgt_ref[s] = gates[s * PREP_GROUP:(s + 1) * PREP_GROUP, :].T[:2 * SUBLANES, :]


def _in_proj(h, g, w_in, c, w_gate, gate_params, *, even):
    n, d = h.shape
    assert c % LANES == 0
    tm = min(IN_ROW_TILE, n)
    return pl.pallas_call(
        functools.partial(_in_proj_kernel, even=even),
        grid=(n // tm,),
        in_specs=[
            pl.BlockSpec((tm, d), lambda i: (i, 0)),
            _resident((1, d)),
            _resident((d, c)),
            _resident((d, LANES)),
            _resident((SUBLANES, LANES)),
        ],
        out_specs=[
            pl.BlockSpec((tm, c), lambda i: (i, 0)),
            pl.BlockSpec((tm, LANES), lambda i: (i, 0)),
            pl.BlockSpec((tm // PREP_GROUP, 2 * SUBLANES, PREP_GROUP), lambda i: (i, 0, 0)),
        ],
        out_shape=[jax.ShapeDtypeStruct((n, c), BF16), jax.ShapeDtypeStruct((n, LANES), F32),
                   jax.ShapeDtypeStruct((n // PREP_GROUP, 2 * SUBLANES, PREP_GROUP), F32)],
        compiler_params=_params("parallel"),
        name="in_proj_even" if even else "in_proj_odd",
    )(h, g.reshape(1, d), w_in, w_gate, gate_params)


def _conv_taps(pad_ref, w_ref, o_ref, bias, *, seq, width, post, tiles_per_step=1):
    half = width // 2
    tiles_per_step = min(tiles_per_step, seq // CONV_TILE)
    step_rows = CONV_TILE * tiles_per_step

    def body(i, carry):
        t0 = pl.multiple_of(i * step_rows, step_rows)
        accs = []
        for s in range(tiles_per_step):
            acc = jnp.zeros((CONV_TILE, LANES), F32)
            for j in range(width):
                acc = acc + pad_ref[pl.ds(t0 + (s * CONV_TILE + CONV_PAD - half + j), CONV_TILE), :] * w_ref[j:j + 1, :]
            accs.append(acc if bias is None else acc + bias)
        for s, out in enumerate(post(accs)):
            o_ref[pl.ds(t0 + s * CONV_TILE, CONV_TILE), :] = out.astype(o_ref.dtype)
        return carry

    lax.fori_loop(0, seq // step_rows, body, 0)


def _fill_padded(pad_ref, x, seq):
    zeros = jnp.zeros((CONV_PAD, LANES), F32)
    pad_ref[0:CONV_PAD, :] = zeros
    pad_ref[CONV_PAD + seq:CONV_PAD + seq + CONV_PAD, :] = zeros
    pad_ref[CONV_PAD:CONV_PAD + seq, :] = x


def _conv_glu_kernel(av_ref, ag_ref, w_ref, b_ref, o_ref, pad_ref, *, seq, width):
    _fill_padded(pad_ref, av_ref[...].astype(F32) * jax.nn.sigmoid(ag_ref[...].astype(F32)), seq)
    _conv_taps(pad_ref, w_ref, o_ref, b_ref[...], seq=seq, width=width, post=lambda tiles: tiles,
               tiles_per_step=GLU_CONV_TILES)


def _conv_glu(z, conv_w, conv_b, *, batch, seq):
    width = conv_w.shape[0]
    nblk = CONV_CH // LANES
    return pl.pallas_call(
        functools.partial(_conv_glu_kernel, seq=seq, width=width),
        grid=(batch, nblk),
        in_specs=[
            pl.BlockSpec((seq, LANES), lambda b, c: (b, c)),
            pl.BlockSpec((seq, LANES), lambda b, c: (b, nblk + c)),
            pl.BlockSpec((width, LANES), lambda b, c: (0, c)),
            pl.BlockSpec((1, LANES), lambda b, c: (0, c)),
        ],
        out_specs=pl.BlockSpec((seq, LANES), lambda b, c: (b, c)),
        out_shape=jax.ShapeDtypeStruct((batch * seq, CONV_CH), BF16),
        scratch_shapes=[pltpu.VMEM((seq + 2 * CONV_PAD, LANES), F32)],
        compiler_params=_params("parallel", "parallel"),
        name="conv_glu",
    )(z, z, conv_w, conv_b.reshape(1, CONV_CH))


def _conv_qkv_kernel(x_ref, w_ref, o_ref, pad_ref, *, seq, width, n_norm_blocks):
    _fill_padded(pad_ref, x_ref[...].astype(F32), seq)
    normalise = pl.program_id(1) < n_norm_blocks

    def post(tiles):
        ys = [a * jax.nn.sigmoid(a) for a in tiles]
        sq = [jnp.sum(y * y, axis=-1, keepdims=True) for y in ys]
        inv = [lax.rsqrt(s + NORM_EPS) for s in sq]
        return [jnp.where(normalise, y * r, y) for y, r in zip(ys, inv)]

    _conv_taps(pad_ref, w_ref, o_ref, None, seq=seq, width=width, post=post, tiles_per_step=QKV_CONV_TILES)


def _conv_qkv(z, dn_conv_w, *, batch, seq, col0):
    width = dn_conv_w.shape[0]
    nblk = 3 * DN_WIDTH // LANES
    blk0 = col0 // LANES
    return pl.pallas_call(
        functools.partial(_conv_qkv_kernel, seq=seq, width=width, n_norm_blocks=2 * DN_HEADS),
        grid=(batch, nblk),
        in_specs=[
            pl.BlockSpec((seq, LANES), lambda b, c: (b, blk0 + c)),
            pl.BlockSpec((width, LANES), lambda b, c: (0, c)),
        ],
        out_specs=pl.BlockSpec((seq, LANES), lambda b, c: (b, c)),
        out_shape=jax.ShapeDtypeStruct((batch * seq, 3 * DN_WIDTH), BF16),
        scratch_shapes=[pltpu.VMEM((seq + 2 * CONV_PAD, LANES), F32)],
        compiler_params=_params("parallel", "parallel"),
        name="conv_qkv",
    )(z, dn_conv_w)


def _split_hi_lo(x):
    hi = x.astype(BF16)
    return jnp.concatenate([hi, (x - hi.astype(F32)).astype(BF16)], axis=-1)


def _bf16_all(mats):
    return [m.astype(BF16) for m in mats]


def _unit_tri_inverses(mats, row, col, n):
    eye = (row == col).astype(F32)
    same = (row >> 3) == (col >> 3)
    l8 = [jnp.where(same, a, 0.0) for a in mats]
    l8h = _bf16_all(l8)
    l2h = _bf16_all([_dot(p, p) for p in l8h])
    l4h = _bf16_all([_dot(p, p) for p in l2h])
    x = [eye - p for p in l8]
    x = [xi + _dot(xi.astype(BF16), p) for xi, p in zip(x, l2h)]
    x = [xi + _dot(xi.astype(BF16), p) for xi, p in zip(x, l4h)]
    shift = 3
    while (1 << shift) < n:
        same_next = (row >> (shift + 1)) == (col >> (shift + 1))
        off = jnp.logical_and(same_next, jnp.logical_not(same))
        ch = _bf16_all([jnp.where(off, a, 0.0) for a in mats])
        xh = _bf16_all(x)
        xch = _bf16_all([_dot(p, c) for p, c in zip(xh, ch)])
        x = [xi - _dot(p, q) for xi, p, q in zip(x, xch, xh)]
        same = same_next
        shift += 1
    return x


def _group_masks(backward):
    n = PREP_GROUP
    row = lax.broadcasted_iota(jnp.int32, (n, n), 0)
    col = lax.broadcasted_iota(jnp.int32, (n, n), 1)
    shift = SCAN_CHUNK.bit_length() - 1
    same = (row >> shift) == (col >> shift)
    d = (col - row) if backward else (row - col)
    land = jnp.logical_and
    return land(same, d >= 0), land(same, d > 0), land(same, d <= 0)


def _as01(mask):
    return jnp.where(mask, 1.0, 0.0).astype(BF16)


def _delta_prep_kernel(q_ref, k_ref, v_ref, gc_ref, gr_ref, u_ref, w_ref, qg_ref, kg_ref, at_ref):
    G, L = PREP_GROUP, SCAN_CHUNK
    row = lax.broadcasted_iota(jnp.int32, (G, G), 0)
    col = lax.broadcasted_iota(jnp.int32, (G, G), 1)
    shift = L.bit_length() - 1
    same01 = _as01((row >> shift) == (col >> shift))
    masks = [_group_masks(False), _group_masks(True)]
    cum_c = [_as01(m[0]) for m in masks]
    cum_r = [_as01(m[2]) for m in masks]
    fold = _as01((lax.broadcasted_iota(jnp.int32, (G, L), 0) & (L - 1)) == lax.broadcasted_iota(jnp.int32, (G, L), 1))
    pad = jnp.zeros((G, DN_HEAD_DIM - L), BF16)
    scale = DN_HEAD_DIM ** -0.5

    step_groups = min(DN_PREP_STEP_GROUPS, q_ref.shape[0] // G)

    def step(i, carry):
        groups = range(step_groups)
        rows = [pl.ds(pl.multiple_of((i * step_groups + g) * G, G), G) for g in groups]
        gc = [gc_ref[r, :] for r in rows]
        gr = [gr_ref[i * step_groups + g] for g in groups]
        g_tot = [_exact_dot01(same01, x) for x in gc]
        g_cum_c = [[_exact_dot01(cum_c[d], x) for d in range(N_DIR)] for x in gc]
        g_cum_r = [[_exact_dot01_r(x, cum_r[d]) for d in range(N_DIR)] for x in gr]
        heads = range(DN_HEADS)
        lanes = [slice(h * DN_HEAD_DIM, (h + 1) * DN_HEAD_DIM) for h in heads]
        gh = [(g, h) for g in groups for h in heads]
        chains = [(g, h, d) for g, h in gh for d in range(N_DIR)]
        k16 = {(g, h): k_ref[rows[g], lanes[h]] for g, h in gh}
        q = {(g, h): q_ref[rows[g], lanes[h]].astype(F32) for g, h in gh}
        k = {c: k16[c].astype(F32) for c in gh}
        kk = {c: _dot_nt(k16[c], k16[c]) for c in gh}
        qk = {c: _dot_nt((q[c] * scale).astype(BF16), k16[c]) for c in gh}
        beta, g_c, g_t, decay = {}, {}, {}, {}
        for c in chains:
            g, h, d = c
            cb = d * DN_HEADS + h
            cl = N_DIR * DN_HEADS + cb
            beta[c] = gc[g][:, cb:cb + 1]
            g_c[c] = g_cum_c[g][d][:, cl:cl + 1]
            g_t[c] = g_tot[g][:, cl:cl + 1]
            g_r = g_cum_r[g][d][cl:cl + 1, :]
            decay[c] = jnp.exp(jnp.where(masks[d][0], g_c[c] - g_r, NEG_BIG))
        a = [jnp.where(masks[c[2]][1], kk[c[:2]] * decay[c], 0.0) * beta[c] for c in chains]
        x = _unit_tri_inverses(a, row, col, L)
        eg = {c: jnp.exp(g_c[c]) for c in chains}
        rhs = [jnp.concatenate([v_ref[rows[c[0]], lanes[c[1]]].astype(F32) * beta[c], k[c[:2]] * (beta[c] * eg[c])],
                               axis=-1).astype(BF16) for c in chains]
        uw = [_dot(xi.astype(BF16), r) for xi, r in zip(x, rhs)]
        attn = [jnp.where(masks[c[2]][0], qk[c[:2]] * decay[c], 0.0).astype(BF16) for c in chains]
        at = [_dot(p, fold) for p in attn]
        for c, uwi, ati in zip(chains, uw, at):
            g, h, d = c
            r, sl = rows[g], lanes[h]
            u_ref[d, r, sl] = uwi[:, :DN_HEAD_DIM].astype(BF16)
            w_ref[d, r, sl] = uwi[:, DN_HEAD_DIM:].astype(BF16)
            qg_ref[d, r, sl] = (q[g, h] * (scale * eg[c])).astype(BF16)
            kg_ref[d, r, sl] = (k[g, h] * jnp.exp(g_t[c] - g_c[c])).astype(BF16)
            at_ref[d, r, sl] = jnp.concatenate([ati.astype(BF16), pad], axis=-1)
        return carry

    lax.fori_loop(0, q_ref.shape[0] // (G * step_groups), step, 0)


def _delta_prep(qkv, zg, gr, *, batch, seq):
    n = batch * seq
    R, G = min(PREP_ROWS, seq), PREP_GROUP
    steps = seq // R
    rowblk = lambda c: (lambda b, r: (b * steps + r, c))
    out = jax.ShapeDtypeStruct((N_DIR, n, DN_WIDTH), BF16)
    out_spec = pl.BlockSpec((N_DIR, R, DN_WIDTH), lambda b, r: (0, b * steps + r, 0))
    return pl.pallas_call(
        _delta_prep_kernel,
        grid=(batch, steps),
        in_specs=[
            pl.BlockSpec((R, DN_WIDTH), rowblk(0)),
            pl.BlockSpec((R, DN_WIDTH), rowblk(1)),
            pl.BlockSpec((R, DN_WIDTH), rowblk(2)),
            pl.BlockSpec((R, LANES), rowblk(0)),
            pl.BlockSpec((R // G, 16, G), lambda b, r: (b * steps + r, 0, 0)),
        ],
        out_specs=[out_spec] * 5,
        out_shape=[out] * 5,
        compiler_params=_params("parallel", "parallel"),
        name="delta_prep",
    )(qkv, qkv, qkv, zg, gr)


def _delta_scan_kernel(uf, wf, qgf, kgf, atf, zgf, ub, wb, qgb, kgb, atb, zgb, of_ref, ob_ref, s_ref, *, batch):
    L, C = SCAN_CHUNK, SCAN_STEP_CHUNKS

    @pl.when(pl.program_id(0) == 0)
    def _():
        s_ref[...] = jnp.zeros_like(s_ref)

    dirs = ((uf, wf, qgf, kgf, atf, zgf, of_ref), (ub, wb, qgb, kgb, atb, zgb, ob_ref))

    chains = [(d, b, h) for d in range(N_DIR) for b in range(batch) for h in range(DN_HEADS)]
    state = {c: s_ref[c] for c in chains}
    for step in range(C):
        chunk = {0: step, 1: C - 1 - step}
        rows = {d: slice(chunk[d] * L, (chunk[d] + 1) * L) for d in range(N_DIR)}
        egt = {(d, b): jnp.exp(jnp.sum(dirs[d][5][b, rows[d], :], axis=0, keepdims=True))
               for d in range(N_DIR) for b in range(batch)}
        ws, v_new, av = {}, {}, {}
        for c in chains:
            d, b, h = c
            sl = slice(h * DN_HEAD_DIM, (h + 1) * DN_HEAD_DIM)
            wq = jnp.concatenate([dirs[d][1][b, rows[d], sl], dirs[d][2][b, rows[d], sl]], axis=0)
            ws[c] = _dot(wq, state[c].astype(BF16))
        for c in chains:
            d, b, h = c
            sl = slice(h * DN_HEAD_DIM, (h + 1) * DN_HEAD_DIM)
            v_new[c] = (dirs[d][0][b, rows[d], sl].astype(F32) - ws[c][:L]).astype(BF16)
        for c in chains:
            d, b, h = c
            at = dirs[d][4][b, rows[d], h * DN_HEAD_DIM:h * DN_HEAD_DIM + L]
            av[c] = _dot(at, v_new[c])
        for c in chains:
            d, b, h = c
            sl = slice(h * DN_HEAD_DIM, (h + 1) * DN_HEAD_DIM)
            cl = (N_DIR + d) * DN_HEADS + h
            dirs[d][6][b, rows[d], sl] = (ws[c][L:] + av[c]).astype(BF16)
            state[c] = state[c] * egt[d, b][:, cl:cl + 1] + _dot_tn(dirs[d][3][b, rows[d], sl], v_new[c])
    for c in chains:
        s_ref[c] = state[c]


def _delta_scan(qkv, zg, gr, *, batch, seq):
    n = batch * seq
    T = SCAN_CHUNK * SCAN_STEP_CHUNKS
    steps = seq // T
    prepped = [t.reshape(N_DIR, batch, seq, DN_WIDTH) for t in _delta_prep(qkv, zg, gr, batch=batch, seq=seq)]
    zg3 = zg.reshape(batch, seq, LANES)
    fwd = lambda s: s
    bwd = lambda s: steps - 1 - s
    specs = []
    for d, pos in ((0, fwd), (1, bwd)):
        specs += [pl.BlockSpec((None, batch, T, DN_WIDTH), lambda s, d=d, pos=pos: (d, 0, pos(s), 0))] * 5
        specs += [pl.BlockSpec((batch, T, LANES), lambda s, pos=pos: (0, pos(s), 0))]
    out = jax.ShapeDtypeStruct((batch, seq, DN_WIDTH), BF16)
    o_f, o_b = pl.pallas_call(
        functools.partial(_delta_scan_kernel, batch=batch),
        grid=(steps,),
        in_specs=specs,
        out_specs=[pl.BlockSpec((batch, T, DN_WIDTH), lambda s: (0, fwd(s), 0)),
                   pl.BlockSpec((batch, T, DN_WIDTH), lambda s: (0, bwd(s), 0))],
        out_shape=[out, out],
        scratch_shapes=[pltpu.VMEM((N_DIR, batch, DN_HEADS, DN_HEAD_DIM, DN_HEAD_DIM), F32)],
        compiler_params=_params("arbitrary"),
        name="delta_scan",
    )(*prepped, zg3, *prepped, zg3)
    return o_f.reshape(n, DN_WIDTH), o_b.reshape(n, DN_WIDTH)


ML_PAIRS = ML_HEADS // 2
ML_AUG = 2 * ML_V_DIM
ML_KV_ROWS = ML_PAIRS * 2 * ML_QK_DIM


def _lane_cols(cols, width):
    rows = cols[0].shape[0]
    lane = lax.broadcasted_iota(jnp.int32, (rows, width), 1)
    out = jnp.zeros((rows, width), F32)
    for j, c in enumerate(cols):
        out = jnp.where(lane == j, c, out)
    return out


def _mlstm_prep_kernel(q_ref, k_ref, v_ref, gc_ref, gr_ref, qs_ref, iv_ref, kv_ref, aux_ref):
    G, L = PREP_GROUP, SCAN_CHUNK
    row = lax.broadcasted_iota(jnp.int32, (G, G), 0)
    col = lax.broadcasted_iota(jnp.int32, (G, G), 1)
    shift = L.bit_length() - 1
    same01 = _as01((row >> shift) == (col >> shift))
    masks = [_group_masks(False), _group_masks(True)]
    cum_c = [_as01(m[0]) for m in masks]
    cum_r = [_as01(m[2]) for m in masks]
    lane = lax.broadcasted_iota(jnp.int32, (G, LANES), 1)
    first_head = lane < ML_QK_DIM
    lane_r = lax.broadcasted_iota(jnp.int32, (1, G), 1)
    row_c = lax.broadcasted_iota(jnp.int32, (G, 1), 0)
    sel_row = lax.broadcasted_iota(jnp.int32, (2 * LANES, G), 0) & (LANES - 1)
    gate_lane = lambda h, d: N_DIR * ML_HEADS + d * ML_HEADS + h
    pick_pair = {(p, d): jnp.concatenate([_as01(sel_row == gate_lane(2 * p, d)),
                                          _as01(sel_row == gate_lane(2 * p + 1, d))], axis=-1)
                 for p in range(ML_PAIRS) for d in range(N_DIR)}
    ones_v = jnp.ones((G, ML_V_DIM), BF16)
    scale = ML_QK_DIM ** -0.5
    heads = range(ML_HEADS)
    chains = [(h, d) for h in heads for d in range(N_DIR)]

    step_groups = min(ML_PREP_STEP_GROUPS, q_ref.shape[0] // G)

    def step(i, carry):
        groups = range(step_groups)
        gidx = [i * step_groups + g for g in groups]
        rows = [pl.ds(pl.multiple_of(x * G, G), G) for x in gidx]
        gc = [gc_ref[r, :] for r in rows]
        gr = [gr_ref[x] for x in gidx]
        bt_c = [_exact_dot01(same01, x) for x in gc]
        bt_r = [_exact_dot01_r(x, same01) for x in gr]
        bc_c = [[_exact_dot01(cum_c[d], x) for d in range(N_DIR)] for x in gc]
        bc_r = [[_exact_dot01_r(x, cum_r[d]) for d in range(N_DIR)] for x in gr]
        gps = [(g, p) for g in groups for p in range(ML_PAIRS)]
        ghs = [(g, h) for g in groups for h in heads]
        chains = [(g, h, d) for g, h in ghs for d in range(N_DIR)]
        qp = {(g, p): q_ref[rows[g], p * LANES:(p + 1) * LANES].astype(F32) * scale for g, p in gps}
        kp16 = {(g, p): k_ref[rows[g], p * LANES:(p + 1) * LANES] for g, p in gps}
        kp = {c: kp16[c].astype(F32) for c in gps}
        v16 = {(g, h): v_ref[rows[g], h * ML_V_DIM:(h + 1) * ML_V_DIM].astype(BF16) for g, h in ghs}
        for g, p in gps:
            qs_ref[rows[g], p * LANES:(p + 1) * LANES] = qp[g, p].astype(BF16)
        own = [first_head, jnp.logical_not(first_head)]
        scores = {(g, h): _dot_nt(jnp.where(own[h % 2], qp[g, h // 2], 0.0).astype(BF16), kp16[g, h // 2])
                  for g, h in ghs}
        col = lambda c: N_DIR * ML_HEADS + c[2] * ML_HEADS + c[1]
        b_c = {c: bc_c[c[0]][c[2]][:, col(c):col(c) + 1] for c in chains}
        b_t = {c: bt_c[c[0]][:, col(c):col(c) + 1] for c in chains}
        b_r = {c: bc_r[c[0]][c[2]][col(c):col(c) + 1, :] for c in chains}
        li_c = {c: gc[c[0]][:, col(c) - N_DIR * ML_HEADS:col(c) - N_DIR * ML_HEADS + 1] for c in chains}
        li_r = {c: gr[c[0]][col(c) - N_DIR * ML_HEADS:col(c) - N_DIR * ML_HEADS + 1, :] for c in chains}
        split = {(g, d): _split_hi_lo(bc_c[g][d]) for g in groups for d in range(N_DIR)}
        b_cb = {}
        for g, p in gps:
            for d in range(N_DIR):
                both = _dot(split[g, d], pick_pair[p, d])
                b_cb[g, 2 * p, d], b_cb[g, 2 * p + 1, d] = both[:, :G], both[:, G:]
        d_mat = {c: jnp.where(masks[c[2]][0], b_cb[c] - b_r[c] + li_r[c], NEG_BIG) for c in chains}
        d_max = {c: jnp.max(d_mat[c], axis=-1, keepdims=True) for c in chains}
        pmat = {c: (jnp.exp(d_mat[c] - d_max[c]) * scores[c[:2]]).astype(BF16) for c in chains}
        w_end_r = {c: bt_r[c[0]][col(c):col(c) + 1, :] - b_r[c] + li_r[c] for c in chains}
        w_max = {}
        for c in chains:
            wm = jnp.full((G, 1), NEG_BIG, F32)
            for j in range(G // L):
                in_j = jnp.logical_and(lane_r >= j * L, lane_r < (j + 1) * L)
                wm_j = jnp.max(jnp.where(in_j, w_end_r[c], NEG_BIG), axis=-1, keepdims=True)
                wm = jnp.where(jnp.logical_and(row_c >= j * L, row_c < (j + 1) * L), wm_j, wm)
            w_max[c] = wm
        sw0 = {c: jnp.exp(b_t[c] - b_c[c] + li_c[c] - w_max[c]) for c in chains}
        iv = {c: _dot(pmat[c], jnp.concatenate([v16[c[:2]], ones_v], axis=-1)) for c in chains}
        for c in chains:
            g, h, d = c
            iv_ref[d, rows[g], h * ML_AUG:(h + 1) * ML_AUG] = iv[c].astype(BF16)
        for g in groups:
            for d in range(N_DIR):
                src = N_DIR * ML_HEADS + d * ML_HEADS
                aux_ref[d, rows[g], :LANES] = jnp.where(
                    lane < ML_HEADS, pltpu.roll(bc_c[g][d], LANES - src, 1),
                    jnp.where(lane < 2 * ML_HEADS, pltpu.roll(bt_c[g], LANES - src + ML_HEADS, 1), 0.0))
                aux_ref[d, rows[g], LANES:] = _lane_cols(
                    [d_max[g, h, d] for h in heads] + [w_max[g, h, d] for h in heads], LANES)
        gpd = [(g, p, d) for g, p in gps for d in range(N_DIR)]
        ks = {(g, p, d): (kp[g, p] * jnp.where(first_head, sw0[g, 2 * p, d], sw0[g, 2 * p + 1, d])).astype(BF16)
              for g, p, d in gpd}
        vcat = {(g, p): jnp.concatenate([v16[g, 2 * p], v16[g, 2 * p + 1], ones_v], axis=-1) for g, p in gps}
        pair_chunks = [(g, p, d, j) for g, p, d in gpd for j in range(G // L)]
        kv = [_dot_tn(ks[g, p, d][j * L:(j + 1) * L], vcat[g, p][j * L:(j + 1) * L]) for g, p, d, j in pair_chunks]
        for (g, p, d, j), t in zip(pair_chunks, kv):
            top = jnp.concatenate([t[:ML_QK_DIM, :ML_V_DIM], t[:ML_QK_DIM, 2 * ML_V_DIM:]], axis=-1)
            bot = jnp.concatenate([t[ML_QK_DIM:, ML_V_DIM:2 * ML_V_DIM], t[ML_QK_DIM:, 2 * ML_V_DIM:]], axis=-1)
            r0 = pl.multiple_of(gidx[g] * (G // L * ML_KV_ROWS) + (j * ML_PAIRS + p) * LANES, LANES)
            kv_ref[d, pl.ds(r0, LANES), :] = jnp.concatenate([top, bot], axis=0).astype(BF16)
        return carry

    lax.fori_loop(0, q_ref.shape[0] // (G * step_groups), step, 0)


def _mlstm_prep(z, zg, gr, *, batch, seq, q_col, k_col, v_col):
    n = batch * seq
    R, G, L = min(PREP_ROWS, seq), PREP_GROUP, SCAN_CHUNK
    steps = seq // R
    rowblk = lambda c: (lambda b, r: (b * steps + r, c))
    dirblk = lambda b, r: (0, b * steps + r, 0)
    kv_rows = ML_KV_ROWS // L
    return pl.pallas_call(
        _mlstm_prep_kernel,
        grid=(batch, steps),
        in_specs=[
            pl.BlockSpec((R, ML_QK_WIDTH), rowblk(q_col // ML_QK_WIDTH)),
            pl.BlockSpec((R, ML_QK_WIDTH), rowblk(k_col // ML_QK_WIDTH)),
            pl.BlockSpec((R, ML_V_WIDTH), rowblk(v_col // ML_V_WIDTH)),
            pl.BlockSpec((R, LANES), rowblk(0)),
            pl.BlockSpec((R // G, 16, G), lambda b, r: (b * steps + r, 0, 0)),
        ],
        out_specs=[
            pl.BlockSpec((R, ML_QK_WIDTH), rowblk(0)),
            pl.BlockSpec((N_DIR, R, ML_HEADS * ML_AUG), dirblk),
            pl.BlockSpec((N_DIR, R * kv_rows, ML_AUG), dirblk),
            pl.BlockSpec((N_DIR, R, 2 * LANES), dirblk),
        ],
        out_shape=[
            jax.ShapeDtypeStruct((n, ML_QK_WIDTH), BF16),
            jax.ShapeDtypeStruct((N_DIR, n, ML_HEADS * ML_AUG), BF16),
            jax.ShapeDtypeStruct((N_DIR, n * kv_rows, ML_AUG), BF16),
            jax.ShapeDtypeStruct((N_DIR, n, 2 * LANES), F32),
        ],
        compiler_params=_params("parallel", "parallel"),
        name="mlstm_prep",
    )(z, z, z, zg, gr)


def _mlstm_scan_kernel(qf, ivf, kvf, auxf, qb, ivb, kvb, auxb, of_ref, ob_ref, c_ref, m_ref, *, batch):
    L, C = SCAN_CHUNK, SCAN_STEP_CHUNKS

    @pl.when(pl.program_id(0) == 0)
    def _():
        c_ref[...] = jnp.zeros_like(c_ref)
        m_ref[...] = jnp.zeros_like(m_ref)

    dirs = ((qf, ivf, kvf, auxf, of_ref), (qb, ivb, kvb, auxb, ob_ref))
    lane = lax.broadcasted_iota(jnp.int32, (L, LANES), 1)
    keep = [jnp.where(lane < ML_QK_DIM, 1.0, 0.0).astype(BF16), jnp.where(lane < ML_QK_DIM, 0.0, 1.0).astype(BF16)]
    first_rows = lax.broadcasted_iota(jnp.int32, (2 * ML_QK_DIM, 1), 0) < ML_QK_DIM
    pair_chains = [(d, b, p) for d in range(N_DIR) for b in range(batch) for p in range(ML_PAIRS)]
    chains = [(d, b, h) for d in range(N_DIR) for b in range(batch) for h in range(ML_HEADS)]
    groups = [(d, b) for d in range(N_DIR) for b in range(batch)]
    H = ML_HEADS
    lane8 = lax.broadcasted_iota(jnp.int32, (SUBLANES, LANES), 1)
    sel_row = lax.broadcasted_iota(jnp.int32, (2 * LANES, LANES), 0) & (LANES - 1)
    lane_select = [_as01(sel_row == j) for j in range(2 * H)]
    state = {c: c_ref[c] for c in pair_chains}
    m_st = {g: m_ref[g] for g in groups}
    for step in range(C):
        chunk = {0: step, 1: C - 1 - step}
        rows = {d: slice(chunk[d] * L, (chunk[d] + 1) * L) for d in range(N_DIR)}
        qc = {}
        for c in pair_chains:
            d, b, p = c
            qp = dirs[d][0][b, rows[d], p * LANES:(p + 1) * LANES]
            qc[c] = _dot(jnp.concatenate([qp * keep[0], qp * keep[1]], axis=0), state[c].astype(BF16))
        w_prev, w_cur, floor = {}, {}, {}
        for g in groups:
            d, b = g
            aux = dirs[d][3][b, rows[d], :]
            x = aux[:, :LANES] + m_st[g][0:1, :]
            y = jnp.maximum(x, aux[:, LANES:])
            w_prev[g] = jnp.exp(x - y)
            w_cur[g] = jnp.exp(aux[:, LANES:] - y)
            floor[g] = jnp.exp(-y)
            y0 = jnp.broadcast_to(y[0:1, :], (SUBLANES, LANES))
            m_st[g] = jnp.where(lane8 < H, pltpu.roll(y0, LANES - H, 1), jnp.where(lane8 < 2 * H, y0, 0.0))
        tiles = [t[g] for g in groups for t in (w_prev, w_cur, floor)]
        per_row = _split_hi_lo(jnp.concatenate(tiles, axis=0))
        per_chunk = _split_hi_lo(jnp.concatenate([t[0:1, :] for t in tiles] + [tiles[0][0:SUBLANES, :]], axis=0))
        bc_row, bc_chunk = [], []
        for p in range(ML_PAIRS):
            both = _dot(per_row, jnp.concatenate([lane_select[2 * p], lane_select[2 * p + 1]], axis=-1))
            bc_row += [both[:, :LANES], both[:, LANES:]]
            both = _dot(per_chunk, jnp.concatenate([lane_select[H + 2 * p], lane_select[H + 2 * p + 1]], axis=-1))
            bc_chunk += [both[:, :LANES], both[:, LANES:]]
        numden = {}
        for c in chains:
            d, b, h = c
            r0 = (h % 2) * L
            t0 = 3 * groups.index((d, b)) * L
            wp = bc_row[h][t0:t0 + L, :]
            wc = bc_row[h][t0 + L:t0 + 2 * L, :]
            iv = dirs[d][1][b, rows[d], h * ML_AUG:(h + 1) * ML_AUG].astype(F32)
            qch = qc[d, b, h // 2][r0:r0 + L, :]
            numden[c] = jnp.concatenate([wp * qch[:, :ML_V_DIM] + wc * iv[:, :ML_V_DIM],
                                         wp * qch[:, ML_V_DIM:] + wc * iv[:, ML_V_DIM:]], axis=-1)
        for c in chains:
            d, b, h = c
            t0 = 3 * groups.index((d, b)) * L
            den = jnp.maximum(jnp.abs(numden[c][:, ML_V_DIM:]), bc_row[h][t0 + 2 * L:t0 + 3 * L, :])
            dirs[d][4][b, rows[d], h * ML_V_DIM:(h + 1) * ML_V_DIM] = (numden[c][:, :ML_V_DIM] / den).astype(BF16)
        for c in pair_chains:
            d, b, p = c
            r0 = (chunk[d] * ML_PAIRS + p) * LANES
            kv = dirs[d][2][b, r0:r0 + LANES, :].astype(F32)
            t0 = 3 * groups.index((d, b))
            cw = jnp.where(first_rows, bc_chunk[2 * p][t0:t0 + 1, :], bc_chunk[2 * p + 1][t0:t0 + 1, :])
            iw = jnp.where(first_rows, bc_chunk[2 * p][t0 + 1:t0 + 2, :], bc_chunk[2 * p + 1][t0 + 1:t0 + 2, :])
            state[c] = jnp.concatenate([cw * state[c][:, :ML_V_DIM] + iw * kv[:, :ML_V_DIM],
                                        cw * state[c][:, ML_V_DIM:] + iw * kv[:, ML_V_DIM:]], axis=-1)
    for c in pair_chains:
        c_ref[c] = state[c]
    for g in groups:
        m_ref[g] = m_st[g]


def _mlstm_scan(z, zg, gr, *, batch, seq, q_col, k_col, v_col):
    n = batch * seq
    T = SCAN_CHUNK * SCAN_STEP_CHUNKS
    steps = seq // T
    kv_rows = ML_KV_ROWS // SCAN_CHUNK
    qs, iv, kv, aux = _mlstm_prep(z, zg, gr, batch=batch, seq=seq, q_col=q_col, k_col=k_col, v_col=v_col)
    qs = qs.reshape(batch, seq, ML_QK_WIDTH)
    iv = iv.reshape(N_DIR, batch, seq, ML_HEADS * ML_AUG)
    kv = kv.reshape(N_DIR, batch, seq * kv_rows, ML_AUG)
    aux = aux.reshape(N_DIR, batch, seq, 2 * LANES)
    fwd = lambda s: s
    bwd = lambda s: steps - 1 - s
    specs = []
    for d, pos in ((0, fwd), (1, bwd)):
        specs += [
            pl.BlockSpec((batch, T, ML_QK_WIDTH), lambda s, pos=pos: (0, pos(s), 0)),
            pl.BlockSpec((None, batch, T, ML_HEADS * ML_AUG), lambda s, d=d, pos=pos: (d, 0, pos(s), 0)),
            pl.BlockSpec((None, batch, T * kv_rows, ML_AUG), lambda s, d=d, pos=pos: (d, 0, pos(s), 0)),
            pl.BlockSpec((None, batch, T, 2 * LANES), lambda s, d=d, pos=pos: (d, 0, pos(s), 0)),
        ]
    out = jax.ShapeDtypeStruct((batch, seq, ML_V_WIDTH), BF16)
    h_f, h_b = pl.pallas_call(
        functools.partial(_mlstm_scan_kernel, batch=batch),
        grid=(steps,),
        in_specs=specs,
        out_specs=[pl.BlockSpec((batch, T, ML_V_WIDTH), lambda s: (0, fwd(s), 0)),
                   pl.BlockSpec((batch, T, ML_V_WIDTH), lambda s: (0, bwd(s), 0))],
        out_shape=[out, out],
        scratch_shapes=[
            pltpu.VMEM((N_DIR, batch, ML_PAIRS, 2 * ML_QK_DIM, ML_AUG), F32),
            pltpu.VMEM((N_DIR, batch, SUBLANES, LANES), F32),
        ],
        compiler_params=_params("arbitrary"),
        name="mlstm_scan",
    )(qs, iv, kv, aux, qs, iv, kv, aux)
    return h_f.reshape(n, ML_V_WIDTH), h_b.reshape(n, ML_V_WIDTH)


def _head_rms_norm(x, g, n_heads, head_dim):
    parts = []
    for h in range(n_heads):
        xh = x[:, h * head_dim:(h + 1) * head_dim]
        ms = jnp.mean(xh * xh, axis=-1, keepdims=True)
        parts.append(xh * lax.rsqrt(ms + NORM_EPS) * g)
    return jnp.concatenate(parts, axis=-1)


def _layer_norm(x, g, b):
    mu = jnp.mean(x, axis=-1, keepdims=True)
    xc = x - mu
    var = jnp.mean(xc * xc, axis=-1, keepdims=True)
    return xc * lax.rsqrt(var + LN_EPS) * g + b


def _swiglu_residual(x, g_ref, w1_ref, w3_ref, w2_ref, fg_ref, final_norm):
    ms = jnp.mean(x * x, axis=-1, keepdims=True)
    hn = (x * lax.rsqrt(ms + NORM_EPS) * g_ref[...]).astype(BF16)
    a = _dot(hn, w1_ref[...])
    b = _dot(hn, w3_ref[...])
    y = x + _dot((a * jax.nn.sigmoid(a) * b).astype(BF16), w2_ref[...])
    if final_norm:
        ms = jnp.mean(y * y, axis=-1, keepdims=True)
        y = y * lax.rsqrt(ms + NORM_EPS) * fg_ref[...]
    return y


def _ffn_operands(layer, g, w1, w3, w2, final_g):
    _, d, f = w1.shape
    pick = (layer, 0, 0)
    specs = [_resident((1, d)), _resident((None, d, f), pick), _resident((None, d, f), pick),
             _resident((None, f, d), pick), _resident((1, d))]
    return specs, (g.reshape(1, d), w1.astype(BF16), w3.astype(BF16), w2.astype(BF16), final_g.reshape(1, d))


def _even_block_kernel(h_ref, c_ref, of_ref, ob_ref, gate_ref, lng_ref, lnb_ref, ng_ref, wa_ref, wb_ref,
                       g_ref, w1_ref, w3_ref, w2_ref, fg_ref, o_ref, *, final_norm):
    ya = _layer_norm(c_ref[...].astype(F32), lng_ref[...], lnb_ref[...])
    ya = ya * jax.nn.sigmoid(ya)
    o = _head_rms_norm(of_ref[...].astype(F32) + ob_ref[...].astype(F32), ng_ref[...], DN_HEADS, DN_HEAD_DIM)
    gate = gate_ref[...].astype(F32)
    o = o * (gate * jax.nn.sigmoid(gate))
    x = h_ref[...] + _dot(ya.astype(BF16), wa_ref[...]) + _dot(o.astype(BF16), wb_ref[...])
    o_ref[...] = _swiglu_residual(x, g_ref, w1_ref, w3_ref, w2_ref, fg_ref, final_norm)


def _even_block(h, conv, o_fwd, o_bwd, z, ln_g, ln_b, norm_g, w_out, ffn, *, gate_col, final_norm):
    n, d = h.shape
    tm = min(ROW_TILE, n)
    gb = gate_col // DN_WIDTH
    row = lambda i: (i, 0)
    ffn_specs, ffn_args = _ffn_operands(*ffn)
    return pl.pallas_call(
        functools.partial(_even_block_kernel, final_norm=final_norm),
        grid=(n // tm,),
        in_specs=[
            pl.BlockSpec((tm, d), row),
            pl.BlockSpec((tm, CONV_CH), row),
            pl.BlockSpec((tm, DN_WIDTH), row),
            pl.BlockSpec((tm, DN_WIDTH), row),
            pl.BlockSpec((tm, DN_WIDTH), lambda i: (i, gb)),
            _resident((1, CONV_CH)),
            _resident((1, CONV_CH)),
            _resident((1, DN_HEAD_DIM)),
            _resident((CONV_CH, d)),
            _resident((DN_WIDTH, d), (CONV_CH // DN_WIDTH, 0)),
        ] + ffn_specs,
        out_specs=pl.BlockSpec((tm, d), row),
        out_shape=jax.ShapeDtypeStruct((n, d), F32),
        compiler_params=_params("parallel"),
        name="even_block",
    )(h, conv, o_fwd, o_bwd, z, ln_g.reshape(1, -1), ln_b.reshape(1, -1), norm_g.reshape(1, -1),
      w_out.astype(BF16), w_out.astype(BF16), *ffn_args)


def _odd_block_kernel(h_ref, u_ref, vp_ref, hf_ref, hb_ref, op_ref, lng_ref, lnb_ref, sgw_ref, sgb_ref, ng_ref,
                      wa_ref, wb_ref, g_ref, w1_ref, w3_ref, w2_ref, fg_ref, o_ref, *, final_norm):
    tm = h_ref.shape[0]
    u = jax.nn.gelu(u_ref[...].astype(F32))
    vv = _layer_norm(jax.nn.gelu(vp_ref[...].astype(F32)), lng_ref[...], lnb_ref[...]).astype(BF16)
    sgb = sgb_ref[...]
    rows = []
    for c in range(tm // SG_CHUNK):
        parts = []
        for g in range(SG_GROUPS):
            blk = vv[c * SG_CHUNK:(c + 1) * SG_CHUNK, g * SG_GROUP_DIM:(g + 1) * SG_GROUP_DIM]
            parts.append(_dot(sgw_ref[g], blk) + sgb[:, g:g + 1])
        rows.append(jnp.concatenate(parts, axis=-1))
    yc = u * jnp.concatenate(rows, axis=0)
    hd = _head_rms_norm(hf_ref[...].astype(F32) + hb_ref[...].astype(F32), ng_ref[...], ML_HEADS, ML_V_DIM)
    hd = hd * jax.nn.sigmoid(op_ref[...].astype(F32))
    x = h_ref[...] + _dot(yc.astype(BF16), wa_ref[...]) + _dot(hd.astype(BF16), wb_ref[...])
    o_ref[...] = _swiglu_residual(x, g_ref, w1_ref, w3_ref, w2_ref, fg_ref, final_norm)


def _odd_block(h, z, h_fwd, h_bwd, ln_g, ln_b, sg_w, sg_b, norm_g, w_out, ffn, *, u_col, v_col, o_col, final_norm):
    n, d = h.shape
    tm = min(ROW_TILE, n)
    row = lambda i: (i, 0)
    ffn_specs, ffn_args = _ffn_operands(*ffn)
    return pl.pallas_call(
        functools.partial(_odd_block_kernel, final_norm=final_norm),
        grid=(n // tm,),
        in_specs=[
            pl.BlockSpec((tm, d), row),
            pl.BlockSpec((tm, SG_WIDTH), lambda i: (i, u_col // SG_WIDTH)),
            pl.BlockSpec((tm, SG_WIDTH), lambda i: (i, v_col // SG_WIDTH)),
            pl.BlockSpec((tm, ML_V_WIDTH), row),
            pl.BlockSpec((tm, ML_V_WIDTH), row),
            pl.BlockSpec((tm, ML_V_WIDTH), lambda i: (i, o_col // ML_V_WIDTH)),
            _resident((1, SG_WIDTH)),
            _resident((1, SG_WIDTH)),
            _resident((SG_GROUPS, SG_CHUNK, SG_CHUNK)),
            _resident((SG_CHUNK, SG_GROUPS)),
            _resident((1, ML_V_DIM)),
            _resident((SG_WIDTH, d)),
            _resident((ML_V_WIDTH, d), (SG_WIDTH // ML_V_WIDTH, 0)),
        ] + ffn_specs,
        out_specs=pl.BlockSpec((tm, d), row),
        out_shape=jax.ShapeDtypeStruct((n, d), F32),
        compiler_params=_params("parallel"),
        name="odd_block",
    )(h, z, z, h_fwd, h_bwd, z, ln_g.reshape(1, -1), ln_b.reshape(1, -1), sg_w.astype(BF16), sg_b.T,
      norm_g.reshape(1, -1), w_out.astype(BF16), w_out.astype(BF16), *ffn_args)


def _gate_weight(w_gate_cols):
    d, c = w_gate_cols.shape
    return jnp.zeros((d, LANES), F32).at[:, :c].set(w_gate_cols).astype(BF16)


def _even_layer(h, j, p, ffn, *, batch, seq, final_norm):
    main = 2 * CONV_CH + 4 * DN_WIDTH
    w_in = p["ev_w_in"][j]
    gate_params = jnp.zeros((SUBLANES, LANES), F32)
    gate_params = gate_params.at[0, 8:16].set(p["ev_dn_a_log"][j].reshape(-1))
    gate_params = gate_params.at[1, 8:16].set(p["ev_dn_dt_bias"][j].reshape(-1))
    z, zg, gr = _in_proj(h, p["mix_norm_g"][2 * j], w_in.astype(BF16), main, _gate_weight(w_in[:, main:]),
                         gate_params, even=True)
    conv = _conv_glu(z, p["ev_conv_w"][j], p["ev_conv_b"][j], batch=batch, seq=seq)
    qkv = _conv_qkv(z, p["ev_dn_conv_w"][j], batch=batch, seq=seq, col0=2 * CONV_CH)
    o_fwd, o_bwd = _delta_scan(qkv, zg, gr, batch=batch, seq=seq)
    return _even_block(h, conv, o_fwd, o_bwd, z, p["ev_conv_ln_g"][j], p["ev_conv_ln_b"][j], p["ev_dn_norm_g"][j],
                       p["ev_w_out"][j], ffn, gate_col=2 * CONV_CH + 3 * DN_WIDTH, final_norm=final_norm)


def _odd_layer(h, j, p, ffn, *, batch, seq, final_norm):
    main = 2 * SG_WIDTH + 2 * ML_QK_WIDTH + 2 * ML_V_WIDTH
    w_in = p["od_w_in"][j]
    gate_params = jnp.zeros((SUBLANES, LANES), F32)
    gate_params = gate_params.at[0, 0:8].set(p["od_ml_i_bias"][j].reshape(-1))
    gate_params = gate_params.at[1, 8:16].set(p["od_ml_f_bias"][j].reshape(-1))
    z, zg, gr = _in_proj(h, p["mix_norm_g"][2 * j + 1], w_in.astype(BF16), main, _gate_weight(w_in[:, main:]),
                         gate_params, even=False)
    q_col = 2 * SG_WIDTH
    k_col = q_col + ML_QK_WIDTH
    v_col = k_col + ML_QK_WIDTH
    o_col = v_col + ML_V_WIDTH
    h_fwd, h_bwd = _mlstm_scan(z, zg, gr, batch=batch, seq=seq, q_col=q_col, k_col=k_col, v_col=v_col)
    return _odd_block(h, z, h_fwd, h_bwd, p["od_sg_ln_g"][j], p["od_sg_ln_b"][j], p["od_sg_w"][j], p["od_sg_b"][j],
                      p["od_ml_norm_g"][j], p["od_w_out"][j], ffn, u_col=0, v_col=SG_WIDTH, o_col=o_col,
                      final_norm=final_norm)


def kernel(x, mix_norm_g, ev_w_in, ev_conv_w, ev_conv_b, ev_conv_ln_g, ev_conv_ln_b, ev_dn_conv_w, ev_dn_a_log, ev_dn_dt_bias, ev_dn_norm_g, ev_w_out, od_w_in, od_sg_ln_g, od_sg_ln_b, od_sg_w, od_sg_b, od_ml_i_bias, od_ml_f_bias, od_ml_norm_g, od_w_out, ffn_norm_g, ffn_w1, ffn_w3, ffn_w2, final_norm_g):
    p = dict(mix_norm_g=mix_norm_g, ev_w_in=ev_w_in, ev_conv_w=ev_conv_w, ev_conv_b=ev_conv_b,
             ev_conv_ln_g=ev_conv_ln_g, ev_conv_ln_b=ev_conv_ln_b, ev_dn_conv_w=ev_dn_conv_w,
             ev_dn_a_log=ev_dn_a_log, ev_dn_dt_bias=ev_dn_dt_bias, ev_dn_norm_g=ev_dn_norm_g, ev_w_out=ev_w_out,
             od_w_in=od_w_in, od_sg_ln_g=od_sg_ln_g, od_sg_ln_b=od_sg_ln_b, od_sg_w=od_sg_w, od_sg_b=od_sg_b,
             od_ml_i_bias=od_ml_i_bias, od_ml_f_bias=od_ml_f_bias, od_ml_norm_g=od_ml_norm_g, od_w_out=od_w_out)
    batch, seq, d = x.shape
    depth = mix_norm_g.shape[0]
    h = x.reshape(batch * seq, d)
    for layer in range(depth):
        ffn = (layer, ffn_norm_g[layer], ffn_w1, ffn_w3, ffn_w2, final_norm_g)
        mixer_layer = _even_layer if layer % 2 == 0 else _odd_layer
        h = mixer_layer(h, layer // 2, p, ffn, batch=batch, seq=seq, final_norm=layer == depth - 1)
    return h.reshape(batch, seq, d)
```

```python
import functools

import jax
import jax.numpy as jnp
from jax import lax
from jax.experimental import pallas as pl
from jax.experimental.pallas import tpu as pltpu

NORM_EPS = 1e-6
LN_EPS = 1e-5
N_DIR = 2

CONV_CH = 512
DN_HEADS = 4
DN_HEAD_DIM = 128
DN_WIDTH = DN_HEADS * DN_HEAD_DIM
SG_GROUPS = 4
SG_GROUP_DIM = 128
SG_WIDTH = SG_GROUPS * SG_GROUP_DIM
SG_CHUNK = 128
ML_HEADS = 4
ML_QK_DIM = 64
ML_V_DIM = 128
ML_QK_WIDTH = ML_HEADS * ML_QK_DIM
ML_V_WIDTH = ML_HEADS * ML_V_DIM

LANES = 128
SUBLANES = 8
VMEM_LIMIT_BYTES = 56 * 1024 * 1024

SCAN_CHUNK = 64
SCAN_STEP_CHUNKS = 4
PREP_GROUP = 128
PREP_ROWS = 512
DN_PREP_STEP_GROUPS = 2
ML_PREP_STEP_GROUPS = 4
ROW_TILE = 512
IN_ROW_TILE = 1024
CONV_TILE = 128
QKV_CONV_TILES = 16
GLU_CONV_TILES = 4
CONV_PAD = 16
NEG_BIG = -1e30

BF16 = jnp.bfloat16
F32 = jnp.float32


def _params(*sem):
    return pltpu.CompilerParams(dimension_semantics=sem, vmem_limit_bytes=VMEM_LIMIT_BYTES)


def _resident(shape, block_index=None):
    index = (0,) * len(shape) if block_index is None else tuple(block_index)
    return pl.BlockSpec(shape, lambda *_: index, pipeline_mode=pl.Buffered(1))


def _dot(a, b):
    return jnp.dot(a, b, preferred_element_type=F32)


def _dot_nt(a, b):
    return lax.dot_general(a, b, (((1,), (1,)), ((), ())), preferred_element_type=F32)


def _dot_tn(a, b):
    return lax.dot_general(a, b, (((0,), (0,)), ((), ())), preferred_element_type=F32)


def _exact_dot01(t01, x):
    x1 = x.astype(BF16)
    r1 = x - x1.astype(F32)
    x2 = r1.astype(BF16)
    x3 = (r1 - x2.astype(F32)).astype(BF16)
    return _dot(t01, x1) + _dot(t01, x2) + _dot(t01, x3)


def _exact_dot01_r(x, t01):
    x1 = x.astype(BF16)
    r1 = x - x1.astype(F32)
    x2 = r1.astype(BF16)
    x3 = (r1 - x2.astype(F32)).astype(BF16)
    return _dot(x1, t01) + _dot(x2, t01) + _dot(x3, t01)


def _in_proj_kernel(h_ref, g_ref, w_ref, wg_ref, gp_ref, zm_ref, zg_ref, zgt_ref, *, even):
    x = h_ref[...]
    ms = jnp.mean(x * x, axis=-1, keepdims=True)
    hn = (x * lax.rsqrt(ms + NORM_EPS) * g_ref[...]).astype(BF16)
    zm_ref[...] = _dot(hn, w_ref[...]).astype(zm_ref.dtype)
    zg = _dot(hn, wg_ref[...])
    p0 = gp_ref[0:1, :]
    p1 = gp_ref[1:2, :]
    lane = lax.broadcasted_iota(jnp.int32, zg.shape, 1)
    if even:
        first = jax.nn.sigmoid(zg)
        second = -jnp.exp(p0) * jax.nn.softplus(zg + p1)
    else:
        first = zg + p0
        second = jax.nn.log_sigmoid(zg + p1)
    gates = jnp.where(lane < 8, first, second)
    zg_ref[...] = gates
    for s in range(gates.shape[0] // PREP_GROUP):
        zgt_ref[s] = gates[s * PREP_GROUP:(s + 1) * PREP_GROUP, :].T[:2 * SUBLANES, :]


def _in_proj(h, g, w_in, c, w_gate, gate_params, *, even):
    n, d = h.shape
    assert c % LANES == 0
    tm = min(IN_ROW_TILE, n)
    return pl.pallas_call(
        functools.partial(_in_proj_kernel, even=even),
        grid=(n // tm,),
        in_specs=[
            pl.BlockSpec((tm, d), lambda i: (i, 0)),
            _resident((1, d)),
            _resident((d, c)),
            _resident((d, LANES)),
            _resident((SUBLANES, LANES)),
        ],
        out_specs=[
            pl.BlockSpec((tm, c), lambda i: (i, 0)),
            pl.BlockSpec((tm, LANES), lambda i: (i, 0)),
            pl.BlockSpec((tm // PREP_GROUP, 2 * SUBLANES, PREP_GROUP), lambda i: (i, 0, 0)),
        ],
        out_shape=[jax.ShapeDtypeStruct((n, c), BF16), jax.ShapeDtypeStruct((n, LANES), F32),
                   jax.ShapeDtypeStruct((n // PREP_GROUP, 2 * SUBLANES, PREP_GROUP), F32)],
        compiler_params=_params("parallel"),
        name="in_proj_even" if even else "in_proj_odd",
    )(h, g.reshape(1, d), w_in, w_gate, gate_params)


def _conv_taps(pad_ref, w_ref, o_ref, bias, *, seq, width, post, tiles_per_step=1):
    half = width // 2
    tiles_per_step = min(tiles_per_step, seq // CONV_TILE)
    step_rows = CONV_TILE * tiles_per_step

    def body(i, carry):
        t0 = pl.multiple_of(i * step_rows, step_rows)
        accs = []
        for s in range(tiles_per_step):
            acc = jnp.zeros((CONV_TILE, LANES), F32)
            for j in range(width):
                acc = acc + pad_ref[pl.ds(t0 + (s * CONV_TILE + CONV_PAD - half + j), CONV_TILE), :] * w_ref[j:j + 1, :]
            accs.append(acc if bias is None else acc + bias)
        for s, out in enumerate(post(accs)):
            o_ref[pl.ds(t0 + s * CONV_TILE, CONV_TILE), :] = out.astype(o_ref.dtype)
        return carry

    lax.fori_loop(0, seq // step_rows, body, 0)


def _fill_padded(pad_ref, x, seq):
    zeros = jnp.zeros((CONV_PAD, LANES), F32)
    pad_ref[0:CONV_PAD, :] = zeros
    pad_ref[CONV_PAD + seq:CONV_PAD + seq + CONV_PAD, :] = zeros
    pad_ref[CONV_PAD:CONV_PAD + seq, :] = x


def _conv_glu_kernel(av_ref, ag_ref, w_ref, b_ref, o_ref, pad_ref, *, seq, width):
    _fill_padded(pad_ref, av_ref[...].astype(F32) * jax.nn.sigmoid(ag_ref[...].astype(F32)), seq)
    _conv_taps(pad_ref, w_ref, o_ref, b_ref[...], seq=seq, width=width, post=lambda tiles: tiles,
               tiles_per_step=GLU_CONV_TILES)


def _conv_glu(z, conv_w, conv_b, *, batch, seq):
    width = conv_w.shape[0]
    nblk = CONV_CH // LANES
    return pl.pallas_call(
        functools.partial(_conv_glu_kernel, seq=seq, width=width),
        grid=(batch, nblk),
        in_specs=[
            pl.BlockSpec((seq, LANES), lambda b, c: (b, c)),
            pl.BlockSpec((seq, LANES), lambda b, c: (b, nblk + c)),
            pl.BlockSpec((width, LANES), lambda b, c: (0, c)),
            pl.BlockSpec((1, LANES), lambda b, c: (0, c)),
        ],
        out_specs=pl.BlockSpec((seq, LANES), lambda b, c: (b, c)),
        out_shape=jax.ShapeDtypeStruct((batch * seq, CONV_CH), BF16),
        scratch_shapes=[pltpu.VMEM((seq + 2 * CONV_PAD, LANES), F32)],
        compiler_params=_params("parallel", "parallel"),
        name="conv_glu",
    )(z, z, conv_w, conv_b.reshape(1, CONV_CH))


def _conv_qkv_kernel(x_ref, w_ref, o_ref, pad_ref, *, seq, width, n_norm_blocks):
    _fill_padded(pad_ref, x_ref[...].astype(F32), seq)
    normalise = pl.program_id(1) < n_norm_blocks

    def post(tiles):
        ys = [a * jax.nn.sigmoid(a) for a in tiles]
        sq = [jnp.sum(y * y, axis=-1, keepdims=True) for y in ys]
        inv = [lax.rsqrt(s + NORM_EPS) for s in sq]
        return [jnp.where(normalise, y * r, y) for y, r in zip(ys, inv)]

    _conv_taps(pad_ref, w_ref, o_ref, None, seq=seq, width=width, post=post, tiles_per_step=QKV_CONV_TILES)


def _conv_qkv(z, dn_conv_w, *, batch, seq, col0):
    width = dn_conv_w.shape[0]
    nblk = 3 * DN_WIDTH // LANES
    blk0 = col0 // LANES
    return pl.pallas_call(
        functools.partial(_conv_qkv_kernel, seq=seq, width=width, n_norm_blocks=2 * DN_HEADS),
        grid=(batch, nblk),
        in_specs=[
            pl.BlockSpec((seq, LANES), lambda b, c: (b, blk0 + c)),
            pl.BlockSpec((width, LANES), lambda b, c: (0, c)),
        ],
        out_specs=pl.BlockSpec((seq, LANES), lambda b, c: (b, c)),
        out_shape=jax.ShapeDtypeStruct((batch * seq, 3 * DN_WIDTH), BF16),
        scratch_shapes=[pltpu.VMEM((seq + 2 * CONV_PAD, LANES), F32)],
        compiler_params=_params("parallel", "parallel"),
        name="conv_qkv",
    )(z, dn_conv_w)


def _split_hi_lo(x):
    hi = x.astype(BF16)
    return jnp.concatenate([hi, (x - hi.astype(F32)).astype(BF16)], axis=-1)


def _bf16_all(mats):
    return [m.astype(BF16) for m in mats]


def _unit_tri_inverses(mats, row, col, n):
    eye = (row == col).astype(F32)
    same = (row >> 3) == (col >> 3)
    l8 = [jnp.where(same, a, 0.0) for a in mats]
    l8h = _bf16_all(l8)
    l2h = _bf16_all([_dot(p, p) for p in l8h])
    l4h = _bf16_all([_dot(p, p) for p in l2h])
    x = [eye - p for p in l8]
    x = [xi + _dot(xi.astype(BF16), p) for xi, p in zip(x, l2h)]
    x = [xi + _dot(xi.astype(BF16), p) for xi, p in zip(x, l4h)]
    shift = 3
    while (1 << shift) < n:
        same_next = (row >> (shift + 1)) == (col >> (shift + 1))
        off = jnp.logical_and(same_next, jnp.logical_not(same))
        ch = _bf16_all([jnp.where(off, a, 0.0) for a in mats])
        xh = _bf16_all(x)
        xch = _bf16_all([_dot(p, c) for p, c in zip(xh, ch)])
        x = [xi - _dot(p, q) for xi, p, q in zip(x, xch, xh)]
        same = same_next
        shift += 1
    return x


def _group_masks(backward):
    n = PREP_GROUP
    row = lax.broadcasted_iota(jnp.int32, (n, n), 0)
    col = lax.broadcasted_iota(jnp.int32, (n, n), 1)
    shift = SCAN_CHUNK.bit_length() - 1
    same = (row >> shift) == (col >> shift)
    d = (col - row) if backward else (row - col)
    land = jnp.logical_and
    return land(same, d >= 0), land(same, d > 0), land(same, d <= 0)


def _as01(mask):
    return jnp.where(mask, 1.0, 0.0).astype(BF16)


def _delta_prep_kernel(q_ref, k_ref, v_ref, gc_ref, gr_ref, u_ref, w_ref, qg_ref, kg_ref, at_ref):
    G, L = PREP_GROUP, SCAN_CHUNK
    row = lax.broadcasted_iota(jnp.int32, (G, G), 0)
    col = lax.broadcasted_iota(jnp.int32, (G, G), 1)
    shift = L.bit_length() - 1
    same01 = _as01((row >> shift) == (col >> shift))
    masks = [_group_masks(False), _group_masks(True)]
    cum_c = [_as01(m[0]) for m in masks]
    cum_r = [_as01(m[2]) for m in masks]
    fold = _as01((lax.broadcasted_iota(jnp.int32, (G, L), 0) & (L - 1)) == lax.broadcasted_iota(jnp.int32, (G, L), 1))
    pad = jnp.zeros((G, DN_HEAD_DIM - L), BF16)
    scale = DN_HEAD_DIM ** -0.5

    step_groups = min(DN_PREP_STEP_GROUPS, q_ref.shape[0] // G)

    def step(i, carry):
        groups = range(step_groups)
        rows = [pl.ds(pl.multiple_of((i * step_groups + g) * G, G), G) for g in groups]
        gc = [gc_ref[r, :] for r in rows]
        gr = [gr_ref[i * step_groups + g] for g in groups]
        g_tot = [_exact_dot01(same01, x) for x in gc]
        g_cum_c = [[_exact_dot01(cum_c[d], x) for d in range(N_DIR)] for x in gc]
        g_cum_r = [[_exact_dot01_r(x, cum_r[d]) for d in range(N_DIR)] for x in gr]
        heads = range(DN_HEADS)
        lanes = [slice(h * DN_HEAD_DIM, (h + 1) * DN_HEAD_DIM) for h in heads]
        gh = [(g, h) for g in groups for h in heads]
        chains = [(g, h, d) for g, h in gh for d in range(N_DIR)]
        k16 = {(g, h): k_ref[rows[g], lanes[h]] for g, h in gh}
        q = {(g, h): q_ref[rows[g], lanes[h]].astype(F32) for g, h in gh}
        k = {c: k16[c].astype(F32) for c in gh}
        kk = {c: _dot_nt(k16[c], k16[c]) for c in gh}
        qk = {c: _dot_nt((q[c] * scale).astype(BF16), k16[c]) for c in gh}
        beta, g_c, g_t, decay = {}, {}, {}, {}
        for c in chains:
            g, h, d = c
            cb = d * DN_HEADS + h
            cl = N_DIR * DN_HEADS + cb
            beta[c] = gc[g][:, cb:cb + 1]
            g_c[c] = g_cum_c[g][d][:, cl:cl + 1]
            g_t[c] = g_tot[g][:, cl:cl + 1]
            g_r = g_cum_r[g][d][cl:cl + 1, :]
            decay[c] = jnp.exp(jnp.where(masks[d][0], g_c[c] - g_r, NEG_BIG))
        a = [jnp.where(masks[c[2]][1], kk[c[:2]] * decay[c], 0.0) * beta[c] for c in chains]
        x = _unit_tri_inverses(a, row, col, L)
        eg = {c: jnp.exp(g_c[c]) for c in chains}
        rhs = [jnp.concatenate([v_ref[rows[c[0]], lanes[c[1]]].astype(F32) * beta[c], k[c[:2]] * (beta[c] * eg[c])],
                               axis=-1).astype(BF16) for c in chains]
        uw = [_dot(xi.astype(BF16), r) for xi, r in zip(x, rhs)]
        attn = [jnp.where(masks[c[2]][0], qk[c[:2]] * decay[c], 0.0).astype(BF16) for c in chains]
        at = [_dot(p, fold) for p in attn]
        for c, uwi, ati in zip(chains, uw, at):
            g, h, d = c
            r, sl = rows[g], lanes[h]
            u_ref[d, r, sl] = uwi[:, :DN_HEAD_DIM].astype(BF16)
            w_ref[d, r, sl] = uwi[:, DN_HEAD_DIM:].astype(BF16)
            qg_ref[d, r, sl] = (q[g, h] * (scale * eg[c])).astype(BF16)
            kg_ref[d, r, sl] = (k[g, h] * jnp.exp(g_t[c] - g_c[c])).astype(BF16)
            at_ref[d, r, sl] = jnp.concatenate([ati.astype(BF16), pad], axis=-1)
        return carry

    lax.fori_loop(0, q_ref.shape[0] // (G * step_groups), step, 0)


def _delta_prep(qkv, zg, gr, *, batch, seq):
    n = batch * seq
    R, G = min(PREP_ROWS, seq), PREP_GROUP
    steps = seq // R
    rowblk = lambda c: (lambda b, r: (b * steps + r, c))
    out = jax.ShapeDtypeStruct((N_DIR, n, DN_WIDTH), BF16)
    out_spec = pl.BlockSpec((N_DIR, R, DN_WIDTH), lambda b, r: (0, b * steps + r, 0))
    return pl.pallas_call(
        _delta_prep_kernel,
        grid=(batch, steps),
        in_specs=[
            pl.BlockSpec((R, DN_WIDTH), rowblk(0)),
            pl.BlockSpec((R, DN_WIDTH), rowblk(1)),
            pl.BlockSpec((R, DN_WIDTH), rowblk(2)),
            pl.BlockSpec((R, LANES), rowblk(0)),
            pl.BlockSpec((R // G, 16, G), lambda b, r: (b * steps + r, 0, 0)),
        ],
        out_specs=[out_spec] * 5,
        out_shape=[out] * 5,
        compiler_params=_params("parallel", "parallel"),
        name="delta_prep",
    )(qkv, qkv, qkv, zg, gr)


def _delta_scan_kernel(uf, wf, qgf, kgf, atf, zgf, ub, wb, qgb, kgb, atb, zgb, of_ref, ob_ref, s_ref, *, batch):
    L, C = SCAN_CHUNK, SCAN_STEP_CHUNKS

    @pl.when(pl.program_id(0) == 0)
    def _():
        s_ref[...] = jnp.zeros_like(s_ref)

    dirs = ((uf, wf, qgf, kgf, atf, zgf, of_ref), (ub, wb, qgb, kgb, atb, zgb, ob_ref))

    chains = [(d, b, h) for d in range(N_DIR) for b in range(batch) for h in range(DN_HEADS)]
    state = {c: s_ref[c] for c in chains}
    for step in range(C):
        chunk = {0: step, 1: C - 1 - step}
        rows = {d: slice(chunk[d] * L, (chunk[d] + 1) * L) for d in range(N_DIR)}
        egt = {(d, b): jnp.exp(jnp.sum(dirs[d][5][b, rows[d], :], axis=0, keepdims=True))
               for d in range(N_DIR) for b in range(batch)}
        ws, v_new, av = {}, {}, {}
        for c in chains:
            d, b, h = c
            sl = slice(h * DN_HEAD_DIM, (h + 1) * DN_HEAD_DIM)
            wq = jnp.concatenate([dirs[d][1][b, rows[d], sl], dirs[d][2][b, rows[d], sl]], axis=0)
            ws[c] = _dot(wq, state[c].astype(BF16))
        for c in chains:
            d, b, h = c
            sl = slice(h * DN_HEAD_DIM, (h + 1) * DN_HEAD_DIM)
            v_new[c] = (dirs[d][0][b, rows[d], sl].astype(F32) - ws[c][:L]).astype(BF16)
        for c in chains:
            d, b, h = c
            at = dirs[d][4][b, rows[d], h * DN_HEAD_DIM:h * DN_HEAD_DIM + L]
            av[c] = _dot(at, v_new[c])
        for c in chains:
            d, b, h = c
            sl = slice(h * DN_HEAD_DIM, (h + 1) * DN_HEAD_DIM)
            cl = (N_DIR + d) * DN_HEADS + h
            dirs[d][6][b, rows[d], sl] = (ws[c][L:] + av[c]).astype(BF16)
            state[c] = state[c] * egt[d, b][:, cl:cl + 1] + _dot_tn(dirs[d][3][b, rows[d], sl], v_new[c])
    for c in chains:
        s_ref[c] = state[c]


def _delta_scan(qkv, zg, gr, *, batch, seq):
    n = batch * seq
    T = SCAN_CHUNK * SCAN_STEP_CHUNKS
    steps = seq // T
    prepped = [t.reshape(N_DIR, batch, seq, DN_WIDTH) for t in _delta_prep(qkv, zg, gr, batch=batch, seq=seq)]
    zg3 = zg.reshape(batch, seq, LANES)
    fwd = lambda s: s
    bwd = lambda s: steps - 1 - s
    specs = []
    for d, pos in ((0, fwd), (1, bwd)):
        specs += [pl.BlockSpec((None, batch, T, DN_WIDTH), lambda s, d=d, pos=pos: (d, 0, pos(s), 0))] * 5
        specs += [pl.BlockSpec((batch, T, LANES), lambda s, pos=pos: (0, pos(s), 0))]
    out = jax.ShapeDtypeStruct((batch, seq, DN_WIDTH), BF16)
    o_f, o_b = pl.pallas_call(
        functools.partial(_delta_scan_kernel, batch=batch),
        grid=(steps,),
        in_specs=specs,
        out_specs=[pl.BlockSpec((batch, T, DN_WIDTH), lambda s: (0, fwd(s), 0)),
                   pl.BlockSpec((batch, T, DN_WIDTH), lambda s: (0, bwd(s), 0))],
        out_shape=[out, out],
        scratch_shapes=[pltpu.VMEM((N_DIR, batch, DN_HEADS, DN_HEAD_DIM, DN_HEAD_DIM), F32)],
        compiler_params=_params("arbitrary"),
        name="delta_scan",
    )(*prepped, zg3, *prepped, zg3)
    return o_f.reshape(n, DN_WIDTH), o_b.reshape(n, DN_WIDTH)


ML_PAIRS = ML_HEADS // 2
ML_AUG = 2 * ML_V_DIM
ML_KV_ROWS = ML_PAIRS * 2 * ML_QK_DIM


def _lane_cols(cols, width):
    rows = cols[0].shape[0]
    lane = lax.broadcasted_iota(jnp.int32, (rows, width), 1)
    out = jnp.zeros((rows, width), F32)
    for j, c in enumerate(cols):
        out = jnp.where(lane == j, c, out)
    return out


def _mlstm_prep_kernel(q_ref, k_ref, v_ref, gc_ref, gr_ref, qs_ref, iv_ref, kv_ref, aux_ref):
    G, L = PREP_GROUP, SCAN_CHUNK
    row = lax.broadcasted_iota(jnp.int32, (G, G), 0)
    col = lax.broadcasted_iota(jnp.int32, (G, G), 1)
    shift = L.bit_length() - 1
    same01 = _as01((row >> shift) == (col >> shift))
    masks = [_group_masks(False), _group_masks(True)]
    cum_c = [_as01(m[0]) for m in masks]
    cum_r = [_as01(m[2]) for m in masks]
    lane = lax.broadcasted_iota(jnp.int32, (G, LANES), 1)
    first_head = lane < ML_QK_DIM
    sel_row = lax.broadcasted_iota(jnp.int32, (2 * LANES, G), 0) & (LANES - 1)
    sel_col = lax.broadcasted_iota(jnp.int32, (2 * LANES, G), 1)
    gate_lane = lambda h, d: N_DIR * ML_HEADS + d * ML_HEADS + h
    pick_pair = {(p, d): jnp.concatenate([_as01(sel_row == gate_lane(2 * p, d)),
                                          _as01(sel_row == gate_lane(2 * p + 1, d))], axis=-1)
                 for p in range(ML_PAIRS) for d in range(N_DIR)}
    pick_halves = {(p, d): _as01(sel_row == jnp.where(sel_col < ML_QK_DIM, gate_lane(2 * p, d), gate_lane(2 * p + 1, d)))
                   for p in range(ML_PAIRS) for d in range(N_DIR)}
    ones_v = jnp.ones((G, ML_V_DIM), BF16)
    scale = ML_QK_DIM ** -0.5
    heads = range(ML_HEADS)
    chains = [(h, d) for h in heads for d in range(N_DIR)]

    step_groups = min(ML_PREP_STEP_GROUPS, q_ref.shape[0] // G)

    def step(i, carry):
        groups = range(step_groups)
        gidx = [i * step_groups + g for g in groups]
        rows = [pl.ds(pl.multiple_of(x * G, G), G) for x in gidx]
        gc = [gc_ref[r, :] for r in rows]
        gr = [gr_ref[x] for x in gidx]
        bt_c = [_exact_dot01(same01, x) for x in gc]
        bc_c = [[_exact_dot01(cum_c[d], x) for d in range(N_DIR)] for x in gc]
        bc_r = [[_exact_dot01_r(x, cum_r[d]) for d in range(N_DIR)] for x in gr]
        gps = [(g, p) for g in groups for p in range(ML_PAIRS)]
        ghs = [(g, h) for g in groups for h in heads]
        chains = [(g, h, d) for g, h in ghs for d in range(N_DIR)]
        qp = {(g, p): q_ref[rows[g], p * LANES:(p + 1) * LANES].astype(F32) * scale for g, p in gps}
        kp16 = {(g, p): k_ref[rows[g], p * LANES:(p + 1) * LANES] for g, p in gps}
        kp = {c: kp16[c].astype(F32) for c in gps}
        v16 = {(g, h): v_ref[rows[g], h * ML_V_DIM:(h + 1) * ML_V_DIM].astype(BF16) for g, h in ghs}
        for g, p in gps:
            qs_ref[rows[g], p * LANES:(p + 1) * LANES] = qp[g, p].astype(BF16)
        own = [first_head, jnp.logical_not(first_head)]
        scores = {(g, h): _dot_nt(jnp.where(own[h % 2], qp[g, h // 2], 0.0).astype(BF16), kp16[g, h // 2])
                  for g, h in ghs}
        col = lambda c: gate_lane(c[1], c[2])
        b_r = {c: bc_r[c[0]][c[2]][col(c):col(c) + 1, :] for c in chains}
        li_r = {c: gr[c[0]][col(c) - N_DIR * ML_HEADS:col(c) - N_DIR * ML_HEADS + 1, :] for c in chains}
        split = {(g, d): _split_hi_lo(bc_c[g][d]) for g in groups for d in range(N_DIR)}
        b_cb = {}
        for g, p in gps:
            for d in range(N_DIR):
                both = _dot(split[g, d], pick_pair[p, d])
                b_cb[g, 2 * p, d], b_cb[g, 2 * p + 1, d] = both[:, :G], both[:, G:]
        d_mat = {c: jnp.where(masks[c[2]][0], b_cb[c] - b_r[c] + li_r[c], NEG_BIG) for c in chains}
        d_max = {c: jnp.max(d_mat[c], axis=-1, keepdims=True) for c in chains}
        pmat = {c: (jnp.exp(d_mat[c] - d_max[c]) * scores[c[:2]]).astype(BF16) for c in chains}
        iv = {c: _dot(pmat[c], jnp.concatenate([v16[c[:2]], ones_v], axis=-1)) for c in chains}
        for c in chains:
            g, h, d = c
            iv_ref[d, rows[g], h * ML_AUG:(h + 1) * ML_AUG] = iv[c].astype(BF16)
        gd = [(g, d) for g in groups for d in range(N_DIR)]
        li_t = [pltpu.roll(x, N_DIR * ML_HEADS, 1) for x in gc]
        w_end = {(g, d): bt_c[g] - bc_c[g][d] + li_t[g] for g, d in gd}
        w_max = {c: jnp.concatenate([jnp.broadcast_to(jnp.max(w_end[c][j * L:(j + 1) * L], axis=0, keepdims=True),
                                                      (L, LANES)) for j in range(G // L)], axis=0) for c in gd}
        sw0 = {c: _split_hi_lo(jnp.exp(w_end[c] - w_max[c])) for c in gd}
        for g, d in gd:
            src = gate_lane(0, d)
            aux_ref[d, rows[g], :LANES] = jnp.where(
                lane < ML_HEADS, pltpu.roll(bc_c[g][d], LANES - src, 1),
                jnp.where(lane < 2 * ML_HEADS, pltpu.roll(bt_c[g], LANES - src + ML_HEADS, 1), 0.0))
            aux_ref[d, rows[g], LANES:] = jnp.where(
                lane < ML_HEADS, _lane_cols([d_max[g, h, d] for h in heads], LANES),
                jnp.where(lane < 2 * ML_HEADS, pltpu.roll(w_max[g, d], LANES - src + ML_HEADS, 1), 0.0))
        gpd = [(g, p, d) for g, p in gps for d in range(N_DIR)]
        ks = {(g, p, d): (kp[g, p] * _dot(sw0[g, d], pick_halves[p, d])).astype(BF16) for g, p, d in gpd}
        vcat = {(g, p): jnp.concatenate([v16[g, 2 * p], v16[g, 2 * p + 1], ones_v], axis=-1) for g, p in gps}
        pair_chunks = [(g, p, d, j) for g, p, d in gpd for j in range(G // L)]
        kv = [_dot_tn(ks[g, p, d][j * L:(j + 1) * L], vcat[g, p][j * L:(j + 1) * L]) for g, p, d, j in pair_chunks]
        for (g, p, d, j), t in zip(pair_chunks, kv):
            top = jnp.concatenate([t[:ML_QK_DIM, :ML_V_DIM], t[:ML_QK_DIM, 2 * ML_V_DIM:]], axis=-1)
            bot = jnp.concatenate([t[ML_QK_DIM:, ML_V_DIM:2 * ML_V_DIM], t[ML_QK_DIM:, 2 * ML_V_DIM:]], axis=-1)
            r0 = pl.multiple_of(gidx[g] * (G // L * ML_KV_ROWS) + (j * ML_PAIRS + p) * LANES, LANES)
            kv_ref[d, pl.ds(r0, LANES), :] = jnp.concatenate([top, bot], axis=0).astype(BF16)
        return carry

    lax.fori_loop(0, q_ref.shape[0] // (G * step_groups), step, 0)


def _mlstm_prep(z, zg, gr, *, batch, seq, q_col, k_col, v_col):
    n = batch * seq
    R, G, L = min(PREP_ROWS, seq), PREP_GROUP, SCAN_CHUNK
    steps = seq // R
    rowblk = lambda c: (lambda b, r: (b * steps + r, c))
    dirblk = lambda b, r: (0, b * steps + r, 0)
    kv_rows = ML_KV_ROWS // L
    return pl.pallas_call(
        _mlstm_prep_kernel,
        grid=(batch, steps),
        in_specs=[
            pl.BlockSpec((R, ML_QK_WIDTH), rowblk(q_col // ML_QK_WIDTH)),
            pl.BlockSpec((R, ML_QK_WIDTH), rowblk(k_col // ML_QK_WIDTH)),
            pl.BlockSpec((R, ML_V_WIDTH), rowblk(v_col // ML_V_WIDTH)),
            pl.BlockSpec((R, LANES), rowblk(0)),
            pl.BlockSpec((R // G, 16, G), lambda b, r: (b * steps + r, 0, 0)),
        ],
        out_specs=[
            pl.BlockSpec((R, ML_QK_WIDTH), rowblk(0)),
            pl.BlockSpec((N_DIR, R, ML_HEADS * ML_AUG), dirblk),
            pl.BlockSpec((N_DIR, R * kv_rows, ML_AUG), dirblk),
            pl.BlockSpec((N_DIR, R, 2 * LANES), dirblk),
        ],
        out_shape=[
            jax.ShapeDtypeStruct((n, ML_QK_WIDTH), BF16),
            jax.ShapeDtypeStruct((N_DIR, n, ML_HEADS * ML_AUG), BF16),
            jax.ShapeDtypeStruct((N_DIR, n * kv_rows, ML_AUG), BF16),
            jax.ShapeDtypeStruct((N_DIR, n, 2 * LANES), F32),
        ],
        compiler_params=_params("parallel", "parallel"),
        name="mlstm_prep",
    )(z, z, z, zg, gr)


def _mlstm_scan_kernel(qf, ivf, kvf, auxf, qb, ivb, kvb, auxb, of_ref, ob_ref, c_ref, m_ref, *, batch):
    L, C = SCAN_CHUNK, SCAN_STEP_CHUNKS

    @pl.when(pl.program_id(0) == 0)
    def _():
        c_ref[...] = jnp.zeros_like(c_ref)
        m_ref[...] = jnp.zeros_like(m_ref)

    dirs = ((qf, ivf, kvf, auxf, of_ref), (qb, ivb, kvb, auxb, ob_ref))
    lane = lax.broadcasted_iota(jnp.int32, (L, LANES), 1)
    keep = [jnp.where(lane < ML_QK_DIM, 1.0, 0.0).astype(BF16), jnp.where(lane < ML_QK_DIM, 0.0, 1.0).astype(BF16)]
    first_rows = lax.broadcasted_iota(jnp.int32, (2 * ML_QK_DIM, 1), 0) < ML_QK_DIM
    pair_chains = [(d, b, p) for d in range(N_DIR) for b in range(batch) for p in range(ML_PAIRS)]
    chains = [(d, b, h) for d in range(N_DIR) for b in range(batch) for h in range(ML_HEADS)]
    groups = [(d, b) for d in range(N_DIR) for b in range(batch)]
    H = ML_HEADS
    lane8 = lax.broadcasted_iota(jnp.int32, (SUBLANES, LANES), 1)
    sel_row = lax.broadcasted_iota(jnp.int32, (2 * LANES, LANES), 0) & (LANES - 1)
    lane_select = [_as01(sel_row == j) for j in range(2 * H)]
    state = {c: c_ref[c] for c in pair_chains}
    m_st = {g: m_ref[g] for g in groups}
    for step in range(C):
        chunk = {0: step, 1: C - 1 - step}
        rows = {d: slice(chunk[d] * L, (chunk[d] + 1) * L) for d in range(N_DIR)}
        qc = {}
        for c in pair_chains:
            d, b, p = c
            qp = dirs[d][0][b, rows[d], p * LANES:(p + 1) * LANES]
            qc[c] = _dot(jnp.concatenate([qp * keep[0], qp * keep[1]], axis=0), state[c].astype(BF16))
        w_prev, w_cur, floor = {}, {}, {}
        for g in groups:
            d, b = g
            aux = dirs[d][3][b, rows[d], :]
            x = aux[:, :LANES] + m_st[g][0:1, :]
            y = jnp.maximum(x, aux[:, LANES:])
            w_prev[g] = jnp.exp(x - y)
            w_cur[g] = jnp.exp(aux[:, LANES:] - y)
            floor[g] = jnp.exp(-y)
            y0 = jnp.broadcast_to(y[0:1, :], (SUBLANES, LANES))
            m_st[g] = jnp.where(lane8 < H, pltpu.roll(y0, LANES - H, 1), jnp.where(lane8 < 2 * H, y0, 0.0))
        tiles = [t[g] for g in groups for t in (w_prev, w_cur, floor)]
        per_row = _split_hi_lo(jnp.concatenate(tiles, axis=0))
        per_chunk = _split_hi_lo(jnp.concatenate([t[0:1, :] for t in tiles] + [tiles[0][0:SUBLANES, :]], axis=0))
        bc_row, bc_chunk = [], []
        for p in range(ML_PAIRS):
            both = _dot(per_row, jnp.concatenate([lane_select[2 * p], lane_select[2 * p + 1]], axis=-1))
            bc_row += [both[:, :LANES], both[:, LANES:]]
            both = _dot(per_chunk, jnp.concatenate([lane_select[H + 2 * p], lane_select[H + 2 * p + 1]], axis=-1))
            bc_chunk += [both[:, :LANES], both[:, LANES:]]
        numden = {}
        for c in chains:
            d, b, h = c
            r0 = (h % 2) * L
            t0 = 3 * groups.index((d, b)) * L
            wp = bc_row[h][t0:t0 + L, :]
            wc = bc_row[h][t0 + L:t0 + 2 * L, :]
            iv = dirs[d][1][b, rows[d], h * ML_AUG:(h + 1) * ML_AUG].astype(F32)
            qch = qc[d, b, h // 2][r0:r0 + L, :]
            numden[c] = jnp.concatenate([wp * qch[:, :ML_V_DIM] + wc * iv[:, :ML_V_DIM],
                                         wp * qch[:, ML_V_DIM:] + wc * iv[:, ML_V_DIM:]], axis=-1)
        for c in chains:
            d, b, h = c
            t0 = 3 * groups.index((d, b)) * L
            den = jnp.maximum(jnp.abs(numden[c][:, ML_V_DIM:]), bc_row[h][t0 + 2 * L:t0 + 3 * L, :])
            dirs[d][4][b, rows[d], h * ML_V_DIM:(h + 1) * ML_V_DIM] = (numden[c][:, :ML_V_DIM] / den).astype(BF16)
        for c in pair_chains:
            d, b, p = c
            r0 = (chunk[d] * ML_PAIRS + p) * LANES
            kv = dirs[d][2][b, r0:r0 + LANES, :].astype(F32)
            t0 = 3 * groups.index((d, b))
            cw = jnp.where(first_rows, bc_chunk[2 * p][t0:t0 + 1, :], bc_chunk[2 * p + 1][t0:t0 + 1, :])
            iw = jnp.where(first_rows, bc_chunk[2 * p][t0 + 1:t0 + 2, :], bc_chunk[2 * p + 1][t0 + 1:t0 + 2, :])
            state[c] = jnp.concatenate([cw * state[c][:, :ML_V_DIM] + iw * kv[:, :ML_V_DIM],
                                        cw * state[c][:, ML_V_DIM:] + iw * kv[:, ML_V_DIM:]], axis=-1)
    for c in pair_chains:
        c_ref[c] = state[c]
    for g in groups:
        m_ref[g] = m_st[g]


def _mlstm_scan(z, zg, gr, *, batch, seq, q_col, k_col, v_col):
    n = batch * seq
    T = SCAN_CHUNK * SCAN_STEP_CHUNKS
    steps = seq // T
    kv_rows = ML_KV_ROWS // SCAN_CHUNK
    qs, iv, kv, aux = _mlstm_prep(z, zg, gr, batch=batch, seq=seq, q_col=q_col, k_col=k_col, v_col=v_col)
    qs = qs.reshape(batch, seq, ML_QK_WIDTH)
    iv = iv.reshape(N_DIR, batch, seq, ML_HEADS * ML_AUG)
    kv = kv.reshape(N_DIR, batch, seq * kv_rows, ML_AUG)
    aux = aux.reshape(N_DIR, batch, seq, 2 * LANES)
    fwd = lambda s: s
    bwd = lambda s: steps - 1 - s
    specs = []
    for d, pos in ((0, fwd), (1, bwd)):
        specs += [
            pl.BlockSpec((batch, T, ML_QK_WIDTH), lambda s, pos=pos: (0, pos(s), 0)),
            pl.BlockSpec((None, batch, T, ML_HEADS * ML_AUG), lambda s, d=d, pos=pos: (d, 0, pos(s), 0)),
            pl.BlockSpec((None, batch, T * kv_rows, ML_AUG), lambda s, d=d, pos=pos: (d, 0, pos(s), 0)),
            pl.BlockSpec((None, batch, T, 2 * LANES), lambda s, d=d, pos=pos: (d, 0, pos(s), 0)),
        ]
    out = jax.ShapeDtypeStruct((batch, seq, ML_V_WIDTH), BF16)
    h_f, h_b = pl.pallas_call(
        functools.partial(_mlstm_scan_kernel, batch=batch),
        grid=(steps,),
        in_specs=specs,
        out_specs=[pl.BlockSpec((batch, T, ML_V_WIDTH), lambda s: (0, fwd(s), 0)),
                   pl.BlockSpec((batch, T, ML_V_WIDTH), lambda s: (0, bwd(s), 0))],
        out_shape=[out, out],
        scratch_shapes=[
            pltpu.VMEM((N_DIR, batch, ML_PAIRS, 2 * ML_QK_DIM, ML_AUG), F32),
            pltpu.VMEM((N_DIR, batch, SUBLANES, LANES), F32),
        ],
        compiler_params=_params("arbitrary"),
        name="mlstm_scan",
    )(qs, iv, kv, aux, qs, iv, kv, aux)
    return h_f.reshape(n, ML_V_WIDTH), h_b.reshape(n, ML_V_WIDTH)


def _head_rms_norm(x, g, n_heads, head_dim):
    parts = []
    for h in range(n_heads):
        xh = x[:, h * head_dim:(h + 1) * head_dim]
        ms = jnp.mean(xh * xh, axis=-1, keepdims=True)
        parts.append(xh * lax.rsqrt(ms + NORM_EPS) * g)
    return jnp.concatenate(parts, axis=-1)


def _layer_norm(x, g, b):
    mu = jnp.mean(x, axis=-1, keepdims=True)
    xc = x - mu
    var = jnp.mean(xc * xc, axis=-1, keepdims=True)
    return xc * lax.rsqrt(var + LN_EPS) * g + b


def _swiglu_residual(x, g_ref, w1_ref, w3_ref, w2_ref, fg_ref, final_norm):
    ms = jnp.mean(x * x, axis=-1, keepdims=True)
    hn = (x * lax.rsqrt(ms + NORM_EPS) * g_ref[...]).astype(BF16)
    a = _dot(hn, w1_ref[...])
    b = _dot(hn, w3_ref[...])
    y = x + _dot((a * jax.nn.sigmoid(a) * b).astype(BF16), w2_ref[...])
    if final_norm:
        ms = jnp.mean(y * y, axis=-1, keepdims=True)
        y = y * lax.rsqrt(ms + NORM_EPS) * fg_ref[...]
    return y


def _ffn_operands(layer, g, w1, w3, w2, final_g):
    _, d, f = w1.shape
    pick = (layer, 0, 0)
    specs = [_resident((1, d)), _resident((None, d, f), pick), _resident((None, d, f), pick),
             _resident((None, f, d), pick), _resident((1, d))]
    return specs, (g.reshape(1, d), w1.astype(BF16), w3.astype(BF16), w2.astype(BF16), final_g.reshape(1, d))


def _even_block_kernel(h_ref, c_ref, of_ref, ob_ref, gate_ref, lng_ref, lnb_ref, ng_ref, wa_ref, wb_ref,
                       g_ref, w1_ref, w3_ref, w2_ref, fg_ref, o_ref, *, final_norm):
    ya = _layer_norm(c_ref[...].astype(F32), lng_ref[...], lnb_ref[...])
    ya = ya * jax.nn.sigmoid(ya)
    o = _head_rms_norm(of_ref[...].astype(F32) + ob_ref[...].astype(F32), ng_ref[...], DN_HEADS, DN_HEAD_DIM)
    gate = gate_ref[...].astype(F32)
    o = o * (gate * jax.nn.sigmoid(gate))
    x = h_ref[...] + _dot(ya.astype(BF16), wa_ref[...]) + _dot(o.astype(BF16), wb_ref[...])
    o_ref[...] = _swiglu_residual(x, g_ref, w1_ref, w3_ref, w2_ref, fg_ref, final_norm)


def _even_block(h, conv, o_fwd, o_bwd, z, ln_g, ln_b, norm_g, w_out, ffn, *, gate_col, final_norm):
    n, d = h.shape
    tm = min(ROW_TILE, n)
    gb = gate_col // DN_WIDTH
    row = lambda i: (i, 0)
    ffn_specs, ffn_args = _ffn_operands(*ffn)
    return pl.pallas_call(
        functools.partial(_even_block_kernel, final_norm=final_norm),
        grid=(n // tm,),
        in_specs=[
            pl.BlockSpec((tm, d), row),
            pl.BlockSpec((tm, CONV_CH), row),
            pl.BlockSpec((tm, DN_WIDTH), row),
            pl.BlockSpec((tm, DN_WIDTH), row),
            pl.BlockSpec((tm, DN_WIDTH), lambda i: (i, gb)),
            _resident((1, CONV_CH)),
            _resident((1, CONV_CH)),
            _resident((1, DN_HEAD_DIM)),
            _resident((CONV_CH, d)),
            _resident((DN_WIDTH, d), (CONV_CH // DN_WIDTH, 0)),
        ] + ffn_specs,
        out_specs=pl.BlockSpec((tm, d), row),
        out_shape=jax.ShapeDtypeStruct((n, d), F32),
        compiler_params=_params("parallel"),
        name="even_block",
    )(h, conv, o_fwd, o_bwd, z, ln_g.reshape(1, -1), ln_b.reshape(1, -1), norm_g.reshape(1, -1),
      w_out.astype(BF16), w_out.astype(BF16), *ffn_args)


def _odd_block_kernel(h_ref, u_ref, vp_ref, hf_ref, hb_ref, op_ref, lng_ref, lnb_ref, sgw_ref, sgb_ref, ng_ref,
                      wa_ref, wb_ref, g_ref, w1_ref, w3_ref, w2_ref, fg_ref, o_ref, *, final_norm):
    tm = h_ref.shape[0]
    u = jax.nn.gelu(u_ref[...].astype(F32))
    vv = _layer_norm(jax.nn.gelu(vp_ref[...].astype(F32)), lng_ref[...], lnb_ref[...]).astype(BF16)
    sgb = sgb_ref[...]
    rows = []
    for c in range(tm // SG_CHUNK):
        parts = []
        for g in range(SG_GROUPS):
            blk = vv[c * SG_CHUNK:(c + 1) * SG_CHUNK, g * SG_GROUP_DIM:(g + 1) * SG_GROUP_DIM]
            parts.append(_dot(sgw_ref[g], blk) + sgb[:, g:g + 1])
        rows.append(jnp.concatenate(parts, axis=-1))
    yc = u * jnp.concatenate(rows, axis=0)
    hd = _head_rms_norm(hf_ref[...].astype(F32) + hb_ref[...].astype(F32), ng_ref[...], ML_HEADS, ML_V_DIM)
    hd = hd * jax.nn.sigmoid(op_ref[...].astype(F32))
    x = h_ref[...] + _dot(yc.astype(BF16), wa_ref[...]) + _dot(hd.astype(BF16), wb_ref[...])
    o_ref[...] = _swiglu_residual(x, g_ref, w1_ref, w3_ref, w2_ref, fg_ref, final_norm)


def _odd_block(h, z, h_fwd, h_bwd, ln_g, ln_b, sg_w, sg_b, norm_g, w_out, ffn, *, u_col, v_col, o_col, final_norm):
    n, d = h.shape
    tm = min(ROW_TILE, n)
    row = lambda i: (i, 0)
    ffn_specs, ffn_args = _ffn_operands(*ffn)
    return pl.pallas_call(
        functools.partial(_odd_block_kernel, final_norm=final_norm),
        grid=(n // tm,),
        in_specs=[
            pl.BlockSpec((tm, d), row),
            pl.BlockSpec((tm, SG_WIDTH), lambda i: (i, u_col // SG_WIDTH)),
            pl.BlockSpec((tm, SG_WIDTH), lambda i: (i, v_col // SG_WIDTH)),
            pl.BlockSpec((tm, ML_V_WIDTH), row),
            pl.BlockSpec((tm, ML_V_WIDTH), row),
            pl.BlockSpec((tm, ML_V_WIDTH), lambda i: (i, o_col // ML_V_WIDTH)),
            _resident((1, SG_WIDTH)),
            _resident((1, SG_WIDTH)),
            _resident((SG_GROUPS, SG_CHUNK, SG_CHUNK)),
            _resident((SG_CHUNK, SG_GROUPS)),
            _resident((1, ML_V_DIM)),
            _resident((SG_WIDTH, d)),
            _resident((ML_V_WIDTH, d), (SG_WIDTH // ML_V_WIDTH, 0)),
        ] + ffn_specs,
        out_specs=pl.BlockSpec((tm, d), row),
        out_shape=jax.ShapeDtypeStruct((n, d), F32),
        compiler_params=_params("parallel"),
        name="odd_block",
    )(h, z, z, h_fwd, h_bwd, z, ln_g.reshape(1, -1), ln_b.reshape(1, -1), sg_w.astype(BF16), sg_b.T,
      norm_g.reshape(1, -1), w_out.astype(BF16), w_out.astype(BF16), *ffn_args)


def _gate_weight(w_gate_cols):
    d, c = w_gate_cols.shape
    return jnp.zeros((d, LANES), F32).at[:, :c].set(w_gate_cols).astype(BF16)


def _even_layer(h, j, p, ffn, *, batch, seq, final_norm):
    main = 2 * CONV_CH + 4 * DN_WIDTH
    w_in = p["ev_w_in"][j]
    gate_params = jnp.zeros((SUBLANES, LANES), F32)
    gate_params = gate_params.at[0, 8:16].set(p["ev_dn_a_log"][j].reshape(-1))
    gate_params = gate_params.at[1, 8:16].set(p["ev_dn_dt_bias"][j].reshape(-1))
    z, zg, gr = _in_proj(h, p["mix_norm_g"][2 * j], w_in.astype(BF16), main, _gate_weight(w_in[:, main:]),
                         gate_params, even=True)
    conv = _conv_glu(z, p["ev_conv_w"][j], p["ev_conv_b"][j], batch=batch, seq=seq)
    qkv = _conv_qkv(z, p["ev_dn_conv_w"][j], batch=batch, seq=seq, col0=2 * CONV_CH)
    o_fwd, o_bwd = _delta_scan(qkv, zg, gr, batch=batch, seq=seq)
    return _even_block(h, conv, o_fwd, o_bwd, z, p["ev_conv_ln_g"][j], p["ev_conv_ln_b"][j], p["ev_dn_norm_g"][j],
                       p["ev_w_out"][j], ffn, gate_col=2 * CONV_CH + 3 * DN_WIDTH, final_norm=final_norm)


def _odd_layer(h, j, p, ffn, *, batch, seq, final_norm):
    main = 2 * SG_WIDTH + 2 * ML_QK_WIDTH + 2 * ML_V_WIDTH
    w_in = p["od_w_in"][j]
    gate_params = jnp.zeros((SUBLANES, LANES), F32)
    gate_params = gate_params.at[0, 0:8].set(p["od_ml_i_bias"][j].reshape(-1))
    gate_params = gate_params.at[1, 8:16].set(p["od_ml_f_bias"][j].reshape(-1))
    z, zg, gr = _in_proj(h, p["mix_norm_g"][2 * j + 1], w_in.astype(BF16), main, _gate_weight(w_in[:, main:]),
                         gate_params, even=False)
    q_col = 2 * SG_WIDTH
    k_col = q_col + ML_QK_WIDTH
    v_col = k_col + ML_QK_WIDTH
    o_col = v_col + ML_V_WIDTH
    h_fwd, h_bwd = _mlstm_scan(z, zg, gr, batch=batch, seq=seq, q_col=q_col, k_col=k_col, v_col=v_col)
    return _odd_block(h, z, h_fwd, h_bwd, p["od_sg_ln_g"][j], p["od_sg_ln_b"][j], p["od_sg_w"][j], p["od_sg_b"][j],
                      p["od_ml_norm_g"][j], p["od_w_out"][j], ffn, u_col=0, v_col=SG_WIDTH, o_col=o_col,
                      final_norm=final_norm)


def kernel(x, mix_norm_g, ev_w_in, ev_conv_w, ev_conv_b, ev_conv_ln_g, ev_conv_ln_b, ev_dn_conv_w, ev_dn_a_log, ev_dn_dt_bias, ev_dn_norm_g, ev_w_out, od_w_in, od_sg_ln_g, od_sg_ln_b, od_sg_w, od_sg_b, od_ml_i_bias, od_ml_f_bias, od_ml_norm_g, od_w_out, ffn_norm_g, ffn_w1, ffn_w3, ffn_w2, final_norm_g):
    p = dict(mix_norm_g=mix_norm_g, ev_w_in=ev_w_in, ev_conv_w=ev_conv_w, ev_conv_b=ev_conv_b,
             ev_conv_ln_g=ev_conv_ln_g, ev_conv_ln_b=ev_conv_ln_b, ev_dn_conv_w=ev_dn_conv_w,
             ev_dn_a_log=ev_dn_a_log, ev_dn_dt_bias=ev_dn_dt_bias, ev_dn_norm_g=ev_dn_norm_g, ev_w_out=ev_w_out,
             od_w_in=od_w_in, od_sg_ln_g=od_sg_ln_g, od_sg_ln_b=od_sg_ln_b, od_sg_w=od_sg_w, od_sg_b=od_sg_b,
             od_ml_i_bias=od_ml_i_bias, od_ml_f_bias=od_ml_f_bias, od_ml_norm_g=od_ml_norm_g, od_w_out=od_w_out)
    batch, seq, d = x.shape
    depth = mix_norm_g.shape[0]
    h = x.reshape(batch * seq, d)
    for layer in range(depth):
        ffn = (layer, ffn_norm_g[layer], ffn_w1, ffn_w3, ffn_w2, final_norm_g)
        mixer_layer = _even_layer if layer % 2 == 0 else _odd_layer
        h = mixer_layer(h, layer // 2, p, ffn, batch=batch, seq=seq, final_norm=layer == depth - 1)
    return h.reshape(batch, seq, d)
```

```python
import functools

import jax
import jax.numpy as jnp
from jax import lax
from jax.experimental import pallas as pl
from jax.experimental.pallas import tpu as pltpu

NORM_EPS = 1e-6
LN_EPS = 1e-5
N_DIR = 2
GATE_KIND_LANES = N_DIR * 4
GATE_LANES = 2 * GATE_KIND_LANES

CONV_CH = 512
DN_HEADS = 4
DN_HEAD_DIM = 128
DN_WIDTH = DN_HEADS * DN_HEAD_DIM
SG_GROUPS = 4
SG_GROUP_DIM = 128
SG_WIDTH = SG_GROUPS * SG_GROUP_DIM
SG_CHUNK = 128
ML_HEADS = 4
ML_QK_DIM = 64
ML_V_DIM = 128
ML_QK_WIDTH = ML_HEADS * ML_QK_DIM
ML_V_WIDTH = ML_HEADS * ML_V_DIM

LANES = 128
SUBLANES = 8
VMEM_LIMIT_BYTES = 56 * 1024 * 1024

SCAN_CHUNK = 64
SCAN_STEP_CHUNKS = 4
PREP_GROUP = 128
PREP_ROWS = 1024
DN_PREP_STEP_GROUPS = 2
ML_PREP_STEP_GROUPS = 4
ROW_TILE = 512
IN_ROW_TILE = 1024
CONV_TILE = 128
QKV_CONV_TILES = 16
GLU_CONV_TILES = 4
CONV_PAD = 16
NEG_BIG = -1e30

BF16 = jnp.bfloat16
F32 = jnp.float32


def _params(*sem):
    return pltpu.CompilerParams(dimension_semantics=sem, vmem_limit_bytes=VMEM_LIMIT_BYTES)


def _resident(shape, block_index=None):
    index = (0,) * len(shape) if block_index is None else tuple(block_index)
    return pl.BlockSpec(shape, lambda *_: index, pipeline_mode=pl.Buffered(1))


def _dot(a, b):
    return jnp.dot(a, b, preferred_element_type=F32)


def _dot_nt(a, b):
    return lax.dot_general(a, b, (((1,), (1,)), ((), ())), preferred_element_type=F32)


def _dot_tn(a, b):
    return lax.dot_general(a, b, (((0,), (0,)), ((), ())), preferred_element_type=F32)


def _exact_dot01(t01, x):
    x1 = x.astype(BF16)
    r1 = x - x1.astype(F32)
    x2 = r1.astype(BF16)
    x3 = (r1 - x2.astype(F32)).astype(BF16)
    return _dot(t01, x1) + _dot(t01, x2) + _dot(t01, x3)


def _exact_dot01_r(x, t01):
    x1 = x.astype(BF16)
    r1 = x - x1.astype(F32)
    x2 = r1.astype(BF16)
    x3 = (r1 - x2.astype(F32)).astype(BF16)
    return _dot(x1, t01) + _dot(x2, t01) + _dot(x3, t01)


def _in_proj_kernel(h_ref, g_ref, w_ref, wg_ref, gp_ref, zm_ref, zg_ref, zgt_ref, *, even):
    x = h_ref[...]
    ms = jnp.mean(x * x, axis=-1, keepdims=True)
    hn = (x * lax.rsqrt(ms + NORM_EPS) * g_ref[...]).astype(BF16)
    zm_ref[...] = _dot(hn, w_ref[...]).astype(zm_ref.dtype)
    zg = _dot(hn, wg_ref[...])
    p0 = gp_ref[0:1, :]
    p1 = gp_ref[1:2, :]
    lane = lax.broadcasted_iota(jnp.int32, zg.shape, 1)
    if even:
        first = jax.nn.sigmoid(zg)
        second = -jnp.exp(p0) * jax.nn.softplus(zg + p1)
    else:
        first = zg + p0
        second = jax.nn.log_sigmoid(zg + p1)
    gates = jnp.where(lane < GATE_KIND_LANES, first, second)
    zg_ref[...] = gates
    for s in range(gates.shape[0] // PREP_GROUP):
        zgt_ref[s] = gates[s * PREP_GROUP:(s + 1) * PREP_GROUP, :].T[:GATE_LANES, :]


def _in_proj(h, g, w_in, c, w_gate, gate_params, *, even):
    n, d = h.shape
    assert c % LANES == 0
    tm = min(IN_ROW_TILE, n)
    return pl.pallas_call(
        functools.partial(_in_proj_kernel, even=even),
        grid=(n // tm,),
        in_specs=[
            pl.BlockSpec((tm, d), lambda i: (i, 0)),
            _resident((1, d)),
            _resident((d, c)),
            _resident((d, LANES)),
            _resident((SUBLANES, LANES)),
        ],
        out_specs=[
            pl.BlockSpec((tm, c), lambda i: (i, 0)),
            pl.BlockSpec((tm, LANES), lambda i: (i, 0)),
            pl.BlockSpec((tm // PREP_GROUP, GATE_LANES, PREP_GROUP), lambda i: (i, 0, 0)),
        ],
        out_shape=[jax.ShapeDtypeStruct((n, c), BF16), jax.ShapeDtypeStruct((n, LANES), F32),
                   jax.ShapeDtypeStruct((n // PREP_GROUP, GATE_LANES, PREP_GROUP), F32)],
        compiler_params=_params("parallel"),
        name="in_proj_even" if even else "in_proj_odd",
    )(h, g.reshape(1, d), w_in, w_gate, gate_params)


def _conv_taps(pad_ref, w_ref, o_ref, bias, *, seq, width, post, tiles_per_step=1):
    half = width // 2
    tiles_per_step = min(tiles_per_step, seq // CONV_TILE)
    step_rows = CONV_TILE * tiles_per_step

    def body(i, carry):
        t0 = pl.multiple_of(i * step_rows, step_rows)
        accs = []
        for s in range(tiles_per_step):
            acc = jnp.zeros((CONV_TILE, LANES), F32)
            for j in range(width):
                acc = acc + pad_ref[pl.ds(t0 + (s * CONV_TILE + CONV_PAD - half + j), CONV_TILE), :] * w_ref[j:j + 1, :]
            accs.append(acc if bias is None else acc + bias)
        for s, out in enumerate(post(accs)):
            o_ref[pl.ds(t0 + s * CONV_TILE, CONV_TILE), :] = out.astype(o_ref.dtype)
        return carry

    lax.fori_loop(0, seq // step_rows, body, 0)


def _fill_padded(pad_ref, x, seq):
    zeros = jnp.zeros((CONV_PAD, LANES), F32)
    pad_ref[0:CONV_PAD, :] = zeros
    pad_ref[CONV_PAD + seq:CONV_PAD + seq + CONV_PAD, :] = zeros
    pad_ref[CONV_PAD:CONV_PAD + seq, :] = x


def _conv_glu_kernel(av_ref, ag_ref, w_ref, b_ref, o_ref, pad_ref, *, seq, width):
    _fill_padded(pad_ref, av_ref[...].astype(F32) * jax.nn.sigmoid(ag_ref[...].astype(F32)), seq)
    _conv_taps(pad_ref, w_ref, o_ref, b_ref[...], seq=seq, width=width, post=lambda tiles: tiles,
               tiles_per_step=GLU_CONV_TILES)


def _conv_glu(z, conv_w, conv_b, *, batch, seq):
    width = conv_w.shape[0]
    nblk = CONV_CH // LANES
    return pl.pallas_call(
        functools.partial(_conv_glu_kernel, seq=seq, width=width),
        grid=(batch, nblk),
        in_specs=[
            pl.BlockSpec((seq, LANES), lambda b, c: (b, c)),
            pl.BlockSpec((seq, LANES), lambda b, c: (b, nblk + c)),
            pl.BlockSpec((width, LANES), lambda b, c: (0, c)),
            pl.BlockSpec((1, LANES), lambda b, c: (0, c)),
        ],
        out_specs=pl.BlockSpec((seq, LANES), lambda b, c: (b, c)),
        out_shape=jax.ShapeDtypeStruct((batch * seq, CONV_CH), BF16),
        scratch_shapes=[pltpu.VMEM((seq + 2 * CONV_PAD, LANES), F32)],
        compiler_params=_params("parallel", "parallel"),
        name="conv_glu",
    )(z, z, conv_w, conv_b.reshape(1, CONV_CH))


def _conv_qkv_kernel(x_ref, w_ref, o_ref, pad_ref, *, seq, width, n_norm_blocks):
    _fill_padded(pad_ref, x_ref[...].astype(F32), seq)
    normalise = pl.program_id(1) < n_norm_blocks

    def post(tiles):
        ys = [a * jax.nn.sigmoid(a) for a in tiles]
        sq = [jnp.sum(y * y, axis=-1, keepdims=True) for y in ys]
        inv = [lax.rsqrt(s + NORM_EPS) for s in sq]
        return [jnp.where(normalise, y * r, y) for y, r in zip(ys, inv)]

    _conv_taps(pad_ref, w_ref, o_ref, None, seq=seq, width=width, post=post, tiles_per_step=QKV_CONV_TILES)


def _conv_qkv(z, dn_conv_w, *, batch, seq, col0):
    width = dn_conv_w.shape[0]
    nblk = 3 * DN_WIDTH // LANES
    blk0 = col0 // LANES
    return pl.pallas_call(
        functools.partial(_conv_qkv_kernel, seq=seq, width=width, n_norm_blocks=2 * DN_HEADS),
        grid=(batch, nblk),
        in_specs=[
            pl.BlockSpec((seq, LANES), lambda b, c: (b, blk0 + c)),
            pl.BlockSpec((width, LANES), lambda b, c: (0, c)),
        ],
        out_specs=pl.BlockSpec((seq, LANES), lambda b, c: (b, c)),
        out_shape=jax.ShapeDtypeStruct((batch * seq, 3 * DN_WIDTH), BF16),
        scratch_shapes=[pltpu.VMEM((seq + 2 * CONV_PAD, LANES), F32)],
        compiler_params=_params("parallel", "parallel"),
        name="conv_qkv",
    )(z, dn_conv_w)


def _split_hi_lo(x):
    hi = x.astype(BF16)
    return jnp.concatenate([hi, (x - hi.astype(F32)).astype(BF16)], axis=-1)


def _bf16_all(mats):
    return [m.astype(BF16) for m in mats]


def _unit_tri_inverses(mats, row, col, n):
    eye = (row == col).astype(F32)
    same = (row >> 3) == (col >> 3)
    l8 = [jnp.where(same, a, 0.0) for a in mats]
    l8h = _bf16_all(l8)
    l2h = _bf16_all([_dot(p, p) for p in l8h])
    l4h = _bf16_all([_dot(p, p) for p in l2h])
    x = [eye - p for p in l8]
    x = [xi + _dot(xi.astype(BF16), p) for xi, p in zip(x, l2h)]
    x = [xi + _dot(xi.astype(BF16), p) for xi, p in zip(x, l4h)]
    shift = 3
    while (1 << shift) < n:
        same_next = (row >> (shift + 1)) == (col >> (shift + 1))
        off = jnp.logical_and(same_next, jnp.logical_not(same))
        ch = _bf16_all([jnp.where(off, a, 0.0) for a in mats])
        xh = _bf16_all(x)
        xch = _bf16_all([_dot(p, c) for p, c in zip(xh, ch)])
        x = [xi - _dot(p, q) for xi, p, q in zip(x, xch, xh)]
        same = same_next
        shift += 1
    return x


def _group_masks(backward):
    n = PREP_GROUP
    row = lax.broadcasted_iota(jnp.int32, (n, n), 0)
    col = lax.broadcasted_iota(jnp.int32, (n, n), 1)
    shift = SCAN_CHUNK.bit_length() - 1
    same = (row >> shift) == (col >> shift)
    d = (col - row) if backward else (row - col)
    land = jnp.logical_and
    return land(same, d >= 0), land(same, d > 0), land(same, d <= 0)


def _as01(mask):
    return jnp.where(mask, 1.0, 0.0).astype(BF16)


def _delta_prep_kernel(q_ref, k_ref, v_ref, gc_ref, gr_ref, u_ref, w_ref, qg_ref, kg_ref, at_ref):
    G, L = PREP_GROUP, SCAN_CHUNK
    row = lax.broadcasted_iota(jnp.int32, (G, G), 0)
    col = lax.broadcasted_iota(jnp.int32, (G, G), 1)
    shift = L.bit_length() - 1
    same01 = _as01((row >> shift) == (col >> shift))
    masks = [_group_masks(False), _group_masks(True)]
    cum_c = [_as01(m[0]) for m in masks]
    cum_r = [_as01(m[2]) for m in masks]
    fold = _as01((lax.broadcasted_iota(jnp.int32, (G, L), 0) & (L - 1)) == lax.broadcasted_iota(jnp.int32, (G, L), 1))
    pad = jnp.zeros((G, DN_HEAD_DIM - L), BF16)
    scale = DN_HEAD_DIM ** -0.5

    step_groups = min(DN_PREP_STEP_GROUPS, q_ref.shape[0] // G)

    def step(i, carry):
        groups = range(step_groups)
        rows = [pl.ds(pl.multiple_of((i * step_groups + g) * G, G), G) for g in groups]
        gc = [gc_ref[r, :] for r in rows]
        gr = [gr_ref[i * step_groups + g] for g in groups]
        g_tot = [_exact_dot01(same01, x) for x in gc]
        g_cum_c = [[_exact_dot01(cum_c[d], x) for d in range(N_DIR)] for x in gc]
        g_cum_r = [[_exact_dot01_r(x, cum_r[d]) for d in range(N_DIR)] for x in gr]
        heads = range(DN_HEADS)
        lanes = [slice(h * DN_HEAD_DIM, (h + 1) * DN_HEAD_DIM) for h in heads]
        gh = [(g, h) for g in groups for h in heads]
        chains = [(g, h, d) for g, h in gh for d in range(N_DIR)]
        k16 = {(g, h): k_ref[rows[g], lanes[h]] for g, h in gh}
        q = {(g, h): q_ref[rows[g], lanes[h]].astype(F32) for g, h in gh}
        k = {c: k16[c].astype(F32) for c in gh}
        kk = {c: _dot_nt(k16[c], k16[c]) for c in gh}
        qk = {c: _dot_nt((q[c] * scale).astype(BF16), k16[c]) for c in gh}
        beta, g_c, g_t, decay = {}, {}, {}, {}
        for c in chains:
            g, h, d = c
            cb = d * DN_HEADS + h
            cl = N_DIR * DN_HEADS + cb
            beta[c] = gc[g][:, cb:cb + 1]
            g_c[c] = g_cum_c[g][d][:, cl:cl + 1]
            g_t[c] = g_tot[g][:, cl:cl + 1]
            g_r = g_cum_r[g][d][cl:cl + 1, :]
            decay[c] = jnp.exp(jnp.where(masks[d][0], g_c[c] - g_r, NEG_BIG))
        a = [jnp.where(masks[c[2]][1], kk[c[:2]] * decay[c], 0.0) * beta[c] for c in chains]
        x = _unit_tri_inverses(a, row, col, L)
        eg = {c: jnp.exp(g_c[c]) for c in chains}
        rhs = [jnp.concatenate([v_ref[rows[c[0]], lanes[c[1]]].astype(F32) * beta[c], k[c[:2]] * (beta[c] * eg[c])],
                               axis=-1).astype(BF16) for c in chains]
        uw = [_dot(xi.astype(BF16), r) for xi, r in zip(x, rhs)]
        attn = [jnp.where(masks[c[2]][0], qk[c[:2]] * decay[c], 0.0).astype(BF16) for c in chains]
        at = [_dot(p, fold) for p in attn]
        for c, uwi, ati in zip(chains, uw, at):
            g, h, d = c
            r, sl = rows[g], lanes[h]
            u_ref[d, r, sl] = uwi[:, :DN_HEAD_DIM].astype(BF16)
            w_ref[d, r, sl] = uwi[:, DN_HEAD_DIM:].astype(BF16)
            qg_ref[d, r, sl] = (q[g, h] * (scale * eg[c])).astype(BF16)
            kg_ref[d, r, sl] = (k[g, h] * jnp.exp(g_t[c] - g_c[c])).astype(BF16)
            at_ref[d, r, sl] = jnp.concatenate([ati.astype(BF16), pad], axis=-1)
        return carry

    lax.fori_loop(0, q_ref.shape[0] // (G * step_groups), step, 0)


def _delta_prep(qkv, zg, gr, *, batch, seq):
    n = batch * seq
    R, G = min(PREP_ROWS, seq), PREP_GROUP
    steps = seq // R
    rowblk = lambda c: (lambda b, r: (b * steps + r, c))
    out = jax.ShapeDtypeStruct((N_DIR, n, DN_WIDTH), BF16)
    out_spec = pl.BlockSpec((N_DIR, R, DN_WIDTH), lambda b, r: (0, b * steps + r, 0))
    return pl.pallas_call(
        _delta_prep_kernel,
        grid=(batch, steps),
        in_specs=[
            pl.BlockSpec((R, DN_WIDTH), rowblk(0)),
            pl.BlockSpec((R, DN_WIDTH), rowblk(1)),
            pl.BlockSpec((R, DN_WIDTH), rowblk(2)),
            pl.BlockSpec((R, LANES), rowblk(0)),
            pl.BlockSpec((R // G, GATE_LANES, G), lambda b, r: (b * steps + r, 0, 0)),
        ],
        out_specs=[out_spec] * 5,
        out_shape=[out] * 5,
        compiler_params=_params("parallel", "parallel"),
        name="delta_prep",
    )(qkv, qkv, qkv, zg, gr)


def _delta_scan_kernel(uf, wf, qgf, kgf, atf, zgf, ub, wb, qgb, kgb, atb, zgb, of_ref, ob_ref, s_ref, *, batch):
    L, C = SCAN_CHUNK, SCAN_STEP_CHUNKS

    @pl.when(pl.program_id(0) == 0)
    def _():
        s_ref[...] = jnp.zeros_like(s_ref)

    dirs = ((uf, wf, qgf, kgf, atf, zgf, of_ref), (ub, wb, qgb, kgb, atb, zgb, ob_ref))

    chains = [(d, b, h) for d in range(N_DIR) for b in range(batch) for h in range(DN_HEADS)]
    state = {c: s_ref[c] for c in chains}
    for step in range(C):
        chunk = {0: step, 1: C - 1 - step}
        rows = {d: slice(chunk[d] * L, (chunk[d] + 1) * L) for d in range(N_DIR)}
        egt = {(d, b): jnp.exp(jnp.sum(dirs[d][5][b, rows[d], :], axis=0, keepdims=True))
               for d in range(N_DIR) for b in range(batch)}
        ws, v_new, av = {}, {}, {}
        for c in chains:
            d, b, h = c
            sl = slice(h * DN_HEAD_DIM, (h + 1) * DN_HEAD_DIM)
            wq = jnp.concatenate([dirs[d][1][b, rows[d], sl], dirs[d][2][b, rows[d], sl]], axis=0)
            ws[c] = _dot(wq, state[c].astype(BF16))
        for c in chains:
            d, b, h = c
            sl = slice(h * DN_HEAD_DIM, (h + 1) * DN_HEAD_DIM)
            v_new[c] = (dirs[d][0][b, rows[d], sl].astype(F32) - ws[c][:L]).astype(BF16)
        for c in chains:
            d, b, h = c
            at = dirs[d][4][b, rows[d], h * DN_HEAD_DIM:h * DN_HEAD_DIM + L]
            av[c] = _dot(at, v_new[c])
        for c in chains:
            d, b, h = c
            sl = slice(h * DN_HEAD_DIM, (h + 1) * DN_HEAD_DIM)
            cl = (N_DIR + d) * DN_HEADS + h
            dirs[d][6][b, rows[d], sl] = (ws[c][L:] + av[c]).astype(BF16)
            state[c] = state[c] * egt[d, b][:, cl:cl + 1] + _dot_tn(dirs[d][3][b, rows[d], sl], v_new[c])
    for c in chains:
        s_ref[c] = state[c]


def _delta_scan(qkv, zg, gr, *, batch, seq):
    n = batch * seq
    T = SCAN_CHUNK * SCAN_STEP_CHUNKS
    steps = seq // T
    prepped = [t.reshape(N_DIR, batch, seq, DN_WIDTH) for t in _delta_prep(qkv, zg, gr, batch=batch, seq=seq)]
    zg3 = zg.reshape(batch, seq, LANES)
    fwd = lambda s: s
    bwd = lambda s: steps - 1 - s
    specs = []
    for d, pos in ((0, fwd), (1, bwd)):
        specs += [pl.BlockSpec((None, batch, T, DN_WIDTH), lambda s, d=d, pos=pos: (d, 0, pos(s), 0))] * 5
        specs += [pl.BlockSpec((batch, T, LANES), lambda s, pos=pos: (0, pos(s), 0))]
    out = jax.ShapeDtypeStruct((batch, seq, DN_WIDTH), BF16)
    o_f, o_b = pl.pallas_call(
        functools.partial(_delta_scan_kernel, batch=batch),
        grid=(steps,),
        in_specs=specs,
        out_specs=[pl.BlockSpec((batch, T, DN_WIDTH), lambda s: (0, fwd(s), 0)),
                   pl.BlockSpec((batch, T, DN_WIDTH), lambda s: (0, bwd(s), 0))],
        out_shape=[out, out],
        scratch_shapes=[pltpu.VMEM((N_DIR, batch, DN_HEADS, DN_HEAD_DIM, DN_HEAD_DIM), F32)],
        compiler_params=_params("arbitrary"),
        name="delta_scan",
    )(*prepped, zg3, *prepped, zg3)
    return o_f.reshape(n, DN_WIDTH), o_b.reshape(n, DN_WIDTH)


ML_PAIRS = ML_HEADS // 2
ML_AUG = 2 * ML_V_DIM
ML_KV_ROWS = ML_PAIRS * 2 * ML_QK_DIM


def _lane_cols(cols, width):
    rows = cols[0].shape[0]
    lane = lax.broadcasted_iota(jnp.int32, (rows, width), 1)
    out = jnp.zeros((rows, width), F32)
    for j, c in enumerate(cols):
        out = jnp.where(lane == j, c, out)
    return out


def _mlstm_prep_kernel(q_ref, k_ref, v_ref, gc_ref, gr_ref, qs_ref, iv_ref, kv_ref, aux_ref):
    G, L = PREP_GROUP, SCAN_CHUNK
    row = lax.broadcasted_iota(jnp.int32, (G, G), 0)
    col = lax.broadcasted_iota(jnp.int32, (G, G), 1)
    shift = L.bit_length() - 1
    same01 = _as01((row >> shift) == (col >> shift))
    masks = [_group_masks(False), _group_masks(True)]
    cum_c = [_as01(m[0]) for m in masks]
    cum_r = [_as01(m[2]) for m in masks]
    lane = lax.broadcasted_iota(jnp.int32, (G, LANES), 1)
    first_head = lane < ML_QK_DIM
    sel_row = lax.broadcasted_iota(jnp.int32, (2 * LANES, G), 0) & (LANES - 1)
    sel_col = lax.broadcasted_iota(jnp.int32, (2 * LANES, G), 1)
    gate_lane = lambda h, d: N_DIR * ML_HEADS + d * ML_HEADS + h
    pick_pair = {(p, d): jnp.concatenate([_as01(sel_row == gate_lane(2 * p, d)),
                                          _as01(sel_row == gate_lane(2 * p + 1, d))], axis=-1)
                 for p in range(ML_PAIRS) for d in range(N_DIR)}
    pick_halves = {(p, d): _as01(sel_row == jnp.where(sel_col < ML_QK_DIM, gate_lane(2 * p, d), gate_lane(2 * p + 1, d)))
                   for p in range(ML_PAIRS) for d in range(N_DIR)}
    ones_v = jnp.ones((G, ML_V_DIM), BF16)
    scale = ML_QK_DIM ** -0.5
    heads = range(ML_HEADS)
    chains = [(h, d) for h in heads for d in range(N_DIR)]

    step_groups = min(ML_PREP_STEP_GROUPS, q_ref.shape[0] // G)

    def step(i, carry):
        groups = range(step_groups)
        gidx = [i * step_groups + g for g in groups]
        rows = [pl.ds(pl.multiple_of(x * G, G), G) for x in gidx]
        gc = [gc_ref[r, :] for r in rows]
        gr = [gr_ref[x] for x in gidx]
        bt_c = [_exact_dot01(same01, x) for x in gc]
        bc_c = [[_exact_dot01(cum_c[d], x) for d in range(N_DIR)] for x in gc]
        bc_r = [[_exact_dot01_r(x, cum_r[d]) for d in range(N_DIR)] for x in gr]
        gps = [(g, p) for g in groups for p in range(ML_PAIRS)]
        ghs = [(g, h) for g in groups for h in heads]
        chains = [(g, h, d) for g, h in ghs for d in range(N_DIR)]
        qp = {(g, p): q_ref[rows[g], p * LANES:(p + 1) * LANES].astype(F32) * scale for g, p in gps}
        kp16 = {(g, p): k_ref[rows[g], p * LANES:(p + 1) * LANES] for g, p in gps}
        kp = {c: kp16[c].astype(F32) for c in gps}
        v16 = {(g, h): v_ref[rows[g], h * ML_V_DIM:(h + 1) * ML_V_DIM].astype(BF16) for g, h in ghs}
        for g, p in gps:
            qs_ref[rows[g], p * LANES:(p + 1) * LANES] = qp[g, p].astype(BF16)
        own = [first_head, jnp.logical_not(first_head)]
        scores = {(g, h): _dot_nt(jnp.where(own[h % 2], qp[g, h // 2], 0.0).astype(BF16), kp16[g, h // 2])
                  for g, h in ghs}
        col = lambda c: gate_lane(c[1], c[2])
        b_r = {c: bc_r[c[0]][c[2]][col(c):col(c) + 1, :] for c in chains}
        li_r = {c: gr[c[0]][col(c) - N_DIR * ML_HEADS:col(c) - N_DIR * ML_HEADS + 1, :] for c in chains}
        split = {(g, d): _split_hi_lo(bc_c[g][d]) for g in groups for d in range(N_DIR)}
        b_cb = {}
        for g, p in gps:
            for d in range(N_DIR):
                both = _dot(split[g, d], pick_pair[p, d])
                b_cb[g, 2 * p, d], b_cb[g, 2 * p + 1, d] = both[:, :G], both[:, G:]
        d_mat = {c: jnp.where(masks[c[2]][0], b_cb[c] - b_r[c] + li_r[c], NEG_BIG) for c in chains}
        d_max = {c: jnp.max(d_mat[c], axis=-1, keepdims=True) for c in chains}
        pmat = {c: (jnp.exp(d_mat[c] - d_max[c]) * scores[c[:2]]).astype(BF16) for c in chains}
        iv = {c: _dot(pmat[c], jnp.concatenate([v16[c[:2]], ones_v], axis=-1)) for c in chains}
        for c in chains:
            g, h, d = c
            iv_ref[d, rows[g], h * ML_AUG:(h + 1) * ML_AUG] = iv[c].astype(BF16)
        gd = [(g, d) for g in groups for d in range(N_DIR)]
        li_t = [pltpu.roll(x, N_DIR * ML_HEADS, 1) for x in gc]
        w_end = {(g, d): bt_c[g] - bc_c[g][d] + li_t[g] for g, d in gd}
        w_max = {c: jnp.concatenate([jnp.broadcast_to(jnp.max(w_end[c][j * L:(j + 1) * L], axis=0, keepdims=True),
                                                      (L, LANES)) for j in range(G // L)], axis=0) for c in gd}
        sw0 = {c: _split_hi_lo(jnp.exp(w_end[c] - w_max[c])) for c in gd}
        for g, d in gd:
            src = gate_lane(0, d)
            aux_ref[d, rows[g], :LANES] = jnp.where(
                lane < ML_HEADS, pltpu.roll(bc_c[g][d], LANES - src, 1),
                jnp.where(lane < 2 * ML_HEADS, pltpu.roll(bt_c[g], LANES - src + ML_HEADS, 1), 0.0))
            aux_ref[d, rows[g], LANES:] = jnp.where(
                lane < ML_HEADS, _lane_cols([d_max[g, h, d] for h in heads], LANES),
                jnp.where(lane < 2 * ML_HEADS, pltpu.roll(w_max[g, d], LANES - src + ML_HEADS, 1), 0.0))
        gpd = [(g, p, d) for g, p in gps for d in range(N_DIR)]
        ks = {(g, p, d): (kp[g, p] * _dot(sw0[g, d], pick_halves[p, d])).astype(BF16) for g, p, d in gpd}
        vcat = {(g, p): jnp.concatenate([v16[g, 2 * p], v16[g, 2 * p + 1], ones_v], axis=-1) for g, p in gps}
        pair_chunks = [(g, p, d, j) for g, p, d in gpd for j in range(G // L)]
        kv = [_dot_tn(ks[g, p, d][j * L:(j + 1) * L], vcat[g, p][j * L:(j + 1) * L]) for g, p, d, j in pair_chunks]
        for (g, p, d, j), t in zip(pair_chunks, kv):
            top = jnp.concatenate([t[:ML_QK_DIM, :ML_V_DIM], t[:ML_QK_DIM, 2 * ML_V_DIM:]], axis=-1)
            bot = jnp.concatenate([t[ML_QK_DIM:, ML_V_DIM:2 * ML_V_DIM], t[ML_QK_DIM:, 2 * ML_V_DIM:]], axis=-1)
            r0 = pl.multiple_of(gidx[g] * (G // L * ML_KV_ROWS) + (j * ML_PAIRS + p) * LANES, LANES)
            kv_ref[d, pl.ds(r0, LANES), :] = jnp.concatenate([top, bot], axis=0).astype(BF16)
        return carry

    lax.fori_loop(0, q_ref.shape[0] // (G * step_groups), step, 0)


def _mlstm_prep(z, zg, gr, *, batch, seq, q_col, k_col, v_col):
    n = batch * seq
    R, G, L = min(PREP_ROWS, seq), PREP_GROUP, SCAN_CHUNK
    steps = seq // R
    rowblk = lambda c: (lambda b, r: (b * steps + r, c))
    dirblk = lambda b, r: (0, b * steps + r, 0)
    kv_rows = ML_KV_ROWS // L
    return pl.pallas_call(
        _mlstm_prep_kernel,
        grid=(batch, steps),
        in_specs=[
            pl.BlockSpec((R, ML_QK_WIDTH), rowblk(q_col // ML_QK_WIDTH)),
            pl.BlockSpec((R, ML_QK_WIDTH), rowblk(k_col // ML_QK_WIDTH)),
            pl.BlockSpec((R, ML_V_WIDTH), rowblk(v_col // ML_V_WIDTH)),
            pl.BlockSpec((R, LANES), rowblk(0)),
            pl.BlockSpec((R // G, GATE_LANES, G), lambda b, r: (b * steps + r, 0, 0)),
        ],
        out_specs=[
            pl.BlockSpec((R, ML_QK_WIDTH), rowblk(0)),
            pl.BlockSpec((N_DIR, R, ML_HEADS * ML_AUG), dirblk),
            pl.BlockSpec((N_DIR, R * kv_rows, ML_AUG), dirblk),
            pl.BlockSpec((N_DIR, R, 2 * LANES), dirblk),
        ],
        out_shape=[
            jax.ShapeDtypeStruct((n, ML_QK_WIDTH), BF16),
            jax.ShapeDtypeStruct((N_DIR, n, ML_HEADS * ML_AUG), BF16),
            jax.ShapeDtypeStruct((N_DIR, n * kv_rows, ML_AUG), BF16),
            jax.ShapeDtypeStruct((N_DIR, n, 2 * LANES), F32),
        ],
        compiler_params=_params("parallel", "parallel"),
        name="mlstm_prep",
    )(z, z, z, zg, gr)


def _mlstm_scan_kernel(qf, ivf, kvf, auxf, qb, ivb, kvb, auxb, of_ref, ob_ref, c_ref, m_ref, *, batch):
    L, C = SCAN_CHUNK, SCAN_STEP_CHUNKS

    @pl.when(pl.program_id(0) == 0)
    def _():
        c_ref[...] = jnp.zeros_like(c_ref)
        m_ref[...] = jnp.zeros_like(m_ref)

    dirs = ((qf, ivf, kvf, auxf, of_ref), (qb, ivb, kvb, auxb, ob_ref))
    lane = lax.broadcasted_iota(jnp.int32, (L, LANES), 1)
    keep = [jnp.where(lane < ML_QK_DIM, 1.0, 0.0).astype(BF16), jnp.where(lane < ML_QK_DIM, 0.0, 1.0).astype(BF16)]
    first_rows = lax.broadcasted_iota(jnp.int32, (2 * ML_QK_DIM, 1), 0) < ML_QK_DIM
    pair_chains = [(d, b, p) for d in range(N_DIR) for b in range(batch) for p in range(ML_PAIRS)]
    chains = [(d, b, h) for d in range(N_DIR) for b in range(batch) for h in range(ML_HEADS)]
    groups = [(d, b) for d in range(N_DIR) for b in range(batch)]
    H = ML_HEADS
    lane8 = lax.broadcasted_iota(jnp.int32, (SUBLANES, LANES), 1)
    sel_row = lax.broadcasted_iota(jnp.int32, (2 * LANES, LANES), 0) & (LANES - 1)
    lane_select = [_as01(sel_row == j) for j in range(2 * H)]
    state = {c: c_ref[c] for c in pair_chains}
    m_st = {g: m_ref[g] for g in groups}
    for step in range(C):
        chunk = {0: step, 1: C - 1 - step}
        rows = {d: slice(chunk[d] * L, (chunk[d] + 1) * L) for d in range(N_DIR)}
        qc = {}
        for c in pair_chains:
            d, b, p = c
            qp = dirs[d][0][b, rows[d], p * LANES:(p + 1) * LANES]
            qc[c] = _dot(jnp.concatenate([qp * keep[0], qp * keep[1]], axis=0), state[c].astype(BF16))
        w_prev, w_cur, floor = {}, {}, {}
        for g in groups:
            d, b = g
            aux = dirs[d][3][b, rows[d], :]
            x = aux[:, :LANES] + m_st[g][0:1, :]
            y = jnp.maximum(x, aux[:, LANES:])
            w_prev[g] = jnp.exp(x - y)
            w_cur[g] = jnp.exp(aux[:, LANES:] - y)
            floor[g] = jnp.exp(-y)
            y0 = jnp.broadcast_to(y[0:1, :], (SUBLANES, LANES))
            m_st[g] = jnp.where(lane8 < H, pltpu.roll(y0, LANES - H, 1), jnp.where(lane8 < 2 * H, y0, 0.0))
        tiles = [t[g] for g in groups for t in (w_prev, w_cur, floor)]
        per_row = _split_hi_lo(jnp.concatenate(tiles, axis=0))
        per_chunk = _split_hi_lo(jnp.concatenate([t[0:1, :] for t in tiles] + [tiles[0][0:SUBLANES, :]], axis=0))
        bc_row, bc_chunk = [], []
        for p in range(ML_PAIRS):
            both = _dot(per_row, jnp.concatenate([lane_select[2 * p], lane_select[2 * p + 1]], axis=-1))
            bc_row += [both[:, :LANES], both[:, LANES:]]
            both = _dot(per_chunk, jnp.concatenate([lane_select[H + 2 * p], lane_select[H + 2 * p + 1]], axis=-1))
            bc_chunk += [both[:, :LANES], both[:, LANES:]]
        numden = {}
        for c in chains:
            d, b, h = c
            r0 = (h % 2) * L
            t0 = 3 * groups.index((d, b)) * L
            wp = bc_row[h][t0:t0 + L, :]
            wc = bc_row[h][t0 + L:t0 + 2 * L, :]
            iv = dirs[d][1][b, rows[d], h * ML_AUG:(h + 1) * ML_AUG].astype(F32)
            qch = qc[d, b, h // 2][r0:r0 + L, :]
            numden[c] = jnp.concatenate([wp * qch[:, :ML_V_DIM] + wc * iv[:, :ML_V_DIM],
                                         wp * qch[:, ML_V_DIM:] + wc * iv[:, ML_V_DIM:]], axis=-1)
        for c in chains:
            d, b, h = c
            t0 = 3 * groups.index((d, b)) * L
            den = jnp.maximum(jnp.abs(numden[c][:, ML_V_DIM:]), bc_row[h][t0 + 2 * L:t0 + 3 * L, :])
            dirs[d][4][b, rows[d], h * ML_V_DIM:(h + 1) * ML_V_DIM] = (numden[c][:, :ML_V_DIM] / den).astype(BF16)
        for c in pair_chains:
            d, b, p = c
            r0 = (chunk[d] * ML_PAIRS + p) * LANES
            kv = dirs[d][2][b, r0:r0 + LANES, :].astype(F32)
            t0 = 3 * groups.index((d, b))
            cw = jnp.where(first_rows, bc_chunk[2 * p][t0:t0 + 1, :], bc_chunk[2 * p + 1][t0:t0 + 1, :])
            iw = jnp.where(first_rows, bc_chunk[2 * p][t0 + 1:t0 + 2, :], bc_chunk[2 * p + 1][t0 + 1:t0 + 2, :])
            state[c] = jnp.concatenate([cw * state[c][:, :ML_V_DIM] + iw * kv[:, :ML_V_DIM],
                                        cw * state[c][:, ML_V_DIM:] + iw * kv[:, ML_V_DIM:]], axis=-1)
    for c in pair_chains:
        c_ref[c] = state[c]
    for g in groups:
        m_ref[g] = m_st[g]


def _mlstm_scan(z, zg, gr, *, batch, seq, q_col, k_col, v_col):
    n = batch * seq
    T = SCAN_CHUNK * SCAN_STEP_CHUNKS
    steps = seq // T
    kv_rows = ML_KV_ROWS // SCAN_CHUNK
    qs, iv, kv, aux = _mlstm_prep(z, zg, gr, batch=batch, seq=seq, q_col=q_col, k_col=k_col, v_col=v_col)
    qs = qs.reshape(batch, seq, ML_QK_WIDTH)
    iv = iv.reshape(N_DIR, batch, seq, ML_HEADS * ML_AUG)
    kv = kv.reshape(N_DIR, batch, seq * kv_rows, ML_AUG)
    aux = aux.reshape(N_DIR, batch, seq, 2 * LANES)
    fwd = lambda s: s
    bwd = lambda s: steps - 1 - s
    specs = []
    for d, pos in ((0, fwd), (1, bwd)):
        specs += [
            pl.BlockSpec((batch, T, ML_QK_WIDTH), lambda s, pos=pos: (0, pos(s), 0)),
            pl.BlockSpec((None, batch, T, ML_HEADS * ML_AUG), lambda s, d=d, pos=pos: (d, 0, pos(s), 0)),
            pl.BlockSpec((None, batch, T * kv_rows, ML_AUG), lambda s, d=d, pos=pos: (d, 0, pos(s), 0)),
            pl.BlockSpec((None, batch, T, 2 * LANES), lambda s, d=d, pos=pos: (d, 0, pos(s), 0)),
        ]
    out = jax.ShapeDtypeStruct((batch, seq, ML_V_WIDTH), BF16)
    h_f, h_b = pl.pallas_call(
        functools.partial(_mlstm_scan_kernel, batch=batch),
        grid=(steps,),
        in_specs=specs,
        out_specs=[pl.BlockSpec((batch, T, ML_V_WIDTH), lambda s: (0, fwd(s), 0)),
                   pl.BlockSpec((batch, T, ML_V_WIDTH), lambda s: (0, bwd(s), 0))],
        out_shape=[out, out],
        scratch_shapes=[
            pltpu.VMEM((N_DIR, batch, ML_PAIRS, 2 * ML_QK_DIM, ML_AUG), F32),
            pltpu.VMEM((N_DIR, batch, SUBLANES, LANES), F32),
        ],
        compiler_params=_params("arbitrary"),
        name="mlstm_scan",
    )(qs, iv, kv, aux, qs, iv, kv, aux)
    return h_f.reshape(n, ML_V_WIDTH), h_b.reshape(n, ML_V_WIDTH)


def _head_rms_norm(x, g, n_heads, head_dim):
    parts = []
    for h in range(n_heads):
        xh = x[:, h * head_dim:(h + 1) * head_dim]
        ms = jnp.mean(xh * xh, axis=-1, keepdims=True)
        parts.append(xh * lax.rsqrt(ms + NORM_EPS) * g)
    return jnp.concatenate(parts, axis=-1)


def _layer_norm(x, g, b):
    mu = jnp.mean(x, axis=-1, keepdims=True)
    xc = x - mu
    var = jnp.mean(xc * xc, axis=-1, keepdims=True)
    return xc * lax.rsqrt(var + LN_EPS) * g + b


def _swiglu_residual(x, g_ref, w1_ref, w3_ref, w2_ref, fg_ref, final_norm):
    ms = jnp.mean(x * x, axis=-1, keepdims=True)
    hn = (x * lax.rsqrt(ms + NORM_EPS) * g_ref[...]).astype(BF16)
    a = _dot(hn, w1_ref[...])
    b = _dot(hn, w3_ref[...])
    y = x + _dot((a * jax.nn.sigmoid(a) * b).astype(BF16), w2_ref[...])
    if final_norm:
        ms = jnp.mean(y * y, axis=-1, keepdims=True)
        y = y * lax.rsqrt(ms + NORM_EPS) * fg_ref[...]
    return y


def _ffn_operands(layer, g, w1, w3, w2, final_g):
    _, d, f = w1.shape
    pick = (layer, 0, 0)
    specs = [_resident((1, d)), _resident((None, d, f), pick), _resident((None, d, f), pick),
             _resident((None, f, d), pick), _resident((1, d))]
    return specs, (g.reshape(1, d), w1.astype(BF16), w3.astype(BF16), w2.astype(BF16), final_g.reshape(1, d))


def _even_block_kernel(h_ref, c_ref, of_ref, ob_ref, gate_ref, lng_ref, lnb_ref, ng_ref, wa_ref, wb_ref,
                       g_ref, w1_ref, w3_ref, w2_ref, fg_ref, o_ref, *, final_norm):
    ya = _layer_norm(c_ref[...].astype(F32), lng_ref[...], lnb_ref[...])
    ya = ya * jax.nn.sigmoid(ya)
    o = _head_rms_norm(of_ref[...].astype(F32) + ob_ref[...].astype(F32), ng_ref[...], DN_HEADS, DN_HEAD_DIM)
    gate = gate_ref[...].astype(F32)
    o = o * (gate * jax.nn.sigmoid(gate))
    x = h_ref[...] + _dot(ya.astype(BF16), wa_ref[...]) + _dot(o.astype(BF16), wb_ref[...])
    o_ref[...] = _swiglu_residual(x, g_ref, w1_ref, w3_ref, w2_ref, fg_ref, final_norm)


def _even_block(h, conv, o_fwd, o_bwd, z, ln_g, ln_b, norm_g, w_out, ffn, *, gate_col, final_norm):
    n, d = h.shape
    tm = min(ROW_TILE, n)
    gb = gate_col // DN_WIDTH
    row = lambda i: (i, 0)
    ffn_specs, ffn_args = _ffn_operands(*ffn)
    return pl.pallas_call(
        functools.partial(_even_block_kernel, final_norm=final_norm),
        grid=(n // tm,),
        in_specs=[
            pl.BlockSpec((tm, d), row),
            pl.BlockSpec((tm, CONV_CH), row),
            pl.BlockSpec((tm, DN_WIDTH), row),
            pl.BlockSpec((tm, DN_WIDTH), row),
            pl.BlockSpec((tm, DN_WIDTH), lambda i: (i, gb)),
            _resident((1, CONV_CH)),
            _resident((1, CONV_CH)),
            _resident((1, DN_HEAD_DIM)),
            _resident((CONV_CH, d)),
            _resident((DN_WIDTH, d), (CONV_CH // DN_WIDTH, 0)),
        ] + ffn_specs,
        out_specs=pl.BlockSpec((tm, d), row),
        out_shape=jax.ShapeDtypeStruct((n, d), F32),
        compiler_params=_params("parallel"),
        name="even_block",
    )(h, conv, o_fwd, o_bwd, z, ln_g.reshape(1, -1), ln_b.reshape(1, -1), norm_g.reshape(1, -1),
      w_out.astype(BF16), w_out.astype(BF16), *ffn_args)


def _odd_block_kernel(h_ref, u_ref, vp_ref, hf_ref, hb_ref, op_ref, lng_ref, lnb_ref, sgw_ref, sgb_ref, ng_ref,
                      wa_ref, wb_ref, g_ref, w1_ref, w3_ref, w2_ref, fg_ref, o_ref, *, final_norm):
    tm = h_ref.shape[0]
    u = jax.nn.gelu(u_ref[...].astype(F32))
    vv = _layer_norm(jax.nn.gelu(vp_ref[...].astype(F32)), lng_ref[...], lnb_ref[...]).astype(BF16)
    sgb = sgb_ref[...]
    rows = []
    for c in range(tm // SG_CHUNK):
        parts = []
        for g in range(SG_GROUPS):
            blk = vv[c * SG_CHUNK:(c + 1) * SG_CHUNK, g * SG_GROUP_DIM:(g + 1) * SG_GROUP_DIM]
            parts.append(_dot(sgw_ref[g], blk) + sgb[:, g:g + 1])
        rows.append(jnp.concatenate(parts, axis=-1))
    yc = u * jnp.concatenate(rows, axis=0)
    hd = _head_rms_norm(hf_ref[...].astype(F32) + hb_ref[...].astype(F32), ng_ref[...], ML_HEADS, ML_V_DIM)
    hd = hd * jax.nn.sigmoid(op_ref[...].astype(F32))
    x = h_ref[...] + _dot(yc.astype(BF16), wa_ref[...]) + _dot(hd.astype(BF16), wb_ref[...])
    o_ref[...] = _swiglu_residual(x, g_ref, w1_ref, w3_ref, w2_ref, fg_ref, final_norm)


def _odd_block(h, z, h_fwd, h_bwd, ln_g, ln_b, sg_w, sg_b, norm_g, w_out, ffn, *, u_col, v_col, o_col, final_norm):
    n, d = h.shape
    tm = min(ROW_TILE, n)
    row = lambda i: (i, 0)
    ffn_specs, ffn_args = _ffn_operands(*ffn)
    return pl.pallas_call(
        functools.partial(_odd_block_kernel, final_norm=final_norm),
        grid=(n // tm,),
        in_specs=[
            pl.BlockSpec((tm, d), row),
            pl.BlockSpec((tm, SG_WIDTH), lambda i: (i, u_col // SG_WIDTH)),
            pl.BlockSpec((tm, SG_WIDTH), lambda i: (i, v_col // SG_WIDTH)),
            pl.BlockSpec((tm, ML_V_WIDTH), row),
            pl.BlockSpec((tm, ML_V_WIDTH), row),
            pl.BlockSpec((tm, ML_V_WIDTH), lambda i: (i, o_col // ML_V_WIDTH)),
            _resident((1, SG_WIDTH)),
            _resident((1, SG_WIDTH)),
            _resident((SG_GROUPS, SG_CHUNK, SG_CHUNK)),
            _resident((SG_CHUNK, SG_GROUPS)),
            _resident((1, ML_V_DIM)),
            _resident((SG_WIDTH, d)),
            _resident((ML_V_WIDTH, d), (SG_WIDTH // ML_V_WIDTH, 0)),
        ] + ffn_specs,
        out_specs=pl.BlockSpec((tm, d), row),
        out_shape=jax.ShapeDtypeStruct((n, d), F32),
        compiler_params=_params("parallel"),
        name="odd_block",
    )(h, z, z, h_fwd, h_bwd, z, ln_g.reshape(1, -1), ln_b.reshape(1, -1), sg_w.astype(BF16), sg_b.T,
      norm_g.reshape(1, -1), w_out.astype(BF16), w_out.astype(BF16), *ffn_args)


def _gate_weight(w_gate_cols):
    d, c = w_gate_cols.shape
    return jnp.zeros((d, LANES), F32).at[:, :c].set(w_gate_cols).astype(BF16)


def _gate_params(first_kind, second_kind):
    table = jnp.zeros((SUBLANES, LANES), F32)
    for r, (kind, values) in enumerate((first_kind, second_kind)):
        table = table.at[r, kind * GATE_KIND_LANES:(kind + 1) * GATE_KIND_LANES].set(values.reshape(-1))
    return table


def _even_layer(h, j, p, ffn, *, batch, seq, final_norm):
    main = 2 * CONV_CH + 4 * DN_WIDTH
    w_in = p["ev_w_in"][j]
    gate_params = _gate_params((1, p["ev_dn_a_log"][j]), (1, p["ev_dn_dt_bias"][j]))
    z, zg, gr = _in_proj(h, p["mix_norm_g"][2 * j], w_in.astype(BF16), main, _gate_weight(w_in[:, main:]),
                         gate_params, even=True)
    conv = _conv_glu(z, p["ev_conv_w"][j], p["ev_conv_b"][j], batch=batch, seq=seq)
    qkv = _conv_qkv(z, p["ev_dn_conv_w"][j], batch=batch, seq=seq, col0=2 * CONV_CH)
    o_fwd, o_bwd = _delta_scan(qkv, zg, gr, batch=batch, seq=seq)
    return _even_block(h, conv, o_fwd, o_bwd, z, p["ev_conv_ln_g"][j], p["ev_conv_ln_b"][j], p["ev_dn_norm_g"][j],
                       p["ev_w_out"][j], ffn, gate_col=2 * CONV_CH + 3 * DN_WIDTH, final_norm=final_norm)


def _odd_layer(h, j, p, ffn, *, batch, seq, final_norm):
    main = 2 * SG_WIDTH + 2 * ML_QK_WIDTH + 2 * ML_V_WIDTH
    w_in = p["od_w_in"][j]
    gate_params = _gate_params((0, p["od_ml_i_bias"][j]), (1, p["od_ml_f_bias"][j]))
    z, zg, gr = _in_proj(h, p["mix_norm_g"][2 * j + 1], w_in.astype(BF16), main, _gate_weight(w_in[:, main:]),
                         gate_params, even=False)
    q_col = 2 * SG_WIDTH
    k_col = q_col + ML_QK_WIDTH
    v_col = k_col + ML_QK_WIDTH
    o_col = v_col + ML_V_WIDTH
    h_fwd, h_bwd = _mlstm_scan(z, zg, gr, batch=batch, seq=seq, q_col=q_col, k_col=k_col, v_col=v_col)
    return _odd_block(h, z, h_fwd, h_bwd, p["od_sg_ln_g"][j], p["od_sg_ln_b"][j], p["od_sg_w"][j], p["od_sg_b"][j],
                      p["od_ml_norm_g"][j], p["od_w_out"][j], ffn, u_col=0, v_col=SG_WIDTH, o_col=o_col,
                      final_norm=final_norm)


def kernel(x, mix_norm_g, ev_w_in, ev_conv_w, ev_conv_b, ev_conv_ln_g, ev_conv_ln_b, ev_dn_conv_w, ev_dn_a_log, ev_dn_dt_bias, ev_dn_norm_g, ev_w_out, od_w_in, od_sg_ln_g, od_sg_ln_b, od_sg_w, od_sg_b, od_ml_i_bias, od_ml_f_bias, od_ml_norm_g, od_w_out, ffn_norm_g, ffn_w1, ffn_w3, ffn_w2, final_norm_g):
    p = dict(mix_norm_g=mix_norm_g, ev_w_in=ev_w_in, ev_conv_w=ev_conv_w, ev_conv_b=ev_conv_b,
             ev_conv_ln_g=ev_conv_ln_g, ev_conv_ln_b=ev_conv_ln_b, ev_dn_conv_w=ev_dn_conv_w,
             ev_dn_a_log=ev_dn_a_log, ev_dn_dt_bias=ev_dn_dt_bias, ev_dn_norm_g=ev_dn_norm_g, ev_w_out=ev_w_out,
             od_w_in=od_w_in, od_sg_ln_g=od_sg_ln_g, od_sg_ln_b=od_sg_ln_b, od_sg_w=od_sg_w, od_sg_b=od_sg_b,
             od_ml_i_bias=od_ml_i_bias, od_ml_f_bias=od_ml_f_bias, od_ml_norm_g=od_ml_norm_g, od_w_out=od_w_out)
    batch, seq, d = x.shape
    depth = mix_norm_g.shape[0]
    h = x.reshape(batch * seq, d)
    for layer in range(depth):
        ffn = (layer, ffn_norm_g[layer], ffn_w1, ffn_w3, ffn_w2, final_norm_g)
        mixer_layer = _even_layer if layer % 2 == 0 else _odd_layer
        h = mixer_layer(h, layer // 2, p, ffn, batch=batch, seq=seq, final_norm=layer == depth - 1)
    return h.reshape(batch, seq, d)
```

```python
import functools

import jax
import jax.numpy as jnp
from jax import lax
from jax.experimental import pallas as pl
from jax.experimental.pallas import tpu as pltpu

NORM_EPS = 1e-6
LN_EPS = 1e-5
N_DIR = 2
GATE_KIND_LANES = N_DIR * 4
GATE_LANES = 2 * GATE_KIND_LANES

CONV_CH = 512
DN_HEADS = 4
DN_HEAD_DIM = 128
DN_WIDTH = DN_HEADS * DN_HEAD_DIM
SG_GROUPS = 4
SG_GROUP_DIM = 128
SG_WIDTH = SG_GROUPS * SG_GROUP_DIM
SG_CHUNK = 128
ML_HEADS = 4
ML_QK_DIM = 64
ML_V_DIM = 128
ML_QK_WIDTH = ML_HEADS * ML_QK_DIM
ML_V_WIDTH = ML_HEADS * ML_V_DIM

LANES = 128
SUBLANES = 8
VMEM_LIMIT_BYTES = 56 * 1024 * 1024

SCAN_CHUNK = 64
SCAN_STEP_CHUNKS = 4
PREP_GROUP = 128
PREP_ROWS = 1024
DN_PREP_STEP_GROUPS = 2
ML_PREP_STEP_GROUPS = 4
ROW_TILE = 512
IN_ROW_TILE = 1024
CONV_TILE = 128
QKV_CONV_TILES = 16
GLU_CONV_TILES = 4
CONV_PAD = 16
NEG_BIG = -1e30

BF16 = jnp.bfloat16
F32 = jnp.float32


def _params(*sem):
    return pltpu.CompilerParams(dimension_semantics=sem, vmem_limit_bytes=VMEM_LIMIT_BYTES)


def _resident(shape, block_index=None):
    index = (0,) * len(shape) if block_index is None else tuple(block_index)
    return pl.BlockSpec(shape, lambda *_: index, pipeline_mode=pl.Buffered(1))


def _dot(a, b):
    return jnp.dot(a, b, preferred_element_type=F32)


def _dot_nt(a, b):
    return lax.dot_general(a, b, (((1,), (1,)), ((), ())), preferred_element_type=F32)


def _dot_tn(a, b):
    return lax.dot_general(a, b, (((0,), (0,)), ((), ())), preferred_element_type=F32)


def _exact_dot01(t01, x):
    x1 = x.astype(BF16)
    r1 = x - x1.astype(F32)
    x2 = r1.astype(BF16)
    x3 = (r1 - x2.astype(F32)).astype(BF16)
    return _dot(t01, x1) + _dot(t01, x2) + _dot(t01, x3)


def _exact_dot01_r(x, t01):
    x1 = x.astype(BF16)
    r1 = x - x1.astype(F32)
    x2 = r1.astype(BF16)
    x3 = (r1 - x2.astype(F32)).astype(BF16)
    return _dot(x1, t01) + _dot(x2, t01) + _dot(x3, t01)


def _in_proj_kernel(h_ref, g_ref, w_ref, wg_ref, gp_ref, zm_ref, zg_ref, zgt_ref, *, even):
    x = h_ref[...]
    ms = jnp.mean(x * x, axis=-1, keepdims=True)
    hn = (x * lax.rsqrt(ms + NORM_EPS) * g_ref[...]).astype(BF16)
    zm_ref[...] = _dot(hn, w_ref[...].astype(BF16)).astype(zm_ref.dtype)
    zg = _dot(hn, wg_ref[...])
    p0 = gp_ref[0:1, :]
    p1 = gp_ref[1:2, :]
    lane = lax.broadcasted_iota(jnp.int32, zg.shape, 1)
    if even:
        first = jax.nn.sigmoid(zg)
        second = -jnp.exp(p0) * jax.nn.softplus(zg + p1)
    else:
        first = zg + p0
        second = jax.nn.log_sigmoid(zg + p1)
    gates = jnp.where(lane < GATE_KIND_LANES, first, second)
    zg_ref[...] = gates
    for s in range(gates.shape[0] // PREP_GROUP):
        zgt_ref[s] = gates[s * PREP_GROUP:(s + 1) * PREP_GROUP, :].T[:GATE_LANES, :]


def _in_proj(h, g, w_in, c, w_gate, gate_params, *, even):
    n, d = h.shape
    assert c % LANES == 0
    tm = min(IN_ROW_TILE, n)
    return pl.pallas_call(
        functools.partial(_in_proj_kernel, even=even),
        grid=(n // tm,),
        in_specs=[
            pl.BlockSpec((tm, d), lambda i: (i, 0)),
            _resident((1, d)),
            _resident((d, c)),
            _resident((d, LANES)),
            _resident((SUBLANES, LANES)),
        ],
        out_specs=[
            pl.BlockSpec((tm, c), lambda i: (i, 0)),
            pl.BlockSpec((tm, LANES), lambda i: (i, 0)),
            pl.BlockSpec((tm // PREP_GROUP, GATE_LANES, PREP_GROUP), lambda i: (i, 0, 0)),
        ],
        out_shape=[jax.ShapeDtypeStruct((n, c), BF16), jax.ShapeDtypeStruct((n, LANES), F32),
                   jax.ShapeDtypeStruct((n // PREP_GROUP, GATE_LANES, PREP_GROUP), F32)],
        compiler_params=_params("parallel"),
        name="in_proj_even" if even else "in_proj_odd",
    )(h, g.reshape(1, d), w_in, w_gate, gate_params)


def _conv_taps(pad_ref, w_ref, o_ref, bias, *, seq, width, post, tiles_per_step=1):
    half = width // 2
    tiles_per_step = min(tiles_per_step, seq // CONV_TILE)
    step_rows = CONV_TILE * tiles_per_step

    def body(i, carry):
        t0 = pl.multiple_of(i * step_rows, step_rows)
        accs = []
        for s in range(tiles_per_step):
            acc = jnp.zeros((CONV_TILE, LANES), F32)
            for j in range(width):
                acc = acc + pad_ref[pl.ds(t0 + (s * CONV_TILE + CONV_PAD - half + j), CONV_TILE), :] * w_ref[j:j + 1, :]
            accs.append(acc if bias is None else acc + bias)
        for s, out in enumerate(post(accs)):
            o_ref[pl.ds(t0 + s * CONV_TILE, CONV_TILE), :] = out.astype(o_ref.dtype)
        return carry

    lax.fori_loop(0, seq // step_rows, body, 0)


def _fill_padded(pad_ref, x, seq):
    zeros = jnp.zeros((CONV_PAD, LANES), F32)
    pad_ref[0:CONV_PAD, :] = zeros
    pad_ref[CONV_PAD + seq:CONV_PAD + seq + CONV_PAD, :] = zeros
    pad_ref[CONV_PAD:CONV_PAD + seq, :] = x


def _conv_glu_kernel(av_ref, ag_ref, w_ref, b_ref, o_ref, pad_ref, *, seq, width):
    _fill_padded(pad_ref, av_ref[...].astype(F32) * jax.nn.sigmoid(ag_ref[...].astype(F32)), seq)
    _conv_taps(pad_ref, w_ref, o_ref, b_ref[...], seq=seq, width=width, post=lambda tiles: tiles,
               tiles_per_step=GLU_CONV_TILES)


def _conv_glu(z, conv_w, conv_b, *, batch, seq):
    width = conv_w.shape[0]
    nblk = CONV_CH // LANES
    return pl.pallas_call(
        functools.partial(_conv_glu_kernel, seq=seq, width=width),
        grid=(batch, nblk),
        in_specs=[
            pl.BlockSpec((seq, LANES), lambda b, c: (b, c)),
            pl.BlockSpec((seq, LANES), lambda b, c: (b, nblk + c)),
            pl.BlockSpec((width, LANES), lambda b, c: (0, c)),
            pl.BlockSpec((1, LANES), lambda b, c: (0, c)),
        ],
        out_specs=pl.BlockSpec((seq, LANES), lambda b, c: (b, c)),
        out_shape=jax.ShapeDtypeStruct((batch * seq, CONV_CH), BF16),
        scratch_shapes=[pltpu.VMEM((seq + 2 * CONV_PAD, LANES), F32)],
        compiler_params=_params("parallel", "parallel"),
        name="conv_glu",
    )(z, z, conv_w, conv_b.reshape(1, CONV_CH))


def _conv_qkv_kernel(x_ref, w_ref, o_ref, pad_ref, *, seq, width, n_norm_blocks):
    _fill_padded(pad_ref, x_ref[...].astype(F32), seq)
    normalise = pl.program_id(1) < n_norm_blocks

    def post(tiles):
        ys = [a * jax.nn.sigmoid(a) for a in tiles]
        sq = [jnp.sum(y * y, axis=-1, keepdims=True) for y in ys]
        inv = [lax.rsqrt(s + NORM_EPS) for s in sq]
        return [jnp.where(normalise, y * r, y) for y, r in zip(ys, inv)]

    _conv_taps(pad_ref, w_ref, o_ref, None, seq=seq, width=width, post=post, tiles_per_step=QKV_CONV_TILES)


def _conv_qkv(z, dn_conv_w, *, batch, seq, col0):
    width = dn_conv_w.shape[0]
    nblk = 3 * DN_WIDTH // LANES
    blk0 = col0 // LANES
    return pl.pallas_call(
        functools.partial(_conv_qkv_kernel, seq=seq, width=width, n_norm_blocks=2 * DN_HEADS),
        grid=(batch, nblk),
        in_specs=[
            pl.BlockSpec((seq, LANES), lambda b, c: (b, blk0 + c)),
            pl.BlockSpec((width, LANES), lambda b, c: (0, c)),
        ],
        out_specs=pl.BlockSpec((seq, LANES), lambda b, c: (b, c)),
        out_shape=jax.ShapeDtypeStruct((batch * seq, 3 * DN_WIDTH), BF16),
        scratch_shapes=[pltpu.VMEM((seq + 2 * CONV_PAD, LANES), F32)],
        compiler_params=_params("parallel", "parallel"),
        name="conv_qkv",
    )(z, dn_conv_w)


def _split_hi_lo(x):
    hi = x.astype(BF16)
    return jnp.concatenate([hi, (x - hi.astype(F32)).astype(BF16)], axis=-1)


def _bf16_all(mats):
    return [m.astype(BF16) for m in mats]


def _unit_tri_inverses(mats, row, col, n):
    eye = (row == col).astype(F32)
    same = (row >> 3) == (col >> 3)
    l8 = [jnp.where(same, a, 0.0) for a in mats]
    l8h = _bf16_all(l8)
    l2h = _bf16_all([_dot(p, p) for p in l8h])
    l4h = _bf16_all([_dot(p, p) for p in l2h])
    x = [eye - p for p in l8]
    x = [xi + _dot(xi.astype(BF16), p) for xi, p in zip(x, l2h)]
    x = [xi + _dot(xi.astype(BF16), p) for xi, p in zip(x, l4h)]
    shift = 3
    while (1 << shift) < n:
        same_next = (row >> (shift + 1)) == (col >> (shift + 1))
        off = jnp.logical_and(same_next, jnp.logical_not(same))
        ch = _bf16_all([jnp.where(off, a, 0.0) for a in mats])
        xh = _bf16_all(x)
        xch = _bf16_all([_dot(p, c) for p, c in zip(xh, ch)])
        x = [xi - _dot(p, q) for xi, p, q in zip(x, xch, xh)]
        same = same_next
        shift += 1
    return x


def _group_masks(backward):
    n = PREP_GROUP
    row = lax.broadcasted_iota(jnp.int32, (n, n), 0)
    col = lax.broadcasted_iota(jnp.int32, (n, n), 1)
    shift = SCAN_CHUNK.bit_length() - 1
    same = (row >> shift) == (col >> shift)
    d = (col - row) if backward else (row - col)
    land = jnp.logical_and
    return land(same, d >= 0), land(same, d > 0), land(same, d <= 0)


def _as01(mask):
    return jnp.where(mask, 1.0, 0.0).astype(BF16)


def _delta_prep_kernel(q_ref, k_ref, v_ref, gc_ref, gr_ref, u_ref, w_ref, qg_ref, kg_ref, at_ref):
    G, L = PREP_GROUP, SCAN_CHUNK
    row = lax.broadcasted_iota(jnp.int32, (G, G), 0)
    col = lax.broadcasted_iota(jnp.int32, (G, G), 1)
    shift = L.bit_length() - 1
    same01 = _as01((row >> shift) == (col >> shift))
    masks = [_group_masks(False), _group_masks(True)]
    cum_c = [_as01(m[0]) for m in masks]
    cum_r = [_as01(m[2]) for m in masks]
    fold = _as01((lax.broadcasted_iota(jnp.int32, (G, L), 0) & (L - 1)) == lax.broadcasted_iota(jnp.int32, (G, L), 1))
    pad = jnp.zeros((G, DN_HEAD_DIM - L), BF16)
    scale = DN_HEAD_DIM ** -0.5

    step_groups = min(DN_PREP_STEP_GROUPS, q_ref.shape[0] // G)

    def step(i, carry):
        groups = range(step_groups)
        rows = [pl.ds(pl.multiple_of((i * step_groups + g) * G, G), G) for g in groups]
        gc = [gc_ref[r, :] for r in rows]
        gr = [gr_ref[i * step_groups + g] for g in groups]
        g_tot = [_exact_dot01(same01, x) for x in gc]
        g_cum_c = [[_exact_dot01(cum_c[d], x) for d in range(N_DIR)] for x in gc]
        g_cum_r = [[_exact_dot01_r(x, cum_r[d]) for d in range(N_DIR)] for x in gr]
        heads = range(DN_HEADS)
        lanes = [slice(h * DN_HEAD_DIM, (h + 1) * DN_HEAD_DIM) for h in heads]
        gh = [(g, h) for g in groups for h in heads]
        chains = [(g, h, d) for g, h in gh for d in range(N_DIR)]
        k16 = {(g, h): k_ref[rows[g], lanes[h]] for g, h in gh}
        q = {(g, h): q_ref[rows[g], lanes[h]].astype(F32) for g, h in gh}
        k = {c: k16[c].astype(F32) for c in gh}
        kk = {c: _dot_nt(k16[c], k16[c]) for c in gh}
        qk = {c: _dot_nt((q[c] * scale).astype(BF16), k16[c]) for c in gh}
        beta, g_c, g_t, decay = {}, {}, {}, {}
        for c in chains:
            g, h, d = c
            cb = d * DN_HEADS + h
            cl = N_DIR * DN_HEADS + cb
            beta[c] = gc[g][:, cb:cb + 1]
            g_c[c] = g_cum_c[g][d][:, cl:cl + 1]
            g_t[c] = g_tot[g][:, cl:cl + 1]
            g_r = g_cum_r[g][d][cl:cl + 1, :]
            decay[c] = jnp.exp(jnp.where(masks[d][0], g_c[c] - g_r, NEG_BIG))
        a = [jnp.where(masks[c[2]][1], kk[c[:2]] * decay[c], 0.0) * beta[c] for c in chains]
        x = _unit_tri_inverses(a, row, col, L)
        eg = {c: jnp.exp(g_c[c]) for c in chains}
        rhs = [jnp.concatenate([v_ref[rows[c[0]], lanes[c[1]]].astype(F32) * beta[c], k[c[:2]] * (beta[c] * eg[c])],
                               axis=-1).astype(BF16) for c in chains]
        uw = [_dot(xi.astype(BF16), r) for xi, r in zip(x, rhs)]
        attn = [jnp.where(masks[c[2]][0], qk[c[:2]] * decay[c], 0.0).astype(BF16) for c in chains]
        at = [_dot(p, fold) for p in attn]
        for c, uwi, ati in zip(chains, uw, at):
            g, h, d = c
            r, sl = rows[g], lanes[h]
            u_ref[d, r, sl] = uwi[:, :DN_HEAD_DIM].astype(BF16)
            w_ref[d, r, sl] = uwi[:, DN_HEAD_DIM:].astype(BF16)
            qg_ref[d, r, sl] = (q[g, h] * (scale * eg[c])).astype(BF16)
            kg_ref[d, r, sl] = (k[g, h] * jnp.exp(g_t[c] - g_c[c])).astype(BF16)
            at_ref[d, r, sl] = jnp.concatenate([ati.astype(BF16), pad], axis=-1)
        return carry

    lax.fori_loop(0, q_ref.shape[0] // (G * step_groups), step, 0)


def _delta_prep(qkv, zg, gr, *, batch, seq):
    n = batch * seq
    R, G = min(PREP_ROWS, seq), PREP_GROUP
    steps = seq // R
    rowblk = lambda c: (lambda b, r: (b * steps + r, c))
    out = jax.ShapeDtypeStruct((N_DIR, n, DN_WIDTH), BF16)
    out_spec = pl.BlockSpec((N_DIR, R, DN_WIDTH), lambda b, r: (0, b * steps + r, 0))
    return pl.pallas_call(
        _delta_prep_kernel,
        grid=(batch, steps),
        in_specs=[
            pl.BlockSpec((R, DN_WIDTH), rowblk(0)),
            pl.BlockSpec((R, DN_WIDTH), rowblk(1)),
            pl.BlockSpec((R, DN_WIDTH), rowblk(2)),
            pl.BlockSpec((R, LANES), rowblk(0)),
            pl.BlockSpec((R // G, GATE_LANES, G), lambda b, r: (b * steps + r, 0, 0)),
        ],
        out_specs=[out_spec] * 5,
        out_shape=[out] * 5,
        compiler_params=_params("parallel", "parallel"),
        name="delta_prep",
    )(qkv, qkv, qkv, zg, gr)


def _delta_scan_kernel(uf, wf, qgf, kgf, atf, zgf, ub, wb, qgb, kgb, atb, zgb, of_ref, ob_ref, s_ref, *, batch):
    L, C = SCAN_CHUNK, SCAN_STEP_CHUNKS

    @pl.when(pl.program_id(0) == 0)
    def _():
        s_ref[...] = jnp.zeros_like(s_ref)

    dirs = ((uf, wf, qgf, kgf, atf, zgf, of_ref), (ub, wb, qgb, kgb, atb, zgb, ob_ref))

    chains = [(d, b, h) for d in range(N_DIR) for b in range(batch) for h in range(DN_HEADS)]
    state = {c: s_ref[c] for c in chains}
    for step in range(C):
        chunk = {0: step, 1: C - 1 - step}
        rows = {d: slice(chunk[d] * L, (chunk[d] + 1) * L) for d in range(N_DIR)}
        egt = {(d, b): jnp.exp(jnp.sum(dirs[d][5][b, rows[d], :], axis=0, keepdims=True))
               for d in range(N_DIR) for b in range(batch)}
        ws, v_new, av = {}, {}, {}
        for c in chains:
            d, b, h = c
            sl = slice(h * DN_HEAD_DIM, (h + 1) * DN_HEAD_DIM)
            wq = jnp.concatenate([dirs[d][1][b, rows[d], sl], dirs[d][2][b, rows[d], sl]], axis=0)
            ws[c] = _dot(wq, state[c].astype(BF16))
        for c in chains:
            d, b, h = c
            sl = slice(h * DN_HEAD_DIM, (h + 1) * DN_HEAD_DIM)
            v_new[c] = (dirs[d][0][b, rows[d], sl].astype(F32) - ws[c][:L]).astype(BF16)
        for c in chains:
            d, b, h = c
            at = dirs[d][4][b, rows[d], h * DN_HEAD_DIM:h * DN_HEAD_DIM + L]
            av[c] = _dot(at, v_new[c])
        for c in chains:
            d, b, h = c
            sl = slice(h * DN_HEAD_DIM, (h + 1) * DN_HEAD_DIM)
            cl = (N_DIR + d) * DN_HEADS + h
            dirs[d][6][b, rows[d], sl] = (ws[c][L:] + av[c]).astype(BF16)
            state[c] = state[c] * egt[d, b][:, cl:cl + 1] + _dot_tn(dirs[d][3][b, rows[d], sl], v_new[c])
    for c in chains:
        s_ref[c] = state[c]


def _delta_scan(qkv, zg, gr, *, batch, seq):
    n = batch * seq
    T = SCAN_CHUNK * SCAN_STEP_CHUNKS
    steps = seq // T
    prepped = [t.reshape(N_DIR, batch, seq, DN_WIDTH) for t in _delta_prep(qkv, zg, gr, batch=batch, seq=seq)]
    zg3 = zg.reshape(batch, seq, LANES)
    fwd = lambda s: s
    bwd = lambda s: steps - 1 - s
    specs = []
    for d, pos in ((0, fwd), (1, bwd)):
        specs += [pl.BlockSpec((None, batch, T, DN_WIDTH), lambda s, d=d, pos=pos: (d, 0, pos(s), 0))] * 5
        specs += [pl.BlockSpec((batch, T, LANES), lambda s, pos=pos: (0, pos(s), 0))]
    out = jax.ShapeDtypeStruct((batch, seq, DN_WIDTH), BF16)
    o_f, o_b = pl.pallas_call(
        functools.partial(_delta_scan_kernel, batch=batch),
        grid=(steps,),
        in_specs=specs,
        out_specs=[pl.BlockSpec((batch, T, DN_WIDTH), lambda s: (0, fwd(s), 0)),
                   pl.BlockSpec((batch, T, DN_WIDTH), lambda s: (0, bwd(s), 0))],
        out_shape=[out, out],
        scratch_shapes=[pltpu.VMEM((N_DIR, batch, DN_HEADS, DN_HEAD_DIM, DN_HEAD_DIM), F32)],
        compiler_params=_params("arbitrary"),
        name="delta_scan",
    )(*prepped, zg3, *prepped, zg3)
    return o_f.reshape(n, DN_WIDTH), o_b.reshape(n, DN_WIDTH)


ML_PAIRS = ML_HEADS // 2
ML_AUG = 2 * ML_V_DIM
ML_KV_ROWS = ML_PAIRS * 2 * ML_QK_DIM


def _lane_cols(cols, width):
    rows = cols[0].shape[0]
    lane = lax.broadcasted_iota(jnp.int32, (rows, width), 1)
    out = jnp.zeros((rows, width), F32)
    for j, c in enumerate(cols):
        out = jnp.where(lane == j, c, out)
    return out


def _mlstm_prep_kernel(q_ref, k_ref, v_ref, gc_ref, gr_ref, qs_ref, iv_ref, kv_ref, aux_ref):
    G, L = PREP_GROUP, SCAN_CHUNK
    row = lax.broadcasted_iota(jnp.int32, (G, G), 0)
    col = lax.broadcasted_iota(jnp.int32, (G, G), 1)
    shift = L.bit_length() - 1
    same01 = _as01((row >> shift) == (col >> shift))
    masks = [_group_masks(False), _group_masks(True)]
    cum_c = [_as01(m[0]) for m in masks]
    cum_r = [_as01(m[2]) for m in masks]
    lane = lax.broadcasted_iota(jnp.int32, (G, LANES), 1)
    first_head = lane < ML_QK_DIM
    sel_row = lax.broadcasted_iota(jnp.int32, (2 * LANES, G), 0) & (LANES - 1)
    sel_col = lax.broadcasted_iota(jnp.int32, (2 * LANES, G), 1)
    gate_lane = lambda h, d: N_DIR * ML_HEADS + d * ML_HEADS + h
    pick_pair = {(p, d): jnp.concatenate([_as01(sel_row == gate_lane(2 * p, d)),
                                          _as01(sel_row == gate_lane(2 * p + 1, d))], axis=-1)
                 for p in range(ML_PAIRS) for d in range(N_DIR)}
    pick_halves = {(p, d): _as01(sel_row == jnp.where(sel_col < ML_QK_DIM, gate_lane(2 * p, d), gate_lane(2 * p + 1, d)))
                   for p in range(ML_PAIRS) for d in range(N_DIR)}
    ones_v = jnp.ones((G, ML_V_DIM), BF16)
    scale = ML_QK_DIM ** -0.5
    heads = range(ML_HEADS)
    chains = [(h, d) for h in heads for d in range(N_DIR)]

    step_groups = min(ML_PREP_STEP_GROUPS, q_ref.shape[0] // G)

    def step(i, carry):
        groups = range(step_groups)
        gidx = [i * step_groups + g for g in groups]
        rows = [pl.ds(pl.multiple_of(x * G, G), G) for x in gidx]
        gc = [gc_ref[r, :] for r in rows]
        gr = [gr_ref[x] for x in gidx]
        bt_c = [_exact_dot01(same01, x) for x in gc]
        bc_c = [[_exact_dot01(cum_c[d], x) for d in range(N_DIR)] for x in gc]
        bc_r = [[_exact_dot01_r(x, cum_r[d]) for d in range(N_DIR)] for x in gr]
        gps = [(g, p) for g in groups for p in range(ML_PAIRS)]
        ghs = [(g, h) for g in groups for h in heads]
        chains = [(g, h, d) for g, h in ghs for d in range(N_DIR)]
        qp = {(g, p): q_ref[rows[g], p * LANES:(p + 1) * LANES].astype(F32) * scale for g, p in gps}
        kp16 = {(g, p): k_ref[rows[g], p * LANES:(p + 1) * LANES] for g, p in gps}
        kp = {c: kp16[c].astype(F32) for c in gps}
        v16 = {(g, h): v_ref[rows[g], h * ML_V_DIM:(h + 1) * ML_V_DIM].astype(BF16) for g, h in ghs}
        for g, p in gps:
            qs_ref[rows[g], p * LANES:(p + 1) * LANES] = qp[g, p].astype(BF16)
        own = [first_head, jnp.logical_not(first_head)]
        scores = {(g, h): _dot_nt(jnp.where(own[h % 2], qp[g, h // 2], 0.0).astype(BF16), kp16[g, h // 2])
                  for g, h in ghs}
        col = lambda c: gate_lane(c[1], c[2])
        b_r = {c: bc_r[c[0]][c[2]][col(c):col(c) + 1, :] for c in chains}
        li_r = {c: gr[c[0]][col(c) - N_DIR * ML_HEADS:col(c) - N_DIR * ML_HEADS + 1, :] for c in chains}
        split = {(g, d): _split_hi_lo(bc_c[g][d]) for g in groups for d in range(N_DIR)}
        b_cb = {}
        for g, p in gps:
            for d in range(N_DIR):
                both = _dot(split[g, d], pick_pair[p, d])
                b_cb[g, 2 * p, d], b_cb[g, 2 * p + 1, d] = both[:, :G], both[:, G:]
        d_mat = {c: jnp.where(masks[c[2]][0], b_cb[c] - b_r[c] + li_r[c], NEG_BIG) for c in chains}
        d_max = {c: jnp.max(d_mat[c], axis=-1, keepdims=True) for c in chains}
        pmat = {c: (jnp.exp(d_mat[c] - d_max[c]) * scores[c[:2]]).astype(BF16) for c in chains}
        iv = {c: _dot(pmat[c], jnp.concatenate([v16[c[:2]], ones_v], axis=-1)) for c in chains}
        for c in chains:
            g, h, d = c
            iv_ref[d, rows[g], h * ML_AUG:(h + 1) * ML_AUG] = iv[c].astype(BF16)
        gd = [(g, d) for g in groups for d in range(N_DIR)]
        li_t = [pltpu.roll(x, N_DIR * ML_HEADS, 1) for x in gc]
        w_end = {(g, d): bt_c[g] - bc_c[g][d] + li_t[g] for g, d in gd}
        w_max = {c: jnp.concatenate([jnp.broadcast_to(jnp.max(w_end[c][j * L:(j + 1) * L], axis=0, keepdims=True),
                                                      (L, LANES)) for j in range(G // L)], axis=0) for c in gd}
        sw0 = {c: _split_hi_lo(jnp.exp(w_end[c] - w_max[c])) for c in gd}
        for g, d in gd:
            src = gate_lane(0, d)
            aux_ref[d, rows[g], :LANES] = jnp.where(
                lane < ML_HEADS, pltpu.roll(bc_c[g][d], LANES - src, 1),
                jnp.where(lane < 2 * ML_HEADS, pltpu.roll(bt_c[g], LANES - src + ML_HEADS, 1), 0.0))
            aux_ref[d, rows[g], LANES:] = jnp.where(
                lane < ML_HEADS, _lane_cols([d_max[g, h, d] for h in heads], LANES),
                jnp.where(lane < 2 * ML_HEADS, pltpu.roll(w_max[g, d], LANES - src + ML_HEADS, 1), 0.0))
        gpd = [(g, p, d) for g, p in gps for d in range(N_DIR)]
        ks = {(g, p, d): (kp[g, p] * _dot(sw0[g, d], pick_halves[p, d])).astype(BF16) for g, p, d in gpd}
        vcat = {(g, p): jnp.concatenate([v16[g, 2 * p], v16[g, 2 * p + 1], ones_v], axis=-1) for g, p in gps}
        pair_chunks = [(g, p, d, j) for g, p, d in gpd for j in range(G // L)]
        kv = [_dot_tn(ks[g, p, d][j * L:(j + 1) * L], vcat[g, p][j * L:(j + 1) * L]) for g, p, d, j in pair_chunks]
        for (g, p, d, j), t in zip(pair_chunks, kv):
            top = jnp.concatenate([t[:ML_QK_DIM, :ML_V_DIM], t[:ML_QK_DIM, 2 * ML_V_DIM:]], axis=-1)
            bot = jnp.concatenate([t[ML_QK_DIM:, ML_V_DIM:2 * ML_V_DIM], t[ML_QK_DIM:, 2 * ML_V_DIM:]], axis=-1)
            r0 = pl.multiple_of(gidx[g] * (G // L * ML_KV_ROWS) + (j * ML_PAIRS + p) * LANES, LANES)
            kv_ref[d, pl.ds(r0, LANES), :] = jnp.concatenate([top, bot], axis=0).astype(BF16)
        return carry

    lax.fori_loop(0, q_ref.shape[0] // (G * step_groups), step, 0)


def _mlstm_prep(z, zg, gr, *, batch, seq, q_col, k_col, v_col):
    n = batch * seq
    R, G, L = min(PREP_ROWS, seq), PREP_GROUP, SCAN_CHUNK
    steps = seq // R
    rowblk = lambda c: (lambda b, r: (b * steps + r, c))
    dirblk = lambda b, r: (0, b * steps + r, 0)
    kv_rows = ML_KV_ROWS // L
    return pl.pallas_call(
        _mlstm_prep_kernel,
        grid=(batch, steps),
        in_specs=[
            pl.BlockSpec((R, ML_QK_WIDTH), rowblk(q_col // ML_QK_WIDTH)),
            pl.BlockSpec((R, ML_QK_WIDTH), rowblk(k_col // ML_QK_WIDTH)),
            pl.BlockSpec((R, ML_V_WIDTH), rowblk(v_col // ML_V_WIDTH)),
            pl.BlockSpec((R, LANES), rowblk(0)),
            pl.BlockSpec((R // G, GATE_LANES, G), lambda b, r: (b * steps + r, 0, 0)),
        ],
        out_specs=[
            pl.BlockSpec((R, ML_QK_WIDTH), rowblk(0)),
            pl.BlockSpec((N_DIR, R, ML_HEADS * ML_AUG), dirblk),
            pl.BlockSpec((N_DIR, R * kv_rows, ML_AUG), dirblk),
            pl.BlockSpec((N_DIR, R, 2 * LANES), dirblk),
        ],
        out_shape=[
            jax.ShapeDtypeStruct((n, ML_QK_WIDTH), BF16),
            jax.ShapeDtypeStruct((N_DIR, n, ML_HEADS * ML_AUG), BF16),
            jax.ShapeDtypeStruct((N_DIR, n * kv_rows, ML_AUG), BF16),
            jax.ShapeDtypeStruct((N_DIR, n, 2 * LANES), F32),
        ],
        compiler_params=_params("parallel", "parallel"),
        name="mlstm_prep",
    )(z, z, z, zg, gr)


def _mlstm_scan_kernel(qf, ivf, kvf, auxf, qb, ivb, kvb, auxb, of_ref, ob_ref, c_ref, m_ref, *, batch):
    L, C = SCAN_CHUNK, SCAN_STEP_CHUNKS

    @pl.when(pl.program_id(0) == 0)
    def _():
        c_ref[...] = jnp.zeros_like(c_ref)
        m_ref[...] = jnp.zeros_like(m_ref)

    dirs = ((qf, ivf, kvf, auxf, of_ref), (qb, ivb, kvb, auxb, ob_ref))
    lane = lax.broadcasted_iota(jnp.int32, (L, LANES), 1)
    keep = [jnp.where(lane < ML_QK_DIM, 1.0, 0.0).astype(BF16), jnp.where(lane < ML_QK_DIM, 0.0, 1.0).astype(BF16)]
    first_rows = lax.broadcasted_iota(jnp.int32, (2 * ML_QK_DIM, 1), 0) < ML_QK_DIM
    pair_chains = [(d, b, p) for d in range(N_DIR) for b in range(batch) for p in range(ML_PAIRS)]
    chains = [(d, b, h) for d in range(N_DIR) for b in range(batch) for h in range(ML_HEADS)]
    groups = [(d, b) for d in range(N_DIR) for b in range(batch)]
    H = ML_HEADS
    lane8 = lax.broadcasted_iota(jnp.int32, (SUBLANES, LANES), 1)
    sel_row = lax.broadcasted_iota(jnp.int32, (2 * LANES, LANES), 0) & (LANES - 1)
    lane_select = [_as01(sel_row == j) for j in range(2 * H)]
    state = {c: c_ref[c] for c in pair_chains}
    m_st = {g: m_ref[g] for g in groups}
    for step in range(C):
        chunk = {0: step, 1: C - 1 - step}
        rows = {d: slice(chunk[d] * L, (chunk[d] + 1) * L) for d in range(N_DIR)}
        qc = {}
        for c in pair_chains:
            d, b, p = c
            qp = dirs[d][0][b, rows[d], p * LANES:(p + 1) * LANES]
            qc[c] = _dot(jnp.concatenate([qp * keep[0], qp * keep[1]], axis=0), state[c].astype(BF16))
        w_prev, w_cur, floor = {}, {}, {}
        for g in groups:
            d, b = g
            aux = dirs[d][3][b, rows[d], :]
            x = aux[:, :LANES] + m_st[g][0:1, :]
            y = jnp.maximum(x, aux[:, LANES:])
            w_prev[g] = jnp.exp(x - y)
            w_cur[g] = jnp.exp(aux[:, LANES:] - y)
            floor[g] = jnp.exp(-y)
            y0 = jnp.broadcast_to(y[0:1, :], (SUBLANES, LANES))
            m_st[g] = jnp.where(lane8 < H, pltpu.roll(y0, LANES - H, 1), jnp.where(lane8 < 2 * H, y0, 0.0))
        tiles = [t[g] for g in groups for t in (w_prev, w_cur, floor)]
        per_row = _split_hi_lo(jnp.concatenate(tiles, axis=0))
        per_chunk = _split_hi_lo(jnp.concatenate([t[0:1, :] for t in tiles] + [tiles[0][0:SUBLANES, :]], axis=0))
        bc_row, bc_chunk = [], []
        for p in range(ML_PAIRS):
            both = _dot(per_row, jnp.concatenate([lane_select[2 * p], lane_select[2 * p + 1]], axis=-1))
            bc_row += [both[:, :LANES], both[:, LANES:]]
            both = _dot(per_chunk, jnp.concatenate([lane_select[H + 2 * p], lane_select[H + 2 * p + 1]], axis=-1))
            bc_chunk += [both[:, :LANES], both[:, LANES:]]
        numden = {}
        for c in chains:
            d, b, h = c
            r0 = (h % 2) * L
            t0 = 3 * groups.index((d, b)) * L
            wp = bc_row[h][t0:t0 + L, :]
            wc = bc_row[h][t0 + L:t0 + 2 * L, :]
            iv = dirs[d][1][b, rows[d], h * ML_AUG:(h + 1) * ML_AUG].astype(F32)
            qch = qc[d, b, h // 2][r0:r0 + L, :]
            numden[c] = jnp.concatenate([wp * qch[:, :ML_V_DIM] + wc * iv[:, :ML_V_DIM],
                                         wp * qch[:, ML_V_DIM:] + wc * iv[:, ML_V_DIM:]], axis=-1)
        for c in chains:
            d, b, h = c
            t0 = 3 * groups.index((d, b)) * L
            den = jnp.maximum(jnp.abs(numden[c][:, ML_V_DIM:]), bc_row[h][t0 + 2 * L:t0 + 3 * L, :])
            dirs[d][4][b, rows[d], h * ML_V_DIM:(h + 1) * ML_V_DIM] = (numden[c][:, :ML_V_DIM] / den).astype(BF16)
        for c in pair_chains:
            d, b, p = c
            r0 = (chunk[d] * ML_PAIRS + p) * LANES
            kv = dirs[d][2][b, r0:r0 + LANES, :].astype(F32)
            t0 = 3 * groups.index((d, b))
            cw = jnp.where(first_rows, bc_chunk[2 * p][t0:t0 + 1, :], bc_chunk[2 * p + 1][t0:t0 + 1, :])
            iw = jnp.where(first_rows, bc_chunk[2 * p][t0 + 1:t0 + 2, :], bc_chunk[2 * p + 1][t0 + 1:t0 + 2, :])
            state[c] = jnp.concatenate([cw * state[c][:, :ML_V_DIM] + iw * kv[:, :ML_V_DIM],
                                        cw * state[c][:, ML_V_DIM:] + iw * kv[:, ML_V_DIM:]], axis=-1)
    for c in pair_chains:
        c_ref[c] = state[c]
    for g in groups:
        m_ref[g] = m_st[g]


def _mlstm_scan(z, zg, gr, *, batch, seq, q_col, k_col, v_col):
    n = batch * seq
    T = SCAN_CHUNK * SCAN_STEP_CHUNKS
    steps = seq // T
    kv_rows = ML_KV_ROWS // SCAN_CHUNK
    qs, iv, kv, aux = _mlstm_prep(z, zg, gr, batch=batch, seq=seq, q_col=q_col, k_col=k_col, v_col=v_col)
    qs = qs.reshape(batch, seq, ML_QK_WIDTH)
    iv = iv.reshape(N_DIR, batch, seq, ML_HEADS * ML_AUG)
    kv = kv.reshape(N_DIR, batch, seq * kv_rows, ML_AUG)
    aux = aux.reshape(N_DIR, batch, seq, 2 * LANES)
    fwd = lambda s: s
    bwd = lambda s: steps - 1 - s
    specs = []
    for d, pos in ((0, fwd), (1, bwd)):
        specs += [
            pl.BlockSpec((batch, T, ML_QK_WIDTH), lambda s, pos=pos: (0, pos(s), 0)),
            pl.BlockSpec((None, batch, T, ML_HEADS * ML_AUG), lambda s, d=d, pos=pos: (d, 0, pos(s), 0)),
            pl.BlockSpec((None, batch, T * kv_rows, ML_AUG), lambda s, d=d, pos=pos: (d, 0, pos(s), 0)),
            pl.BlockSpec((None, batch, T, 2 * LANES), lambda s, d=d, pos=pos: (d, 0, pos(s), 0)),
        ]
    out = jax.ShapeDtypeStruct((batch, seq, ML_V_WIDTH), BF16)
    h_f, h_b = pl.pallas_call(
        functools.partial(_mlstm_scan_kernel, batch=batch),
        grid=(steps,),
        in_specs=specs,
        out_specs=[pl.BlockSpec((batch, T, ML_V_WIDTH), lambda s: (0, fwd(s), 0)),
                   pl.BlockSpec((batch, T, ML_V_WIDTH), lambda s: (0, bwd(s), 0))],
        out_shape=[out, out],
        scratch_shapes=[
            pltpu.VMEM((N_DIR, batch, ML_PAIRS, 2 * ML_QK_DIM, ML_AUG), F32),
            pltpu.VMEM((N_DIR, batch, SUBLANES, LANES), F32),
        ],
        compiler_params=_params("arbitrary"),
        name="mlstm_scan",
    )(qs, iv, kv, aux, qs, iv, kv, aux)
    return h_f.reshape(n, ML_V_WIDTH), h_b.reshape(n, ML_V_WIDTH)


def _head_rms_norm(x, g, n_heads, head_dim):
    parts = []
    for h in range(n_heads):
        xh = x[:, h * head_dim:(h + 1) * head_dim]
        ms = jnp.mean(xh * xh, axis=-1, keepdims=True)
        parts.append(xh * lax.rsqrt(ms + NORM_EPS) * g)
    return jnp.concatenate(parts, axis=-1)


def _layer_norm(x, g, b):
    mu = jnp.mean(x, axis=-1, keepdims=True)
    xc = x - mu
    var = jnp.mean(xc * xc, axis=-1, keepdims=True)
    return xc * lax.rsqrt(var + LN_EPS) * g + b


def _swiglu_residual(x, g_ref, w1_ref, w3_ref, w2_ref, fg_ref, final_norm):
    ms = jnp.mean(x * x, axis=-1, keepdims=True)
    hn = (x * lax.rsqrt(ms + NORM_EPS) * g_ref[...]).astype(BF16)
    a = _dot(hn, w1_ref[...])
    b = _dot(hn, w3_ref[...])
    y = x + _dot((a * jax.nn.sigmoid(a) * b).astype(BF16), w2_ref[...])
    if final_norm:
        ms = jnp.mean(y * y, axis=-1, keepdims=True)
        y = y * lax.rsqrt(ms + NORM_EPS) * fg_ref[...]
    return y


def _ffn_operands(layer, g, w1, w3, w2, final_g):
    _, d, f = w1.shape
    pick = (layer, 0, 0)
    specs = [_resident((1, d)), _resident((None, d, f), pick), _resident((None, d, f), pick),
             _resident((None, f, d), pick), _resident((1, d))]
    return specs, (g.reshape(1, d), w1.astype(BF16), w3.astype(BF16), w2.astype(BF16), final_g.reshape(1, d))


def _even_block_kernel(h_ref, c_ref, of_ref, ob_ref, gate_ref, lng_ref, lnb_ref, ng_ref, wa_ref, wb_ref,
                       g_ref, w1_ref, w3_ref, w2_ref, fg_ref, o_ref, *, final_norm):
    ya = _layer_norm(c_ref[...].astype(F32), lng_ref[...], lnb_ref[...])
    ya = ya * jax.nn.sigmoid(ya)
    o = _head_rms_norm(of_ref[...].astype(F32) + ob_ref[...].astype(F32), ng_ref[...], DN_HEADS, DN_HEAD_DIM)
    gate = gate_ref[...].astype(F32)
    o = o * (gate * jax.nn.sigmoid(gate))
    x = h_ref[...] + _dot(ya.astype(BF16), wa_ref[...]) + _dot(o.astype(BF16), wb_ref[...])
    o_ref[...] = _swiglu_residual(x, g_ref, w1_ref, w3_ref, w2_ref, fg_ref, final_norm)


def _even_block(h, conv, o_fwd, o_bwd, z, ln_g, ln_b, norm_g, w_out, ffn, *, gate_col, final_norm):
    n, d = h.shape
    tm = min(ROW_TILE, n)
    gb = gate_col // DN_WIDTH
    row = lambda i: (i, 0)
    ffn_specs, ffn_args = _ffn_operands(*ffn)
    return pl.pallas_call(
        functools.partial(_even_block_kernel, final_norm=final_norm),
        grid=(n // tm,),
        in_specs=[
            pl.BlockSpec((tm, d), row),
            pl.BlockSpec((tm, CONV_CH), row),
            pl.BlockSpec((tm, DN_WIDTH), row),
            pl.BlockSpec((tm, DN_WIDTH), row),
            pl.BlockSpec((tm, DN_WIDTH), lambda i: (i, gb)),
            _resident((1, CONV_CH)),
            _resident((1, CONV_CH)),
            _resident((1, DN_HEAD_DIM)),
            _resident((CONV_CH, d)),
            _resident((DN_WIDTH, d), (CONV_CH // DN_WIDTH, 0)),
        ] + ffn_specs,
        out_specs=pl.BlockSpec((tm, d), row),
        out_shape=jax.ShapeDtypeStruct((n, d), F32),
        compiler_params=_params("parallel"),
        name="even_block",
    )(h, conv, o_fwd, o_bwd, z, ln_g.reshape(1, -1), ln_b.reshape(1, -1), norm_g.reshape(1, -1),
      w_out.astype(BF16), w_out.astype(BF16), *ffn_args)


def _odd_block_kernel(h_ref, u_ref, vp_ref, hf_ref, hb_ref, op_ref, lng_ref, lnb_ref, sgw_ref, sgb_ref, ng_ref,
                      wa_ref, wb_ref, g_ref, w1_ref, w3_ref, w2_ref, fg_ref, o_ref, *, final_norm):
    tm = h_ref.shape[0]
    u = jax.nn.gelu(u_ref[...].astype(F32))
    vv = _layer_norm(jax.nn.gelu(vp_ref[...].astype(F32)), lng_ref[...], lnb_ref[...]).astype(BF16)
    sgb = sgb_ref[...]
    rows = []
    for c in range(tm // SG_CHUNK):
        parts = []
        for g in range(SG_GROUPS):
            blk = vv[c * SG_CHUNK:(c + 1) * SG_CHUNK, g * SG_GROUP_DIM:(g + 1) * SG_GROUP_DIM]
            parts.append(_dot(sgw_ref[g], blk) + sgb[:, g:g + 1])
        rows.append(jnp.concatenate(parts, axis=-1))
    yc = u * jnp.concatenate(rows, axis=0)
    hd = _head_rms_norm(hf_ref[...].astype(F32) + hb_ref[...].astype(F32), ng_ref[...], ML_HEADS, ML_V_DIM)
    hd = hd * jax.nn.sigmoid(op_ref[...].astype(F32))
    x = h_ref[...] + _dot(yc.astype(BF16), wa_ref[...]) + _dot(hd.astype(BF16), wb_ref[...])
    o_ref[...] = _swiglu_residual(x, g_ref, w1_ref, w3_ref, w2_ref, fg_ref, final_norm)


def _odd_block(h, z, h_fwd, h_bwd, ln_g, ln_b, sg_w, sg_b, norm_g, w_out, ffn, *, u_col, v_col, o_col, final_norm):
    n, d = h.shape
    tm = min(ROW_TILE, n)
    row = lambda i: (i, 0)
    ffn_specs, ffn_args = _ffn_operands(*ffn)
    return pl.pallas_call(
        functools.partial(_odd_block_kernel, final_norm=final_norm),
        grid=(n // tm,),
        in_specs=[
            pl.BlockSpec((tm, d), row),
            pl.BlockSpec((tm, SG_WIDTH), lambda i: (i, u_col // SG_WIDTH)),
            pl.BlockSpec((tm, SG_WIDTH), lambda i: (i, v_col // SG_WIDTH)),
            pl.BlockSpec((tm, ML_V_WIDTH), row),
            pl.BlockSpec((tm, ML_V_WIDTH), row),
            pl.BlockSpec((tm, ML_V_WIDTH), lambda i: (i, o_col // ML_V_WIDTH)),
            _resident((1, SG_WIDTH)),
            _resident((1, SG_WIDTH)),
            _resident((SG_GROUPS, SG_CHUNK, SG_CHUNK)),
            _resident((SG_CHUNK, SG_GROUPS)),
            _resident((1, ML_V_DIM)),
            _resident((SG_WIDTH, d)),
            _resident((ML_V_WIDTH, d), (SG_WIDTH // ML_V_WIDTH, 0)),
        ] + ffn_specs,
        out_specs=pl.BlockSpec((tm, d), row),
        out_shape=jax.ShapeDtypeStruct((n, d), F32),
        compiler_params=_params("parallel"),
        name="odd_block",
    )(h, z, z, h_fwd, h_bwd, z, ln_g.reshape(1, -1), ln_b.reshape(1, -1), sg_w.astype(BF16), sg_b.T,
      norm_g.reshape(1, -1), w_out.astype(BF16), w_out.astype(BF16), *ffn_args)


def _gate_weight(w_gate_cols):
    d, c = w_gate_cols.shape
    return jnp.zeros((d, LANES), F32).at[:, :c].set(w_gate_cols).astype(BF16)


def _gate_params(first_kind, second_kind):
    table = jnp.zeros((SUBLANES, LANES), F32)
    for r, (kind, values) in enumerate((first_kind, second_kind)):
        table = table.at[r, kind * GATE_KIND_LANES:(kind + 1) * GATE_KIND_LANES].set(values.reshape(-1))
    return table


def _even_layer(h, j, p, ffn, *, batch, seq, final_norm):
    main = 2 * CONV_CH + 4 * DN_WIDTH
    w_in = p["ev_w_in"][j]
    gate_params = _gate_params((1, p["ev_dn_a_log"][j]), (1, p["ev_dn_dt_bias"][j]))
    z, zg, gr = _in_proj(h, p["mix_norm_g"][2 * j], w_in, main, _gate_weight(w_in[:, main:]),
                         gate_params, even=True)
    conv = _conv_glu(z, p["ev_conv_w"][j], p["ev_conv_b"][j], batch=batch, seq=seq)
    qkv = _conv_qkv(z, p["ev_dn_conv_w"][j], batch=batch, seq=seq, col0=2 * CONV_CH)
    o_fwd, o_bwd = _delta_scan(qkv, zg, gr, batch=batch, seq=seq)
    return _even_block(h, conv, o_fwd, o_bwd, z, p["ev_conv_ln_g"][j], p["ev_conv_ln_b"][j], p["ev_dn_norm_g"][j],
                       p["ev_w_out"][j], ffn, gate_col=2 * CONV_CH + 3 * DN_WIDTH, final_norm=final_norm)


def _odd_layer(h, j, p, ffn, *, batch, seq, final_norm):
    main = 2 * SG_WIDTH + 2 * ML_QK_WIDTH + 2 * ML_V_WIDTH
    w_in = p["od_w_in"][j]
    gate_params = _gate_params((0, p["od_ml_i_bias"][j]), (1, p["od_ml_f_bias"][j]))
    z, zg, gr = _in_proj(h, p["mix_norm_g"][2 * j + 1], w_in, main, _gate_weight(w_in[:, main:]),
                         gate_params, even=False)
    q_col = 2 * SG_WIDTH
    k_col = q_col + ML_QK_WIDTH
    v_col = k_col + ML_QK_WIDTH
    o_col = v_col + ML_V_WIDTH
    h_fwd, h_bwd = _mlstm_scan(z, zg, gr, batch=batch, seq=seq, q_col=q_col, k_col=k_col, v_col=v_col)
    return _odd_block(h, z, h_fwd, h_bwd, p["od_sg_ln_g"][j], p["od_sg_ln_b"][j], p["od_sg_w"][j], p["od_sg_b"][j],
                      p["od_ml_norm_g"][j], p["od_w_out"][j], ffn, u_col=0, v_col=SG_WIDTH, o_col=o_col,
                      final_norm=final_norm)


def kernel(x, mix_norm_g, ev_w_in, ev_conv_w, ev_conv_b, ev_conv_ln_g, ev_conv_ln_b, ev_dn_conv_w, ev_dn_a_log, ev_dn_dt_bias, ev_dn_norm_g, ev_w_out, od_w_in, od_sg_ln_g, od_sg_ln_b, od_sg_w, od_sg_b, od_ml_i_bias, od_ml_f_bias, od_ml_norm_g, od_w_out, ffn_norm_g, ffn_w1, ffn_w3, ffn_w2, final_norm_g):
    p = dict(mix_norm_g=mix_norm_g, ev_w_in=ev_w_in, ev_conv_w=ev_conv_w, ev_conv_b=ev_conv_b,
             ev_conv_ln_g=ev_conv_ln_g, ev_conv_ln_b=ev_conv_ln_b, ev_dn_conv_w=ev_dn_conv_w,
             ev_dn_a_log=ev_dn_a_log, ev_dn_dt_bias=ev_dn_dt_bias, ev_dn_norm_g=ev_dn_norm_g, ev_w_out=ev_w_out,
             od_w_in=od_w_in, od_sg_ln_g=od_sg_ln_g, od_sg_ln_b=od_sg_ln_b, od_sg_w=od_sg_w, od_sg_b=od_sg_b,
             od_ml_i_bias=od_ml_i_bias, od_ml_f_bias=od_ml_f_bias, od_ml_norm_g=od_ml_norm_g, od_w_out=od_w_out)
    batch, seq, d = x.shape
    depth = mix_norm_g.shape[0]
    h = x.reshape(batch * seq, d)
    for layer in range(depth):
        ffn = (layer, ffn_norm_g[layer], ffn_w1, ffn_w3, ffn_w2, final_norm_g)
        mixer_layer = _even_layer if layer % 2 == 0 else _odd_layer
        h = mixer_layer(h, layer // 2, p, ffn, batch=batch, seq=seq, final_norm=layer == depth - 1)
    return h.reshape(batch, seq, d)
```

```python
import functools

import jax
import jax.numpy as jnp
from jax import lax
from jax.experimental import pallas as pl
from jax.experimental.pallas import tpu as pltpu

NORM_EPS = 1e-6
LN_EPS = 1e-5
N_DIR = 2
GATE_KIND_LANES = N_DIR * 4
GATE_LANES = 2 * GATE_KIND_LANES

CONV_CH = 512
DN_HEADS = 4
DN_HEAD_DIM = 128
DN_WIDTH = DN_HEADS * DN_HEAD_DIM
SG_GROUPS = 4
SG_GROUP_DIM = 128
SG_WIDTH = SG_GROUPS * SG_GROUP_DIM
SG_CHUNK = 128
ML_HEADS = 4
ML_QK_DIM = 64
ML_V_DIM = 128
ML_QK_WIDTH = ML_HEADS * ML_QK_DIM
ML_V_WIDTH = ML_HEADS * ML_V_DIM

LANES = 128
SUBLANES = 8
VMEM_LIMIT_BYTES = 56 * 1024 * 1024

SCAN_CHUNK = 64
SCAN_STEP_CHUNKS = 4
PREP_GROUP = 128
PREP_ROWS = 1024
DN_PREP_STEP_GROUPS = 2
ML_PREP_STEP_GROUPS = 4
ROW_TILE = 512
IN_ROW_TILE = 1024
CONV_TILE = 128
QKV_CONV_TILES = 16
GLU_CONV_TILES = 4
CONV_PAD = 16
NEG_BIG = -1e30

BF16 = jnp.bfloat16
F32 = jnp.float32


def _params(*sem):
    return pltpu.CompilerParams(dimension_semantics=sem, vmem_limit_bytes=VMEM_LIMIT_BYTES)


def _resident(shape, block_index=None):
    index = (0,) * len(shape) if block_index is None else tuple(block_index)
    return pl.BlockSpec(shape, lambda *_: index, pipeline_mode=pl.Buffered(1))


def _dot(a, b):
    return jnp.dot(a, b, preferred_element_type=F32)


def _dot_nt(a, b):
    return lax.dot_general(a, b, (((1,), (1,)), ((), ())), preferred_element_type=F32)


def _dot_tn(a, b):
    return lax.dot_general(a, b, (((0,), (0,)), ((), ())), preferred_element_type=F32)


def _exact_dot01(t01, x):
    x1 = x.astype(BF16)
    r1 = x - x1.astype(F32)
    x2 = r1.astype(BF16)
    x3 = (r1 - x2.astype(F32)).astype(BF16)
    return _dot(t01, x1) + _dot(t01, x2) + _dot(t01, x3)


def _exact_dot01_r(x, t01):
    x1 = x.astype(BF16)
    r1 = x - x1.astype(F32)
    x2 = r1.astype(BF16)
    x3 = (r1 - x2.astype(F32)).astype(BF16)
    return _dot(x1, t01) + _dot(x2, t01) + _dot(x3, t01)


def _in_proj_kernel(h_ref, g_ref, w_ref, wg_ref, gp_ref, zm_ref, zg_ref, zgt_ref, *, even):
    x = h_ref[...]
    ms = jnp.mean(x * x, axis=-1, keepdims=True)
    hn = (x * lax.rsqrt(ms + NORM_EPS) * g_ref[...]).astype(BF16)
    zm_ref[...] = _dot(hn, w_ref[...]).astype(zm_ref.dtype)
    zg = _dot(hn, wg_ref[...])
    p0 = gp_ref[0:1, :]
    p1 = gp_ref[1:2, :]
    lane = lax.broadcasted_iota(jnp.int32, zg.shape, 1)
    if even:
        first = jax.nn.sigmoid(zg)
        second = -jnp.exp(p0) * jax.nn.softplus(zg + p1)
    else:
        first = zg + p0
        second = jax.nn.log_sigmoid(zg + p1)
    gates = jnp.where(lane < GATE_KIND_LANES, first, second)
    zg_ref[...] = gates
    for s in range(gates.shape[0] // PREP_GROUP):
        zgt_ref[s] = gates[s * PREP_GROUP:(s + 1) * PREP_GROUP, :].T[:GATE_LANES, :]


def _in_proj(h, g, w_in, c, w_gate, gate_params, *, even):
    n, d = h.shape
    assert c % LANES == 0
    tm = min(IN_ROW_TILE, n)
    return pl.pallas_call(
        functools.partial(_in_proj_kernel, even=even),
        grid=(n // tm,),
        in_specs=[
            pl.BlockSpec((tm, d), lambda i: (i, 0)),
            _resident((1, d)),
            _resident((d, c)),
            _resident((d, LANES)),
            _resident((SUBLANES, LANES)),
        ],
        out_specs=[
            pl.BlockSpec((tm, c), lambda i: (i, 0)),
            pl.BlockSpec((tm, LANES), lambda i: (i, 0)),
            pl.BlockSpec((tm // PREP_GROUP, GATE_LANES, PREP_GROUP), lambda i: (i, 0, 0)),
        ],
        out_shape=[jax.ShapeDtypeStruct((n, c), BF16), jax.ShapeDtypeStruct((n, LANES), F32),
                   jax.ShapeDtypeStruct((n // PREP_GROUP, GATE_LANES, PREP_GROUP), F32)],
        compiler_params=_params("parallel"),
        name="in_proj_even" if even else "in_proj_odd",
    )(h, g.reshape(1, d), w_in, w_gate, gate_params)


def _conv_taps(pad_ref, w_ref, o_ref, bias, *, seq, width, post, tiles_per_step=1):
    half = width // 2
    tiles_per_step = min(tiles_per_step, seq // CONV_TILE)
    step_rows = CONV_TILE * tiles_per_step

    def body(i, carry):
        t0 = pl.multiple_of(i * step_rows, step_rows)
        accs = []
        for s in range(tiles_per_step):
            acc = jnp.zeros((CONV_TILE, LANES), F32)
            for j in range(width):
                acc = acc + pad_ref[pl.ds(t0 + (s * CONV_TILE + CONV_PAD - half + j), CONV_TILE), :] * w_ref[j:j + 1, :]
            accs.append(acc if bias is None else acc + bias)
        for s, out in enumerate(post(accs)):
            o_ref[pl.ds(t0 + s * CONV_TILE, CONV_TILE), :] = out.astype(o_ref.dtype)
        return carry

    lax.fori_loop(0, seq // step_rows, body, 0)


def _fill_padded(pad_ref, x, seq):
    zeros = jnp.zeros((CONV_PAD, LANES), F32)
    pad_ref[0:CONV_PAD, :] = zeros
    pad_ref[CONV_PAD + seq:CONV_PAD + seq + CONV_PAD, :] = zeros
    pad_ref[CONV_PAD:CONV_PAD + seq, :] = x


def _conv_glu_kernel(av_ref, ag_ref, w_ref, b_ref, o_ref, pad_ref, *, seq, width):
    _fill_padded(pad_ref, av_ref[...].astype(F32) * jax.nn.sigmoid(ag_ref[...].astype(F32)), seq)
    _conv_taps(pad_ref, w_ref, o_ref, b_ref[...], seq=seq, width=width, post=lambda tiles: tiles,
               tiles_per_step=GLU_CONV_TILES)


def _conv_glu(z, conv_w, conv_b, *, batch, seq):
    width = conv_w.shape[0]
    nblk = CONV_CH // LANES
    return pl.pallas_call(
        functools.partial(_conv_glu_kernel, seq=seq, width=width),
        grid=(batch, nblk),
        in_specs=[
            pl.BlockSpec((seq, LANES), lambda b, c: (b, c)),
            pl.BlockSpec((seq, LANES), lambda b, c: (b, nblk + c)),
            pl.BlockSpec((width, LANES), lambda b, c: (0, c)),
            pl.BlockSpec((1, LANES), lambda b, c: (0, c)),
        ],
        out_specs=pl.BlockSpec((seq, LANES), lambda b, c: (b, c)),
        out_shape=jax.ShapeDtypeStruct((batch * seq, CONV_CH), BF16),
        scratch_shapes=[pltpu.VMEM((seq + 2 * CONV_PAD, LANES), F32)],
        compiler_params=_params("parallel", "parallel"),
        name="conv_glu",
    )(z, z, conv_w, conv_b.reshape(1, CONV_CH))


def _conv_qkv_kernel(x_ref, w_ref, o_ref, pad_ref, *, seq, width, n_norm_blocks):
    _fill_padded(pad_ref, x_ref[...].astype(F32), seq)
    normalise = pl.program_id(1) < n_norm_blocks

    def post(tiles):
        ys = [a * jax.nn.sigmoid(a) for a in tiles]
        sq = [jnp.sum(y * y, axis=-1, keepdims=True) for y in ys]
        inv = [lax.rsqrt(s + NORM_EPS) for s in sq]
        return [jnp.where(normalise, y * r, y) for y, r in zip(ys, inv)]

    _conv_taps(pad_ref, w_ref, o_ref, None, seq=seq, width=width, post=post, tiles_per_step=QKV_CONV_TILES)


def _conv_qkv(z, dn_conv_w, *, batch, seq, col0):
    width = dn_conv_w.shape[0]
    nblk = 3 * DN_WIDTH // LANES
    blk0 = col0 // LANES
    return pl.pallas_call(
        functools.partial(_conv_qkv_kernel, seq=seq, width=width, n_norm_blocks=2 * DN_HEADS),
        grid=(batch, nblk),
        in_specs=[
            pl.BlockSpec((seq, LANES), lambda b, c: (b, blk0 + c)),
            pl.BlockSpec((width, LANES), lambda b, c: (0, c)),
        ],
        out_specs=pl.BlockSpec((seq, LANES), lambda b, c: (b, c)),
        out_shape=jax.ShapeDtypeStruct((batch * seq, 3 * DN_WIDTH), BF16),
        scratch_shapes=[pltpu.VMEM((seq + 2 * CONV_PAD, LANES), F32)],
        compiler_params=_params("parallel", "parallel"),
        name="conv_qkv",
    )(z, dn_conv_w)


def _split_hi_lo(x):
    hi = x.astype(BF16)
    return jnp.concatenate([hi, (x - hi.astype(F32)).astype(BF16)], axis=-1)


def _bf16_all(mats):
    return [m.astype(BF16) for m in mats]


def _unit_tri_inverses(mats, row, col, n):
    eye = (row == col).astype(F32)
    same = (row >> 3) == (col >> 3)
    l8 = [jnp.where(same, a, 0.0) for a in mats]
    l8h = _bf16_all(l8)
    l2h = _bf16_all([_dot(p, p) for p in l8h])
    l4h = _bf16_all([_dot(p, p) for p in l2h])
    x = [eye - p for p in l8]
    x = [xi + _dot(xi.astype(BF16), p) for xi, p in zip(x, l2h)]
    x = [xi + _dot(xi.astype(BF16), p) for xi, p in zip(x, l4h)]
    shift = 3
    while (1 << shift) < n:
        same_next = (row >> (shift + 1)) == (col >> (shift + 1))
        off = jnp.logical_and(same_next, jnp.logical_not(same))
        ch = _bf16_all([jnp.where(off, a, 0.0) for a in mats])
        xh = _bf16_all(x)
        xch = _bf16_all([_dot(p, c) for p, c in zip(xh, ch)])
        x = [xi - _dot(p, q) for xi, p, q in zip(x, xch, xh)]
        same = same_next
        shift += 1
    return x


def _group_masks(backward):
    n = PREP_GROUP
    row = lax.broadcasted_iota(jnp.int32, (n, n), 0)
    col = lax.broadcasted_iota(jnp.int32, (n, n), 1)
    shift = SCAN_CHUNK.bit_length() - 1
    same = (row >> shift) == (col >> shift)
    d = (col - row) if backward else (row - col)
    land = jnp.logical_and
    return land(same, d >= 0), land(same, d > 0), land(same, d <= 0)


def _as01(mask):
    return jnp.where(mask, 1.0, 0.0).astype(BF16)


def _delta_prep_kernel(q_ref, k_ref, v_ref, gc_ref, gr_ref, u_ref, w_ref, qg_ref, kg_ref, at_ref):
    G, L = PREP_GROUP, SCAN_CHUNK
    row = lax.broadcasted_iota(jnp.int32, (G, G), 0)
    col = lax.broadcasted_iota(jnp.int32, (G, G), 1)
    shift = L.bit_length() - 1
    same01 = _as01((row >> shift) == (col >> shift))
    masks = [_group_masks(False), _group_masks(True)]
    cum_c = [_as01(m[0]) for m in masks]
    cum_r = [_as01(m[2]) for m in masks]
    fold = _as01((lax.broadcasted_iota(jnp.int32, (G, L), 0) & (L - 1)) == lax.broadcasted_iota(jnp.int32, (G, L), 1))
    pad = jnp.zeros((G, DN_HEAD_DIM - L), BF16)
    scale = DN_HEAD_DIM ** -0.5

    step_groups = min(DN_PREP_STEP_GROUPS, q_ref.shape[0] // G)

    def step(i, carry):
        groups = range(step_groups)
        rows = [pl.ds(pl.multiple_of((i * step_groups + g) * G, G), G) for g in groups]
        gc = [gc_ref[r, :] for r in rows]
        gr = [gr_ref[i * step_groups + g] for g in groups]
        g_tot = [_exact_dot01(same01, x) for x in gc]
        g_cum_c = [[_exact_dot01(cum_c[d], x) for d in range(N_DIR)] for x in gc]
        g_cum_r = [[_exact_dot01_r(x, cum_r[d]) for d in range(N_DIR)] for x in gr]
        heads = range(DN_HEADS)
        lanes = [slice(h * DN_HEAD_DIM, (h + 1) * DN_HEAD_DIM) for h in heads]
        gh = [(g, h) for g in groups for h in heads]
        chains = [(g, h, d) for g, h in gh for d in range(N_DIR)]
        k16 = {(g, h): k_ref[rows[g], lanes[h]] for g, h in gh}
        q = {(g, h): q_ref[rows[g], lanes[h]].astype(F32) for g, h in gh}
        k = {c: k16[c].astype(F32) for c in gh}
        kk = {c: _dot_nt(k16[c], k16[c]) for c in gh}
        qk = {c: _dot_nt((q[c] * scale).astype(BF16), k16[c]) for c in gh}
        beta, g_c, g_t, decay = {}, {}, {}, {}
        for c in chains:
            g, h, d = c
            cb = d * DN_HEADS + h
            cl = N_DIR * DN_HEADS + cb
            beta[c] = gc[g][:, cb:cb + 1]
            g_c[c] = g_cum_c[g][d][:, cl:cl + 1]
            g_t[c] = g_tot[g][:, cl:cl + 1]
            g_r = g_cum_r[g][d][cl:cl + 1, :]
            decay[c] = jnp.exp(jnp.where(masks[d][0], g_c[c] - g_r, NEG_BIG))
        a = [jnp.where(masks[c[2]][1], kk[c[:2]] * decay[c], 0.0) * beta[c] for c in chains]
        x = _unit_tri_inverses(a, row, col, L)
        eg = {c: jnp.exp(g_c[c]) for c in chains}
        rhs = [jnp.concatenate([v_ref[rows[c[0]], lanes[c[1]]].astype(F32) * beta[c], k[c[:2]] * (beta[c] * eg[c])],
                               axis=-1).astype(BF16) for c in chains]
        uw = [_dot(xi.astype(BF16), r) for xi, r in zip(x, rhs)]
        attn = [jnp.where(masks[c[2]][0], qk[c[:2]] * decay[c], 0.0).astype(BF16) for c in chains]
        at = [_dot(p, fold) for p in attn]
        for c, uwi, ati in zip(chains, uw, at):
            g, h, d = c
            r, sl = rows[g], lanes[h]
            u_ref[d, r, sl] = uwi[:, :DN_HEAD_DIM].astype(BF16)
            w_ref[d, r, sl] = uwi[:, DN_HEAD_DIM:].astype(BF16)
            qg_ref[d, r, sl] = (q[g, h] * (scale * eg[c])).astype(BF16)
            kg_ref[d, r, sl] = (k[g, h] * jnp.exp(g_t[c] - g_c[c])).astype(BF16)
            at_ref[d, r, sl] = jnp.concatenate([ati.astype(BF16), pad], axis=-1)
        return carry

    lax.fori_loop(0, q_ref.shape[0] // (G * step_groups), step, 0)


def _delta_prep(qkv, zg, gr, *, batch, seq):
    n = batch * seq
    R, G = min(PREP_ROWS, seq), PREP_GROUP
    steps = seq // R
    rowblk = lambda c: (lambda b, r: (b * steps + r, c))
    out = jax.ShapeDtypeStruct((N_DIR, n, DN_WIDTH), BF16)
    out_spec = pl.BlockSpec((N_DIR, R, DN_WIDTH), lambda b, r: (0, b * steps + r, 0))
    return pl.pallas_call(
        _delta_prep_kernel,
        grid=(batch, steps),
        in_specs=[
            pl.BlockSpec((R, DN_WIDTH), rowblk(0)),
            pl.BlockSpec((R, DN_WIDTH), rowblk(1)),
            pl.BlockSpec((R, DN_WIDTH), rowblk(2)),
            pl.BlockSpec((R, LANES), rowblk(0)),
            pl.BlockSpec((R // G, GATE_LANES, G), lambda b, r: (b * steps + r, 0, 0)),
        ],
        out_specs=[out_spec] * 5,
        out_shape=[out] * 5,
        compiler_params=_params("parallel", "parallel"),
        name="delta_prep",
    )(qkv, qkv, qkv, zg, gr)


def _delta_scan_kernel(uf, wf, qgf, kgf, atf, zgf, ub, wb, qgb, kgb, atb, zgb, of_ref, ob_ref, s_ref, *, batch):
    L, C = SCAN_CHUNK, SCAN_STEP_CHUNKS

    @pl.when(pl.program_id(0) == 0)
    def _():
        s_ref[...] = jnp.zeros_like(s_ref)

    dirs = ((uf, wf, qgf, kgf, atf, zgf, of_ref), (ub, wb, qgb, kgb, atb, zgb, ob_ref))

    chains = [(d, b, h) for d in range(N_DIR) for b in range(batch) for h in range(DN_HEADS)]
    state = {c: s_ref[c] for c in chains}
    for step in range(C):
        chunk = {0: step, 1: C - 1 - step}
        rows = {d: slice(chunk[d] * L, (chunk[d] + 1) * L) for d in range(N_DIR)}
        egt = {(d, b): jnp.exp(jnp.sum(dirs[d][5][b, rows[d], :], axis=0, keepdims=True))
               for d in range(N_DIR) for b in range(batch)}
        ws, v_new, av = {}, {}, {}
        for c in chains:
            d, b, h = c
            sl = slice(h * DN_HEAD_DIM, (h + 1) * DN_HEAD_DIM)
            wq = jnp.concatenate([dirs[d][1][b, rows[d], sl], dirs[d][2][b, rows[d], sl]], axis=0)
            ws[c] = _dot(wq, state[c].astype(BF16))
        for c in chains:
            d, b, h = c
            sl = slice(h * DN_HEAD_DIM, (h + 1) * DN_HEAD_DIM)
            v_new[c] = (dirs[d][0][b, rows[d], sl].astype(F32) - ws[c][:L]).astype(BF16)
        for c in chains:
            d, b, h = c
            at = dirs[d][4][b, rows[d], h * DN_HEAD_DIM:h * DN_HEAD_DIM + L]
            av[c] = _dot(at, v_new[c])
        for c in chains:
            d, b, h = c
            sl = slice(h * DN_HEAD_DIM, (h + 1) * DN_HEAD_DIM)
            cl = (N_DIR + d) * DN_HEADS + h
            dirs[d][6][b, rows[d], sl] = (ws[c][L:] + av[c]).astype(BF16)
            state[c] = state[c] * egt[d, b][:, cl:cl + 1] + _dot_tn(dirs[d][3][b, rows[d], sl], v_new[c])
    for c in chains:
        s_ref[c] = state[c]


def _delta_scan(qkv, zg, gr, *, batch, seq):
    n = batch * seq
    T = SCAN_CHUNK * SCAN_STEP_CHUNKS
    steps = seq // T
    prepped = [t.reshape(N_DIR, batch, seq, DN_WIDTH) for t in _delta_prep(qkv, zg, gr, batch=batch, seq=seq)]
    zg3 = zg.reshape(batch, seq, LANES)
    fwd = lambda s: s
    bwd = lambda s: steps - 1 - s
    specs = []
    for d, pos in ((0, fwd), (1, bwd)):
        specs += [pl.BlockSpec((None, batch, T, DN_WIDTH), lambda s, d=d, pos=pos: (d, 0, pos(s), 0))] * 5
        specs += [pl.BlockSpec((batch, T, LANES), lambda s, pos=pos: (0, pos(s), 0))]
    out = jax.ShapeDtypeStruct((batch, seq, DN_WIDTH), BF16)
    o_f, o_b = pl.pallas_call(
        functools.partial(_delta_scan_kernel, batch=batch),
        grid=(steps,),
        in_specs=specs,
        out_specs=[pl.BlockSpec((batch, T, DN_WIDTH), lambda s: (0, fwd(s), 0)),
                   pl.BlockSpec((batch, T, DN_WIDTH), lambda s: (0, bwd(s), 0))],
        out_shape=[out, out],
        scratch_shapes=[pltpu.VMEM((N_DIR, batch, DN_HEADS, DN_HEAD_DIM, DN_HEAD_DIM), F32)],
        compiler_params=_params("arbitrary"),
        name="delta_scan",
    )(*prepped, zg3, *prepped, zg3)
    return o_f.reshape(n, DN_WIDTH), o_b.reshape(n, DN_WIDTH)


ML_PAIRS = ML_HEADS // 2
ML_AUG = 2 * ML_V_DIM
ML_KV_ROWS = ML_PAIRS * 2 * ML_QK_DIM


def _lane_cols(cols, width):
    rows = cols[0].shape[0]
    lane = lax.broadcasted_iota(jnp.int32, (rows, width), 1)
    out = jnp.zeros((rows, width), F32)
    for j, c in enumerate(cols):
        out = jnp.where(lane == j, c, out)
    return out


def _mlstm_prep_kernel(q_ref, k_ref, v_ref, gc_ref, gr_ref, qs_ref, iv_ref, kv_ref, aux_ref):
    G, L = PREP_GROUP, SCAN_CHUNK
    row = lax.broadcasted_iota(jnp.int32, (G, G), 0)
    col = lax.broadcasted_iota(jnp.int32, (G, G), 1)
    shift = L.bit_length() - 1
    same01 = _as01((row >> shift) == (col >> shift))
    masks = [_group_masks(False), _group_masks(True)]
    cum_c = [_as01(m[0]) for m in masks]
    cum_r = [_as01(m[2]) for m in masks]
    lane = lax.broadcasted_iota(jnp.int32, (G, LANES), 1)
    first_head = lane < ML_QK_DIM
    sel_row = lax.broadcasted_iota(jnp.int32, (2 * LANES, G), 0) & (LANES - 1)
    sel_col = lax.broadcasted_iota(jnp.int32, (2 * LANES, G), 1)
    gate_lane = lambda h, d: N_DIR * ML_HEADS + d * ML_HEADS + h
    pick_pair = {(p, d): jnp.concatenate([_as01(sel_row == gate_lane(2 * p, d)),
                                          _as01(sel_row == gate_lane(2 * p + 1, d))], axis=-1)
                 for p in range(ML_PAIRS) for d in range(N_DIR)}
    pick_halves = {(p, d): _as01(sel_row == jnp.where(sel_col < ML_QK_DIM, gate_lane(2 * p, d), gate_lane(2 * p + 1, d)))
                   for p in range(ML_PAIRS) for d in range(N_DIR)}
    ones_v = jnp.ones((G, ML_V_DIM), BF16)
    scale = ML_QK_DIM ** -0.5
    heads = range(ML_HEADS)
    chains = [(h, d) for h in heads for d in range(N_DIR)]

    step_groups = min(ML_PREP_STEP_GROUPS, q_ref.shape[0] // G)

    def step(i, carry):
        groups = range(step_groups)
        gidx = [i * step_groups + g for g in groups]
        rows = [pl.ds(pl.multiple_of(x * G, G), G) for x in gidx]
        gc = [gc_ref[r, :] for r in rows]
        gr = [gr_ref[x] for x in gidx]
        bt_c = [_exact_dot01(same01, x) for x in gc]
        bc_c = [[_exact_dot01(cum_c[d], x) for d in range(N_DIR)] for x in gc]
        bc_r = [[_exact_dot01_r(x, cum_r[d]) for d in range(N_DIR)] for x in gr]
        gps = [(g, p) for g in groups for p in range(ML_PAIRS)]
        ghs = [(g, h) for g in groups for h in heads]
        chains = [(g, h, d) for g, h in ghs for d in range(N_DIR)]
        qp = {(g, p): q_ref[rows[g], p * LANES:(p + 1) * LANES].astype(F32) * scale for g, p in gps}
        kp16 = {(g, p): k_ref[rows[g], p * LANES:(p + 1) * LANES] for g, p in gps}
        kp = {c: kp16[c].astype(F32) for c in gps}
        v16 = {(g, h): v_ref[rows[g], h * ML_V_DIM:(h + 1) * ML_V_DIM].astype(BF16) for g, h in ghs}
        for g, p in gps:
            qs_ref[rows[g], p * LANES:(p + 1) * LANES] = qp[g, p].astype(BF16)
        own = [first_head, jnp.logical_not(first_head)]
        scores = {(g, h): _dot_nt(jnp.where(own[h % 2], qp[g, h // 2], 0.0).astype(BF16), kp16[g, h // 2])
                  for g, h in ghs}
        col = lambda c: gate_lane(c[1], c[2])
        b_r = {c: bc_r[c[0]][c[2]][col(c):col(c) + 1, :] for c in chains}
        li_r = {c: gr[c[0]][col(c) - N_DIR * ML_HEADS:col(c) - N_DIR * ML_HEADS + 1, :] for c in chains}
        split = {(g, d): _split_hi_lo(bc_c[g][d]) for g in groups for d in range(N_DIR)}
        b_cb = {}
        for g, p in gps:
            for d in range(N_DIR):
                both = _dot(split[g, d], pick_pair[p, d])
                b_cb[g, 2 * p, d], b_cb[g, 2 * p + 1, d] = both[:, :G], both[:, G:]
        d_mat = {c: jnp.where(masks[c[2]][0], b_cb[c] - b_r[c] + li_r[c], NEG_BIG) for c in chains}
        d_max = {c: jnp.max(d_mat[c], axis=-1, keepdims=True) for c in chains}
        pmat = {c: (jnp.exp(d_mat[c] - d_max[c]) * scores[c[:2]]).astype(BF16) for c in chains}
        iv = {c: _dot(pmat[c], jnp.concatenate([v16[c[:2]], ones_v], axis=-1)) for c in chains}
        for c in chains:
            g, h, d = c
            iv_ref[d, rows[g], h * ML_AUG:(h + 1) * ML_AUG] = iv[c].astype(BF16)
        gd = [(g, d) for g in groups for d in range(N_DIR)]
        li_t = [pltpu.roll(x, N_DIR * ML_HEADS, 1) for x in gc]
        w_end = {(g, d): bt_c[g] - bc_c[g][d] + li_t[g] for g, d in gd}
        w_max = {c: jnp.concatenate([jnp.broadcast_to(jnp.max(w_end[c][j * L:(j + 1) * L], axis=0, keepdims=True),
                                                      (L, LANES)) for j in range(G // L)], axis=0) for c in gd}
        sw0 = {c: _split_hi_lo(jnp.exp(w_end[c] - w_max[c])) for c in gd}
        for g, d in gd:
            src = gate_lane(0, d)
            aux_ref[d, rows[g], :LANES] = jnp.where(
                lane < ML_HEADS, pltpu.roll(bc_c[g][d], LANES - src, 1),
                jnp.where(lane < 2 * ML_HEADS, pltpu.roll(bt_c[g], LANES - src + ML_HEADS, 1), 0.0))
            aux_ref[d, rows[g], LANES:] = jnp.where(
                lane < ML_HEADS, _lane_cols([d_max[g, h, d] for h in heads], LANES),
                jnp.where(lane < 2 * ML_HEADS, pltpu.roll(w_max[g, d], LANES - src + ML_HEADS, 1), 0.0))
        gpd = [(g, p, d) for g, p in gps for d in range(N_DIR)]
        ks = {(g, p, d): (kp[g, p] * _dot(sw0[g, d], pick_halves[p, d])).astype(BF16) for g, p, d in gpd}
        vcat = {(g, p): jnp.concatenate([v16[g, 2 * p], v16[g, 2 * p + 1], ones_v], axis=-1) for g, p in gps}
        pair_chunks = [(g, p, d, j) for g, p, d in gpd for j in range(G // L)]
        kv = [_dot_tn(ks[g, p, d][j * L:(j + 1) * L], vcat[g, p][j * L:(j + 1) * L]) for g, p, d, j in pair_chunks]
        for (g, p, d, j), t in zip(pair_chunks, kv):
            top = jnp.concatenate([t[:ML_QK_DIM, :ML_V_DIM], t[:ML_QK_DIM, 2 * ML_V_DIM:]], axis=-1)
            bot = jnp.concatenate([t[ML_QK_DIM:, ML_V_DIM:2 * ML_V_DIM], t[ML_QK_DIM:, 2 * ML_V_DIM:]], axis=-1)
            r0 = pl.multiple_of(gidx[g] * (G // L * ML_KV_ROWS) + (j * ML_PAIRS + p) * LANES, LANES)
            kv_ref[d, pl.ds(r0, LANES), :] = jnp.concatenate([top, bot], axis=0).astype(BF16)
        return carry

    lax.fori_loop(0, q_ref.shape[0] // (G * step_groups), step, 0)


def _mlstm_prep(z, zg, gr, *, batch, seq, q_col, k_col, v_col):
    n = batch * seq
    R, G, L = min(PREP_ROWS, seq), PREP_GROUP, SCAN_CHUNK
    steps = seq // R
    rowblk = lambda c: (lambda b, r: (b * steps + r, c))
    dirblk = lambda b, r: (0, b * steps + r, 0)
    kv_rows = ML_KV_ROWS // L
    return pl.pallas_call(
        _mlstm_prep_kernel,
        grid=(batch, steps),
        in_specs=[
            pl.BlockSpec((R, ML_QK_WIDTH), rowblk(q_col // ML_QK_WIDTH)),
            pl.BlockSpec((R, ML_QK_WIDTH), rowblk(k_col // ML_QK_WIDTH)),
            pl.BlockSpec((R, ML_V_WIDTH), rowblk(v_col // ML_V_WIDTH)),
            pl.BlockSpec((R, LANES), rowblk(0)),
            pl.BlockSpec((R // G, GATE_LANES, G), lambda b, r: (b * steps + r, 0, 0)),
        ],
        out_specs=[
            pl.BlockSpec((R, ML_QK_WIDTH), rowblk(0)),
            pl.BlockSpec((N_DIR, R, ML_HEADS * ML_AUG), dirblk),
            pl.BlockSpec((N_DIR, R * kv_rows, ML_AUG), dirblk),
            pl.BlockSpec((N_DIR, R, 2 * LANES), dirblk),
        ],
        out_shape=[
            jax.ShapeDtypeStruct((n, ML_QK_WIDTH), BF16),
            jax.ShapeDtypeStruct((N_DIR, n, ML_HEADS * ML_AUG), BF16),
            jax.ShapeDtypeStruct((N_DIR, n * kv_rows, ML_AUG), BF16),
            jax.ShapeDtypeStruct((N_DIR, n, 2 * LANES), F32),
        ],
        compiler_params=_params("parallel", "parallel"),
        name="mlstm_prep",
    )(z, z, z, zg, gr)


def _mlstm_scan_kernel(qf, ivf, kvf, auxf, qb, ivb, kvb, auxb, of_ref, ob_ref, c_ref, m_ref, *, batch):
    L, C = SCAN_CHUNK, SCAN_STEP_CHUNKS

    @pl.when(pl.program_id(0) == 0)
    def _():
        c_ref[...] = jnp.zeros_like(c_ref)
        m_ref[...] = jnp.zeros_like(m_ref)

    dirs = ((qf, ivf, kvf, auxf, of_ref), (qb, ivb, kvb, auxb, ob_ref))
    lane = lax.broadcasted_iota(jnp.int32, (L, LANES), 1)
    keep = [jnp.where(lane < ML_QK_DIM, 1.0, 0.0).astype(BF16), jnp.where(lane < ML_QK_DIM, 0.0, 1.0).astype(BF16)]
    first_rows = lax.broadcasted_iota(jnp.int32, (2 * ML_QK_DIM, 1), 0) < ML_QK_DIM
    pair_chains = [(d, b, p) for d in range(N_DIR) for b in range(batch) for p in range(ML_PAIRS)]
    chains = [(d, b, h) for d in range(N_DIR) for b in range(batch) for h in range(ML_HEADS)]
    groups = [(d, b) for d in range(N_DIR) for b in range(batch)]
    H = ML_HEADS
    lane8 = lax.broadcasted_iota(jnp.int32, (SUBLANES, LANES), 1)
    sel_row = lax.broadcasted_iota(jnp.int32, (2 * LANES, LANES), 0) & (LANES - 1)
    lane_select = [_as01(sel_row == j) for j in range(2 * H)]
    state = {c: c_ref[c] for c in pair_chains}
    m_st = {g: m_ref[g] for g in groups}
    for step in range(C):
        chunk = {0: step, 1: C - 1 - step}
        rows = {d: slice(chunk[d] * L, (chunk[d] + 1) * L) for d in range(N_DIR)}
        qc = {}
        for c in pair_chains:
            d, b, p = c
            qp = dirs[d][0][b, rows[d], p * LANES:(p + 1) * LANES]
            qc[c] = _dot(jnp.concatenate([qp * keep[0], qp * keep[1]], axis=0), state[c].astype(BF16))
        w_prev, w_cur, floor = {}, {}, {}
        for g in groups:
            d, b = g
            aux = dirs[d][3][b, rows[d], :]
            x = aux[:, :LANES] + m_st[g][0:1, :]
            y = jnp.maximum(x, aux[:, LANES:])
            w_prev[g] = jnp.exp(x - y)
            w_cur[g] = jnp.exp(aux[:, LANES:] - y)
            floor[g] = jnp.exp(-y)
            y0 = jnp.broadcast_to(y[0:1, :], (SUBLANES, LANES))
            m_st[g] = jnp.where(lane8 < H, pltpu.roll(y0, LANES - H, 1), jnp.where(lane8 < 2 * H, y0, 0.0))
        tiles = [t[g] for g in groups for t in (w_prev, w_cur, floor)]
        per_row = _split_hi_lo(jnp.concatenate(tiles, axis=0))
        per_chunk = _split_hi_lo(jnp.concatenate([t[0:1, :] for t in tiles] + [tiles[0][0:SUBLANES, :]], axis=0))
        bc_row, bc_chunk = [], []
        for p in range(ML_PAIRS):
            both = _dot(per_row, jnp.concatenate([lane_select[2 * p], lane_select[2 * p + 1]], axis=-1))
            bc_row += [both[:, :LANES], both[:, LANES:]]
            both = _dot(per_chunk, jnp.concatenate([lane_select[H + 2 * p], lane_select[H + 2 * p + 1]], axis=-1))
            bc_chunk += [both[:, :LANES], both[:, LANES:]]
        numden = {}
        for c in chains:
            d, b, h = c
            r0 = (h % 2) * L
            t0 = 3 * groups.index((d, b)) * L
            wp = bc_row[h][t0:t0 + L, :]
            wc = bc_row[h][t0 + L:t0 + 2 * L, :]
            iv = dirs[d][1][b, rows[d], h * ML_AUG:(h + 1) * ML_AUG].astype(F32)
            qch = qc[d, b, h // 2][r0:r0 + L, :]
            numden[c] = jnp.concatenate([wp * qch[:, :ML_V_DIM] + wc * iv[:, :ML_V_DIM],
                                         wp * qch[:, ML_V_DIM:] + wc * iv[:, ML_V_DIM:]], axis=-1)
        for c in chains:
            d, b, h = c
            t0 = 3 * groups.index((d, b)) * L
            den = jnp.maximum(jnp.abs(numden[c][:, ML_V_DIM:]), bc_row[h][t0 + 2 * L:t0 + 3 * L, :])
            dirs[d][4][b, rows[d], h * ML_V_DIM:(h + 1) * ML_V_DIM] = (numden[c][:, :ML_V_DIM] / den).astype(BF16)
        for c in pair_chains:
            d, b, p = c
            r0 = (chunk[d] * ML_PAIRS + p) * LANES
            kv = dirs[d][2][b, r0:r0 + LANES, :].astype(F32)
            t0 = 3 * groups.index((d, b))
            cw = jnp.where(first_rows, bc_chunk[2 * p][t0:t0 + 1, :], bc_chunk[2 * p + 1][t0:t0 + 1, :])
            iw = jnp.where(first_rows, bc_chunk[2 * p][t0 + 1:t0 + 2, :], bc_chunk[2 * p + 1][t0 + 1:t0 + 2, :])
            state[c] = jnp.concatenate([cw * state[c][:, :ML_V_DIM] + iw * kv[:, :ML_V_DIM],
                                        cw * state[c][:, ML_V_DIM:] + iw * kv[:, ML_V_DIM:]], axis=-1)
    for c in pair_chains:
        c_ref[c] = state[c]
    for g in groups:
        m_ref[g] = m_st[g]


def _mlstm_scan(z, zg, gr, *, batch, seq, q_col, k_col, v_col):
    n = batch * seq
    T = SCAN_CHUNK * SCAN_STEP_CHUNKS
    steps = seq // T
    kv_rows = ML_KV_ROWS // SCAN_CHUNK
    qs, iv, kv, aux = _mlstm_prep(z, zg, gr, batch=batch, seq=seq, q_col=q_col, k_col=k_col, v_col=v_col)
    qs = qs.reshape(batch, seq, ML_QK_WIDTH)
    iv = iv.reshape(N_DIR, batch, seq, ML_HEADS * ML_AUG)
    kv = kv.reshape(N_DIR, batch, seq * kv_rows, ML_AUG)
    aux = aux.reshape(N_DIR, batch, seq, 2 * LANES)
    fwd = lambda s: s
    bwd = lambda s: steps - 1 - s
    specs = []
    for d, pos in ((0, fwd), (1, bwd)):
        specs += [
            pl.BlockSpec((batch, T, ML_QK_WIDTH), lambda s, pos=pos: (0, pos(s), 0)),
            pl.BlockSpec((None, batch, T, ML_HEADS * ML_AUG), lambda s, d=d, pos=pos: (d, 0, pos(s), 0)),
            pl.BlockSpec((None, batch, T * kv_rows, ML_AUG), lambda s, d=d, pos=pos: (d, 0, pos(s), 0)),
            pl.BlockSpec((None, batch, T, 2 * LANES), lambda s, d=d, pos=pos: (d, 0, pos(s), 0)),
        ]
    out = jax.ShapeDtypeStruct((batch, seq, ML_V_WIDTH), BF16)
    h_f, h_b = pl.pallas_call(
        functools.partial(_mlstm_scan_kernel, batch=batch),
        grid=(steps,),
        in_specs=specs,
        out_specs=[pl.BlockSpec((batch, T, ML_V_WIDTH), lambda s: (0, fwd(s), 0)),
                   pl.BlockSpec((batch, T, ML_V_WIDTH), lambda s: (0, bwd(s), 0))],
        out_shape=[out, out],
        scratch_shapes=[
            pltpu.VMEM((N_DIR, batch, ML_PAIRS, 2 * ML_QK_DIM, ML_AUG), F32),
            pltpu.VMEM((N_DIR, batch, SUBLANES, LANES), F32),
        ],
        compiler_params=_params("arbitrary"),
        name="mlstm_scan",
    )(qs, iv, kv, aux, qs, iv, kv, aux)
    return h_f.reshape(n, ML_V_WIDTH), h_b.reshape(n, ML_V_WIDTH)


def _head_rms_norm(x, g, n_heads, head_dim):
    parts = []
    for h in range(n_heads):
        xh = x[:, h * head_dim:(h + 1) * head_dim]
        ms = jnp.mean(xh * xh, axis=-1, keepdims=True)
        parts.append(xh * lax.rsqrt(ms + NORM_EPS) * g)
    return jnp.concatenate(parts, axis=-1)


def _layer_norm(x, g, b):
    mu = jnp.mean(x, axis=-1, keepdims=True)
    xc = x - mu
    var = jnp.mean(xc * xc, axis=-1, keepdims=True)
    return xc * lax.rsqrt(var + LN_EPS) * g + b


def _swiglu_residual(x, g_ref, w1_ref, w3_ref, w2_ref, fg_ref, final_norm):
    ms = jnp.mean(x * x, axis=-1, keepdims=True)
    hn = (x * lax.rsqrt(ms + NORM_EPS) * g_ref[...]).astype(BF16)
    a = _dot(hn, w1_ref[...])
    b = _dot(hn, w3_ref[...])
    y = x + _dot((a * jax.nn.sigmoid(a) * b).astype(BF16), w2_ref[...])
    if final_norm:
        ms = jnp.mean(y * y, axis=-1, keepdims=True)
        y = y * lax.rsqrt(ms + NORM_EPS) * fg_ref[...]
    return y


def _ffn_operands(layer, g, w1, w3, w2, final_g):
    _, d, f = w1.shape
    pick = (layer, 0, 0)
    specs = [_resident((1, d)), _resident((None, d, f), pick), _resident((None, d, f), pick),
             _resident((None, f, d), pick), _resident((1, d))]
    return specs, (g.reshape(1, d), w1.astype(BF16), w3.astype(BF16), w2.astype(BF16), final_g.reshape(1, d))


def _even_block_kernel(h_ref, c_ref, of_ref, ob_ref, gate_ref, lng_ref, lnb_ref, ng_ref, wa_ref, wb_ref,
                       g_ref, w1_ref, w3_ref, w2_ref, fg_ref, o_ref, *, final_norm):
    ya = _layer_norm(c_ref[...].astype(F32), lng_ref[...], lnb_ref[...])
    ya = ya * jax.nn.sigmoid(ya)
    o = _head_rms_norm(of_ref[...].astype(F32) + ob_ref[...].astype(F32), ng_ref[...], DN_HEADS, DN_HEAD_DIM)
    gate = gate_ref[...].astype(F32)
    o = o * (gate * jax.nn.sigmoid(gate))
    x = h_ref[...] + _dot(ya.astype(BF16), wa_ref[...]) + _dot(o.astype(BF16), wb_ref[...])
    o_ref[...] = _swiglu_residual(x, g_ref, w1_ref, w3_ref, w2_ref, fg_ref, final_norm)


def _even_block(h, conv, o_fwd, o_bwd, z, ln_g, ln_b, norm_g, w_out, ffn, *, gate_col, final_norm):
    n, d = h.shape
    tm = min(ROW_TILE, n)
    gb = gate_col // DN_WIDTH
    row = lambda i: (i, 0)
    ffn_specs, ffn_args = _ffn_operands(*ffn)
    return pl.pallas_call(
        functools.partial(_even_block_kernel, final_norm=final_norm),
        grid=(n // tm,),
        in_specs=[
            pl.BlockSpec((tm, d), row),
            pl.BlockSpec((tm, CONV_CH), row),
            pl.BlockSpec((tm, DN_WIDTH), row),
            pl.BlockSpec((tm, DN_WIDTH), row),
            pl.BlockSpec((tm, DN_WIDTH), lambda i: (i, gb)),
            _resident((1, CONV_CH)),
            _resident((1, CONV_CH)),
            _resident((1, DN_HEAD_DIM)),
            _resident((CONV_CH, d)),
            _resident((DN_WIDTH, d), (CONV_CH // DN_WIDTH, 0)),
        ] + ffn_specs,
        out_specs=pl.BlockSpec((tm, d), row),
        out_shape=jax.ShapeDtypeStruct((n, d), F32),
        compiler_params=_params("parallel"),
        name="even_block",
    )(h, conv, o_fwd, o_bwd, z, ln_g.reshape(1, -1), ln_b.reshape(1, -1), norm_g.reshape(1, -1),
      w_out.astype(BF16), w_out.astype(BF16), *ffn_args)


def _odd_block_kernel(h_ref, u_ref, vp_ref, hf_ref, hb_ref, op_ref, lng_ref, lnb_ref, sgw_ref, sgb_ref, ng_ref,
                      wa_ref, wb_ref, g_ref, w1_ref, w3_ref, w2_ref, fg_ref, o_ref, *, final_norm):
    tm = h_ref.shape[0]
    u = jax.nn.gelu(u_ref[...].astype(F32))
    vv = _layer_norm(jax.nn.gelu(vp_ref[...].astype(F32)), lng_ref[...], lnb_ref[...]).astype(BF16)
    sgb = sgb_ref[...]
    rows = []
    for c in range(tm // SG_CHUNK):
        parts = []
        for g in range(SG_GROUPS):
            blk = vv[c * SG_CHUNK:(c + 1) * SG_CHUNK, g * SG_GROUP_DIM:(g + 1) * SG_GROUP_DIM]
            parts.append(_dot(sgw_ref[g], blk) + sgb[:, g:g + 1])
        rows.append(jnp.concatenate(parts, axis=-1))
    yc = u * jnp.concatenate(rows, axis=0)
    hd = _head_rms_norm(hf_ref[...].astype(F32) + hb_ref[...].astype(F32), ng_ref[...], ML_HEADS, ML_V_DIM)
    hd = hd * jax.nn.sigmoid(op_ref[...].astype(F32))
    x = h_ref[...] + _dot(yc.astype(BF16), wa_ref[...]) + _dot(hd.astype(BF16), wb_ref[...])
    o_ref[...] = _swiglu_residual(x, g_ref, w1_ref, w3_ref, w2_ref, fg_ref, final_norm)


def _odd_block(h, z, h_fwd, h_bwd, ln_g, ln_b, sg_w, sg_b, norm_g, w_out, ffn, *, u_col, v_col, o_col, final_norm):
    n, d = h.shape
    tm = min(ROW_TILE, n)
    row = lambda i: (i, 0)
    ffn_specs, ffn_args = _ffn_operands(*ffn)
    return pl.pallas_call(
        functools.partial(_odd_block_kernel, final_norm=final_norm),
        grid=(n // tm,),
        in_specs=[
            pl.BlockSpec((tm, d), row),
            pl.BlockSpec((tm, SG_WIDTH), lambda i: (i, u_col // SG_WIDTH)),
            pl.BlockSpec((tm, SG_WIDTH), lambda i: (i, v_col // SG_WIDTH)),
            pl.BlockSpec((tm, ML_V_WIDTH), row),
            pl.BlockSpec((tm, ML_V_WIDTH), row),
            pl.BlockSpec((tm, ML_V_WIDTH), lambda i: (i, o_col // ML_V_WIDTH)),
            _resident((1, SG_WIDTH)),
            _resident((1, SG_WIDTH)),
            _resident((SG_GROUPS, SG_CHUNK, SG_CHUNK)),
            _resident((SG_CHUNK, SG_GROUPS)),
            _resident((1, ML_V_DIM)),
            _resident((SG_WIDTH, d)),
            _resident((ML_V_WIDTH, d), (SG_WIDTH // ML_V_WIDTH, 0)),
        ] + ffn_specs,
        out_specs=pl.BlockSpec((tm, d), row),
        out_shape=jax.ShapeDtypeStruct((n, d), F32),
        compiler_params=_params("parallel"),
        name="odd_block",
    )(h, z, z, h_fwd, h_bwd, z, ln_g.reshape(1, -1), ln_b.reshape(1, -1), sg_w.astype(BF16), sg_b.T,
      norm_g.reshape(1, -1), w_out.astype(BF16), w_out.astype(BF16), *ffn_args)


def _gate_weight(w_gate_cols):
    d, c = w_gate_cols.shape
    return jnp.zeros((d, LANES), F32).at[:, :c].set(w_gate_cols).astype(BF16)


def _gate_params(first_kind, second_kind):
    table = jnp.zeros((SUBLANES, LANES), F32)
    for r, (kind, values) in enumerate((first_kind, second_kind)):
        table = table.at[r, kind * GATE_KIND_LANES:(kind + 1) * GATE_KIND_LANES].set(values.reshape(-1))
    return table


def _even_layer(h, j, p, ffn, *, batch, seq, final_norm):
    main = 2 * CONV_CH + 4 * DN_WIDTH
    w_in = p["ev_w_in"][j]
    gate_params = _gate_params((1, p["ev_dn_a_log"][j]), (1, p["ev_dn_dt_bias"][j]))
    z, zg, gr = _in_proj(h, p["mix_norm_g"][2 * j], w_in.astype(BF16), main, _gate_weight(w_in[:, main:]),
                         gate_params, even=True)
    conv = _conv_glu(z, p["ev_conv_w"][j], p["ev_conv_b"][j], batch=batch, seq=seq)
    qkv = _conv_qkv(z, p["ev_dn_conv_w"][j], batch=batch, seq=seq, col0=2 * CONV_CH)
    o_fwd, o_bwd = _delta_scan(qkv, zg, gr, batch=batch, seq=seq)
    return _even_block(h, conv, o_fwd, o_bwd, z, p["ev_conv_ln_g"][j], p["ev_conv_ln_b"][j], p["ev_dn_norm_g"][j],
                       p["ev_w_out"][j], ffn, gate_col=2 * CONV_CH + 3 * DN_WIDTH, final_norm=final_norm)


def _odd_layer(h, j, p, ffn, *, batch, seq, final_norm):
    main = 2 * SG_WIDTH + 2 * ML_QK_WIDTH + 2 * ML_V_WIDTH
    w_in = p["od_w_in"][j]
    gate_params = _gate_params((0, p["od_ml_i_bias"][j]), (1, p["od_ml_f_bias"][j]))
    z, zg, gr = _in_proj(h, p["mix_norm_g"][2 * j + 1], w_in.astype(BF16), main, _gate_weight(w_in[:, main:]),
                         gate_params, even=False)
    q_col = 2 * SG_WIDTH
    k_col = q_col + ML_QK_WIDTH
    v_col = k_col + ML_QK_WIDTH
    o_col = v_col + ML_V_WIDTH
    h_fwd, h_bwd = _mlstm_scan(z, zg, gr, batch=batch, seq=seq, q_col=q_col, k_col=k_col, v_col=v_col)
    return _odd_block(h, z, h_fwd, h_bwd, p["od_sg_ln_g"][j], p["od_sg_ln_b"][j], p["od_sg_w"][j], p["od_sg_b"][j],
                      p["od_ml_norm_g"][j], p["od_w_out"][j], ffn, u_col=0, v_col=SG_WIDTH, o_col=o_col,
                      final_norm=final_norm)


def kernel(x, mix_norm_g, ev_w_in, ev_conv_w, ev_conv_b, ev_conv_ln_g, ev_conv_ln_b, ev_dn_conv_w, ev_dn_a_log, ev_dn_dt_bias, ev_dn_norm_g, ev_w_out, od_w_in, od_sg_ln_g, od_sg_ln_b, od_sg_w, od_sg_b, od_ml_i_bias, od_ml_f_bias, od_ml_norm_g, od_w_out, ffn_norm_g, ffn_w1, ffn_w3, ffn_w2, final_norm_g):
    p = dict(mix_norm_g=mix_norm_g, ev_w_in=ev_w_in, ev_conv_w=ev_conv_w, ev_conv_b=ev_conv_b,
             ev_conv_ln_g=ev_conv_ln_g, ev_conv_ln_b=ev_conv_ln_b, ev_dn_conv_w=ev_dn_conv_w,
             ev_dn_a_log=ev_dn_a_log, ev_dn_dt_bias=ev_dn_dt_bias, ev_dn_norm_g=ev_dn_norm_g, ev_w_out=ev_w_out,
             od_w_in=od_w_in, od_sg_ln_g=od_sg_ln_g, od_sg_ln_b=od_sg_ln_b, od_sg_w=od_sg_w, od_sg_b=od_sg_b,
             od_ml_i_bias=od_ml_i_bias, od_ml_f_bias=od_ml_f_bias, od_ml_norm_g=od_ml_norm_g, od_w_out=od_w_out)
    batch, seq, d = x.shape
    depth = mix_norm_g.shape[0]
    h = x.reshape(batch * seq, d)
    for layer in range(depth):
        ffn = (layer, ffn_norm_g[layer], ffn_w1, ffn_w3, ffn_w2, final_norm_g)
        mixer_layer = _even_layer if layer % 2 == 0 else _odd_layer
        h = mixer_layer(h, layer // 2, p, ffn, batch=batch, seq=seq, final_norm=layer == depth - 1)
    return h.reshape(batch, seq, d)
```
